```python
import jax
import jax.numpy as jnp
from jax import lax
import numpy as np

D_MODEL = 1024
BATCH = 32
SEQ = 2048
DEPTH = 2

CHUNK = 64
N_MEM = 256
GROUP_WIDTH = D_MODEL // 4
N_GROUPS = 5
D_MIX = N_GROUPS * GROUP_WIDTH
N_HEADS = 4
HEAD_DIM = GROUP_WIDTH // N_HEADS
POOL_WINDOWS = (2, 4, 8, 16)
POOL_CH = GROUP_WIDTH // len(POOL_WINDOWS)
Q_BLOCK = 128
EPS = 1e-6
NEG_BIG = -1e30
LB_FLOOR = 1e-30
IN_SPLITS = (GROUP_WIDTH,) * 4 + (N_HEADS,) + (GROUP_WIDTH,) * 12
D_IN = sum(IN_SPLITS)

kernel_name = "hybrid_fox_stickbreak_hgrn2_pool_memory"

F32 = jnp.float32


def _rms(x, g):
    xf = x.astype(F32)
    y = xf * lax.rsqrt(jnp.mean(xf * xf, axis=-1, keepdims=True) + EPS)
    return (y * g.astype(F32)).astype(x.dtype)


def _split_heads(t):
    b, s, _ = t.shape
    return t.reshape(b, s, N_HEADS, HEAD_DIM).transpose(0, 2, 1, 3)


def _merge_heads(t):
    b, h, s, d = t.shape
    return t.transpose(0, 2, 1, 3).reshape(b, s, h * d)


def _forgetting_attention(q, k, v, log_f):
    s_len = q.shape[2]
    c = jnp.cumsum(log_f, axis=-1)
    scale = HEAD_DIM ** -0.5
    outs = []
    for i in range(s_len // Q_BLOCK):
        t0, t1 = i * Q_BLOCK, (i + 1) * Q_BLOCK
        logits = jnp.einsum('bhtd,bhsd->bhts', q[:, :, t0:t1], k[:, :, :t1]).astype(F32) * scale
        logits = logits + c[:, :, t0:t1, None] - c[:, :, None, :t1]
        mask = jnp.arange(t1)[None, :] <= jnp.arange(t0, t1)[:, None]
        probs = jax.nn.softmax(jnp.where(mask, logits, NEG_BIG), axis=-1)
        outs.append(jnp.einsum('bhts,bhsd->bhtd', probs.astype(v.dtype), v[:, :, :t1]))
    return jnp.concatenate(outs, axis=2)


def _stick_breaking_attention(q, k, v):
    s_len = q.shape[2]
    scale = HEAD_DIM ** -0.5
    outs = []
    for i in range(s_len // Q_BLOCK):
        t0, t1 = i * Q_BLOCK, (i + 1) * Q_BLOCK
        z = jnp.einsum('bhtd,bhsd->bhts', q[:, :, t0:t1], k[:, :, :t1]).astype(F32) * scale
        mask = jnp.arange(t1)[None, :] < jnp.arange(t0, t1)[:, None]
        log_one_minus = jnp.where(mask, jax.nn.log_sigmoid(-z), 0.0)
        log_between = lax.cumsum(log_one_minus, axis=3, reverse=True) - log_one_minus
        log_w = jnp.where(mask, jax.nn.log_sigmoid(z) + log_between, NEG_BIG)
        weights = jnp.where(mask, jnp.exp(log_w), 0.0)
        outs.append(jnp.einsum('bhts,bhsd->bhtd', weights.astype(v.dtype), v[:, :, :t1]))
    return jnp.concatenate(outs, axis=2)


def _hgrn2(q, k, v, log_f):
    b, h, s_len, dk = q.shape
    dv = v.shape[-1]
    n_chunks = s_len // CHUNK

    def chunks(t):
        return t.astype(F32).reshape(b, h, n_chunks, CHUNK, t.shape[-1]).transpose(2, 0, 1, 3, 4)

    causal = jnp.tril(jnp.ones((CHUNK, CHUNK), dtype=bool))[:, :, None]

    def step(state, inp):
        qc, kc, vc, gc = inp
        bcum = jnp.cumsum(gc, axis=2)
        o_inter = jnp.einsum('bhtd,bhde->bhte', qc * jnp.exp(bcum), state)
        diff = bcum[:, :, :, None, :] - bcum[:, :, None, :, :]
        decay = jnp.where(causal, jnp.exp(jnp.where(causal, diff, 0.0)), 0.0)
        scores = jnp.einsum('bhtd,bhsd,bhtsd->bhts', qc, kc, decay)
        o_intra = jnp.einsum('bhts,bhse->bhte', scores, vc)
        b_last = bcum[:, :, -1, :]
        state = jnp.exp(b_last)[..., None] * state + jnp.einsum(
            'bhsd,bhse->bhde', kc * jnp.exp(b_last[:, :, None, :] - bcum), vc)
        return state, o_inter + o_intra

    state0 = jnp.zeros((b, h, dk, dv), F32)
    _, o = lax.scan(step, state0, (chunks(q), chunks(k), chunks(v), chunks(log_f)))
    return o.transpose(1, 2, 0, 3, 4).reshape(b, h, s_len, dv)


def _pool_mixer(u, w, scale):
    b, s_len, _ = u.shape
    n_g = len(POOL_WINDOWS)
    uf = u.astype(F32).reshape(b, s_len, n_g, POOL_CH)
    cs = jnp.cumsum(uf, axis=1)
    cs = jnp.concatenate([jnp.zeros_like(cs[:, :1]), cs], axis=1)
    pos = jnp.arange(1, s_len + 1, dtype=F32)
    means = []
    for gi, win in enumerate(POOL_WINDOWS):
        c = cs[:, :, gi]
        hi = c[:, 1:]
        lo = jnp.pad(c[:, :s_len + 1 - win], ((0, 0), (win - 1, 0), (0, 0)))
        means.append((hi - lo) / jnp.minimum(pos, win)[None, :, None])
    pooled = jnp.stack(means, axis=2)
    y = jnp.einsum('bsgc,gcd->bsgd', pooled - uf, w.astype(F32))
    y = y * scale.astype(F32).reshape(n_g, POOL_CH)
    return y.reshape(b, s_len, GROUP_WIDTH)


def _memory_attention(q, mem, mem_norm_g, mem_w_kv, q_norm, k_norm):
    mn = _rms(mem, mem_norm_g)
    kv = jnp.einsum('bmd,dn->bmn', mn, mem_w_kv)
    k, v = jnp.split(kv, 2, axis=-1)
    qh = _rms(_split_heads(q), q_norm)
    kh = _rms(_split_heads(k), k_norm)
    vh = _split_heads(v)
    logits = jnp.einsum('bhtd,bhmd->bhtm', qh, kh).astype(F32) * (HEAD_DIM ** -0.5)
    probs = jax.nn.softmax(logits, axis=-1)
    return _merge_heads(jnp.einsum('bhtm,bhmd->bhtd', probs.astype(vh.dtype), vh))


def _hybrid_layer(x, mem, norm_g, w_in, fox_f_bias, fox_q_norm, fox_k_norm, lower_bound,
                  hgrn_out_norm, pool_w, pool_scale, mem_norm_g, mem_w_kv, mem_q_norm,
                  mem_k_norm, w_out):
    h = _rms(x, norm_g)
    proj = jnp.einsum('bsd,dn->bsn', h, w_in)
    split_points = np.cumsum(IN_SPLITS)[:-1].tolist()
    (fq, fk, fv, fg, ff, sq, sk, sv, sg, hq, hf, hi, hg, pv, pg, mq, mg) = jnp.split(
        proj, split_points, axis=-1)

    log_f_fox = jax.nn.log_sigmoid((ff + fox_f_bias).astype(F32)).transpose(0, 2, 1)
    qa = _rms(_split_heads(fq), fox_q_norm)
    ka = _rms(_split_heads(fk), fox_k_norm)
    out_a = _merge_heads(_forgetting_attention(qa, ka, _split_heads(fv), log_f_fox))

    out_b = _merge_heads(_stick_breaking_attention(_split_heads(sq), _split_heads(sk), _split_heads(sv)))

    lb = lower_bound.astype(F32)
    hf32 = hf.astype(F32)
    log_lb = jnp.log(jnp.maximum(lb, LB_FLOOR))
    log_f_h = jnp.logaddexp(log_lb, jnp.log1p(-lb) + jax.nn.log_sigmoid(hf32))
    k_h = (1.0 - lb) * jax.nn.sigmoid(-hf32)
    oc = _hgrn2(_split_heads(jax.nn.silu(hq)), _split_heads(k_h), _split_heads(hi), _split_heads(log_f_h))
    oc = _rms(oc, hgrn_out_norm.reshape(N_HEADS, 1, HEAD_DIM))
    out_c = _merge_heads(oc)

    out_d = _pool_mixer(pv, pool_w, pool_scale)

    out_e = _memory_attention(mq, mem, mem_norm_g, mem_w_kv, mem_q_norm, mem_k_norm)

    mixed = jnp.concatenate([
        out_a.astype(x.dtype) * jax.nn.silu(fg),
        out_b.astype(x.dtype) * jax.nn.silu(sg),
        out_c.astype(x.dtype) * jax.nn.silu(hg),
        out_d.astype(x.dtype) * jax.nn.silu(pg),
        out_e.astype(x.dtype) * jax.nn.silu(mg),
    ], axis=-1)
    return x + jnp.einsum('bsn,nd->bsd', mixed, w_out).astype(x.dtype)


def setup_inputs(seed: int = 0) -> dict:
    key = jax.random.key(seed)
    ks = jax.random.split(key, 16)

    def nrm(k, shape, scale):
        return scale * jax.random.normal(k, shape, F32)

    return {
        "x": nrm(ks[0], (BATCH, SEQ, D_MODEL), 1.0),
        "mem": nrm(ks[1], (BATCH, N_MEM, D_MODEL), 1.0),
        "norm_g": 1.0 + nrm(ks[2], (DEPTH, D_MODEL), 0.02),
        "w_in": nrm(ks[3], (DEPTH, D_MODEL, D_IN), D_MODEL ** -0.5),
        "fox_f_bias": nrm(ks[4], (DEPTH, N_HEADS), 0.1),
        "fox_q_norm": 1.0 + nrm(ks[5], (DEPTH, HEAD_DIM), 0.02),
        "fox_k_norm": 1.0 + nrm(ks[6], (DEPTH, HEAD_DIM), 0.02),
        "hgrn_lb_logits": nrm(ks[7], (DEPTH, GROUP_WIDTH), 0.5),
        "hgrn_out_norm": 1.0 + nrm(ks[8], (DEPTH, GROUP_WIDTH), 0.02),
        "pool_w": nrm(ks[9], (DEPTH, len(POOL_WINDOWS), POOL_CH, POOL_CH), POOL_CH ** -0.5),
        "pool_scale": 1.0 + nrm(ks[10], (DEPTH, GROUP_WIDTH), 0.1),
        "mem_norm_g": 1.0 + nrm(ks[11], (DEPTH, D_MODEL), 0.02),
        "mem_w_kv": nrm(ks[12], (DEPTH, D_MODEL, 2 * GROUP_WIDTH), D_MODEL ** -0.5),
        "mem_q_norm": 1.0 + nrm(ks[13], (DEPTH, HEAD_DIM), 0.02),
        "mem_k_norm": 1.0 + nrm(ks[14], (DEPTH, HEAD_DIM), 0.02),
        "w_out": nrm(ks[15], (DEPTH, D_MIX, D_MODEL), D_MIX ** -0.5),
    }


def reference(x, mem, norm_g, w_in, fox_f_bias, fox_q_norm, fox_k_norm, hgrn_lb_logits,
              hgrn_out_norm, pool_w, pool_scale, mem_norm_g, mem_w_kv, mem_q_norm,
              mem_k_norm, w_out):
    p = jax.nn.softmax(hgrn_lb_logits.astype(F32), axis=0)
    lower_bounds = jnp.clip(jnp.cumsum(p, axis=0) - p[0:1], 0.0, 1.0 - 1e-6)
    for l in range(DEPTH):
        x = _hybrid_layer(x, mem, norm_g[l], w_in[l], fox_f_bias[l], fox_q_norm[l], fox_k_norm[l],
                          lower_bounds[l], hgrn_out_norm[l], pool_w[l], pool_scale[l],
                          mem_norm_g[l], mem_w_kv[l], mem_q_norm[l], mem_k_norm[l], w_out[l])
    return x
```

```python
import functools

import numpy as np
import jax
import jax.numpy as jnp
from jax import lax
from jax.experimental import pallas as pl
from jax.experimental.pallas import tpu as pltpu

F32 = jnp.float32
BF16 = jnp.bfloat16

D_MODEL = 1024
GW = 256
NH = 4
HD = 64
CHUNK = 64
POOL_WINDOWS = (2, 4, 8, 16)
EPS = 1e-6
NEG_BIG = -1e30
LB_FLOOR = 1e-30
SCALE = HD ** -0.5

NP_COLS = 14 * GW
NF_COLS = 2 * GW + 128
VMEM_LIMIT = 56 * 1024 * 1024


def _dot(a, b):
    return jnp.dot(a, b, preferred_element_type=F32)


def _dot_nt(a, b):
    return lax.dot_general(a, b, (((1,), (1,)), ((), ())), preferred_element_type=F32)


def _dot_tn(a, b):
    return lax.dot_general(a, b, (((0,), (0,)), ((), ())), preferred_element_type=F32)


def _split_bf16(x, n):
    parts = []
    r = x
    for i in range(n):
        p = r.astype(BF16)
        parts.append(p)
        if i + 1 < n:
            r = r - p.astype(F32)
    return parts


def _dot_f32_rhs01(x, m01, n=3):
    acc = None
    for p in _split_bf16(x, n):
        t = _dot(p, m01)
        acc = t if acc is None else acc + t
    return acc


def _dot_f32_lhs01(m01, x, n=3):
    acc = None
    for p in _split_bf16(x, n):
        t = _dot(m01, p)
        acc = t if acc is None else acc + t
    return acc


def _sigmoid(x):
    e = jnp.exp(-jnp.abs(x))
    r = 1.0 / (1.0 + e)
    return jnp.where(x >= 0, r, e * r)


def _silu(x):
    return x * _sigmoid(x)


def _log_sigmoid(x):
    return jnp.minimum(x, 0.0) - jnp.log(1.0 + jnp.exp(-jnp.abs(x)))


def _head_rms(x, gain, bd):
    ss = _dot_f32_rhs01(x * x, bd, 2)
    return x * lax.rsqrt(ss * (1.0 / HD) + EPS) * gain


def _bd_ones():
    h = np.arange(GW) // HD
    return (h[:, None] == h[None, :]).astype(np.float32)


def _hgrn_constants():
    c = CHUNK
    t = np.arange(c)
    mats = []
    masks = []
    for n in (64, 32, 16, 8, 4):
        blk, pos = t // n, t % n
        ref = blk * n + n // 2 - 1
        j = np.arange(c)[None, :]
        aq = ((pos[:, None] >= n // 2) & (j > ref[:, None]) & (j <= t[:, None])).astype(np.float32)
        ak = ((pos[:, None] < n // 2) & (j > t[:, None]) & (j <= ref[:, None])).astype(np.float32)
        mats += [aq, ak]
        m = (blk[:, None] == blk[None, :]) & (pos[:, None] >= n // 2) & (pos[None, :] < n // 2)
        masks.append(m.astype(np.float32))
    odd = (t % 2 == 1)
    aq = np.diag(odd.astype(np.float32))
    ak = -np.diag(odd.astype(np.float32))
    mats += [aq, ak]
    m = ((t[:, None] // 2) == (t[None, :] // 2)) & (t[None, :] <= t[:, None])
    masks.append(m.astype(np.float32))
    j = np.arange(c)[None, :]
    mats.append((j <= t[:, None]).astype(np.float32))
    mats.append((j > t[:, None]).astype(np.float32))
    a_all = np.concatenate(mats, axis=0)
    a_all = np.concatenate([a_all, a_all], axis=1)
    masks = np.stack([np.tile(m, (1, NH)) for m in masks])
    total = masks[:, :, :c].sum(0)
    assert np.array_equal(total, np.tril(np.ones((c, c), np.float32)))
    hm = (np.arange(NH * c)[:, None] // c == np.arange(GW)[None, :] // HD).astype(np.float32)
    return a_all, masks, hm


def _proj_kernel(x_ref, g_ref, wb_ref, wf_ref, pb_ref, pf_ref):
    x = x_ref[...]
    ms = jnp.mean(x * x, axis=-1, keepdims=True)
    h = (x * lax.rsqrt(ms + EPS) * g_ref[...]).astype(BF16)
    nb = 512
    for c0 in range(0, NP_COLS, nb):
        pb_ref[:, c0:c0 + nb] = _dot(h, wb_ref[:, c0:c0 + nb]).astype(BF16)
    pf_ref[...] = _dot(h, wf_ref[...])


def _proj(x2, g, wb, wf, tm):
    m = x2.shape[0]
    return pl.pallas_call(
        _proj_kernel,
        grid=(m // tm,),
        in_specs=[
            pl.BlockSpec((tm, D_MODEL), lambda i: (i, 0)),
            pl.BlockSpec((1, D_MODEL), lambda i: (0, 0)),
            pl.BlockSpec((D_MODEL, NP_COLS), lambda i: (0, 0)),
            pl.BlockSpec((D_MODEL, NF_COLS), lambda i: (0, 0)),
        ],
        out_specs=[
            pl.BlockSpec((tm, NP_COLS), lambda i: (i, 0)),
            pl.BlockSpec((tm, NF_COLS), lambda i: (i, 0)),
        ],
        out_shape=[
            jax.ShapeDtypeStruct((m, NP_COLS), BF16),
            jax.ShapeDtypeStruct((m, NF_COLS), F32),
        ],
        compiler_params=pltpu.CompilerParams(
            dimension_semantics=("arbitrary",), vmem_limit_bytes=VMEM_LIMIT),
    )(x2, g, wb, wf)


def _fox_kernel(p_ref, ff_ref, bias_ref, gq_ref, gk_ref, bd_ref, tri_ref, o_ref,
                qh, kh, vh, ccol, crow, oscr, *, seq, tq):
    bd = bd_ref[...]
    tri = tri_ref[...]
    nblk = seq // tq
    carry = jnp.zeros((1, 128), F32)
    for b in range(nblk):
        r = slice(b * tq, (b + 1) * tq)
        lf = _log_sigmoid(ff_ref[0, r, :] + bias_ref[...])
        cb = _dot_f32_lhs01(tri, lf, 3) + carry
        carry = cb[tq - 1:tq, :]
        ccol[r, :] = cb
        crow[:, r] = cb.T[:8, :]
        q = _head_rms(p_ref[0, r, 0:GW].astype(F32), gq_ref[...], bd) * SCALE
        k = _head_rms(p_ref[0, r, GW:2 * GW].astype(F32), gk_ref[...], bd)
        qb = q.astype(BF16)
        kb = k.astype(BF16)
        v = p_ref[0, r, 2 * GW:3 * GW]
        for h in range(NH):
            ls = slice(h * HD, (h + 1) * HD)
            qh[h, r, :] = qb[:, ls]
            kh[h, r, :] = kb[:, ls]
            vh[h, r, :] = v[:, ls]

    row = lax.broadcasted_iota(jnp.int32, (tq, tq), 0)
    col = lax.broadcasted_iota(jnp.int32, (tq, tq), 1)
    causal = col <= row

    for h in range(NH):
        def q_block(i, _):
            r0 = pl.multiple_of(i * tq, tq)
            q = qh[h, pl.ds(r0, tq), :]
            ct = ccol[pl.ds(r0, tq), h:h + 1]

            def scores(j):
                s0 = pl.multiple_of(j * tq, tq)
                s = _dot_nt(q, kh[h, pl.ds(s0, tq), :])
                return s + (ct - crow[h:h + 1, pl.ds(s0, tq)]), s0

            def update(s, s0, m, l, acc):
                m_new = jnp.maximum(m, jnp.max(s, axis=-1, keepdims=True))
                p = jnp.exp(s - m_new)
                alpha = jnp.exp(m - m_new)
                l = alpha * l + jnp.sum(p, axis=-1, keepdims=True)
                acc = alpha * acc + _dot(p.astype(BF16), vh[h, pl.ds(s0, tq), :])
                return m_new, l, acc

            def kv_step(j, carry):
                s, s0 = scores(j)
                return update(s, s0, *carry)

            init = (jnp.full((tq, 1), NEG_BIG, F32), jnp.zeros((tq, 1), F32), jnp.zeros((tq, HD), F32))
            m, l, acc = lax.fori_loop(0, i, kv_step, init)
            s, s0 = scores(i)
            s = jnp.where(causal, s, NEG_BIG)
            m, l, acc = update(s, s0, m, l, acc)
            oscr[pl.ds(r0, tq), h * HD:(h + 1) * HD] = acc / l
            return 0

        lax.fori_loop(0, nblk, q_block, 0)

    for b in range(nblk):
        r = slice(b * tq, (b + 1) * tq)
        g = p_ref[0, r, 3 * GW:4 * GW].astype(F32)
        o_ref[0, r, :] = (oscr[r, :] * _silu(g)).astype(BF16)


def _fox(p3, f3, bias, gq, gk, bd, tri, tq):
    b, s, _ = p3.shape
    kern = functools.partial(_fox_kernel, seq=s, tq=tq)
    c2 = lambda i: (0, 0)
    return pl.pallas_call(
        kern,
        grid=(b,),
        in_specs=[
            pl.BlockSpec((1, s, 4 * GW), lambda i: (i, 0, 0)),
            pl.BlockSpec((1, s, 128), lambda i: (i, 0, 4)),
            pl.BlockSpec((1, 128), c2),
            pl.BlockSpec((1, GW), c2),
            pl.BlockSpec((1, GW), c2),
            pl.BlockSpec((GW, GW), c2),
            pl.BlockSpec((tq, tq), c2),
        ],
        out_specs=pl.BlockSpec((1, s, GW), lambda i: (i, 0, 0)),
        out_shape=jax.ShapeDtypeStruct((b, s, GW), BF16),
        scratch_shapes=[
            pltpu.VMEM((NH, s, HD), BF16),
            pltpu.VMEM((NH, s, HD), BF16),
            pltpu.VMEM((NH, s, HD), BF16),
            pltpu.VMEM((s, 128), F32),
            pltpu.VMEM((8, s), F32),
            pltpu.VMEM((s, GW), F32),
        ],
        compiler_params=pltpu.CompilerParams(
            dimension_semantics=("arbitrary",), vmem_limit_bytes=VMEM_LIMIT),
    )(p3, f3, bias, gq, gk, bd, tri)


def _sb_kernel(p_ref, to_ref, o_ref, qh, kh, vh, oscr, *, seq, tq):
    to = to_ref[...]
    nblk = seq // tq
    big = 256
    for b in range(seq // big):
        r = slice(b * big, (b + 1) * big)
        qb = (p_ref[0, r, 0:GW].astype(F32) * SCALE).astype(BF16)
        kb = p_ref[0, r, GW:2 * GW]
        v = p_ref[0, r, 2 * GW:3 * GW]
        for h in range(NH):
            ls = slice(h * HD, (h + 1) * HD)
            qh[h, r, :] = qb[:, ls]
            kh[h, r, :] = kb[:, ls]
            vh[h, r, :] = v[:, ls]

    row = lax.broadcasted_iota(jnp.int32, (tq, tq), 0)
    col = lax.broadcasted_iota(jnp.int32, (tq, tq), 1)
    strict = col < row

    for h in range(NH):
        def q_block(i, _):
            r0 = pl.multiple_of(i * tq, tq)
            q = qh[h, pl.ds(r0, tq), :]

            def block(j, carry, acc, diag):
                s0 = pl.multiple_of(j * tq, tq)
                z = _dot_nt(q, kh[h, pl.ds(s0, tq), :])
                lom = -(jnp.maximum(z, 0.0) + jnp.log(1.0 + jnp.exp(-jnp.abs(z))))
                if diag:
                    lom = jnp.where(strict, lom, 0.0)
                hi = lom.astype(BF16)
                lo = (lom - hi.astype(F32)).astype(BF16)
                cs = _dot(jnp.concatenate([hi, lo], axis=1), to)
                w = jnp.exp(z + lom + cs[:, :tq] + carry)
                if diag:
                    w = jnp.where(strict, w, 0.0)
                acc = acc + _dot(w.astype(BF16), vh[h, pl.ds(s0, tq), :])
                return carry + cs[:, tq:], acc

            carry, acc = block(i, jnp.zeros((tq, tq), F32), jnp.zeros((tq, HD), F32), True)

            def kv_step(jj, c):
                return block(i - 1 - jj, c[0], c[1], False)

            carry, acc = lax.fori_loop(0, i, kv_step, (carry, acc))
            oscr[pl.ds(r0, tq), h * HD:(h + 1) * HD] = acc
            return 0

        lax.fori_loop(0, nblk, q_block, 0)

    for b in range(seq // big):
        r = slice(b * big, (b + 1) * big)
        g = p_ref[0, r, 3 * GW:4 * GW].astype(F32)
        o_ref[0, r, :] = (oscr[r, :] * _silu(g)).astype(BF16)


def _sb(p3, to, tq):
    b, s, _ = p3.shape
    kern = functools.partial(_sb_kernel, seq=s, tq=tq)
    return pl.pallas_call(
        kern,
        grid=(b,),
        in_specs=[
            pl.BlockSpec((1, s, 4 * GW), lambda i: (i, 0, 1)),
            pl.BlockSpec((2 * tq, 2 * tq), lambda i: (0, 0)),
        ],
        out_specs=pl.BlockSpec((1, s, GW), lambda i: (i, 0, 0)),
        out_shape=jax.ShapeDtypeStruct((b, s, GW), BF16),
        scratch_shapes=[
            pltpu.VMEM((NH, s, HD), BF16),
            pltpu.VMEM((NH, s, HD), BF16),
            pltpu.VMEM((NH, s, HD), BF16),
            pltpu.VMEM((s, GW), F32),
        ],
        compiler_params=pltpu.CompilerParams(
            dimension_semantics=("arbitrary",), vmem_limit_bytes=VMEM_LIMIT),
    )(p3, to)


def _hgrn_kernel(hq_ref, hi_ref, hg_ref, hf_ref, lbm_ref, oml_ref, gout_ref, bd_ref, a_ref,
                 mask_ref, hm_ref, o_ref, st_ref, *, seq):
    bd = bd_ref[...]
    a_all = a_ref[...]
    hm = hm_ref[...]
    lbm = lbm_ref[...]
    oml = oml_ref[...]
    c = CHUNK
    st_ref[...] = jnp.zeros((GW, GW), F32)

    def chunk(ci, _):
        r0 = pl.multiple_of(ci * c, c)
        rows = pl.ds(r0, c)
        hf = hf_ref[0, rows, :]
        e = jnp.exp(-jnp.abs(hf))
        rr = 1.0 / (1.0 + e)
        sg = jnp.where(hf >= 0, rr, e * rr)
        sgn = jnp.where(hf >= 0, e * rr, rr)
        g = jnp.log(lbm + oml * sg)
        kk = oml * sgn
        q = _silu(hq_ref[0, rows, :].astype(F32))
        v = hi_ref[0, rows, :]

        gh = g.astype(BF16)
        gl = (g - gh.astype(F32)).astype(BF16)
        ex = _dot(a_all, jnp.concatenate([gh, gl], axis=0))

        vx = jnp.concatenate([v] * NH, axis=0) * hm
        p = jnp.zeros((c, GW), F32)
        for l in range(6):
            eq = ex[(2 * l) * c:(2 * l + 1) * c]
            ek = ex[(2 * l + 1) * c:(2 * l + 2) * c]
            qf = (q * jnp.exp(eq)).astype(BF16)
            kf = (kk * jnp.exp(ek)).astype(BF16)
            kx = jnp.concatenate([kf] * NH, axis=0) * hm
            p = p + _dot_nt(qf, kx) * mask_ref[l]
        o = _dot(p.astype(BF16), vx)

        eb = ex[12 * c:13 * c]
        er = ex[13 * c:14 * c]
        st = st_ref[...]
        o = o + _dot_nt((q * jnp.exp(eb)).astype(BF16), st.astype(BF16))
        kd = (kk * jnp.exp(er)).astype(BF16)
        upd = _dot_tn(v, kd)
        st_ref[...] = st * jnp.exp(eb[c - 1:c, :]) + upd * bd.astype(F32)

        o = _head_rms(o, gout_ref[...], bd)
        o_ref[0, rows, :] = (o * _silu(hg_ref[0, rows, :].astype(F32))).astype(BF16)
        return 0

    lax.fori_loop(0, seq // c, chunk, 0)


def _hgrn(p3, f3, lbm, oml, gout, bd, a_all, masks, hm):
    b, s, _ = p3.shape
    kern = functools.partial(_hgrn_kernel, seq=s)
    c2 = lambda i: (0, 0)
    return pl.pallas_call(
        kern,
        grid=(b,),
        in_specs=[
            pl.BlockSpec((1, s, GW), lambda i: (i, 0, 8)),
            pl.BlockSpec((1, s, GW), lambda i: (i, 0, 9)),
            pl.BlockSpec((1, s, GW), lambda i: (i, 0, 10)),
            pl.BlockSpec((1, s, GW), lambda i: (i, 0, 0)),
            pl.BlockSpec((1, GW), c2),
            pl.BlockSpec((1, GW), c2),
            pl.BlockSpec((1, GW), c2),
            pl.BlockSpec((GW, GW), c2),
            pl.BlockSpec(a_all.shape, c2),
            pl.BlockSpec(masks.shape, lambda i: (0, 0, 0)),
            pl.BlockSpec(hm.shape, c2),
        ],
        out_specs=pl.BlockSpec((1, s, GW), lambda i: (i, 0, 0)),
        out_shape=jax.ShapeDtypeStruct((b, s, GW), BF16),
        scratch_shapes=[pltpu.VMEM((GW, GW), F32)],
        compiler_params=pltpu.CompilerParams(
            dimension_semantics=("arbitrary",), vmem_limit_bytes=VMEM_LIMIT),
    )(p3, p3, p3, f3, lbm, oml, gout, bd, a_all, masks, hm)


def _pm_kernel(pv_ref, pg_ref, mq_ref, mg_ref, mem_ref, mng_ref, wkv_ref, gmq_ref, gmk_ref, bd_ref,
               wp_ref, ps_ref, win_ref, d_ref, e_ref, ubuf, kh, vh, *, seq, tq):
    bd = bd_ref[...]
    halo = 16
    mem = mem_ref[0]
    ms = jnp.mean(mem * mem, axis=-1, keepdims=True)
    mn = (mem * lax.rsqrt(ms + EPS) * mng_ref[...]).astype(BF16)
    kv = _dot(mn, wkv_ref[...])
    kn = _head_rms(kv[:, :GW], gmk_ref[...], bd).astype(BF16)
    vv = kv[:, GW:].astype(BF16)
    for h in range(NH):
        kh[h] = kn[:, h * HD:(h + 1) * HD]
        vh[h] = vv[:, h * HD:(h + 1) * HD]

    ubuf[0:halo, :] = jnp.zeros((halo, GW), F32)
    ubuf[halo:halo + seq, :] = pv_ref[0]
    win = win_ref[...]

    for b in range(seq // tq):
        r = slice(b * tq, (b + 1) * tq)
        base = halo + b * tq

        def ld(j):
            return ubuf[base - j:base - j + tq, :]

        u = ld(0)
        s2 = u + ld(1)
        s4 = s2 + ld(2) + ld(3)
        s8 = s4 + ld(4) + ld(5) + ld(6) + ld(7)
        s16 = s8
        for j in range(8, 16):
            s16 = s16 + ld(j)
        sw = jnp.where(win == 2.0, s2, jnp.where(win == 4.0, s4, jnp.where(win == 8.0, s8, s16)))
        pos = (lax.broadcasted_iota(jnp.int32, (tq, GW), 0) + (b * tq + 1)).astype(F32)
        pooled = sw / jnp.minimum(pos, win)
        y = _dot((pooled - u).astype(BF16), wp_ref[...]) * ps_ref[...]
        d_ref[0, r, :] = (y * _silu(pg_ref[0, r, :].astype(F32))).astype(BF16)

        qn = (_head_rms(mq_ref[0, r, :].astype(F32), gmq_ref[...], bd) * SCALE).astype(BF16)
        outs = []
        for h in range(NH):
            s = _dot_nt(qn[:, h * HD:(h + 1) * HD], kh[h])
            m = jnp.max(s, axis=-1, keepdims=True)
            p = jnp.exp(s - m)
            l = jnp.sum(p, axis=-1, keepdims=True)
            outs.append(_dot(p.astype(BF16), vh[h]) / l)
        oe = jnp.concatenate(outs, axis=1)
        e_ref[0, r, :] = (oe * _silu(mg_ref[0, r, :].astype(F32))).astype(BF16)


def _pm(p3, f3, mem, mng, wkv, gmq, gmk, bd, wp, ps, win, tq):
    b, s, _ = p3.shape
    nm = mem.shape[1]
    kern = functools.partial(_pm_kernel, seq=s, tq=tq)
    c2 = lambda i: (0, 0)
    return pl.pallas_call(
        kern,
        grid=(b,),
        in_specs=[
            pl.BlockSpec((1, s, GW), lambda i: (i, 0, 1)),
            pl.BlockSpec((1, s, GW), lambda i: (i, 0, 11)),
            pl.BlockSpec((1, s, GW), lambda i: (i, 0, 12)),
            pl.BlockSpec((1, s, GW), lambda i: (i, 0, 13)),
            pl.BlockSpec((1, nm, D_MODEL), lambda i: (i, 0, 0)),
            pl.BlockSpec((1, D_MODEL), c2),
            pl.BlockSpec((D_MODEL, 2 * GW), c2),
            pl.BlockSpec((1, GW), c2),
            pl.BlockSpec((1, GW), c2),
            pl.BlockSpec((GW, GW), c2),
            pl.BlockSpec((GW, GW), c2),
            pl.BlockSpec((1, GW), c2),
            pl.BlockSpec((1, GW), c2),
        ],
        out_specs=[
            pl.BlockSpec((1, s, GW), lambda i: (i, 0, 0)),
            pl.BlockSpec((1, s, GW), lambda i: (i, 0, 0)),
        ],
        out_shape=[
            jax.ShapeDtypeStruct((b, s, GW), BF16),
            jax.ShapeDtypeStruct((b, s, GW), BF16),
        ],
        scratch_shapes=[
            pltpu.VMEM((s + 16, GW), F32),
            pltpu.VMEM((NH, nm, HD), BF16),
            pltpu.VMEM((NH, nm, HD), BF16),
        ],
        compiler_params=pltpu.CompilerParams(
            dimension_semantics=("arbitrary",), vmem_limit_bytes=VMEM_LIMIT),
    )(f3, p3, p3, p3, mem, mng, wkv, gmq, gmk, bd, wp, ps, win)


def _out_kernel(a_ref, b_ref, c_ref, d_ref, e_ref, w_ref, x_ref, o_ref):
    mixed = jnp.concatenate([a_ref[...], b_ref[...], c_ref[...], d_ref[...], e_ref[...]], axis=1)
    o_ref[...] = x_ref[...] + _dot(mixed, w_ref[...])


def _out(parts, w, x2, tm):
    m = x2.shape[0]
    gspec = pl.BlockSpec((tm, GW), lambda i: (i, 0))
    return pl.pallas_call(
        _out_kernel,
        grid=(m // tm,),
        in_specs=[gspec] * 5 + [
            pl.BlockSpec((5 * GW, D_MODEL), lambda i: (0, 0)),
            pl.BlockSpec((tm, D_MODEL), lambda i: (i, 0)),
        ],
        out_specs=pl.BlockSpec((tm, D_MODEL), lambda i: (i, 0)),
        out_shape=jax.ShapeDtypeStruct((m, D_MODEL), F32),
        compiler_params=pltpu.CompilerParams(
            dimension_semantics=("arbitrary",), vmem_limit_bytes=VMEM_LIMIT),
    )(*parts, w, x2)


def _tile_heads(g):
    return jnp.tile(g.astype(F32), NH).reshape(1, GW)


def kernel(x, mem, norm_g, w_in, fox_f_bias, fox_q_norm, fox_k_norm, hgrn_lb_logits, hgrn_out_norm,
           pool_w, pool_scale, mem_norm_g, mem_w_kv, mem_q_norm, mem_k_norm, w_out):
    bsz, seq, _ = x.shape
    depth = w_in.shape[0]
    m = bsz * seq
    tq = 256
    tsb = 128
    tm = 512

    pr = jax.nn.softmax(hgrn_lb_logits.astype(F32), axis=0)
    lower_bounds = jnp.clip(jnp.cumsum(pr, axis=0) - pr[0:1], 0.0, 1.0 - 1e-6)

    bd = jnp.asarray(_bd_ones(), BF16)
    tri = jnp.asarray(np.tril(np.ones((tq, tq), np.float32)), BF16)
    jj = np.arange(tsb)
    suffix = (jj[:, None] > jj[None, :]).astype(np.float32)
    to_half = np.concatenate([suffix, np.ones((tsb, tsb), np.float32)], axis=1)
    to = jnp.asarray(np.concatenate([to_half, to_half], axis=0), BF16)
    a_np, masks_np, hm_np = _hgrn_constants()
    a_all = jnp.asarray(a_np, BF16)
    masks = jnp.asarray(masks_np, F32)
    hm = jnp.asarray(hm_np, BF16)
    win = jnp.asarray(np.repeat(np.array(POOL_WINDOWS, np.float32), HD).reshape(1, GW))

    g = GW
    o_ff, o_sb, o_hg, o_pl, o_mm = 4 * g, 4 * g + NH, 8 * g + NH, 12 * g + NH, 14 * g + NH

    x2 = x.reshape(m, D_MODEL)
    for l in range(depth):
        w = w_in[l]
        wb = jnp.concatenate([
            w[:, 0:4 * g], w[:, o_sb:o_sb + 4 * g],
            w[:, o_hg:o_hg + g], w[:, o_hg + 2 * g:o_hg + 4 * g],
            w[:, o_pl + g:o_pl + 2 * g],
            w[:, o_mm:o_mm + 2 * g],
        ], axis=1).astype(BF16)
        wf = jnp.concatenate([
            w[:, o_hg + g:o_hg + 2 * g],
            w[:, o_pl:o_pl + g],
            w[:, o_ff:o_ff + NH], jnp.zeros((D_MODEL, 128 - NH), F32),
        ], axis=1).astype(BF16)
        pb, pf = _proj(x2, norm_g[l].reshape(1, D_MODEL).astype(F32), wb, wf, tm)
        p3 = pb.reshape(bsz, seq, NP_COLS)
        f3 = pf.reshape(bsz, seq, NF_COLS)

        bias = jnp.concatenate([fox_f_bias[l].astype(F32), jnp.zeros((128 - NH,), F32)]).reshape(1, 128)
        out_a = _fox(p3, f3, bias, _tile_heads(fox_q_norm[l]), _tile_heads(fox_k_norm[l]), bd, tri, tq)
        out_b = _sb(p3, to, tsb)

        lb = lower_bounds[l].reshape(1, g)
        out_c = _hgrn(p3, f3, jnp.maximum(lb, LB_FLOOR), 1.0 - lb,
                      hgrn_out_norm[l].reshape(1, g).astype(F32), bd, a_all, masks, hm)

        wp = jax.scipy.linalg.block_diag(*[pool_w[l, i] for i in range(len(POOL_WINDOWS))]).astype(BF16)
        out_d, out_e = _pm(p3, f3, mem, mem_norm_g[l].reshape(1, D_MODEL).astype(F32),
                           mem_w_kv[l].astype(BF16), _tile_heads(mem_q_norm[l]), _tile_heads(mem_k_norm[l]),
                           bd, wp, pool_scale[l].reshape(1, g).astype(F32), win, tq)

        x2 = _out([out_a.reshape(m, g), out_b.reshape(m, g), out_c.reshape(m, g),
                   out_d.reshape(m, g), out_e.reshape(m, g)], w_out[l].astype(BF16), x2, tm)
    return x2.reshape(bsz, seq, D_MODEL)
```

```python
import functools

import numpy as np
import jax
import jax.numpy as jnp
from jax import lax
from jax.experimental import pallas as pl
from jax.experimental.pallas import tpu as pltpu

F32 = jnp.float32
BF16 = jnp.bfloat16

D_MODEL = 1024
GW = 256
NH = 4
HD = 64
CHUNK = 64
POOL_WINDOWS = (2, 4, 8, 16)
EPS = 1e-6
NEG_BIG = -1e30
LB_FLOOR = 1e-30
SCALE = HD ** -0.5

NP_COLS = 14 * GW
NF_COLS = 2 * GW + 128
VMEM_LIMIT = 56 * 1024 * 1024

TQ = 128
WIDE = 3 * TQ
EXP_ZERO = -104.0
NORM_SLACK = 1.01
LOGIT_SLACK = 0.05
HSB = 4


def _dot(a, b):
    return jnp.dot(a, b, preferred_element_type=F32)


def _dot_nt(a, b):
    return lax.dot_general(a, b, (((1,), (1,)), ((), ())), preferred_element_type=F32)


def _dot_tn(a, b):
    return lax.dot_general(a, b, (((0,), (0,)), ((), ())), preferred_element_type=F32)


def _split_bf16(x, n):
    parts = []
    r = x
    for i in range(n):
        p = r.astype(BF16)
        parts.append(p)
        if i + 1 < n:
            r = r - p.astype(F32)
    return parts


def _dot_f32_rhs01(x, m01, n=3):
    acc = None
    for p in _split_bf16(x, n):
        t = _dot(p, m01)
        acc = t if acc is None else acc + t
    return acc


def _dot_f32_lhs01(m01, x, n=3):
    acc = None
    for p in _split_bf16(x, n):
        t = _dot(m01, p)
        acc = t if acc is None else acc + t
    return acc


def _sigmoid(x):
    e = jnp.exp(-jnp.abs(x))
    r = 1.0 / (1.0 + e)
    return jnp.where(x >= 0, r, e * r)


def _silu(x):
    return x * _sigmoid(x)


def _log_sigmoid(x):
    return jnp.minimum(x, 0.0) - jnp.log(1.0 + jnp.exp(-jnp.abs(x)))


def _head_rms(x, gain, bd):
    ss = _dot_f32_rhs01(x * x, bd, 2)
    return x * lax.rsqrt(ss * (1.0 / HD) + EPS) * gain


def _bd_ones():
    h = np.arange(GW) // HD
    return (h[:, None] == h[None, :]).astype(np.float32)


def _hgrn_constants():
    c = CHUNK
    t = np.arange(c)
    mats = []
    masks = []
    for n in (64, 32, 16, 8, 4):
        blk, pos = t // n, t % n
        ref = blk * n + n // 2 - 1
        j = np.arange(c)[None, :]
        aq = ((pos[:, None] >= n // 2) & (j > ref[:, None]) & (j <= t[:, None])).astype(np.float32)
        ak = ((pos[:, None] < n // 2) & (j > t[:, None]) & (j <= ref[:, None])).astype(np.float32)
        mats += [aq, ak]
        m = (blk[:, None] == blk[None, :]) & (pos[:, None] >= n // 2) & (pos[None, :] < n // 2)
        masks.append(m.astype(np.float32))
    odd = (t % 2 == 1)
    aq = np.diag(odd.astype(np.float32))
    ak = -np.diag(odd.astype(np.float32))
    mats += [aq, ak]
    m = ((t[:, None] // 2) == (t[None, :] // 2)) & (t[None, :] <= t[:, None])
    masks.append(m.astype(np.float32))
    j = np.arange(c)[None, :]
    mats.append((j <= t[:, None]).astype(np.float32))
    mats.append((j > t[:, None]).astype(np.float32))
    a_all = np.concatenate(mats, axis=0)
    a_all = np.concatenate([a_all, a_all], axis=1)
    masks = np.stack([np.tile(m, (1, NH)) for m in masks])
    total = masks[:, :, :c].sum(0)
    assert np.array_equal(total, np.tril(np.ones((c, c), np.float32)))
    hm = (np.arange(NH * c)[:, None] // c == np.arange(GW)[None, :] // HD).astype(np.float32)
    return a_all, masks, hm


def _regroup_kernel(w_ref, wb_ref, wf_ref):
    g = GW
    o_ff, o_sb, o_hg, o_pl, o_mm = 4 * g, 4 * g + NH, 8 * g + NH, 12 * g + NH, 14 * g + NH
    w = w_ref[0]
    bf_src = [0, g, 2 * g, 3 * g,
              o_sb, o_sb + g, o_sb + 2 * g, o_sb + 3 * g,
              o_hg, o_hg + 2 * g, o_hg + 3 * g,
              o_pl + g,
              o_mm, o_mm + g]
    for i, c0 in enumerate(bf_src):
        wb_ref[0, :, i * g:(i + 1) * g] = w[:, c0:c0 + g].astype(BF16)
    wf_ref[0, :, 0:g] = w[:, o_hg + g:o_hg + 2 * g].astype(BF16)
    wf_ref[0, :, g:2 * g] = w[:, o_pl:o_pl + g].astype(BF16)
    ff = jnp.concatenate([w[:, o_ff:o_ff + NH], jnp.zeros((w.shape[0], 128 - NH), F32)], axis=1)
    wf_ref[0, :, 2 * g:2 * g + 128] = ff.astype(BF16)


def _regroup_w_in(w_in):
    depth, d, n = w_in.shape
    tr = 128
    return pl.pallas_call(
        _regroup_kernel,
        grid=(depth, d // tr),
        in_specs=[pl.BlockSpec((1, tr, n), lambda l, i: (l, i, 0))],
        out_specs=[
            pl.BlockSpec((1, tr, NP_COLS), lambda l, i: (l, i, 0)),
            pl.BlockSpec((1, tr, NF_COLS), lambda l, i: (l, i, 0)),
        ],
        out_shape=[
            jax.ShapeDtypeStruct((depth, d, NP_COLS), BF16),
            jax.ShapeDtypeStruct((depth, d, NF_COLS), BF16),
        ],
        compiler_params=pltpu.CompilerParams(
            dimension_semantics=("arbitrary", "arbitrary"), vmem_limit_bytes=VMEM_LIMIT),
    )(w_in)


def _proj_kernel(x_ref, g_ref, wb_ref, wf_ref, pb_ref, pf_ref):
    x = x_ref[...]
    ms = jnp.mean(x * x, axis=-1, keepdims=True)
    h = (x * lax.rsqrt(ms + EPS) * g_ref[...]).astype(BF16)
    nb = 512
    for c0 in range(0, NP_COLS, nb):
        pb_ref[:, c0:c0 + nb] = _dot(h, wb_ref[:, c0:c0 + nb]).astype(BF16)
    pf_ref[...] = _dot(h, wf_ref[...])


def _proj(x2, g, wb, wf, tm):
    m = x2.shape[0]
    return pl.pallas_call(
        _proj_kernel,
        grid=(m // tm,),
        in_specs=[
            pl.BlockSpec((tm, D_MODEL), lambda i: (i, 0)),
            pl.BlockSpec((1, D_MODEL), lambda i: (0, 0)),
            pl.BlockSpec((D_MODEL, NP_COLS), lambda i: (0, 0)),
            pl.BlockSpec((D_MODEL, NF_COLS), lambda i: (0, 0)),
        ],
        out_specs=[
            pl.BlockSpec((tm, NP_COLS), lambda i: (i, 0)),
            pl.BlockSpec((tm, NF_COLS), lambda i: (i, 0)),
        ],
        out_shape=[
            jax.ShapeDtypeStruct((m, NP_COLS), BF16),
            jax.ShapeDtypeStruct((m, NF_COLS), F32),
        ],
        compiler_params=pltpu.CompilerParams(
            dimension_semantics=("arbitrary",), vmem_limit_bytes=VMEM_LIMIT),
    )(x2, g, wb, wf)


def _fox_kernel(p_ref, ff_ref, bias_ref, gq_ref, gk_ref, bd_ref, hsel_ref, tri_ref, o_ref,
                qh, kh, vh, ccol, crow, qnorm, oscr, *, seq):
    bd = bd_ref[...]
    tri = tri_ref[...]
    hsel = hsel_ref[...]
    pb = tri.shape[0]
    carry = jnp.zeros((1, 128), F32)
    kmax2 = jnp.zeros((1, 128), F32)
    for b in range(seq // pb):
        r = slice(b * pb, (b + 1) * pb)
        lf = _log_sigmoid(ff_ref[0, r, :] + bias_ref[...])
        cb = _dot_f32_lhs01(tri, lf, 3) + carry
        carry = cb[pb - 1:pb, :]
        ccol[r, :] = cb
        crow[:, r] = cb.T[:8, :]
        q = _head_rms(p_ref[0, r, 0:GW].astype(F32), gq_ref[...], bd) * SCALE
        k = _head_rms(p_ref[0, r, GW:2 * GW].astype(F32), gk_ref[...], bd)
        qb = q.astype(BF16)
        kb = k.astype(BF16)
        v = p_ref[0, r, 2 * GW:3 * GW]
        for h in range(NH):
            ls = slice(h * HD, (h + 1) * HD)
            qh[h, r, :] = qb[:, ls]
            kh[h, r, :] = kb[:, ls]
            vh[h, r, :] = v[:, ls]
        qf = qb.astype(F32)
        kf = kb.astype(F32)
        qnorm[r, :] = jnp.sqrt(_dot_f32_rhs01(qf * qf, hsel, 2))
        kmax2 = jnp.maximum(kmax2, jnp.max(_dot_f32_rhs01(kf * kf, hsel, 2), axis=0, keepdims=True))
    kmax = jnp.sqrt(kmax2) * NORM_SLACK

    lane = lax.broadcasted_iota(jnp.int32, (1, 128), 1)
    row_w = lax.broadcasted_iota(jnp.int32, (TQ, WIDE), 0)
    col_w = lax.broadcasted_iota(jnp.int32, (TQ, WIDE), 1)

    def q_block(i, _):
        r0 = pl.multiple_of(i * TQ, TQ)
        s0 = pl.multiple_of(jnp.maximum(r0 - (WIDE - TQ), 0), TQ)
        rows = pl.ds(r0, TQ)
        causal = (col_w + s0) <= (row_w + r0)
        cq = ccol[rows, :]
        bq = qnorm[rows, :] * kmax + cq + LOGIT_SLACK

        def tile(h, start, width):
            s = _dot_nt(qh[h, rows, :], kh[h, pl.ds(start, width), :])
            return s + (cq[:, h:h + 1] - crow[h:h + 1, pl.ds(start, width)])

        def keep_going(s_end, ms):
            mc = jnp.zeros((TQ, 128), F32)
            for h in range(NH):
                mc = jnp.where(lane == h, ms[h], mc)
            c_last = ccol[pl.ds(jnp.maximum(s_end - 1, 0), 1), :]
            bound = jnp.max(bq - mc, axis=0, keepdims=True) - c_last
            bound = jnp.where(lane < NH, bound, NEG_BIG)
            return (jnp.max(bound) > EXP_ZERO).astype(jnp.int32)

        ms, ls, accs = [], [], []
        for h in range(NH):
            s = jnp.where(causal, tile(h, s0, WIDE), NEG_BIG)
            m = jnp.max(s, axis=-1, keepdims=True)
            p = jnp.exp(s - m)
            ms.append(m)
            ls.append(jnp.sum(p, axis=-1, keepdims=True))
            accs.append(_dot(p.astype(BF16), vh[h, pl.ds(s0, WIDE), :]))

        def cond(st):
            return jnp.logical_and(st[0] > 0, st[1] > 0)

        def body(st):
            s_end, _, ms, ls, accs = st
            sb = pl.multiple_of(s_end - TQ, TQ)
            ms2, ls2, accs2 = [], [], []
            for h in range(NH):
                s = tile(h, sb, TQ)
                m_new = jnp.maximum(ms[h], jnp.max(s, axis=-1, keepdims=True))
                p = jnp.exp(s - m_new)
                alpha = jnp.exp(ms[h] - m_new)
                ms2.append(m_new)
                ls2.append(alpha * ls[h] + jnp.sum(p, axis=-1, keepdims=True))
                accs2.append(alpha * accs[h] + _dot(p.astype(BF16), vh[h, pl.ds(sb, TQ), :]))
            return sb, keep_going(sb, ms2), tuple(ms2), tuple(ls2), tuple(accs2)

        st = lax.while_loop(cond, body, (s0, keep_going(s0, ms), tuple(ms), tuple(ls), tuple(accs)))
        _, _, ms, ls, accs = st
        for h in range(NH):
            oscr[rows, h * HD:(h + 1) * HD] = accs[h] / ls[h]
        return 0

    lax.fori_loop(0, seq // TQ, q_block, 0)

    for b in range(seq // pb):
        r = slice(b * pb, (b + 1) * pb)
        g = p_ref[0, r, 3 * GW:4 * GW].astype(F32)
        o_ref[0, r, :] = (oscr[r, :] * _silu(g)).astype(BF16)


def _fox(p3, f3, bias, gq, gk, bd, hsel, tri):
    b, s, _ = p3.shape
    assert s % tri.shape[0] == 0 and s >= WIDE
    kern = functools.partial(_fox_kernel, seq=s)
    c2 = lambda i: (0, 0)
    return pl.pallas_call(
        kern,
        grid=(b,),
        in_specs=[
            pl.BlockSpec((1, s, 4 * GW), lambda i: (i, 0, 0)),
            pl.BlockSpec((1, s, 128), lambda i: (i, 0, 4)),
            pl.BlockSpec((1, 128), c2),
            pl.BlockSpec((1, GW), c2),
            pl.BlockSpec((1, GW), c2),
            pl.BlockSpec((GW, GW), c2),
            pl.BlockSpec((GW, 128), c2),
            pl.BlockSpec(tri.shape, c2),
        ],
        out_specs=pl.BlockSpec((1, s, GW), lambda i: (i, 0, 0)),
        out_shape=jax.ShapeDtypeStruct((b, s, GW), BF16),
        scratch_shapes=[
            pltpu.VMEM((NH, s, HD), BF16),
            pltpu.VMEM((NH, s, HD), BF16),
            pltpu.VMEM((NH, s, HD), BF16),
            pltpu.VMEM((s, 128), F32),
            pltpu.VMEM((8, s), F32),
            pltpu.VMEM((s, 128), F32),
            pltpu.VMEM((s, GW), F32),
        ],
        compiler_params=pltpu.CompilerParams(
            dimension_semantics=("arbitrary",), vmem_limit_bytes=VMEM_LIMIT),
    )(p3, f3, bias, gq, gk, bd, hsel, tri)


def _sb_kernel(p_ref, to_ref, o_ref, qh, kh, vh, oscr, *, seq):
    to = to_ref[...]
    pb = 256
    for b in range(seq // pb):
        r = slice(b * pb, (b + 1) * pb)
        qb = (p_ref[0, r, 0:GW].astype(F32) * SCALE).astype(BF16)
        kb = p_ref[0, r, GW:2 * GW]
        v = p_ref[0, r, 2 * GW:3 * GW]
        for h in range(NH):
            ls = slice(h * HD, (h + 1) * HD)
            qh[h, r, :] = qb[:, ls]
            kh[h, r, :] = kb[:, ls]
            vh[h, r, :] = v[:, ls]

    row_w = lax.broadcasted_iota(jnp.int32, (TQ, WIDE), 0)
    col_w = lax.broadcasted_iota(jnp.int32, (TQ, WIDE), 1)
    nsub = WIDE // TQ

    def log_one_minus_sigmoid(z):
        return -(jnp.maximum(z, 0.0) + jnp.log(1.0 + jnp.exp(-jnp.abs(z))))

    def suffix_sums(lom):
        hi = lom.astype(BF16)
        lo = (lom - hi.astype(F32)).astype(BF16)
        cs = _dot(jnp.concatenate([hi, lo], axis=1), to)
        return cs[:, :TQ], cs[:, TQ:]

    def q_block(i, _):
        r0 = pl.multiple_of(i * TQ, TQ)
        s0 = pl.multiple_of(jnp.maximum(r0 - (WIDE - TQ), 0), TQ)
        rows = pl.ds(r0, TQ)
        strict = (col_w + s0) < (row_w + r0)

        carries, accs = [], []
        for h in range(NH):
            z = _dot_nt(qh[h, rows, :], kh[h, pl.ds(s0, WIDE), :])
            lom = jnp.where(strict, log_one_minus_sigmoid(z), 0.0)
            between = [None] * nsub
            carry = jnp.zeros((TQ, TQ), F32)
            for c in reversed(range(nsub)):
                rc, tot = suffix_sums(lom[:, c * TQ:(c + 1) * TQ])
                between[c] = rc + carry
                carry = carry + tot
            w = jnp.exp(z + lom + jnp.concatenate(between, axis=1))
            w = jnp.where(strict, w, 0.0)
            accs.append(_dot(w.astype(BF16), vh[h, pl.ds(s0, WIDE), :]))
            carries.append(carry)

        def keep_going(carries):
            cm = jnp.maximum(jnp.maximum(carries[0], carries[1]), jnp.maximum(carries[2], carries[3]))
            return (jnp.max(cm) > EXP_ZERO).astype(jnp.int32)

        def cond(st):
            return jnp.logical_and(st[0] > 0, st[1] > 0)

        def body(st):
            s_end, _, carries, accs = st
            sb = pl.multiple_of(s_end - TQ, TQ)
            c2, a2 = [], []
            for h in range(NH):
                z = _dot_nt(qh[h, rows, :], kh[h, pl.ds(sb, TQ), :])
                lom = log_one_minus_sigmoid(z)
                rc, tot = suffix_sums(lom)
                w = jnp.exp(z + lom + rc + carries[h])
                a2.append(accs[h] + _dot(w.astype(BF16), vh[h, pl.ds(sb, TQ), :]))
                c2.append(carries[h] + tot)
            return sb, keep_going(c2), tuple(c2), tuple(a2)

        st = lax.while_loop(cond, body, (s0, keep_going(carries), tuple(carries), tuple(accs)))
        accs = st[3]
        for h in range(NH):
            oscr[rows, h * HD:(h + 1) * HD] = accs[h]
        return 0

    lax.fori_loop(0, seq // TQ, q_block, 0)

    for b in range(seq // pb):
        r = slice(b * pb, (b + 1) * pb)
        g = p_ref[0, r, 3 * GW:4 * GW].astype(F32)
        o_ref[0, r, :] = (oscr[r, :] * _silu(g)).astype(BF16)


def _sb(p3, to):
    b, s, _ = p3.shape
    assert s % 256 == 0 and s >= WIDE
    kern = functools.partial(_sb_kernel, seq=s)
    return pl.pallas_call(
        kern,
        grid=(b,),
        in_specs=[
            pl.BlockSpec((1, s, 4 * GW), lambda i: (i, 0, 1)),
            pl.BlockSpec((2 * TQ, 2 * TQ), lambda i: (0, 0)),
        ],
        out_specs=pl.BlockSpec((1, s, GW), lambda i: (i, 0, 0)),
        out_shape=jax.ShapeDtypeStruct((b, s, GW), BF16),
        scratch_shapes=[
            pltpu.VMEM((NH, s, HD), BF16),
            pltpu.VMEM((NH, s, HD), BF16),
            pltpu.VMEM((NH, s, HD), BF16),
            pltpu.VMEM((s, GW), F32),
        ],
        compiler_params=pltpu.CompilerParams(
            dimension_semantics=("arbitrary",), vmem_limit_bytes=VMEM_LIMIT),
    )(p3, to)


def _hgrn_kernel(hq_ref, hi_ref, hg_ref, hf_ref, lbm_ref, oml_ref, gout_ref, bd_ref, a_ref,
                 mask_ref, hm_ref, o_ref, st_ref, *, seq):
    bd = bd_ref[...]
    a_all = a_ref[...]
    hm = hm_ref[...]
    lbm = lbm_ref[...]
    oml = oml_ref[...]
    c = CHUNK
    st_ref[...] = jnp.zeros((GW, GW), F32)

    n = HSB * c
    bdf = bd.astype(F32)

    def superblock(bi, _):
        r0 = pl.multiple_of(bi * n, n)
        rows = pl.ds(r0, n)
        hf = hf_ref[0, rows, :]
        e = jnp.exp(-jnp.abs(hf))
        rr = 1.0 / (1.0 + e)
        sg = jnp.where(hf >= 0, rr, e * rr)
        sgn = jnp.where(hf >= 0, e * rr, rr)
        g = jnp.log(lbm + oml * sg)
        kk = oml * sgn
        q = _silu(hq_ref[0, rows, :].astype(F32))
        v = hi_ref[0, rows, :]
        gh = g.astype(BF16)
        gl = (g - gh.astype(F32)).astype(BF16)

        o_intra, qd, upd, dl = [], [], [], []
        for ci in range(HSB):
            sl = slice(ci * c, (ci + 1) * c)
            ex = _dot(a_all, jnp.concatenate([gh[sl], gl[sl]], axis=0))
            qc, kc, vc = q[sl], kk[sl], v[sl]
            vx = jnp.concatenate([vc] * NH, axis=0) * hm
            p = jnp.zeros((c, GW), F32)
            for l in range(6):
                eq = ex[(2 * l) * c:(2 * l + 1) * c]
                ek = ex[(2 * l + 1) * c:(2 * l + 2) * c]
                qf = (qc * jnp.exp(eq)).astype(BF16)
                kf = (kc * jnp.exp(ek)).astype(BF16)
                kx = jnp.concatenate([kf] * NH, axis=0) * hm
                p = p + _dot_nt(qf, kx) * mask_ref[l]
            o_intra.append(_dot(p.astype(BF16), vx))
            eb = ex[12 * c:13 * c]
            er = ex[13 * c:14 * c]
            qd.append((qc * jnp.exp(eb)).astype(BF16))
            upd.append(_dot_tn(vc, (kc * jnp.exp(er)).astype(BF16)) * bdf)
            dl.append(jnp.exp(eb[c - 1:c, :]))

        st = st_ref[...]
        outs = []
        for ci in range(HSB):
            outs.append(o_intra[ci] + _dot_nt(qd[ci], st.astype(BF16)))
            st = st * dl[ci] + upd[ci]
        st_ref[...] = st

        o = _head_rms(jnp.concatenate(outs, axis=0), gout_ref[...], bd)
        o_ref[0, rows, :] = (o * _silu(hg_ref[0, rows, :].astype(F32))).astype(BF16)
        return 0

    lax.fori_loop(0, seq // n, superblock, 0)


def _hgrn(p3, f3, lbm, oml, gout, bd, a_all, masks, hm):
    b, s, _ = p3.shape
    assert s % (HSB * CHUNK) == 0
    kern = functools.partial(_hgrn_kernel, seq=s)
    c2 = lambda i: (0, 0)
    return pl.pallas_call(
        kern,
        grid=(b,),
        in_specs=[
            pl.BlockSpec((1, s, GW), lambda i: (i, 0, 8)),
            pl.BlockSpec((1, s, GW), lambda i: (i, 0, 9)),
            pl.BlockSpec((1, s, GW), lambda i: (i, 0, 10)),
            pl.BlockSpec((1, s, GW), lambda i: (i, 0, 0)),
            pl.BlockSpec((1, GW), c2),
            pl.BlockSpec((1, GW), c2),
            pl.BlockSpec((1, GW), c2),
            pl.BlockSpec((GW, GW), c2),
            pl.BlockSpec(a_all.shape, c2),
            pl.BlockSpec(masks.shape, lambda i: (0, 0, 0)),
            pl.BlockSpec(hm.shape, c2),
        ],
        out_specs=pl.BlockSpec((1, s, GW), lambda i: (i, 0, 0)),
        out_shape=jax.ShapeDtypeStruct((b, s, GW), BF16),
        scratch_shapes=[pltpu.VMEM((GW, GW), F32)],
        compiler_params=pltpu.CompilerParams(
            dimension_semantics=("arbitrary",), vmem_limit_bytes=VMEM_LIMIT),
    )(p3, p3, p3, f3, lbm, oml, gout, bd, a_all, masks, hm)


def _pm_kernel(pv_ref, pg_ref, mq_ref, mg_ref, mem_ref, mng_ref, wkv_ref, gmq_ref, gmk_ref, bd_ref,
               wp_ref, ps_ref, win_ref, d_ref, e_ref, ubuf, kh, vh, *, seq, tq):
    bd = bd_ref[...]
    halo = 16
    mem = mem_ref[0]
    ms = jnp.mean(mem * mem, axis=-1, keepdims=True)
    mn = (mem * lax.rsqrt(ms + EPS) * mng_ref[...]).astype(BF16)
    kv = _dot(mn, wkv_ref[...])
    kn = _head_rms(kv[:, :GW], gmk_ref[...], bd).astype(BF16)
    vv = kv[:, GW:].astype(BF16)
    for h in range(NH):
        kh[h] = kn[:, h * HD:(h + 1) * HD]
        vh[h] = vv[:, h * HD:(h + 1) * HD]

    ubuf[0:halo, :] = jnp.zeros((halo, GW), F32)
    ubuf[halo:halo + seq, :] = pv_ref[0]
    win = win_ref[...]

    for b in range(seq // tq):
        r = slice(b * tq, (b + 1) * tq)
        base = halo + b * tq

        def ld(j):
            return ubuf[base - j:base - j + tq, :]

        u = ld(0)
        s2 = u + ld(1)
        s4 = s2 + ld(2) + ld(3)
        s8 = s4 + ld(4) + ld(5) + ld(6) + ld(7)
        s16 = s8
        for j in range(8, 16):
            s16 = s16 + ld(j)
        sw = jnp.where(win == 2.0, s2, jnp.where(win == 4.0, s4, jnp.where(win == 8.0, s8, s16)))
        pos = (lax.broadcasted_iota(jnp.int32, (tq, GW), 0) + (b * tq + 1)).astype(F32)
        pooled = sw / jnp.minimum(pos, win)
        y = _dot((pooled - u).astype(BF16), wp_ref[...]) * ps_ref[...]
        d_ref[0, r, :] = (y * _silu(pg_ref[0, r, :].astype(F32))).astype(BF16)

        qn = (_head_rms(mq_ref[0, r, :].astype(F32), gmq_ref[...], bd) * SCALE).astype(BF16)
        outs = []
        for h in range(NH):
            s = _dot_nt(qn[:, h * HD:(h + 1) * HD], kh[h])
            m = jnp.max(s, axis=-1, keepdims=True)
            p = jnp.exp(s - m)
            l = jnp.sum(p, axis=-1, keepdims=True)
            outs.append(_dot(p.astype(BF16), vh[h]) / l)
        oe = jnp.concatenate(outs, axis=1)
        e_ref[0, r, :] = (oe * _silu(mg_ref[0, r, :].astype(F32))).astype(BF16)


def _pm(p3, f3, mem, mng, wkv, gmq, gmk, bd, wp, ps, win, tq):
    b, s, _ = p3.shape
    nm = mem.shape[1]
    kern = functools.partial(_pm_kernel, seq=s, tq=tq)
    c2 = lambda i: (0, 0)
    return pl.pallas_call(
        kern,
        grid=(b,),
        in_specs=[
            pl.BlockSpec((1, s, GW), lambda i: (i, 0, 1)),
            pl.BlockSpec((1, s, GW), lambda i: (i, 0, 11)),
            pl.BlockSpec((1, s, GW), lambda i: (i, 0, 12)),
            pl.BlockSpec((1, s, GW), lambda i: (i, 0, 13)),
            pl.BlockSpec((1, nm, D_MODEL), lambda i: (i, 0, 0)),
            pl.BlockSpec((1, D_MODEL), c2),
            pl.BlockSpec((D_MODEL, 2 * GW), c2),
            pl.BlockSpec((1, GW), c2),
            pl.BlockSpec((1, GW), c2),
            pl.BlockSpec((GW, GW), c2),
            pl.BlockSpec((GW, GW), c2),
            pl.BlockSpec((1, GW), c2),
            pl.BlockSpec((1, GW), c2),
        ],
        out_specs=[
            pl.BlockSpec((1, s, GW), lambda i: (i, 0, 0)),
            pl.BlockSpec((1, s, GW), lambda i: (i, 0, 0)),
        ],
        out_shape=[
            jax.ShapeDtypeStruct((b, s, GW), BF16),
            jax.ShapeDtypeStruct((b, s, GW), BF16),
        ],
        scratch_shapes=[
            pltpu.VMEM((s + 16, GW), F32),
            pltpu.VMEM((NH, nm, HD), BF16),
            pltpu.VMEM((NH, nm, HD), BF16),
        ],
        compiler_params=pltpu.CompilerParams(
            dimension_semantics=("arbitrary",), vmem_limit_bytes=VMEM_LIMIT),
    )(f3, p3, p3, p3, mem, mng, wkv, gmq, gmk, bd, wp, ps, win)


def _out_kernel(a_ref, b_ref, c_ref, d_ref, e_ref, w_ref, x_ref, o_ref):
    mixed = jnp.concatenate([a_ref[...], b_ref[...], c_ref[...], d_ref[...], e_ref[...]], axis=1)
    o_ref[...] = x_ref[...] + _dot(mixed, w_ref[...])


def _out(parts, w, x2, tm):
    m = x2.shape[0]
    gspec = pl.BlockSpec((tm, GW), lambda i: (i, 0))
    return pl.pallas_call(
        _out_kernel,
        grid=(m // tm,),
        in_specs=[gspec] * 5 + [
            pl.BlockSpec((5 * GW, D_MODEL), lambda i: (0, 0)),
            pl.BlockSpec((tm, D_MODEL), lambda i: (i, 0)),
        ],
        out_specs=pl.BlockSpec((tm, D_MODEL), lambda i: (i, 0)),
        out_shape=jax.ShapeDtypeStruct((m, D_MODEL), F32),
        compiler_params=pltpu.CompilerParams(
            dimension_semantics=("arbitrary",), vmem_limit_bytes=VMEM_LIMIT),
    )(*parts, w, x2)


def _tile_heads(g):
    return jnp.tile(g.astype(F32), NH).reshape(1, GW)


def kernel(x, mem, norm_g, w_in, fox_f_bias, fox_q_norm, fox_k_norm, hgrn_lb_logits, hgrn_out_norm,
           pool_w, pool_scale, mem_norm_g, mem_w_kv, mem_q_norm, mem_k_norm, w_out):
    bsz, seq, _ = x.shape
    depth = w_in.shape[0]
    m = bsz * seq
    tq = 256
    tm = 512

    pr = jax.nn.softmax(hgrn_lb_logits.astype(F32), axis=0)
    lower_bounds = jnp.clip(jnp.cumsum(pr, axis=0) - pr[0:1], 0.0, 1.0 - 1e-6)

    bd_np = _bd_ones()
    bd = jnp.asarray(bd_np, BF16)
    hsel = jnp.asarray(np.pad(bd_np[:, ::HD], ((0, 0), (0, 128 - NH))), BF16)
    tri = jnp.asarray(np.tril(np.ones((tq, tq), np.float32)), BF16)
    jj = np.arange(TQ)
    suffix = (jj[:, None] > jj[None, :]).astype(np.float32)
    to_half = np.concatenate([suffix, np.ones((TQ, TQ), np.float32)], axis=1)
    to = jnp.asarray(np.concatenate([to_half, to_half], axis=0), BF16)
    a_np, masks_np, hm_np = _hgrn_constants()
    a_all = jnp.asarray(a_np, BF16)
    masks = jnp.asarray(masks_np, F32)
    hm = jnp.asarray(hm_np, BF16)
    win = jnp.asarray(np.repeat(np.array(POOL_WINDOWS, np.float32), HD).reshape(1, GW))

    g = GW
    wb_all, wf_all = _regroup_w_in(w_in)
    x2 = x.reshape(m, D_MODEL)
    for l in range(depth):
        pb, pf = _proj(x2, norm_g[l].reshape(1, D_MODEL).astype(F32), wb_all[l], wf_all[l], tm)
        p3 = pb.reshape(bsz, seq, NP_COLS)
        f3 = pf.reshape(bsz, seq, NF_COLS)

        bias = jnp.concatenate([fox_f_bias[l].astype(F32), jnp.zeros((128 - NH,), F32)]).reshape(1, 128)
        out_a = _fox(p3, f3, bias, _tile_heads(fox_q_norm[l]), _tile_heads(fox_k_norm[l]), bd, hsel, tri)
        out_b = _sb(p3, to)

        lb = lower_bounds[l].reshape(1, g)
        out_c = _hgrn(p3, f3, jnp.maximum(lb, LB_FLOOR), 1.0 - lb,
                      hgrn_out_norm[l].reshape(1, g).astype(F32), bd, a_all, masks, hm)

        wp = jax.scipy.linalg.block_diag(*[pool_w[l, i] for i in range(len(POOL_WINDOWS))]).astype(BF16)
        out_d, out_e = _pm(p3, f3, mem, mem_norm_g[l].reshape(1, D_MODEL).astype(F32),
                           mem_w_kv[l].astype(BF16), _tile_heads(mem_q_norm[l]), _tile_heads(mem_k_norm[l]),
                           bd, wp, pool_scale[l].reshape(1, g).astype(F32), win, tq)

        x2 = _out([out_a.reshape(m, g), out_b.reshape(m, g), out_c.reshape(m, g),
                   out_d.reshape(m, g), out_e.reshape(m, g)], w_out[l].astype(BF16), x2, tm)
    return x2.reshape(bsz, seq, D_MODEL)
```

```python
import functools

import numpy as np
import jax
import jax.numpy as jnp
from jax import lax
from jax.experimental import pallas as pl
from jax.experimental.pallas import tpu as pltpu

F32 = jnp.float32
BF16 = jnp.bfloat16

D_MODEL = 1024
GW = 256
NH = 4
HD = 64
CHUNK = 64
POOL_WINDOWS = (2, 4, 8, 16)
EPS = 1e-6
NEG_BIG = -1e30
LB_FLOOR = 1e-30
SCALE = HD ** -0.5

NP_COLS = 14 * GW
NF_COLS = 2 * GW + 128
VMEM_LIMIT = 56 * 1024 * 1024

TQ = 128
WIDE = 3 * TQ
EXP_ZERO = -104.0
NORM_SLACK = 1.01
LOGIT_SLACK = 0.05
HSB = 4


def _dot(a, b):
    return jnp.dot(a, b, preferred_element_type=F32)


def _dot_nt(a, b):
    return lax.dot_general(a, b, (((1,), (1,)), ((), ())), preferred_element_type=F32)


def _dot_tn(a, b):
    return lax.dot_general(a, b, (((0,), (0,)), ((), ())), preferred_element_type=F32)


def _split_bf16(x, n):
    parts = []
    r = x
    for i in range(n):
        p = r.astype(BF16)
        parts.append(p)
        if i + 1 < n:
            r = r - p.astype(F32)
    return parts


def _dot_f32_rhs01(x, m01, n=3):
    acc = None
    for p in _split_bf16(x, n):
        t = _dot(p, m01)
        acc = t if acc is None else acc + t
    return acc


def _dot_f32_lhs01(m01, x, n=3):
    acc = None
    for p in _split_bf16(x, n):
        t = _dot(m01, p)
        acc = t if acc is None else acc + t
    return acc


def _sigmoid(x):
    e = jnp.exp(-jnp.abs(x))
    r = 1.0 / (1.0 + e)
    return jnp.where(x >= 0, r, e * r)


def _silu(x):
    return x * _sigmoid(x)


def _log_sigmoid(x):
    return jnp.minimum(x, 0.0) - jnp.log(1.0 + jnp.exp(-jnp.abs(x)))


def _head_rms(x, gain, bd):
    ss = _dot_f32_rhs01(x * x, bd, 2)
    return x * lax.rsqrt(ss * (1.0 / HD) + EPS) * gain


def _bd_ones():
    h = np.arange(GW) // HD
    return (h[:, None] == h[None, :]).astype(np.float32)


def _hgrn_constants():
    c = CHUNK
    t = np.arange(c)
    mats = []
    masks = []
    for n in (64, 32, 16, 8, 4):
        blk, pos = t // n, t % n
        ref = blk * n + n // 2 - 1
        j = np.arange(c)[None, :]
        aq = ((pos[:, None] >= n // 2) & (j > ref[:, None]) & (j <= t[:, None])).astype(np.float32)
        ak = ((pos[:, None] < n // 2) & (j > t[:, None]) & (j <= ref[:, None])).astype(np.float32)
        mats += [aq, ak]
        m = (blk[:, None] == blk[None, :]) & (pos[:, None] >= n // 2) & (pos[None, :] < n // 2)
        masks.append(m.astype(np.float32))
    odd = (t % 2 == 1)
    aq = np.diag(odd.astype(np.float32))
    ak = -np.diag(odd.astype(np.float32))
    mats += [aq, ak]
    m = ((t[:, None] // 2) == (t[None, :] // 2)) & (t[None, :] <= t[:, None])
    masks.append(m.astype(np.float32))
    j = np.arange(c)[None, :]
    mats.append((j <= t[:, None]).astype(np.float32))
    mats.append((j > t[:, None]).astype(np.float32))
    a_all = np.concatenate(mats, axis=0)
    a_all = np.concatenate([a_all, a_all], axis=1)
    masks = np.stack([np.tile(m, (1, NH)) for m in masks])
    total = masks[:, :, :c].sum(0)
    assert np.array_equal(total, np.tril(np.ones((c, c), np.float32)))
    hm = (np.arange(NH * c)[:, None] // c == np.arange(GW)[None, :] // HD).astype(np.float32)
    return a_all, masks, hm


def _regroup_kernel(w_ref, wb_ref, wf_ref):
    g = GW
    o_ff, o_sb, o_hg, o_pl, o_mm = 4 * g, 4 * g + NH, 8 * g + NH, 12 * g + NH, 14 * g + NH
    w = w_ref[0]
    bf_src = [0, g, 2 * g, 3 * g,
              o_sb, o_sb + g, o_sb + 2 * g, o_sb + 3 * g,
              o_hg, o_hg + 2 * g, o_hg + 3 * g,
              o_pl + g,
              o_mm, o_mm + g]
    for i, c0 in enumerate(bf_src):
        wb_ref[0, :, i * g:(i + 1) * g] = w[:, c0:c0 + g].astype(BF16)
    wf_ref[0, :, 0:g] = w[:, o_hg + g:o_hg + 2 * g].astype(BF16)
    wf_ref[0, :, g:2 * g] = w[:, o_pl:o_pl + g].astype(BF16)
    ff = jnp.concatenate([w[:, o_ff:o_ff + NH], jnp.zeros((w.shape[0], 128 - NH), F32)], axis=1)
    wf_ref[0, :, 2 * g:2 * g + 128] = ff.astype(BF16)


def _regroup_w_in(w_in):
    depth, d, n = w_in.shape
    tr = 128
    return pl.pallas_call(
        _regroup_kernel,
        grid=(depth, d // tr),
        in_specs=[pl.BlockSpec((1, tr, n), lambda l, i: (l, i, 0))],
        out_specs=[
            pl.BlockSpec((1, tr, NP_COLS), lambda l, i: (l, i, 0)),
            pl.BlockSpec((1, tr, NF_COLS), lambda l, i: (l, i, 0)),
        ],
        out_shape=[
            jax.ShapeDtypeStruct((depth, d, NP_COLS), BF16),
            jax.ShapeDtypeStruct((depth, d, NF_COLS), BF16),
        ],
        compiler_params=pltpu.CompilerParams(
            dimension_semantics=("arbitrary", "arbitrary"), vmem_limit_bytes=VMEM_LIMIT),
    )(w_in)


def _proj_kernel(x_ref, g_ref, wb_ref, wf_ref, pb_ref, pf_ref):
    x = x_ref[...]
    ms = jnp.mean(x * x, axis=-1, keepdims=True)
    h = (x * lax.rsqrt(ms + EPS) * g_ref[...]).astype(BF16)
    nb = 512
    for c0 in range(0, NP_COLS, nb):
        pb_ref[:, c0:c0 + nb] = _dot(h, wb_ref[:, c0:c0 + nb]).astype(BF16)
    pf_ref[...] = _dot(h, wf_ref[...])


def _proj(x2, g, wb, wf, tm):
    m = x2.shape[0]
    return pl.pallas_call(
        _proj_kernel,
        grid=(m // tm,),
        in_specs=[
            pl.BlockSpec((tm, D_MODEL), lambda i: (i, 0)),
            pl.BlockSpec((1, D_MODEL), lambda i: (0, 0)),
            pl.BlockSpec((D_MODEL, NP_COLS), lambda i: (0, 0)),
            pl.BlockSpec((D_MODEL, NF_COLS), lambda i: (0, 0)),
        ],
        out_specs=[
            pl.BlockSpec((tm, NP_COLS), lambda i: (i, 0)),
            pl.BlockSpec((tm, NF_COLS), lambda i: (i, 0)),
        ],
        out_shape=[
            jax.ShapeDtypeStruct((m, NP_COLS), BF16),
            jax.ShapeDtypeStruct((m, NF_COLS), F32),
        ],
        compiler_params=pltpu.CompilerParams(
            dimension_semantics=("arbitrary",), vmem_limit_bytes=VMEM_LIMIT),
    )(x2, g, wb, wf)


def _fox_kernel(p_ref, ff_ref, bias_ref, gq_ref, gk_ref, bd_ref, tri_ref, o_ref,
                qh, kh, vh, ccol, crow, oscr, *, seq):
    bd = bd_ref[...]
    tri = tri_ref[...]
    pb = tri.shape[0]
    qk_max = (HD * SCALE * NORM_SLACK) * (jnp.max(jnp.abs(gq_ref[...]), axis=-1, keepdims=True)
                                          * jnp.max(jnp.abs(gk_ref[...]), axis=-1, keepdims=True))
    carry = jnp.zeros((1, 128), F32)
    for b in range(seq // pb):
        r = slice(b * pb, (b + 1) * pb)
        lf = _log_sigmoid(ff_ref[0, r, :] + bias_ref[...])
        cb = _dot_f32_lhs01(tri, lf, 3) + carry
        carry = cb[pb - 1:pb, :]
        ccol[r, :] = cb
        crow[:, r] = cb.T[:8, :]
        q = _head_rms(p_ref[0, r, 0:GW].astype(F32), gq_ref[...], bd) * SCALE
        k = _head_rms(p_ref[0, r, GW:2 * GW].astype(F32), gk_ref[...], bd)
        qb = q.astype(BF16)
        kb = k.astype(BF16)
        v = p_ref[0, r, 2 * GW:3 * GW]
        for h in range(NH):
            ls = slice(h * HD, (h + 1) * HD)
            qh[h, r, :] = qb[:, ls]
            kh[h, r, :] = kb[:, ls]
            vh[h, r, :] = v[:, ls]

    lane = lax.broadcasted_iota(jnp.int32, (1, 128), 1)
    row_w = lax.broadcasted_iota(jnp.int32, (TQ, WIDE), 0)
    col_w = lax.broadcasted_iota(jnp.int32, (TQ, WIDE), 1)

    def q_block(i, _):
        r0 = pl.multiple_of(i * TQ, TQ)
        s0 = pl.multiple_of(jnp.maximum(r0 - (WIDE - TQ), 0), TQ)
        rows = pl.ds(r0, TQ)
        causal = (col_w + s0) <= (row_w + r0)
        cq = ccol[rows, :]
        bq = cq + (qk_max + LOGIT_SLACK)

        def tile(h, start, width):
            s = _dot_nt(qh[h, rows, :], kh[h, pl.ds(start, width), :])
            return s + (cq[:, h:h + 1] - crow[h:h + 1, pl.ds(start, width)])

        def keep_going(s_end, ms):
            mc = jnp.zeros((TQ, 128), F32)
            for h in range(NH):
                mc = jnp.where(lane == h, ms[h], mc)
            c_last = ccol[pl.ds(jnp.maximum(s_end - 1, 0), 1), :]
            bound = jnp.max(bq - mc, axis=0, keepdims=True) - c_last
            bound = jnp.where(lane < NH, bound, NEG_BIG)
            return (jnp.max(bound) > EXP_ZERO).astype(jnp.int32)

        ss = [jnp.where(causal, tile(h, s0, WIDE), NEG_BIG) for h in range(NH)]
        ms = [jnp.max(s, axis=-1, keepdims=True) for s in ss]
        ps = [jnp.exp(s - m) for s, m in zip(ss, ms)]
        ls = [jnp.sum(p, axis=-1, keepdims=True) for p in ps]
        accs = [_dot(p.astype(BF16), vh[h, pl.ds(s0, WIDE), :]) for h, p in enumerate(ps)]

        def cond(st):
            return jnp.logical_and(st[0] > 0, st[1] > 0)

        def body(st):
            s_end, _, ms, ls, accs = st
            sb = pl.multiple_of(s_end - TQ, TQ)
            ss = [tile(h, sb, TQ) for h in range(NH)]
            ms2 = [jnp.maximum(m, jnp.max(s, axis=-1, keepdims=True)) for s, m in zip(ss, ms)]
            ps = [jnp.exp(s - m) for s, m in zip(ss, ms2)]
            alphas = [jnp.exp(m - m2) for m, m2 in zip(ms, ms2)]
            ls2 = [a * l + jnp.sum(p, axis=-1, keepdims=True) for a, l, p in zip(alphas, ls, ps)]
            pvs = [_dot(p.astype(BF16), vh[h, pl.ds(sb, TQ), :]) for h, p in enumerate(ps)]
            accs2 = [a * acc + pv for a, acc, pv in zip(alphas, accs, pvs)]
            return sb, keep_going(sb, ms2), tuple(ms2), tuple(ls2), tuple(accs2)

        st = lax.while_loop(cond, body, (s0, keep_going(s0, ms), tuple(ms), tuple(ls), tuple(accs)))
        _, _, ms, ls, accs = st
        for h in range(NH):
            oscr[rows, h * HD:(h + 1) * HD] = accs[h] / ls[h]
        return 0

    lax.fori_loop(0, seq // TQ, q_block, 0)

    for b in range(seq // pb):
        r = slice(b * pb, (b + 1) * pb)
        g = p_ref[0, r, 3 * GW:4 * GW].astype(F32)
        o_ref[0, r, :] = (oscr[r, :] * _silu(g)).astype(BF16)


def _fox(p3, f3, bias, gq, gk, bd, tri):
    b, s, _ = p3.shape
    assert s % tri.shape[0] == 0 and s >= WIDE
    kern = functools.partial(_fox_kernel, seq=s)
    c2 = lambda i: (0, 0)
    return pl.pallas_call(
        kern,
        grid=(b,),
        in_specs=[
            pl.BlockSpec((1, s, 4 * GW), lambda i: (i, 0, 0)),
            pl.BlockSpec((1, s, 128), lambda i: (i, 0, 4)),
            pl.BlockSpec((1, 128), c2),
            pl.BlockSpec((1, GW), c2),
            pl.BlockSpec((1, GW), c2),
            pl.BlockSpec((GW, GW), c2),
            pl.BlockSpec(tri.shape, c2),
        ],
        out_specs=pl.BlockSpec((1, s, GW), lambda i: (i, 0, 0)),
        out_shape=jax.ShapeDtypeStruct((b, s, GW), BF16),
        scratch_shapes=[
            pltpu.VMEM((NH, s, HD), BF16),
            pltpu.VMEM((NH, s, HD), BF16),
            pltpu.VMEM((NH, s, HD), BF16),
            pltpu.VMEM((s, 128), F32),
            pltpu.VMEM((8, s), F32),
            pltpu.VMEM((s, GW), F32),
        ],
        compiler_params=pltpu.CompilerParams(
            dimension_semantics=("arbitrary",), vmem_limit_bytes=VMEM_LIMIT),
    )(p3, f3, bias, gq, gk, bd, tri)


def _sb_kernel(p_ref, to_ref, o_ref, qh, kh, vh, oscr, *, seq):
    to = to_ref[...]
    pb = 256
    for b in range(seq // pb):
        r = slice(b * pb, (b + 1) * pb)
        qb = (p_ref[0, r, 0:GW].astype(F32) * SCALE).astype(BF16)
        kb = p_ref[0, r, GW:2 * GW]
        v = p_ref[0, r, 2 * GW:3 * GW]
        for h in range(NH):
            ls = slice(h * HD, (h + 1) * HD)
            qh[h, r, :] = qb[:, ls]
            kh[h, r, :] = kb[:, ls]
            vh[h, r, :] = v[:, ls]

    row_w = lax.broadcasted_iota(jnp.int32, (TQ, WIDE), 0)
    col_w = lax.broadcasted_iota(jnp.int32, (TQ, WIDE), 1)
    nsub = WIDE // TQ

    def log_one_minus_sigmoid(z):
        return -(jnp.maximum(z, 0.0) + jnp.log(1.0 + jnp.exp(-jnp.abs(z))))

    def suffix_sums(lom):
        hi = lom.astype(BF16)
        lo = (lom - hi.astype(F32)).astype(BF16)
        cs = _dot(jnp.concatenate([hi, lo], axis=1), to)
        return cs[:, :TQ], cs[:, TQ:]

    def q_block(i, _):
        r0 = pl.multiple_of(i * TQ, TQ)
        s0 = pl.multiple_of(jnp.maximum(r0 - (WIDE - TQ), 0), TQ)
        rows = pl.ds(r0, TQ)
        strict = (col_w + s0) < (row_w + r0)

        zs = [_dot_nt(qh[h, rows, :], kh[h, pl.ds(s0, WIDE), :]) for h in range(NH)]
        loms = [jnp.where(strict, log_one_minus_sigmoid(z), 0.0) for z in zs]
        sums = [[suffix_sums(lom[:, c * TQ:(c + 1) * TQ]) for c in range(nsub)] for lom in loms]
        carries, accs = [], []
        for h in range(NH):
            between = [None] * nsub
            carry = jnp.zeros((TQ, TQ), F32)
            for c in reversed(range(nsub)):
                rc, tot = sums[h][c]
                between[c] = rc + carry
                carry = carry + tot
            w = jnp.exp(zs[h] + loms[h] + jnp.concatenate(between, axis=1))
            w = jnp.where(strict, w, 0.0)
            accs.append(_dot(w.astype(BF16), vh[h, pl.ds(s0, WIDE), :]))
            carries.append(carry)

        def keep_going(carries):
            cm = jnp.maximum(jnp.maximum(carries[0], carries[1]), jnp.maximum(carries[2], carries[3]))
            return (jnp.max(cm) > EXP_ZERO).astype(jnp.int32)

        def cond(st):
            return jnp.logical_and(st[0] > 0, st[1] > 0)

        def body(st):
            s_end, _, carries, accs = st
            sb = pl.multiple_of(s_end - TQ, TQ)
            zs = [_dot_nt(qh[h, rows, :], kh[h, pl.ds(sb, TQ), :]) for h in range(NH)]
            loms = [log_one_minus_sigmoid(z) for z in zs]
            sums = [suffix_sums(lom) for lom in loms]
            ws = [jnp.exp(z + lom + rc + cr) for z, lom, (rc, _), cr in zip(zs, loms, sums, carries)]
            pvs = [_dot(w.astype(BF16), vh[h, pl.ds(sb, TQ), :]) for h, w in enumerate(ws)]
            a2 = [acc + pv for acc, pv in zip(accs, pvs)]
            c2 = [cr + tot for cr, (_, tot) in zip(carries, sums)]
            return sb, keep_going(c2), tuple(c2), tuple(a2)

        st = lax.while_loop(cond, body, (s0, keep_going(carries), tuple(carries), tuple(accs)))
        accs = st[3]
        for h in range(NH):
            oscr[rows, h * HD:(h + 1) * HD] = accs[h]
        return 0

    lax.fori_loop(0, seq // TQ, q_block, 0)

    for b in range(seq // pb):
        r = slice(b * pb, (b + 1) * pb)
        g = p_ref[0, r, 3 * GW:4 * GW].astype(F32)
        o_ref[0, r, :] = (oscr[r, :] * _silu(g)).astype(BF16)


def _sb(p3, to):
    b, s, _ = p3.shape
    assert s % 256 == 0 and s >= WIDE
    kern = functools.partial(_sb_kernel, seq=s)
    return pl.pallas_call(
        kern,
        grid=(b,),
        in_specs=[
            pl.BlockSpec((1, s, 4 * GW), lambda i: (i, 0, 1)),
            pl.BlockSpec((2 * TQ, 2 * TQ), lambda i: (0, 0)),
        ],
        out_specs=pl.BlockSpec((1, s, GW), lambda i: (i, 0, 0)),
        out_shape=jax.ShapeDtypeStruct((b, s, GW), BF16),
        scratch_shapes=[
            pltpu.VMEM((NH, s, HD), BF16),
            pltpu.VMEM((NH, s, HD), BF16),
            pltpu.VMEM((NH, s, HD), BF16),
            pltpu.VMEM((s, GW), F32),
        ],
        compiler_params=pltpu.CompilerParams(
            dimension_semantics=("arbitrary",), vmem_limit_bytes=VMEM_LIMIT),
    )(p3, to)


def _hgrn_kernel(hq_ref, hi_ref, hg_ref, hf_ref, lbm_ref, oml_ref, gout_ref, bd_ref, a_ref,
                 mask_ref, hm_ref, o_ref, st_ref, *, seq):
    bd = bd_ref[...]
    a_all = a_ref[...]
    hm = hm_ref[...]
    lbm = lbm_ref[...]
    oml = oml_ref[...]
    c = CHUNK
    st_ref[...] = jnp.zeros((GW, GW), F32)

    n = HSB * c
    bdf = bd.astype(F32)

    def superblock(bi, _):
        r0 = pl.multiple_of(bi * n, n)
        rows = pl.ds(r0, n)
        hf = hf_ref[0, rows, :]
        e = jnp.exp(-jnp.abs(hf))
        rr = 1.0 / (1.0 + e)
        sg = jnp.where(hf >= 0, rr, e * rr)
        sgn = jnp.where(hf >= 0, e * rr, rr)
        g = jnp.log(lbm + oml * sg)
        kk = oml * sgn
        q = _silu(hq_ref[0, rows, :].astype(F32))
        v = hi_ref[0, rows, :]
        gh = g.astype(BF16)
        gl = (g - gh.astype(F32)).astype(BF16)

        sls = [slice(ci * c, (ci + 1) * c) for ci in range(HSB)]
        exs = [_dot(a_all, jnp.concatenate([gh[sl], gl[sl]], axis=0)) for sl in sls]
        ps = [jnp.zeros((c, GW), F32) for _ in sls]
        for l in range(6):
            for ci, sl in enumerate(sls):
                eq = exs[ci][(2 * l) * c:(2 * l + 1) * c]
                ek = exs[ci][(2 * l + 1) * c:(2 * l + 2) * c]
                qf = (q[sl] * jnp.exp(eq)).astype(BF16)
                kf = (kk[sl] * jnp.exp(ek)).astype(BF16)
                kx = jnp.concatenate([kf] * NH, axis=0) * hm
                ps[ci] = ps[ci] + _dot_nt(qf, kx) * mask_ref[l]
        o_intra, qd, upd, dl = [], [], [], []
        for ci, sl in enumerate(sls):
            vx = jnp.concatenate([v[sl]] * NH, axis=0) * hm
            o_intra.append(_dot(ps[ci].astype(BF16), vx))
            eb = exs[ci][12 * c:13 * c]
            er = exs[ci][13 * c:14 * c]
            qd.append((q[sl] * jnp.exp(eb)).astype(BF16))
            upd.append(_dot_tn(v[sl], (kk[sl] * jnp.exp(er)).astype(BF16)) * bdf)
            dl.append(jnp.exp(eb[c - 1:c, :]))

        st = st_ref[...]
        outs = []
        for ci in range(HSB):
            outs.append(o_intra[ci] + _dot_nt(qd[ci], st.astype(BF16)))
            st = st * dl[ci] + upd[ci]
        st_ref[...] = st

        o = _head_rms(jnp.concatenate(outs, axis=0), gout_ref[...], bd)
        o_ref[0, rows, :] = (o * _silu(hg_ref[0, rows, :].astype(F32))).astype(BF16)
        return 0

    lax.fori_loop(0, seq // n, superblock, 0)


def _hgrn(p3, f3, lbm, oml, gout, bd, a_all, masks, hm):
    b, s, _ = p3.shape
    assert s % (HSB * CHUNK) == 0
    kern = functools.partial(_hgrn_kernel, seq=s)
    c2 = lambda i: (0, 0)
    return pl.pallas_call(
        kern,
        grid=(b,),
        in_specs=[
            pl.BlockSpec((1, s, GW), lambda i: (i, 0, 8)),
            pl.BlockSpec((1, s, GW), lambda i: (i, 0, 9)),
            pl.BlockSpec((1, s, GW), lambda i: (i, 0, 10)),
            pl.BlockSpec((1, s, GW), lambda i: (i, 0, 0)),
            pl.BlockSpec((1, GW), c2),
            pl.BlockSpec((1, GW), c2),
            pl.BlockSpec((1, GW), c2),
            pl.BlockSpec((GW, GW), c2),
            pl.BlockSpec(a_all.shape, c2),
            pl.BlockSpec(masks.shape, lambda i: (0, 0, 0)),
            pl.BlockSpec(hm.shape, c2),
        ],
        out_specs=pl.BlockSpec((1, s, GW), lambda i: (i, 0, 0)),
        out_shape=jax.ShapeDtypeStruct((b, s, GW), BF16),
        scratch_shapes=[pltpu.VMEM((GW, GW), F32)],
        compiler_params=pltpu.CompilerParams(
            dimension_semantics=("arbitrary",), vmem_limit_bytes=VMEM_LIMIT),
    )(p3, p3, p3, f3, lbm, oml, gout, bd, a_all, masks, hm)


def _pm_kernel(pv_ref, pg_ref, mq_ref, mg_ref, mem_ref, mng_ref, wkv_ref, gmq_ref, gmk_ref, bd_ref,
               wp_ref, ps_ref, win_ref, d_ref, e_ref, ubuf, kh, vh, *, seq, tq):
    bd = bd_ref[...]
    halo = 16
    mem = mem_ref[0]
    ms = jnp.mean(mem * mem, axis=-1, keepdims=True)
    mn = (mem * lax.rsqrt(ms + EPS) * mng_ref[...]).astype(BF16)
    kv = _dot(mn, wkv_ref[...])
    kn = _head_rms(kv[:, :GW], gmk_ref[...], bd).astype(BF16)
    vv = kv[:, GW:].astype(BF16)
    for h in range(NH):
        kh[h] = kn[:, h * HD:(h + 1) * HD]
        vh[h] = vv[:, h * HD:(h + 1) * HD]

    ubuf[0:halo, :] = jnp.zeros((halo, GW), F32)
    ubuf[halo:halo + seq, :] = pv_ref[0]
    win = win_ref[...]

    for b in range(seq // tq):
        r = slice(b * tq, (b + 1) * tq)
        base = halo + b * tq

        def ld(j):
            return ubuf[base - j:base - j + tq, :]

        u = ld(0)
        s2 = u + ld(1)
        s4 = s2 + ld(2) + ld(3)
        s8 = s4 + ld(4) + ld(5) + ld(6) + ld(7)
        s16 = s8
        for j in range(8, 16):
            s16 = s16 + ld(j)
        sw = jnp.where(win == 2.0, s2, jnp.where(win == 4.0, s4, jnp.where(win == 8.0, s8, s16)))
        pos = (lax.broadcasted_iota(jnp.int32, (tq, GW), 0) + (b * tq + 1)).astype(F32)
        pooled = sw / jnp.minimum(pos, win)
        y = _dot((pooled - u).astype(BF16), wp_ref[...]) * ps_ref[...]
        d_ref[0, r, :] = (y * _silu(pg_ref[0, r, :].astype(F32))).astype(BF16)

        qn = (_head_rms(mq_ref[0, r, :].astype(F32), gmq_ref[...], bd) * SCALE).astype(BF16)
        outs = []
        for h in range(NH):
            s = _dot_nt(qn[:, h * HD:(h + 1) * HD], kh[h])
            m = jnp.max(s, axis=-1, keepdims=True)
            p = jnp.exp(s - m)
            l = jnp.sum(p, axis=-1, keepdims=True)
            outs.append(_dot(p.astype(BF16), vh[h]) / l)
        oe = jnp.concatenate(outs, axis=1)
        e_ref[0, r, :] = (oe * _silu(mg_ref[0, r, :].astype(F32))).astype(BF16)


def _pm(p3, f3, mem, mng, wkv, gmq, gmk, bd, wp, ps, win, tq):
    b, s, _ = p3.shape
    nm = mem.shape[1]
    kern = functools.partial(_pm_kernel, seq=s, tq=tq)
    c2 = lambda i: (0, 0)
    return pl.pallas_call(
        kern,
        grid=(b,),
        in_specs=[
            pl.BlockSpec((1, s, GW), lambda i: (i, 0, 1)),
            pl.BlockSpec((1, s, GW), lambda i: (i, 0, 11)),
            pl.BlockSpec((1, s, GW), lambda i: (i, 0, 12)),
            pl.BlockSpec((1, s, GW), lambda i: (i, 0, 13)),
            pl.BlockSpec((1, nm, D_MODEL), lambda i: (i, 0, 0)),
            pl.BlockSpec((1, D_MODEL), c2),
            pl.BlockSpec((D_MODEL, 2 * GW), c2),
            pl.BlockSpec((1, GW), c2),
            pl.BlockSpec((1, GW), c2),
            pl.BlockSpec((GW, GW), c2),
            pl.BlockSpec((GW, GW), c2),
            pl.BlockSpec((1, GW), c2),
            pl.BlockSpec((1, GW), c2),
        ],
        out_specs=[
            pl.BlockSpec((1, s, GW), lambda i: (i, 0, 0)),
            pl.BlockSpec((1, s, GW), lambda i: (i, 0, 0)),
        ],
        out_shape=[
            jax.ShapeDtypeStruct((b, s, GW), BF16),
            jax.ShapeDtypeStruct((b, s, GW), BF16),
        ],
        scratch_shapes=[
            pltpu.VMEM((s + 16, GW), F32),
            pltpu.VMEM((NH, nm, HD), BF16),
            pltpu.VMEM((NH, nm, HD), BF16),
        ],
        compiler_params=pltpu.CompilerParams(
            dimension_semantics=("arbitrary",), vmem_limit_bytes=VMEM_LIMIT),
    )(f3, p3, p3, p3, mem, mng, wkv, gmq, gmk, bd, wp, ps, win)


def _out_kernel(a_ref, b_ref, c_ref, d_ref, e_ref, w_ref, x_ref, o_ref):
    mixed = jnp.concatenate([a_ref[...], b_ref[...], c_ref[...], d_ref[...], e_ref[...]], axis=1)
    o_ref[...] = x_ref[...] + _dot(mixed, w_ref[...])


def _out(parts, w, x2, tm):
    m = x2.shape[0]
    gspec = pl.BlockSpec((tm, GW), lambda i: (i, 0))
    return pl.pallas_call(
        _out_kernel,
        grid=(m // tm,),
        in_specs=[gspec] * 5 + [
            pl.BlockSpec((5 * GW, D_MODEL), lambda i: (0, 0)),
            pl.BlockSpec((tm, D_MODEL), lambda i: (i, 0)),
        ],
        out_specs=pl.BlockSpec((tm, D_MODEL), lambda i: (i, 0)),
        out_shape=jax.ShapeDtypeStruct((m, D_MODEL), F32),
        compiler_params=pltpu.CompilerParams(
            dimension_semantics=("arbitrary",), vmem_limit_bytes=VMEM_LIMIT),
    )(*parts, w, x2)


def _tile_heads(g):
    return jnp.tile(g.astype(F32), NH).reshape(1, GW)


def kernel(x, mem, norm_g, w_in, fox_f_bias, fox_q_norm, fox_k_norm, hgrn_lb_logits, hgrn_out_norm,
           pool_w, pool_scale, mem_norm_g, mem_w_kv, mem_q_norm, mem_k_norm, w_out):
    bsz, seq, _ = x.shape
    depth = w_in.shape[0]
    m = bsz * seq
    tq = 256
    tm = 512

    pr = jax.nn.softmax(hgrn_lb_logits.astype(F32), axis=0)
    lower_bounds = jnp.clip(jnp.cumsum(pr, axis=0) - pr[0:1], 0.0, 1.0 - 1e-6)

    bd_np = _bd_ones()
    bd = jnp.asarray(bd_np, BF16)
    tri = jnp.asarray(np.tril(np.ones((tq, tq), np.float32)), BF16)
    jj = np.arange(TQ)
    suffix = (jj[:, None] > jj[None, :]).astype(np.float32)
    to_half = np.concatenate([suffix, np.ones((TQ, TQ), np.float32)], axis=1)
    to = jnp.asarray(np.concatenate([to_half, to_half], axis=0), BF16)
    a_np, masks_np, hm_np = _hgrn_constants()
    a_all = jnp.asarray(a_np, BF16)
    masks = jnp.asarray(masks_np, F32)
    hm = jnp.asarray(hm_np, BF16)
    win = jnp.asarray(np.repeat(np.array(POOL_WINDOWS, np.float32), HD).reshape(1, GW))

    g = GW
    wb_all, wf_all = _regroup_w_in(w_in)
    x2 = x.reshape(m, D_MODEL)
    for l in range(depth):
        pb, pf = _proj(x2, norm_g[l].reshape(1, D_MODEL).astype(F32), wb_all[l], wf_all[l], tm)
        p3 = pb.reshape(bsz, seq, NP_COLS)
        f3 = pf.reshape(bsz, seq, NF_COLS)

        bias = jnp.concatenate([fox_f_bias[l].astype(F32), jnp.zeros((128 - NH,), F32)]).reshape(1, 128)
        out_a = _fox(p3, f3, bias, _tile_heads(fox_q_norm[l]), _tile_heads(fox_k_norm[l]), bd, tri)
        out_b = _sb(p3, to)

        lb = lower_bounds[l].reshape(1, g)
        out_c = _hgrn(p3, f3, jnp.maximum(lb, LB_FLOOR), 1.0 - lb,
                      hgrn_out_norm[l].reshape(1, g).astype(F32), bd, a_all, masks, hm)

        wp = jax.scipy.linalg.block_diag(*[pool_w[l, i] for i in range(len(POOL_WINDOWS))]).astype(BF16)
        out_d, out_e = _pm(p3, f3, mem, mem_norm_g[l].reshape(1, D_MODEL).astype(F32),
                           mem_w_kv[l].astype(BF16), _tile_heads(mem_q_norm[l]), _tile_heads(mem_k_norm[l]),
                           bd, wp, pool_scale[l].reshape(1, g).astype(F32), win, tq)

        x2 = _out([out_a.reshape(m, g), out_b.reshape(m, g), out_c.reshape(m, g),
                   out_d.reshape(m, g), out_e.reshape(m, g)], w_out[l].astype(BF16), x2, tm)
    return x2.reshape(bsz, seq, D_MODEL)
```

```python
import functools

import numpy as np
import jax
import jax.numpy as jnp
from jax import lax
from jax.experimental import pallas as pl
from jax.experimental.pallas import tpu as pltpu

F32 = jnp.float32
BF16 = jnp.bfloat16

D_MODEL = 1024
GW = 256
NH = 4
HD = 64
CHUNK = 64
POOL_WINDOWS = (2, 4, 8, 16)
EPS = 1e-6
NEG_BIG = -1e30
LB_FLOOR = 1e-30
SCALE = HD ** -0.5

NP_COLS = 14 * GW
NF_COLS = 2 * GW + 128
VMEM_LIMIT = 56 * 1024 * 1024

TQ = 128
WIDE = 3 * TQ
EXP_ZERO = -104.0
NORM_SLACK = 1.01
LOGIT_SLACK = 0.05
HSB = 4


def _dot(a, b):
    return jnp.dot(a, b, preferred_element_type=F32)


def _dot_nt(a, b):
    return lax.dot_general(a, b, (((1,), (1,)), ((), ())), preferred_element_type=F32)


def _dot_tn(a, b):
    return lax.dot_general(a, b, (((0,), (0,)), ((), ())), preferred_element_type=F32)


def _split_bf16(x, n):
    parts = []
    r = x
    for i in range(n):
        p = r.astype(BF16)
        parts.append(p)
        if i + 1 < n:
            r = r - p.astype(F32)
    return parts


def _dot_f32_rhs01(x, m01, n=3):
    acc = None
    for p in _split_bf16(x, n):
        t = _dot(p, m01)
        acc = t if acc is None else acc + t
    return acc


def _dot_f32_lhs01(m01, x, n=3):
    acc = None
    for p in _split_bf16(x, n):
        t = _dot(m01, p)
        acc = t if acc is None else acc + t
    return acc


def _silu(x):
    return x / (1.0 + jnp.exp(-x))


def _log_sigmoid(x):
    return jnp.minimum(x, 0.0) - jnp.log(1.0 + jnp.exp(-jnp.abs(x)))


def _head_rms(x, gain, bd):
    ss = _dot((x * x).astype(BF16), bd)
    return x * lax.rsqrt(ss * (1.0 / HD) + EPS) * gain


def _bd_ones():
    h = np.arange(GW) // HD
    return (h[:, None] == h[None, :]).astype(np.float32)


def _hgrn_constants():
    c = CHUNK
    t = np.arange(c)
    j = np.arange(c)[None, :]
    mats = []
    masks = []
    for n in (64, 32, 16, 8, 4):
        blk, pos = t // n, t % n
        ref = blk * n + n // 2 - 1
        aq = (pos[:, None] >= n // 2) & (j > ref[:, None]) & (j <= t[:, None])
        ak = (pos[:, None] < n // 2) & (j > t[:, None]) & (j <= ref[:, None])
        mats.append((aq | ak).astype(np.float32))
        m = (blk[:, None] == blk[None, :]) & (pos[:, None] >= n // 2) & (pos[None, :] < n // 2)
        masks.append(m.astype(np.float32))
    m = ((t[:, None] // 2) == (t[None, :] // 2)) & (t[None, :] <= t[:, None])
    masks.append(m.astype(np.float32))
    mats.append((j <= t[:, None]).astype(np.float32))
    mats.append((j > t[:, None]).astype(np.float32))
    a_all = np.concatenate(mats, axis=0)
    a_all = np.concatenate([a_all, a_all], axis=1)
    masks = np.stack([np.tile(m, (1, NH)) for m in masks])
    total = masks[:, :, :c].sum(0)
    assert np.array_equal(total, np.tril(np.ones((c, c), np.float32)))
    hm = (np.arange(NH * c)[:, None] // c == np.arange(GW)[None, :] // HD).astype(np.float32)
    return a_all, masks, hm


def _regroup_kernel(w_ref, wb_ref, wf_ref):
    g = GW
    o_ff, o_sb, o_hg, o_pl, o_mm = 4 * g, 4 * g + NH, 8 * g + NH, 12 * g + NH, 14 * g + NH
    w = w_ref[0]
    bf_src = [0, g, 2 * g, 3 * g,
              o_sb, o_sb + g, o_sb + 2 * g, o_sb + 3 * g,
              o_hg, o_hg + 2 * g, o_hg + 3 * g,
              o_pl + g,
              o_mm, o_mm + g]
    for i, c0 in enumerate(bf_src):
        wb_ref[0, :, i * g:(i + 1) * g] = w[:, c0:c0 + g].astype(BF16)
    wf_ref[0, :, 0:g] = w[:, o_hg + g:o_hg + 2 * g].astype(BF16)
    wf_ref[0, :, g:2 * g] = w[:, o_pl:o_pl + g].astype(BF16)
    ff = jnp.concatenate([w[:, o_ff:o_ff + NH], jnp.zeros((w.shape[0], 128 - NH), F32)], axis=1)
    wf_ref[0, :, 2 * g:2 * g + 128] = ff.astype(BF16)


def _regroup_w_in(w_in):
    depth, d, n = w_in.shape
    tr = 128
    return pl.pallas_call(
        _regroup_kernel,
        grid=(depth, d // tr),
        in_specs=[pl.BlockSpec((1, tr, n), lambda l, i: (l, i, 0))],
        out_specs=[
            pl.BlockSpec((1, tr, NP_COLS), lambda l, i: (l, i, 0)),
            pl.BlockSpec((1, tr, NF_COLS), lambda l, i: (l, i, 0)),
        ],
        out_shape=[
            jax.ShapeDtypeStruct((depth, d, NP_COLS), BF16),
            jax.ShapeDtypeStruct((depth, d, NF_COLS), BF16),
        ],
        compiler_params=pltpu.CompilerParams(
            dimension_semantics=("arbitrary", "arbitrary"), vmem_limit_bytes=VMEM_LIMIT),
    )(w_in)


def _proj_kernel(x_ref, g_ref, wb_ref, wf_ref, pb_ref, pf_ref):
    x = x_ref[...]
    ms = jnp.mean(x * x, axis=-1, keepdims=True)
    h = (x * lax.rsqrt(ms + EPS) * g_ref[...]).astype(BF16)
    nb = 512
    for c0 in range(0, NP_COLS, nb):
        pb_ref[:, c0:c0 + nb] = _dot(h, wb_ref[:, c0:c0 + nb]).astype(BF16)
    pf_ref[...] = _dot(h, wf_ref[...])


def _proj(x2, g, wb, wf, tm):
    m = x2.shape[0]
    return pl.pallas_call(
        _proj_kernel,
        grid=(m // tm,),
        in_specs=[
            pl.BlockSpec((tm, D_MODEL), lambda i: (i, 0)),
            pl.BlockSpec((1, D_MODEL), lambda i: (0, 0)),
            pl.BlockSpec((D_MODEL, NP_COLS), lambda i: (0, 0)),
            pl.BlockSpec((D_MODEL, NF_COLS), lambda i: (0, 0)),
        ],
        out_specs=[
            pl.BlockSpec((tm, NP_COLS), lambda i: (i, 0)),
            pl.BlockSpec((tm, NF_COLS), lambda i: (i, 0)),
        ],
        out_shape=[
            jax.ShapeDtypeStruct((m, NP_COLS), BF16),
            jax.ShapeDtypeStruct((m, NF_COLS), F32),
        ],
        compiler_params=pltpu.CompilerParams(
            dimension_semantics=("arbitrary",), vmem_limit_bytes=VMEM_LIMIT),
    )(x2, g, wb, wf)


def _fox_kernel(p_ref, ff_ref, bias_ref, gq_ref, gk_ref, bd_ref, tri_ref, o_ref,
                qh, kh, vh, ccol, crow, oscr, *, seq):
    bd = bd_ref[...]
    tri = tri_ref[...]
    pb = tri.shape[0]
    qk_max = (HD * SCALE * NORM_SLACK) * (jnp.max(jnp.abs(gq_ref[...]), axis=-1, keepdims=True)
                                          * jnp.max(jnp.abs(gk_ref[...]), axis=-1, keepdims=True))
    carry = jnp.zeros((1, 128), F32)
    for b in range(seq // pb):
        r = slice(b * pb, (b + 1) * pb)
        lf = _log_sigmoid(ff_ref[0, r, :] + bias_ref[...])
        cb = _dot_f32_lhs01(tri, lf, 3) + carry
        carry = cb[pb - 1:pb, :]
        ccol[r, :] = cb
        crow[:, r] = cb.T[:8, :]
        q = _head_rms(p_ref[0, r, 0:GW].astype(F32), gq_ref[...], bd) * SCALE
        k = _head_rms(p_ref[0, r, GW:2 * GW].astype(F32), gk_ref[...], bd)
        qb = q.astype(BF16)
        kb = k.astype(BF16)
        v = p_ref[0, r, 2 * GW:3 * GW]
        for h in range(NH):
            ls = slice(h * HD, (h + 1) * HD)
            qh[h, r, :] = qb[:, ls]
            kh[h, r, :] = kb[:, ls]
            vh[h, r, :] = v[:, ls]

    lane = lax.broadcasted_iota(jnp.int32, (1, 128), 1)
    rel_w = (lax.broadcasted_iota(jnp.int32, (TQ, WIDE), 1) - lax.broadcasted_iota(jnp.int32, (TQ, WIDE), 0))
    nfull = (WIDE - TQ) // TQ

    def q_block(i, full_tile):
        r0 = pl.multiple_of(i * TQ, TQ)
        s0 = pl.multiple_of(jnp.maximum(r0 - (WIDE - TQ), 0), TQ)
        rows = pl.ds(r0, TQ)
        cq = ccol[rows, :]

        def mask_wide(s):
            if full_tile:
                diag = jnp.where(rel_w[:, WIDE - TQ:] <= WIDE - TQ, s[:, WIDE - TQ:], NEG_BIG)
                return jnp.concatenate([s[:, :WIDE - TQ], diag], axis=1)
            return jnp.where(rel_w <= r0 - s0, s, NEG_BIG)
        bq = cq + (qk_max + LOGIT_SLACK)

        def tile(h, start, width):
            s = _dot_nt(qh[h, rows, :], kh[h, pl.ds(start, width), :])
            return s + (cq[:, h:h + 1] - crow[h:h + 1, pl.ds(start, width)])

        def keep_going(s_end, ms):
            mc = jnp.zeros((TQ, 128), F32)
            for h in range(NH):
                mc = jnp.where(lane == h, ms[h], mc)
            c_last = ccol[pl.ds(jnp.maximum(s_end - 1, 0), 1), :]
            bound = jnp.max(bq - mc, axis=0, keepdims=True) - c_last
            bound = jnp.where(lane < NH, bound, NEG_BIG)
            return (jnp.max(bound) > EXP_ZERO).astype(jnp.int32)

        ss = [mask_wide(tile(h, s0, WIDE)) for h in range(NH)]
        ms = [jnp.max(s, axis=-1, keepdims=True) for s in ss]
        ps = [jnp.exp(s - m) for s, m in zip(ss, ms)]
        ls = [jnp.sum(p, axis=-1, keepdims=True) for p in ps]
        accs = [_dot(p.astype(BF16), vh[h, pl.ds(s0, WIDE), :]) for h, p in enumerate(ps)]

        def cond(st):
            return jnp.logical_and(st[0] > 0, st[1] > 0)

        def body(st):
            s_end, _, ms, ls, accs = st
            sb = pl.multiple_of(s_end - TQ, TQ)
            ss = [tile(h, sb, TQ) for h in range(NH)]
            ms2 = [jnp.maximum(m, jnp.max(s, axis=-1, keepdims=True)) for s, m in zip(ss, ms)]
            ps = [jnp.exp(s - m) for s, m in zip(ss, ms2)]
            alphas = [jnp.exp(m - m2) for m, m2 in zip(ms, ms2)]
            ls2 = [a * l + jnp.sum(p, axis=-1, keepdims=True) for a, l, p in zip(alphas, ls, ps)]
            pvs = [_dot(p.astype(BF16), vh[h, pl.ds(sb, TQ), :]) for h, p in enumerate(ps)]
            accs2 = [a * acc + pv for a, acc, pv in zip(alphas, accs, pvs)]
            return sb, keep_going(sb, ms2), tuple(ms2), tuple(ls2), tuple(accs2)

        if full_tile:
            st = lax.while_loop(cond, body, (s0, keep_going(s0, ms), tuple(ms), tuple(ls), tuple(accs)))
            _, _, ms, ls, accs = st
        for h in range(NH):
            oscr[rows, h * HD:(h + 1) * HD] = accs[h] / ls[h]
        return 0

    lax.fori_loop(0, nfull, lambda i, _: q_block(i, False), 0)
    lax.fori_loop(nfull, seq // TQ, lambda i, _: q_block(i, True), 0)

    for b in range(seq // pb):
        r = slice(b * pb, (b + 1) * pb)
        g = p_ref[0, r, 3 * GW:4 * GW].astype(F32)
        o_ref[0, r, :] = (oscr[r, :] * _silu(g)).astype(BF16)


def _fox(p3, f3, bias, gq, gk, bd, tri):
    b, s, _ = p3.shape
    assert s % tri.shape[0] == 0 and s >= WIDE
    kern = functools.partial(_fox_kernel, seq=s)
    c2 = lambda i: (0, 0)
    return pl.pallas_call(
        kern,
        grid=(b,),
        in_specs=[
            pl.BlockSpec((1, s, 4 * GW), lambda i: (i, 0, 0)),
            pl.BlockSpec((1, s, 128), lambda i: (i, 0, 4)),
            pl.BlockSpec((1, 128), c2),
            pl.BlockSpec((1, GW), c2),
            pl.BlockSpec((1, GW), c2),
            pl.BlockSpec((GW, GW), c2),
            pl.BlockSpec(tri.shape, c2),
        ],
        out_specs=pl.BlockSpec((1, s, GW), lambda i: (i, 0, 0)),
        out_shape=jax.ShapeDtypeStruct((b, s, GW), BF16),
        scratch_shapes=[
            pltpu.VMEM((NH, s, HD), BF16),
            pltpu.VMEM((NH, s, HD), BF16),
            pltpu.VMEM((NH, s, HD), BF16),
            pltpu.VMEM((s, 128), F32),
            pltpu.VMEM((8, s), F32),
            pltpu.VMEM((s, GW), F32),
        ],
        compiler_params=pltpu.CompilerParams(
            dimension_semantics=("arbitrary",), vmem_limit_bytes=VMEM_LIMIT),
    )(p3, f3, bias, gq, gk, bd, tri)


def _sb_kernel(p_ref, to_ref, o_ref, qh, kh, vh, oscr, *, seq):
    to = to_ref[...]
    pb = 256
    for b in range(seq // pb):
        r = slice(b * pb, (b + 1) * pb)
        qb = (p_ref[0, r, 0:GW].astype(F32) * SCALE).astype(BF16)
        kb = p_ref[0, r, GW:2 * GW]
        v = p_ref[0, r, 2 * GW:3 * GW]
        for h in range(NH):
            ls = slice(h * HD, (h + 1) * HD)
            qh[h, r, :] = qb[:, ls]
            kh[h, r, :] = kb[:, ls]
            vh[h, r, :] = v[:, ls]

    rel_w = (lax.broadcasted_iota(jnp.int32, (TQ, WIDE), 1) - lax.broadcasted_iota(jnp.int32, (TQ, WIDE), 0))
    nsub = WIDE // TQ
    nfull = nsub - 1

    def log_one_minus_sigmoid(z):
        return -(jnp.maximum(z, 0.0) + jnp.log(1.0 + jnp.exp(-jnp.abs(z))))

    def suffix_sums(lom):
        hi = lom.astype(BF16)
        lo = (lom - hi.astype(F32)).astype(BF16)
        cs = _dot(jnp.concatenate([hi, lo], axis=1), to)
        return cs[:, :TQ], cs[:, TQ:]

    def q_block(i, full_tile):
        r0 = pl.multiple_of(i * TQ, TQ)
        s0 = pl.multiple_of(jnp.maximum(r0 - (WIDE - TQ), 0), TQ)
        rows = pl.ds(r0, TQ)

        def mask_wide(x):
            if full_tile:
                diag = jnp.where(rel_w[:, WIDE - TQ:] < WIDE - TQ, x[:, WIDE - TQ:], 0.0)
                return jnp.concatenate([x[:, :WIDE - TQ], diag], axis=1)
            return jnp.where(rel_w < r0 - s0, x, 0.0)

        zs = [_dot_nt(qh[h, rows, :], kh[h, pl.ds(s0, WIDE), :]) for h in range(NH)]
        loms = [mask_wide(log_one_minus_sigmoid(z)) for z in zs]
        sums = [[suffix_sums(lom[:, c * TQ:(c + 1) * TQ]) for c in range(nsub)] for lom in loms]
        carries, accs = [], []
        for h in range(NH):
            between = [None] * nsub
            carry = jnp.zeros((TQ, TQ), F32)
            for c in reversed(range(nsub)):
                rc, tot = sums[h][c]
                between[c] = rc + carry
                carry = carry + tot
            w = jnp.exp(zs[h] + loms[h] + jnp.concatenate(between, axis=1))
            w = mask_wide(w)
            accs.append(_dot(w.astype(BF16), vh[h, pl.ds(s0, WIDE), :]))
            carries.append(carry)

        def keep_going(carries):
            cm = jnp.maximum(jnp.maximum(carries[0], carries[1]), jnp.maximum(carries[2], carries[3]))
            return (jnp.max(cm) > EXP_ZERO).astype(jnp.int32)

        def cond(st):
            return jnp.logical_and(st[0] > 0, st[1] > 0)

        def body(st):
            s_end, _, carries, accs = st
            sb = pl.multiple_of(s_end - TQ, TQ)
            zs = [_dot_nt(qh[h, rows, :], kh[h, pl.ds(sb, TQ), :]) for h in range(NH)]
            loms = [log_one_minus_sigmoid(z) for z in zs]
            sums = [suffix_sums(lom) for lom in loms]
            ws = [jnp.exp(z + lom + rc + cr) for z, lom, (rc, _), cr in zip(zs, loms, sums, carries)]
            pvs = [_dot(w.astype(BF16), vh[h, pl.ds(sb, TQ), :]) for h, w in enumerate(ws)]
            a2 = [acc + pv for acc, pv in zip(accs, pvs)]
            c2 = [cr + tot for cr, (_, tot) in zip(carries, sums)]
            return sb, keep_going(c2), tuple(c2), tuple(a2)

        if full_tile:
            st = lax.while_loop(cond, body, (s0, keep_going(carries), tuple(carries), tuple(accs)))
            accs = st[3]
        for h in range(NH):
            oscr[rows, h * HD:(h + 1) * HD] = accs[h]
        return 0

    lax.fori_loop(0, nfull, lambda i, _: q_block(i, False), 0)
    lax.fori_loop(nfull, seq // TQ, lambda i, _: q_block(i, True), 0)

    for b in range(seq // pb):
        r = slice(b * pb, (b + 1) * pb)
        g = p_ref[0, r, 3 * GW:4 * GW].astype(F32)
        o_ref[0, r, :] = (oscr[r, :] * _silu(g)).astype(BF16)


def _sb(p3, to):
    b, s, _ = p3.shape
    assert s % 256 == 0 and s >= WIDE
    kern = functools.partial(_sb_kernel, seq=s)
    return pl.pallas_call(
        kern,
        grid=(b,),
        in_specs=[
            pl.BlockSpec((1, s, 4 * GW), lambda i: (i, 0, 1)),
            pl.BlockSpec((2 * TQ, 2 * TQ), lambda i: (0, 0)),
        ],
        out_specs=pl.BlockSpec((1, s, GW), lambda i: (i, 0, 0)),
        out_shape=jax.ShapeDtypeStruct((b, s, GW), BF16),
        scratch_shapes=[
            pltpu.VMEM((NH, s, HD), BF16),
            pltpu.VMEM((NH, s, HD), BF16),
            pltpu.VMEM((NH, s, HD), BF16),
            pltpu.VMEM((s, GW), F32),
        ],
        compiler_params=pltpu.CompilerParams(
            dimension_semantics=("arbitrary",), vmem_limit_bytes=VMEM_LIMIT),
    )(p3, to)


def _hgrn_kernel(hq_ref, hi_ref, hg_ref, hf_ref, lbm_ref, oml_ref, gout_ref, bd_ref, a_ref,
                 mask_ref, hm_ref, o_ref, st_ref, *, seq):
    bd = bd_ref[...]
    a_all = a_ref[...]
    hm = hm_ref[...]
    lbm = lbm_ref[...]
    oml = oml_ref[...]
    c = CHUNK
    st_ref[...] = jnp.zeros((GW, GW), F32)

    n = HSB * c
    nlev = mask_ref.shape[0] - 1
    bdf = bd.astype(F32)
    odd = (lax.broadcasted_iota(jnp.int32, (n, GW), 0) & 1) == 1

    def superblock(bi, _):
        r0 = pl.multiple_of(bi * n, n)
        rows = pl.ds(r0, n)
        hf = hf_ref[0, rows, :]
        e = jnp.exp(-jnp.abs(hf))
        rr = 1.0 / (1.0 + e)
        sg = jnp.where(hf >= 0, rr, e * rr)
        sgn = jnp.where(hf >= 0, e * rr, rr)
        f = lbm + oml * sg
        g = jnp.log(f)
        kk = oml * sgn
        q = _silu(hq_ref[0, rows, :].astype(F32))
        v = hi_ref[0, rows, :]
        gh = g.astype(BF16)
        gl = (g - gh.astype(F32)).astype(BF16)
        q2 = (q * jnp.where(odd, f, 1.0)).astype(BF16)
        k2 = (kk * jnp.where(odd, 1.0 / f, 1.0)).astype(BF16)

        def scores(qf, kf, l):
            kx = jnp.concatenate([kf] * NH, axis=0) * hm
            return _dot_nt(qf, kx) * mask_ref[l]

        sls = [slice(ci * c, (ci + 1) * c) for ci in range(HSB)]
        exs = [_dot(a_all, jnp.concatenate([gh[sl], gl[sl]], axis=0)) for sl in sls]
        ps = [scores(q2[sl], k2[sl], nlev) for sl in sls]
        for l in range(nlev):
            for ci, sl in enumerate(sls):
                x = jnp.exp(exs[ci][l * c:(l + 1) * c])
                ps[ci] = ps[ci] + scores((q[sl] * x).astype(BF16), (kk[sl] * x).astype(BF16), l)
        o_intra, qd, upd, dl = [], [], [], []
        for ci, sl in enumerate(sls):
            vx = jnp.concatenate([v[sl]] * NH, axis=0) * hm
            o_intra.append(_dot(ps[ci].astype(BF16), vx))
            eb = exs[ci][nlev * c:(nlev + 1) * c]
            er = exs[ci][(nlev + 1) * c:(nlev + 2) * c]
            qd.append((q[sl] * jnp.exp(eb)).astype(BF16))
            upd.append(_dot_tn(v[sl], (kk[sl] * jnp.exp(er)).astype(BF16)) * bdf)
            dl.append(jnp.exp(eb[c - 1:c, :]))

        st = st_ref[...]
        outs = []
        for ci in range(HSB):
            outs.append(o_intra[ci] + _dot_nt(qd[ci], st.astype(BF16)))
            st = st * dl[ci] + upd[ci]
        st_ref[...] = st

        o = _head_rms(jnp.concatenate(outs, axis=0), gout_ref[...], bd)
        o_ref[0, rows, :] = (o * _silu(hg_ref[0, rows, :].astype(F32))).astype(BF16)
        return 0

    lax.fori_loop(0, seq // n, superblock, 0)


def _hgrn(p3, f3, lbm, oml, gout, bd, a_all, masks, hm):
    b, s, _ = p3.shape
    assert s % (HSB * CHUNK) == 0
    kern = functools.partial(_hgrn_kernel, seq=s)
    c2 = lambda i: (0, 0)
    return pl.pallas_call(
        kern,
        grid=(b,),
        in_specs=[
            pl.BlockSpec((1, s, GW), lambda i: (i, 0, 8)),
            pl.BlockSpec((1, s, GW), lambda i: (i, 0, 9)),
            pl.BlockSpec((1, s, GW), lambda i: (i, 0, 10)),
            pl.BlockSpec((1, s, GW), lambda i: (i, 0, 0)),
            pl.BlockSpec((1, GW), c2),
            pl.BlockSpec((1, GW), c2),
            pl.BlockSpec((1, GW), c2),
            pl.BlockSpec((GW, GW), c2),
            pl.BlockSpec(a_all.shape, c2),
            pl.BlockSpec(masks.shape, lambda i: (0, 0, 0)),
            pl.BlockSpec(hm.shape, c2),
        ],
        out_specs=pl.BlockSpec((1, s, GW), lambda i: (i, 0, 0)),
        out_shape=jax.ShapeDtypeStruct((b, s, GW), BF16),
        scratch_shapes=[pltpu.VMEM((GW, GW), F32)],
        compiler_params=pltpu.CompilerParams(
            dimension_semantics=("arbitrary",), vmem_limit_bytes=VMEM_LIMIT),
    )(p3, p3, p3, f3, lbm, oml, gout, bd, a_all, masks, hm)


def _pm_kernel(pv_ref, pg_ref, mq_ref, mg_ref, mem_ref, mng_ref, wkv_ref, gmq_ref, gmk_ref, bd_ref,
               wp_ref, ps_ref, win_ref, d_ref, e_ref, ubuf, wbuf, kh, vh, *, seq, tq):
    bd = bd_ref[...]
    halo = 16
    mem = mem_ref[0]
    ms = jnp.mean(mem * mem, axis=-1, keepdims=True)
    mn = (mem * lax.rsqrt(ms + EPS) * mng_ref[...]).astype(BF16)
    kv = _dot(mn, wkv_ref[...])
    kn = _head_rms(kv[:, :GW], gmk_ref[...], bd).astype(BF16)
    vv = kv[:, GW:].astype(BF16)
    for h in range(NH):
        kh[h] = kn[:, h * HD:(h + 1) * HD]
        vh[h] = vv[:, h * HD:(h + 1) * HD]

    ubuf[0:halo, :] = jnp.zeros((halo, GW), F32)
    ubuf[halo:halo + seq, :] = pv_ref[0]
    win = win_ref[...]
    for k in range(3):
        sh = 1 << k
        wbuf[k, 0:halo, :] = jnp.zeros((halo, GW), F32)
        for b in range(seq // tq):
            base = halo + b * tq
            if k == 0:
                wbuf[k, base:base + tq, :] = ubuf[base:base + tq, :] + ubuf[base - sh:base - sh + tq, :]
            else:
                wbuf[k, base:base + tq, :] = (wbuf[k - 1, base:base + tq, :]
                                              + wbuf[k - 1, base - sh:base - sh + tq, :])

    for b in range(seq // tq):
        r = slice(b * tq, (b + 1) * tq)
        base = halo + b * tq
        u = ubuf[base:base + tq, :]
        s2 = wbuf[0, base:base + tq, :]
        s4 = wbuf[1, base:base + tq, :]
        s8 = wbuf[2, base:base + tq, :]
        s16 = s8 + wbuf[2, base - 8:base - 8 + tq, :]
        sw = jnp.where(win == 2.0, s2, jnp.where(win == 4.0, s4, jnp.where(win == 8.0, s8, s16)))
        pos = (lax.broadcasted_iota(jnp.int32, (tq, GW), 0) + (b * tq + 1)).astype(F32)
        pooled = sw / jnp.minimum(pos, win)
        y = _dot((pooled - u).astype(BF16), wp_ref[...]) * ps_ref[...]
        d_ref[0, r, :] = (y * _silu(pg_ref[0, r, :].astype(F32))).astype(BF16)

        qn = (_head_rms(mq_ref[0, r, :].astype(F32), gmq_ref[...], bd) * SCALE).astype(BF16)
        ss = [_dot_nt(qn[:, h * HD:(h + 1) * HD], kh[h]) for h in range(NH)]
        ps = [jnp.exp(s - jnp.max(s, axis=-1, keepdims=True)) for s in ss]
        ls = [jnp.sum(p, axis=-1, keepdims=True) for p in ps]
        outs = [_dot(p.astype(BF16), vh[h]) / l for h, (p, l) in enumerate(zip(ps, ls))]
        oe = jnp.concatenate(outs, axis=1)
        e_ref[0, r, :] = (oe * _silu(mg_ref[0, r, :].astype(F32))).astype(BF16)


def _pm(p3, f3, mem, mng, wkv, gmq, gmk, bd, wp, ps, win, tq):
    b, s, _ = p3.shape
    nm = mem.shape[1]
    kern = functools.partial(_pm_kernel, seq=s, tq=tq)
    c2 = lambda i: (0, 0)
    return pl.pallas_call(
        kern,
        grid=(b,),
        in_specs=[
            pl.BlockSpec((1, s, GW), lambda i: (i, 0, 1)),
            pl.BlockSpec((1, s, GW), lambda i: (i, 0, 11)),
            pl.BlockSpec((1, s, GW), lambda i: (i, 0, 12)),
            pl.BlockSpec((1, s, GW), lambda i: (i, 0, 13)),
            pl.BlockSpec((1, nm, D_MODEL), lambda i: (i, 0, 0)),
            pl.BlockSpec((1, D_MODEL), c2),
            pl.BlockSpec((D_MODEL, 2 * GW), c2),
            pl.BlockSpec((1, GW), c2),
            pl.BlockSpec((1, GW), c2),
            pl.BlockSpec((GW, GW), c2),
            pl.BlockSpec((GW, GW), c2),
            pl.BlockSpec((1, GW), c2),
            pl.BlockSpec((1, GW), c2),
        ],
        out_specs=[
            pl.BlockSpec((1, s, GW), lambda i: (i, 0, 0)),
            pl.BlockSpec((1, s, GW), lambda i: (i, 0, 0)),
        ],
        out_shape=[
            jax.ShapeDtypeStruct((b, s, GW), BF16),
            jax.ShapeDtypeStruct((b, s, GW), BF16),
        ],
        scratch_shapes=[
            pltpu.VMEM((s + 16, GW), F32),
            pltpu.VMEM((3, s + 16, GW), F32),
            pltpu.VMEM((NH, nm, HD), BF16),
            pltpu.VMEM((NH, nm, HD), BF16),
        ],
        compiler_params=pltpu.CompilerParams(
            dimension_semantics=("arbitrary",), vmem_limit_bytes=VMEM_LIMIT),
    )(f3, p3, p3, p3, mem, mng, wkv, gmq, gmk, bd, wp, ps, win)


def _out_kernel(a_ref, b_ref, c_ref, d_ref, e_ref, w_ref, x_ref, o_ref):
    mixed = jnp.concatenate([a_ref[...], b_ref[...], c_ref[...], d_ref[...], e_ref[...]], axis=1)
    o_ref[...] = x_ref[...] + _dot(mixed, w_ref[...])


def _out(parts, w, x2, tm):
    m = x2.shape[0]
    gspec = pl.BlockSpec((tm, GW), lambda i: (i, 0))
    return pl.pallas_call(
        _out_kernel,
        grid=(m // tm,),
        in_specs=[gspec] * 5 + [
            pl.BlockSpec((5 * GW, D_MODEL), lambda i: (0, 0)),
            pl.BlockSpec((tm, D_MODEL), lambda i: (i, 0)),
        ],
        out_specs=pl.BlockSpec((tm, D_MODEL), lambda i: (i, 0)),
        out_shape=jax.ShapeDtypeStruct((m, D_MODEL), F32),
        compiler_params=pltpu.CompilerParams(
            dimension_semantics=("arbitrary",), vmem_limit_bytes=VMEM_LIMIT),
    )(*parts, w, x2)


def _tile_heads(g):
    return jnp.tile(g.astype(F32), NH).reshape(1, GW)


def kernel(x, mem, norm_g, w_in, fox_f_bias, fox_q_norm, fox_k_norm, hgrn_lb_logits, hgrn_out_norm,
           pool_w, pool_scale, mem_norm_g, mem_w_kv, mem_q_norm, mem_k_norm, w_out):
    bsz, seq, _ = x.shape
    depth = w_in.shape[0]
    m = bsz * seq
    tq = 256
    tm = 512

    pr = jax.nn.softmax(hgrn_lb_logits.astype(F32), axis=0)
    lower_bounds = jnp.clip(jnp.cumsum(pr, axis=0) - pr[0:1], 0.0, 1.0 - 1e-6)

    bd_np = _bd_ones()
    bd = jnp.asarray(bd_np, BF16)
    tri = jnp.asarray(np.tril(np.ones((tq, tq), np.float32)), BF16)
    jj = np.arange(TQ)
    suffix = (jj[:, None] > jj[None, :]).astype(np.float32)
    to_half = np.concatenate([suffix, np.ones((TQ, TQ), np.float32)], axis=1)
    to = jnp.asarray(np.concatenate([to_half, to_half], axis=0), BF16)
    a_np, masks_np, hm_np = _hgrn_constants()
    a_all = jnp.asarray(a_np, BF16)
    masks = jnp.asarray(masks_np, F32)
    hm = jnp.asarray(hm_np, BF16)
    win = jnp.asarray(np.repeat(np.array(POOL_WINDOWS, np.float32), HD).reshape(1, GW))

    g = GW
    wb_all, wf_all = _regroup_w_in(w_in)
    x2 = x.reshape(m, D_MODEL)
    for l in range(depth):
        pb, pf = _proj(x2, norm_g[l].reshape(1, D_MODEL).astype(F32), wb_all[l], wf_all[l], tm)
        p3 = pb.reshape(bsz, seq, NP_COLS)
        f3 = pf.reshape(bsz, seq, NF_COLS)

        bias = jnp.concatenate([fox_f_bias[l].astype(F32), jnp.zeros((128 - NH,), F32)]).reshape(1, 128)
        out_a = _fox(p3, f3, bias, _tile_heads(fox_q_norm[l]), _tile_heads(fox_k_norm[l]), bd, tri)
        out_b = _sb(p3, to)

        lb = lower_bounds[l].reshape(1, g)
        out_c = _hgrn(p3, f3, jnp.maximum(lb, LB_FLOOR), 1.0 - lb,
                      hgrn_out_norm[l].reshape(1, g).astype(F32), bd, a_all, masks, hm)

        wp = jax.scipy.linalg.block_diag(*[pool_w[l, i] for i in range(len(POOL_WINDOWS))]).astype(BF16)
        out_d, out_e = _pm(p3, f3, mem, mem_norm_g[l].reshape(1, D_MODEL).astype(F32),
                           mem_w_kv[l].astype(BF16), _tile_heads(mem_q_norm[l]), _tile_heads(mem_k_norm[l]),
                           bd, wp, pool_scale[l].reshape(1, g).astype(F32), win, tq)

        x2 = _out([out_a.reshape(m, g), out_b.reshape(m, g), out_c.reshape(m, g),
                   out_d.reshape(m, g), out_e.reshape(m, g)], w_out[l].astype(BF16), x2, tm)
    return x2.reshape(bsz, seq, D_MODEL)
```

```python
import functools

import numpy as np
import jax
import jax.numpy as jnp
from jax import lax
from jax.experimental import pallas as pl
from jax.experimental.pallas import tpu as pltpu

F32 = jnp.float32
BF16 = jnp.bfloat16

D_MODEL = 1024
GW = 256
NH = 4
HD = 64
CHUNK = 64
POOL_WINDOWS = (2, 4, 8, 16)
EPS = 1e-6
NEG_BIG = -1e30
LB_FLOOR = 1e-30
SCALE = HD ** -0.5

NP_COLS = 14 * GW
NF_COLS = 2 * GW + 128
VMEM_LIMIT = 56 * 1024 * 1024

TQ = 128
WIDE = 3 * TQ
EXP_ZERO = -104.0
NORM_SLACK = 1.01
LOGIT_SLACK = 0.05
HSB = 4


def _dot(a, b):
    return jnp.dot(a, b, preferred_element_type=F32)


def _dot_nt(a, b):
    return lax.dot_general(a, b, (((1,), (1,)), ((), ())), preferred_element_type=F32)


def _dot_tn(a, b):
    return lax.dot_general(a, b, (((0,), (0,)), ((), ())), preferred_element_type=F32)


def _split_bf16(x, n):
    parts = []
    r = x
    for i in range(n):
        p = r.astype(BF16)
        parts.append(p)
        if i + 1 < n:
            r = r - p.astype(F32)
    return parts


def _dot_f32_rhs01(x, m01, n=3):
    acc = None
    for p in _split_bf16(x, n):
        t = _dot(p, m01)
        acc = t if acc is None else acc + t
    return acc


def _dot_f32_lhs01(m01, x, n=3):
    acc = None
    for p in _split_bf16(x, n):
        t = _dot(m01, p)
        acc = t if acc is None else acc + t
    return acc


def _silu(x):
    return x / (1.0 + jnp.exp(-x))


def _log_sigmoid(x):
    return jnp.minimum(x, 0.0) - jnp.log(1.0 + jnp.exp(-jnp.abs(x)))


def _head_rms(x, gain, bd):
    ss = _dot((x * x).astype(BF16), bd)
    return x * lax.rsqrt(ss * (1.0 / HD) + EPS) * gain


def _bd_ones():
    h = np.arange(GW) // HD
    return (h[:, None] == h[None, :]).astype(np.float32)


def _hgrn_constants():
    c = CHUNK
    t = np.arange(c)
    j = np.arange(c)[None, :]
    mats = []
    masks = []
    for n in (64, 32, 16, 8, 4):
        blk, pos = t // n, t % n
        ref = blk * n + n // 2 - 1
        aq = (pos[:, None] >= n // 2) & (j > ref[:, None]) & (j <= t[:, None])
        ak = (pos[:, None] < n // 2) & (j > t[:, None]) & (j <= ref[:, None])
        mats.append((aq | ak).astype(np.float32))
        m = (blk[:, None] == blk[None, :]) & (pos[:, None] >= n // 2) & (pos[None, :] < n // 2)
        masks.append(m.astype(np.float32))
    m = ((t[:, None] // 2) == (t[None, :] // 2)) & (t[None, :] <= t[:, None])
    masks.append(m.astype(np.float32))
    mats.append((j <= t[:, None]).astype(np.float32))
    mats.append((j > t[:, None]).astype(np.float32))
    a_all = np.concatenate(mats, axis=0)
    a_all = np.concatenate([a_all, a_all], axis=1)
    masks = np.stack([np.tile(m, (1, NH)) for m in masks])
    total = masks[:, :, :c].sum(0)
    assert np.array_equal(total, np.tril(np.ones((c, c), np.float32)))
    hm = (np.arange(NH * c)[:, None] // c == np.arange(GW)[None, :] // HD).astype(np.float32)
    return a_all, masks, hm


def _regroup_kernel(w_ref, wb_ref, wf_ref):
    g = GW
    o_ff, o_sb, o_hg, o_pl, o_mm = 4 * g, 4 * g + NH, 8 * g + NH, 12 * g + NH, 14 * g + NH
    w = w_ref[0]
    bf_src = [0, g, 2 * g, 3 * g,
              o_sb, o_sb + g, o_sb + 2 * g, o_sb + 3 * g,
              o_hg, o_hg + 2 * g, o_hg + 3 * g,
              o_pl + g,
              o_mm, o_mm + g]
    for i, c0 in enumerate(bf_src):
        wb_ref[0, :, i * g:(i + 1) * g] = w[:, c0:c0 + g].astype(BF16)
    wf_ref[0, :, 0:g] = w[:, o_hg + g:o_hg + 2 * g].astype(BF16)
    wf_ref[0, :, g:2 * g] = w[:, o_pl:o_pl + g].astype(BF16)
    ff = jnp.concatenate([w[:, o_ff:o_ff + NH], jnp.zeros((w.shape[0], 128 - NH), F32)], axis=1)
    wf_ref[0, :, 2 * g:2 * g + 128] = ff.astype(BF16)


def _regroup_w_in(w_in):
    depth, d, n = w_in.shape
    tr = 128
    return pl.pallas_call(
        _regroup_kernel,
        grid=(depth, d // tr),
        in_specs=[pl.BlockSpec((1, tr, n), lambda l, i: (l, i, 0))],
        out_specs=[
            pl.BlockSpec((1, tr, NP_COLS), lambda l, i: (l, i, 0)),
            pl.BlockSpec((1, tr, NF_COLS), lambda l, i: (l, i, 0)),
        ],
        out_shape=[
            jax.ShapeDtypeStruct((depth, d, NP_COLS), BF16),
            jax.ShapeDtypeStruct((depth, d, NF_COLS), BF16),
        ],
        compiler_params=pltpu.CompilerParams(
            dimension_semantics=("arbitrary", "arbitrary"), vmem_limit_bytes=VMEM_LIMIT),
    )(w_in)


def _proj_kernel(x_ref, g_ref, wb_ref, wf_ref, pb_ref, pf_ref):
    x = x_ref[...]
    ms = jnp.mean(x * x, axis=-1, keepdims=True)
    h = (x * lax.rsqrt(ms + EPS) * g_ref[...]).astype(BF16)
    nb = 512
    for c0 in range(0, NP_COLS, nb):
        pb_ref[:, c0:c0 + nb] = _dot(h, wb_ref[:, c0:c0 + nb]).astype(BF16)
    pf_ref[...] = _dot(h, wf_ref[...])


def _proj(x2, g, wb, wf, tm):
    m = x2.shape[0]
    return pl.pallas_call(
        _proj_kernel,
        grid=(m // tm,),
        in_specs=[
            pl.BlockSpec((tm, D_MODEL), lambda i: (i, 0)),
            pl.BlockSpec((1, D_MODEL), lambda i: (0, 0)),
            pl.BlockSpec((D_MODEL, NP_COLS), lambda i: (0, 0)),
            pl.BlockSpec((D_MODEL, NF_COLS), lambda i: (0, 0)),
        ],
        out_specs=[
            pl.BlockSpec((tm, NP_COLS), lambda i: (i, 0)),
            pl.BlockSpec((tm, NF_COLS), lambda i: (i, 0)),
        ],
        out_shape=[
            jax.ShapeDtypeStruct((m, NP_COLS), BF16),
            jax.ShapeDtypeStruct((m, NF_COLS), F32),
        ],
        compiler_params=pltpu.CompilerParams(
            dimension_semantics=("arbitrary",), vmem_limit_bytes=VMEM_LIMIT),
    )(x2, g, wb, wf)


def _fox_kernel(p_ref, ff_ref, bias_ref, gq_ref, gk_ref, bd_ref, tri_ref, o_ref,
                qs, ks, vx, ccol, crow, qk_buf, *, seq):
    bd = bd_ref[...]
    tri = tri_ref[...]
    pb = tri.shape[0]
    hmask = [bd[h * HD:h * HD + 1, :] for h in range(NH)]
    hmask_f = [m.astype(F32) for m in hmask]
    qk_max = (HD * SCALE * NORM_SLACK) * (jnp.max(jnp.abs(gq_ref[...]), axis=-1, keepdims=True)
                                          * jnp.max(jnp.abs(gk_ref[...]), axis=-1, keepdims=True))
    carry = jnp.zeros((1, 128), F32)
    for b in range(seq // pb):
        r = slice(b * pb, (b + 1) * pb)
        lf = _log_sigmoid(ff_ref[0, r, :] + bias_ref[...])
        cb = _dot_f32_lhs01(tri, lf, 3) + carry
        carry = cb[pb - 1:pb, :]
        ccol[r, :] = cb
        crow[:, r] = cb.T[:8, :]
        q = _head_rms(p_ref[0, r, 0:GW].astype(F32), gq_ref[...], bd) * SCALE
        k = _head_rms(p_ref[0, r, GW:2 * GW].astype(F32), gk_ref[...], bd)
        qs[r, :] = q.astype(BF16)
        ks[r, :] = k.astype(BF16)
        v = p_ref[0, r, 2 * GW:3 * GW]
        for h in range(NH):
            vx[h, r, :] = v * hmask[h]

    lane = lax.broadcasted_iota(jnp.int32, (1, 128), 1)
    rel_w = (lax.broadcasted_iota(jnp.int32, (TQ, WIDE), 1) - lax.broadcasted_iota(jnp.int32, (TQ, WIDE), 0))
    nfull = (WIDE - TQ) // TQ

    def qk_all_heads(r0, start, width):
        qb = qs[pl.ds(r0, TQ), :]
        qx = jnp.concatenate([qb * m for m in hmask], axis=0)
        return _dot_nt(qx, ks[pl.ds(start, width), :])

    def wide_start(r0):
        return pl.multiple_of(jnp.maximum(r0 - (WIDE - TQ), 0), TQ)

    def wide_qk(i):
        r0 = pl.multiple_of(i * TQ, TQ)
        return qk_all_heads(r0, wide_start(r0), WIDE)

    def q_block(i, full_tile, qk_wide):
        r0 = pl.multiple_of(i * TQ, TQ)
        s0 = wide_start(r0)
        rows = pl.ds(r0, TQ)
        cq = ccol[rows, :]

        def mask_wide(s):
            if full_tile:
                diag = jnp.where(rel_w[:, WIDE - TQ:] <= WIDE - TQ, s[:, WIDE - TQ:], NEG_BIG)
                return jnp.concatenate([s[:, :WIDE - TQ], diag], axis=1)
            return jnp.where(rel_w <= r0 - s0, s, NEG_BIG)
        bq = cq + (qk_max + LOGIT_SLACK)

        def tiles(qk, start, width):
            return [qk[h * TQ:(h + 1) * TQ] + (cq[:, h:h + 1] - crow[h:h + 1, pl.ds(start, width)])
                    for h in range(NH)]

        def weighted_values(ps, start, width):
            p_all = jnp.concatenate([p.astype(BF16) for p in ps], axis=1)
            v_all = jnp.concatenate([vx[h, pl.ds(start, width), :] for h in range(NH)], axis=0)
            return _dot(p_all, v_all)

        def per_head_lanes(cols):
            out = cols[0] * hmask_f[0]
            for h in range(1, NH):
                out = out + cols[h] * hmask_f[h]
            return out

        def keep_going(s_end, ms):
            mc = jnp.zeros((TQ, 128), F32)
            for h in range(NH):
                mc = jnp.where(lane == h, ms[h], mc)
            c_last = ccol[pl.ds(jnp.maximum(s_end - 1, 0), 1), :]
            bound = jnp.max(bq - mc, axis=0, keepdims=True) - c_last
            bound = jnp.where(lane < NH, bound, NEG_BIG)
            return (jnp.max(bound) > EXP_ZERO).astype(jnp.int32)

        ss = [mask_wide(s) for s in tiles(qk_wide, s0, WIDE)]
        ms = [jnp.max(s, axis=-1, keepdims=True) for s in ss]
        ps = [jnp.exp(s - m) for s, m in zip(ss, ms)]
        ls = [jnp.sum(p, axis=-1, keepdims=True) for p in ps]
        acc = weighted_values(ps, s0, WIDE)

        def emit(acc, ls):
            g = p_ref[0, rows, 3 * GW:4 * GW].astype(F32)
            o_ref[0, rows, :] = (acc * per_head_lanes([1.0 / l for l in ls]) * _silu(g)).astype(BF16)

        emit(acc, ls)

        def cond(st):
            return jnp.logical_and(st[0] > 0, st[1] > 0)

        def body(st):
            s_end, _, ms, ls, acc = st
            sb = pl.multiple_of(s_end - TQ, TQ)
            ss = tiles(qk_all_heads(r0, sb, TQ), sb, TQ)
            ms2 = [jnp.maximum(m, jnp.max(s, axis=-1, keepdims=True)) for s, m in zip(ss, ms)]
            ps = [jnp.exp(s - m) for s, m in zip(ss, ms2)]
            alphas = [jnp.exp(m - m2) for m, m2 in zip(ms, ms2)]
            ls2 = [a * l + jnp.sum(p, axis=-1, keepdims=True) for a, l, p in zip(alphas, ls, ps)]
            acc2 = acc * per_head_lanes(alphas) + weighted_values(ps, sb, TQ)
            return sb, keep_going(sb, ms2), tuple(ms2), tuple(ls2), acc2

        if full_tile:
            @pl.when(jnp.logical_and(s0 > 0, keep_going(s0, ms) > 0))
            def _():
                _, _, _, ls_f, acc_f = lax.while_loop(cond, body, (s0, jnp.int32(1), tuple(ms), tuple(ls), acc))
                emit(acc_f, ls_f)
        return 0

    lax.fori_loop(0, nfull, lambda i, _: q_block(i, False, wide_qk(i)), 0)

    nblk = seq // TQ
    qk_buf[0] = wide_qk(nfull)

    def block_pair(j, _):
        i = nfull + 2 * j
        qk_buf[1] = wide_qk(i + 1)
        q_block(i, True, qk_buf[0])
        qk_buf[0] = wide_qk(jnp.minimum(i + 2, nblk - 1))
        q_block(i + 1, True, qk_buf[1])
        return 0

    lax.fori_loop(0, (nblk - nfull) // 2, block_pair, 0)


def _fox(p3, f3, bias, gq, gk, bd, tri):
    b, s, _ = p3.shape
    assert s % tri.shape[0] == 0 and s >= WIDE and (s // TQ - (WIDE - TQ) // TQ) % 2 == 0
    kern = functools.partial(_fox_kernel, seq=s)
    c2 = lambda i: (0, 0)
    return pl.pallas_call(
        kern,
        grid=(b,),
        in_specs=[
            pl.BlockSpec((1, s, 4 * GW), lambda i: (i, 0, 0)),
            pl.BlockSpec((1, s, 128), lambda i: (i, 0, 4)),
            pl.BlockSpec((1, 128), c2),
            pl.BlockSpec((1, GW), c2),
            pl.BlockSpec((1, GW), c2),
            pl.BlockSpec((GW, GW), c2),
            pl.BlockSpec(tri.shape, c2),
        ],
        out_specs=pl.BlockSpec((1, s, GW), lambda i: (i, 0, 0)),
        out_shape=jax.ShapeDtypeStruct((b, s, GW), BF16),
        scratch_shapes=[
            pltpu.VMEM((s, GW), BF16),
            pltpu.VMEM((s, GW), BF16),
            pltpu.VMEM((NH, s, GW), BF16),
            pltpu.VMEM((s, 128), F32),
            pltpu.VMEM((8, s), F32),
            pltpu.VMEM((2, NH * TQ, WIDE), F32),
        ],
        compiler_params=pltpu.CompilerParams(
            dimension_semantics=("arbitrary",), vmem_limit_bytes=VMEM_LIMIT),
    )(p3, f3, bias, gq, gk, bd, tri)


def _sb_kernel(p_ref, to_ref, bd_ref, o_ref, vx, qk_buf, *, seq):
    to = to_ref[...]
    hmask = [bd_ref[h * HD:h * HD + 1, :] for h in range(NH)]
    qmask = [m * SCALE for m in hmask]
    pb = 256
    for b in range(seq // pb):
        r = slice(b * pb, (b + 1) * pb)
        v = p_ref[0, r, 2 * GW:3 * GW]
        for h in range(NH):
            vx[h, r, :] = v * hmask[h]

    rel_w = (lax.broadcasted_iota(jnp.int32, (TQ, WIDE), 1) - lax.broadcasted_iota(jnp.int32, (TQ, WIDE), 0))
    nsub = WIDE // TQ
    nfull = nsub - 1

    def log_one_minus_sigmoid(z):
        return -(jnp.maximum(z, 0.0) + jnp.log(1.0 + jnp.exp(-jnp.abs(z))))

    def suffix_sums(lom):
        hi = lom.astype(BF16)
        lo = (lom - hi.astype(F32)).astype(BF16)
        cs = _dot(jnp.concatenate([hi, lo], axis=1), to)
        return cs[:, :TQ], cs[:, TQ:]

    def qk_all_heads(r0, start, width):
        qb = p_ref[0, pl.ds(r0, TQ), 0:GW]
        qx = jnp.concatenate([qb * m for m in qmask], axis=0)
        return _dot_nt(qx, p_ref[0, pl.ds(start, width), GW:2 * GW])

    def wide_start(r0):
        return pl.multiple_of(jnp.maximum(r0 - (WIDE - TQ), 0), TQ)

    def wide_qk(i):
        r0 = pl.multiple_of(i * TQ, TQ)
        return qk_all_heads(r0, wide_start(r0), WIDE)

    def q_block(i, full_tile, qk_wide):
        r0 = pl.multiple_of(i * TQ, TQ)
        s0 = wide_start(r0)
        rows = pl.ds(r0, TQ)

        def mask_wide(x):
            if full_tile:
                diag = jnp.where(rel_w[:, WIDE - TQ:] < WIDE - TQ, x[:, WIDE - TQ:], 0.0)
                return jnp.concatenate([x[:, :WIDE - TQ], diag], axis=1)
            return jnp.where(rel_w < r0 - s0, x, 0.0)

        def per_head(qk):
            return [qk[h * TQ:(h + 1) * TQ] for h in range(NH)]

        def weighted_values(ws, start, width):
            w_all = jnp.concatenate([w.astype(BF16) for w in ws], axis=1)
            v_all = jnp.concatenate([vx[h, pl.ds(start, width), :] for h in range(NH)], axis=0)
            return _dot(w_all, v_all)

        def emit(acc):
            g = p_ref[0, rows, 3 * GW:4 * GW].astype(F32)
            o_ref[0, rows, :] = (acc * _silu(g)).astype(BF16)

        zs = per_head(qk_wide)
        loms = [mask_wide(log_one_minus_sigmoid(z)) for z in zs]
        sums = [[suffix_sums(lom[:, c * TQ:(c + 1) * TQ]) for c in range(nsub)] for lom in loms]
        carries, ws = [], []
        for h in range(NH):
            between = [None] * nsub
            carry = jnp.zeros((TQ, TQ), F32)
            for c in reversed(range(nsub)):
                rc, tot = sums[h][c]
                between[c] = rc + carry
                carry = carry + tot
            ws.append(mask_wide(jnp.exp(zs[h] + loms[h] + jnp.concatenate(between, axis=1))))
            carries.append(carry)
        acc = weighted_values(ws, s0, WIDE)
        emit(acc)

        def keep_going(carries):
            cm = jnp.maximum(jnp.maximum(carries[0], carries[1]), jnp.maximum(carries[2], carries[3]))
            return (jnp.max(cm) > EXP_ZERO).astype(jnp.int32)

        def cond(st):
            return jnp.logical_and(st[0] > 0, st[1] > 0)

        def body(st):
            s_end, _, carries, acc = st
            sb = pl.multiple_of(s_end - TQ, TQ)
            zs = per_head(qk_all_heads(r0, sb, TQ))
            loms = [log_one_minus_sigmoid(z) for z in zs]
            sums = [suffix_sums(lom) for lom in loms]
            ws = [jnp.exp(z + lom + rc + cr) for z, lom, (rc, _), cr in zip(zs, loms, sums, carries)]
            c2 = [cr + tot for cr, (_, tot) in zip(carries, sums)]
            return sb, keep_going(c2), tuple(c2), acc + weighted_values(ws, sb, TQ)

        if full_tile:
            @pl.when(jnp.logical_and(s0 > 0, keep_going(carries) > 0))
            def _():
                st = lax.while_loop(cond, body, (s0, jnp.int32(1), tuple(carries), acc))
                emit(st[3])
        return 0

    lax.fori_loop(0, nfull, lambda i, _: q_block(i, False, wide_qk(i)), 0)

    nblk = seq // TQ
    qk_buf[0] = wide_qk(nfull)

    def block_pair(j, _):
        i = nfull + 2 * j
        qk_buf[1] = wide_qk(i + 1)
        q_block(i, True, qk_buf[0])
        qk_buf[0] = wide_qk(jnp.minimum(i + 2, nblk - 1))
        q_block(i + 1, True, qk_buf[1])
        return 0

    lax.fori_loop(0, (nblk - nfull) // 2, block_pair, 0)


def _sb(p3, to, bd):
    b, s, _ = p3.shape
    assert s % 256 == 0 and s >= WIDE and (s // TQ - (WIDE - TQ) // TQ) % 2 == 0
    kern = functools.partial(_sb_kernel, seq=s)
    return pl.pallas_call(
        kern,
        grid=(b,),
        in_specs=[
            pl.BlockSpec((1, s, 4 * GW), lambda i: (i, 0, 1)),
            pl.BlockSpec((2 * TQ, 2 * TQ), lambda i: (0, 0)),
            pl.BlockSpec((GW, GW), lambda i: (0, 0)),
        ],
        out_specs=pl.BlockSpec((1, s, GW), lambda i: (i, 0, 0)),
        out_shape=jax.ShapeDtypeStruct((b, s, GW), BF16),
        scratch_shapes=[
            pltpu.VMEM((NH, s, GW), BF16),
            pltpu.VMEM((2, NH * TQ, WIDE), F32),
        ],
        compiler_params=pltpu.CompilerParams(
            dimension_semantics=("arbitrary",), vmem_limit_bytes=VMEM_LIMIT),
    )(p3, to, bd)


def _hgrn_kernel(hq_ref, hi_ref, hg_ref, hf_ref, lbm_ref, oml_ref, gout_ref, bd_ref, a_ref,
                 mask_ref, hm_ref, o_ref, st_ref, *, seq):
    bd = bd_ref[...]
    a_all = a_ref[...]
    hm = hm_ref[...]
    lbm = lbm_ref[...]
    oml = oml_ref[...]
    c = CHUNK
    st_ref[...] = jnp.zeros((GW, GW), F32)

    n = HSB * c
    nlev = mask_ref.shape[0] - 1
    bdf = bd.astype(F32)
    odd = (lax.broadcasted_iota(jnp.int32, (n, GW), 0) & 1) == 1

    def superblock(bi, _):
        r0 = pl.multiple_of(bi * n, n)
        rows = pl.ds(r0, n)
        hf = hf_ref[0, rows, :]
        e = jnp.exp(-jnp.abs(hf))
        rr = 1.0 / (1.0 + e)
        sg = jnp.where(hf >= 0, rr, e * rr)
        sgn = jnp.where(hf >= 0, e * rr, rr)
        f = lbm + oml * sg
        g = jnp.log(f)
        kk = oml * sgn
        q = _silu(hq_ref[0, rows, :].astype(F32))
        v = hi_ref[0, rows, :]
        gh = g.astype(BF16)
        gl = (g - gh.astype(F32)).astype(BF16)
        q2 = (q * jnp.where(odd, f, 1.0)).astype(BF16)
        k2 = (kk * jnp.where(odd, 1.0 / f, 1.0)).astype(BF16)

        def scores(qf, kf, l):
            kx = jnp.concatenate([kf] * NH, axis=0) * hm
            return _dot_nt(qf, kx) * mask_ref[l]

        sls = [slice(ci * c, (ci + 1) * c) for ci in range(HSB)]
        exs = [_dot(a_all, jnp.concatenate([gh[sl], gl[sl]], axis=0)) for sl in sls]
        ps = [scores(q2[sl], k2[sl], nlev) for sl in sls]
        for l in range(nlev):
            for ci, sl in enumerate(sls):
                x = jnp.exp(exs[ci][l * c:(l + 1) * c])
                ps[ci] = ps[ci] + scores((q[sl] * x).astype(BF16), (kk[sl] * x).astype(BF16), l)
        o_intra, qd, upd, dl = [], [], [], []
        for ci, sl in enumerate(sls):
            vx = jnp.concatenate([v[sl]] * NH, axis=0) * hm
            o_intra.append(_dot(ps[ci].astype(BF16), vx))
            eb = exs[ci][nlev * c:(nlev + 1) * c]
            er = exs[ci][(nlev + 1) * c:(nlev + 2) * c]
            qd.append((q[sl] * jnp.exp(eb)).astype(BF16))
            upd.append(_dot_tn(v[sl], (kk[sl] * jnp.exp(er)).astype(BF16)) * bdf)
            dl.append(jnp.exp(eb[c - 1:c, :]))

        st = st_ref[...]
        outs = []
        for ci in range(HSB):
            outs.append(o_intra[ci] + _dot_nt(qd[ci], st.astype(BF16)))
            st = st * dl[ci] + upd[ci]
        st_ref[...] = st

        o = _head_rms(jnp.concatenate(outs, axis=0), gout_ref[...], bd)
        o_ref[0, rows, :] = (o * _silu(hg_ref[0, rows, :].astype(F32))).astype(BF16)
        return 0

    lax.fori_loop(0, seq // n, superblock, 0)


def _hgrn(p3, f3, lbm, oml, gout, bd, a_all, masks, hm):
    b, s, _ = p3.shape
    assert s % (HSB * CHUNK) == 0
    kern = functools.partial(_hgrn_kernel, seq=s)
    c2 = lambda i: (0, 0)
    return pl.pallas_call(
        kern,
        grid=(b,),
        in_specs=[
            pl.BlockSpec((1, s, GW), lambda i: (i, 0, 8)),
            pl.BlockSpec((1, s, GW), lambda i: (i, 0, 9)),
            pl.BlockSpec((1, s, GW), lambda i: (i, 0, 10)),
            pl.BlockSpec((1, s, GW), lambda i: (i, 0, 0)),
            pl.BlockSpec((1, GW), c2),
            pl.BlockSpec((1, GW), c2),
            pl.BlockSpec((1, GW), c2),
            pl.BlockSpec((GW, GW), c2),
            pl.BlockSpec(a_all.shape, c2),
            pl.BlockSpec(masks.shape, lambda i: (0, 0, 0)),
            pl.BlockSpec(hm.shape, c2),
        ],
        out_specs=pl.BlockSpec((1, s, GW), lambda i: (i, 0, 0)),
        out_shape=jax.ShapeDtypeStruct((b, s, GW), BF16),
        scratch_shapes=[pltpu.VMEM((GW, GW), F32)],
        compiler_params=pltpu.CompilerParams(
            dimension_semantics=("arbitrary",), vmem_limit_bytes=VMEM_LIMIT),
    )(p3, p3, p3, f3, lbm, oml, gout, bd, a_all, masks, hm)


def _pm_kernel(pv_ref, pg_ref, mq_ref, mg_ref, mem_ref, mng_ref, wkv_ref, gmq_ref, gmk_ref, bd_ref,
               wp_ref, ps_ref, win_ref, d_ref, e_ref, ubuf, wbuf, kh, vh, *, seq, tq):
    bd = bd_ref[...]
    halo = 16
    mem = mem_ref[0]
    ms = jnp.mean(mem * mem, axis=-1, keepdims=True)
    mn = (mem * lax.rsqrt(ms + EPS) * mng_ref[...]).astype(BF16)
    kv = _dot(mn, wkv_ref[...])
    kn = _head_rms(kv[:, :GW], gmk_ref[...], bd).astype(BF16)
    vv = kv[:, GW:].astype(BF16)
    for h in range(NH):
        kh[h] = kn[:, h * HD:(h + 1) * HD]
        vh[h] = vv[:, h * HD:(h + 1) * HD]

    ubuf[0:halo, :] = jnp.zeros((halo, GW), F32)
    ubuf[halo:halo + seq, :] = pv_ref[0]
    win = win_ref[...]
    for k in range(3):
        sh = 1 << k
        wbuf[k, 0:halo, :] = jnp.zeros((halo, GW), F32)
        for b in range(seq // tq):
            base = halo + b * tq
            if k == 0:
                wbuf[k, base:base + tq, :] = ubuf[base:base + tq, :] + ubuf[base - sh:base - sh + tq, :]
            else:
                wbuf[k, base:base + tq, :] = (wbuf[k - 1, base:base + tq, :]
                                              + wbuf[k - 1, base - sh:base - sh + tq, :])

    for b in range(seq // tq):
        r = slice(b * tq, (b + 1) * tq)
        base = halo + b * tq
        u = ubuf[base:base + tq, :]
        s2 = wbuf[0, base:base + tq, :]
        s4 = wbuf[1, base:base + tq, :]
        s8 = wbuf[2, base:base + tq, :]
        s16 = s8 + wbuf[2, base - 8:base - 8 + tq, :]
        sw = jnp.where(win == 2.0, s2, jnp.where(win == 4.0, s4, jnp.where(win == 8.0, s8, s16)))
        pos = (lax.broadcasted_iota(jnp.int32, (tq, GW), 0) + (b * tq + 1)).astype(F32)
        pooled = sw / jnp.minimum(pos, win)
        y = _dot((pooled - u).astype(BF16), wp_ref[...]) * ps_ref[...]
        d_ref[0, r, :] = (y * _silu(pg_ref[0, r, :].astype(F32))).astype(BF16)

        qn = (_head_rms(mq_ref[0, r, :].astype(F32), gmq_ref[...], bd) * SCALE).astype(BF16)
        ss = [_dot_nt(qn[:, h * HD:(h + 1) * HD], kh[h]) for h in range(NH)]
        ps = [jnp.exp(s - jnp.max(s, axis=-1, keepdims=True)) for s in ss]
        ls = [jnp.sum(p, axis=-1, keepdims=True) for p in ps]
        outs = [_dot(p.astype(BF16), vh[h]) / l for h, (p, l) in enumerate(zip(ps, ls))]
        oe = jnp.concatenate(outs, axis=1)
        e_ref[0, r, :] = (oe * _silu(mg_ref[0, r, :].astype(F32))).astype(BF16)


def _pm(p3, f3, mem, mng, wkv, gmq, gmk, bd, wp, ps, win, tq):
    b, s, _ = p3.shape
    nm = mem.shape[1]
    kern = functools.partial(_pm_kernel, seq=s, tq=tq)
    c2 = lambda i: (0, 0)
    return pl.pallas_call(
        kern,
        grid=(b,),
        in_specs=[
            pl.BlockSpec((1, s, GW), lambda i: (i, 0, 1)),
            pl.BlockSpec((1, s, GW), lambda i: (i, 0, 11)),
            pl.BlockSpec((1, s, GW), lambda i: (i, 0, 12)),
            pl.BlockSpec((1, s, GW), lambda i: (i, 0, 13)),
            pl.BlockSpec((1, nm, D_MODEL), lambda i: (i, 0, 0)),
            pl.BlockSpec((1, D_MODEL), c2),
            pl.BlockSpec((D_MODEL, 2 * GW), c2),
            pl.BlockSpec((1, GW), c2),
            pl.BlockSpec((1, GW), c2),
            pl.BlockSpec((GW, GW), c2),
            pl.BlockSpec((GW, GW), c2),
            pl.BlockSpec((1, GW), c2),
            pl.BlockSpec((1, GW), c2),
        ],
        out_specs=[
            pl.BlockSpec((1, s, GW), lambda i: (i, 0, 0)),
            pl.BlockSpec((1, s, GW), lambda i: (i, 0, 0)),
        ],
        out_shape=[
            jax.ShapeDtypeStruct((b, s, GW), BF16),
            jax.ShapeDtypeStruct((b, s, GW), BF16),
        ],
        scratch_shapes=[
            pltpu.VMEM((s + 16, GW), F32),
            pltpu.VMEM((3, s + 16, GW), F32),
            pltpu.VMEM((NH, nm, HD), BF16),
            pltpu.VMEM((NH, nm, HD), BF16),
        ],
        compiler_params=pltpu.CompilerParams(
            dimension_semantics=("arbitrary",), vmem_limit_bytes=VMEM_LIMIT),
    )(f3, p3, p3, p3, mem, mng, wkv, gmq, gmk, bd, wp, ps, win)


def _out_kernel(a_ref, b_ref, c_ref, d_ref, e_ref, w_ref, x_ref, o_ref):
    mixed = jnp.concatenate([a_ref[...], b_ref[...], c_ref[...], d_ref[...], e_ref[...]], axis=1)
    o_ref[...] = x_ref[...] + _dot(mixed, w_ref[...])


def _out(parts, w, x2, tm):
    m = x2.shape[0]
    gspec = pl.BlockSpec((tm, GW), lambda i: (i, 0))
    return pl.pallas_call(
        _out_kernel,
        grid=(m // tm,),
        in_specs=[gspec] * 5 + [
            pl.BlockSpec((5 * GW, D_MODEL), lambda i: (0, 0)),
            pl.BlockSpec((tm, D_MODEL), lambda i: (i, 0)),
        ],
        out_specs=pl.BlockSpec((tm, D_MODEL), lambda i: (i, 0)),
        out_shape=jax.ShapeDtypeStruct((m, D_MODEL), F32),
        compiler_params=pltpu.CompilerParams(
            dimension_semantics=("arbitrary",), vmem_limit_bytes=VMEM_LIMIT),
    )(*parts, w, x2)


def _tile_heads(g):
    return jnp.tile(g.astype(F32), NH).reshape(1, GW)


def kernel(x, mem, norm_g, w_in, fox_f_bias, fox_q_norm, fox_k_norm, hgrn_lb_logits, hgrn_out_norm,
           pool_w, pool_scale, mem_norm_g, mem_w_kv, mem_q_norm, mem_k_norm, w_out):
    bsz, seq, _ = x.shape
    depth = w_in.shape[0]
    m = bsz * seq
    tq = 256
    tm = 512

    pr = jax.nn.softmax(hgrn_lb_logits.astype(F32), axis=0)
    lower_bounds = jnp.clip(jnp.cumsum(pr, axis=0) - pr[0:1], 0.0, 1.0 - 1e-6)

    bd_np = _bd_ones()
    bd = jnp.asarray(bd_np, BF16)
    tri = jnp.asarray(np.tril(np.ones((tq, tq), np.float32)), BF16)
    jj = np.arange(TQ)
    suffix = (jj[:, None] > jj[None, :]).astype(np.float32)
    to_half = np.concatenate([suffix, np.ones((TQ, TQ), np.float32)], axis=1)
    to = jnp.asarray(np.concatenate([to_half, to_half], axis=0), BF16)
    a_np, masks_np, hm_np = _hgrn_constants()
    a_all = jnp.asarray(a_np, BF16)
    masks = jnp.asarray(masks_np, F32)
    hm = jnp.asarray(hm_np, BF16)
    win = jnp.asarray(np.repeat(np.array(POOL_WINDOWS, np.float32), HD).reshape(1, GW))

    g = GW
    wb_all, wf_all = _regroup_w_in(w_in)
    x2 = x.reshape(m, D_MODEL)
    for l in range(depth):
        pb, pf = _proj(x2, norm_g[l].reshape(1, D_MODEL).astype(F32), wb_all[l], wf_all[l], tm)
        p3 = pb.reshape(bsz, seq, NP_COLS)
        f3 = pf.reshape(bsz, seq, NF_COLS)

        bias = jnp.concatenate([fox_f_bias[l].astype(F32), jnp.zeros((128 - NH,), F32)]).reshape(1, 128)
        out_a = _fox(p3, f3, bias, _tile_heads(fox_q_norm[l]), _tile_heads(fox_k_norm[l]), bd, tri)
        out_b = _sb(p3, to, bd)

        lb = lower_bounds[l].reshape(1, g)
        out_c = _hgrn(p3, f3, jnp.maximum(lb, LB_FLOOR), 1.0 - lb,
                      hgrn_out_norm[l].reshape(1, g).astype(F32), bd, a_all, masks, hm)

        wp = jax.scipy.linalg.block_diag(*[pool_w[l, i] for i in range(len(POOL_WINDOWS))]).astype(BF16)
        out_d, out_e = _pm(p3, f3, mem, mem_norm_g[l].reshape(1, D_MODEL).astype(F32),
                           mem_w_kv[l].astype(BF16), _tile_heads(mem_q_norm[l]), _tile_heads(mem_k_norm[l]),
                           bd, wp, pool_scale[l].reshape(1, g).astype(F32), win, tq)

        x2 = _out([out_a.reshape(m, g), out_b.reshape(m, g), out_c.reshape(m, g),
                   out_d.reshape(m, g), out_e.reshape(m, g)], w_out[l].astype(BF16), x2, tm)
    return x2.reshape(bsz, seq, D_MODEL)
```

```python
import functools

import numpy as np
import jax
import jax.numpy as jnp
from jax import lax
from jax.experimental import pallas as pl
from jax.experimental.pallas import tpu as pltpu

F32 = jnp.float32
BF16 = jnp.bfloat16

D_MODEL = 1024
GW = 256
NH = 4
HD = 64
CHUNK = 64
POOL_WINDOWS = (2, 4, 8, 16)
EPS = 1e-6
NEG_BIG = -1e30
LB_FLOOR = 1e-30
SCALE = HD ** -0.5

NP_COLS = 14 * GW
NF_COLS = 2 * GW + 128
VMEM_LIMIT = 56 * 1024 * 1024

TQ = 128
WIDE = 3 * TQ
EXP_ZERO = -104.0
NORM_SLACK = 1.01
LOGIT_SLACK = 0.05
HSB = 4
SOFTPLUS_LINEAR = 80.0


def _dot(a, b):
    return jnp.dot(a, b, preferred_element_type=F32)


def _dot_nt(a, b):
    return lax.dot_general(a, b, (((1,), (1,)), ((), ())), preferred_element_type=F32)


def _dot_tn(a, b):
    return lax.dot_general(a, b, (((0,), (0,)), ((), ())), preferred_element_type=F32)


def _split_bf16(x, n):
    parts = []
    r = x
    for i in range(n):
        p = r.astype(BF16)
        parts.append(p)
        if i + 1 < n:
            r = r - p.astype(F32)
    return parts


def _dot_f32_rhs01(x, m01, n=3):
    acc = None
    for p in _split_bf16(x, n):
        t = _dot(p, m01)
        acc = t if acc is None else acc + t
    return acc


def _dot_f32_lhs01(m01, x, n=3):
    acc = None
    for p in _split_bf16(x, n):
        t = _dot(m01, p)
        acc = t if acc is None else acc + t
    return acc


def _silu(x):
    return x / (1.0 + jnp.exp(-x))


def _log_sigmoid(x):
    return jnp.minimum(x, 0.0) - jnp.log(1.0 + jnp.exp(-jnp.abs(x)))


def _head_rms(x, gain, bd):
    ss = _dot((x * x).astype(BF16), bd)
    return x * lax.rsqrt(ss * (1.0 / HD) + EPS) * gain


def _bd_ones():
    h = np.arange(GW) // HD
    return (h[:, None] == h[None, :]).astype(np.float32)


def _hgrn_constants():
    c = CHUNK
    t = np.arange(c)
    j = np.arange(c)[None, :]
    mats = []
    masks = []
    for n in (64, 32, 16, 8, 4):
        blk, pos = t // n, t % n
        ref = blk * n + n // 2 - 1
        aq = (pos[:, None] >= n // 2) & (j > ref[:, None]) & (j <= t[:, None])
        ak = (pos[:, None] < n // 2) & (j > t[:, None]) & (j <= ref[:, None])
        mats.append((aq | ak).astype(np.float32))
        m = (blk[:, None] == blk[None, :]) & (pos[:, None] >= n // 2) & (pos[None, :] < n // 2)
        masks.append(m.astype(np.float32))
    m = ((t[:, None] // 2) == (t[None, :] // 2)) & (t[None, :] <= t[:, None])
    masks.append(m.astype(np.float32))
    mats.append((j <= t[:, None]).astype(np.float32))
    mats.append((j > t[:, None]).astype(np.float32))
    a_all = np.concatenate(mats, axis=0)
    a_all = np.concatenate([a_all, a_all], axis=1)
    masks = np.stack([np.tile(m, (1, NH)) for m in masks])
    total = masks[:, :, :c].sum(0)
    assert np.array_equal(total, np.tril(np.ones((c, c), np.float32)))
    hm = (np.arange(NH * c)[:, None] // c == np.arange(GW)[None, :] // HD).astype(np.float32)
    return a_all, masks, hm


def _regroup_kernel(w_ref, wb_ref, wf_ref):
    g = GW
    o_ff, o_sb, o_hg, o_pl, o_mm = 4 * g, 4 * g + NH, 8 * g + NH, 12 * g + NH, 14 * g + NH
    w = w_ref[0]
    bf_src = [0, g, 2 * g, 3 * g,
              o_sb, o_sb + g, o_sb + 2 * g, o_sb + 3 * g,
              o_hg, o_hg + 2 * g, o_hg + 3 * g,
              o_pl + g,
              o_mm, o_mm + g]
    for i, c0 in enumerate(bf_src):
        wb_ref[0, :, i * g:(i + 1) * g] = w[:, c0:c0 + g].astype(BF16)
    wf_ref[0, :, 0:g] = w[:, o_hg + g:o_hg + 2 * g].astype(BF16)
    wf_ref[0, :, g:2 * g] = w[:, o_pl:o_pl + g].astype(BF16)
    ff = jnp.concatenate([w[:, o_ff:o_ff + NH], jnp.zeros((w.shape[0], 128 - NH), F32)], axis=1)
    wf_ref[0, :, 2 * g:2 * g + 128] = ff.astype(BF16)


def _regroup_w_in(w_in):
    depth, d, n = w_in.shape
    tr = 128
    return pl.pallas_call(
        _regroup_kernel,
        grid=(depth, d // tr),
        in_specs=[pl.BlockSpec((1, tr, n), lambda l, i: (l, i, 0))],
        out_specs=[
            pl.BlockSpec((1, tr, NP_COLS), lambda l, i: (l, i, 0)),
            pl.BlockSpec((1, tr, NF_COLS), lambda l, i: (l, i, 0)),
        ],
        out_shape=[
            jax.ShapeDtypeStruct((depth, d, NP_COLS), BF16),
            jax.ShapeDtypeStruct((depth, d, NF_COLS), BF16),
        ],
        compiler_params=pltpu.CompilerParams(
            dimension_semantics=("arbitrary", "arbitrary"), vmem_limit_bytes=VMEM_LIMIT),
    )(w_in)


def _proj_kernel(x_ref, g_ref, wb_ref, wf_ref, pb_ref, pf_ref):
    x = x_ref[...]
    ms = jnp.mean(x * x, axis=-1, keepdims=True)
    h = (x * lax.rsqrt(ms + EPS) * g_ref[...]).astype(BF16)
    nb = 512
    for c0 in range(0, NP_COLS, nb):
        pb_ref[:, c0:c0 + nb] = _dot(h, wb_ref[:, c0:c0 + nb]).astype(BF16)
    pf_ref[...] = _dot(h, wf_ref[...])


def _proj(x2, g, wb, wf, tm):
    m = x2.shape[0]
    return pl.pallas_call(
        _proj_kernel,
        grid=(m // tm,),
        in_specs=[
            pl.BlockSpec((tm, D_MODEL), lambda i: (i, 0)),
            pl.BlockSpec((1, D_MODEL), lambda i: (0, 0)),
            pl.BlockSpec((D_MODEL, NP_COLS), lambda i: (0, 0)),
            pl.BlockSpec((D_MODEL, NF_COLS), lambda i: (0, 0)),
        ],
        out_specs=[
            pl.BlockSpec((tm, NP_COLS), lambda i: (i, 0)),
            pl.BlockSpec((tm, NF_COLS), lambda i: (i, 0)),
        ],
        out_shape=[
            jax.ShapeDtypeStruct((m, NP_COLS), BF16),
            jax.ShapeDtypeStruct((m, NF_COLS), F32),
        ],
        compiler_params=pltpu.CompilerParams(
            dimension_semantics=("arbitrary",), vmem_limit_bytes=VMEM_LIMIT),
    )(x2, g, wb, wf)


def _fox_kernel(p_ref, ff_ref, bias_ref, gq_ref, gk_ref, bd_ref, tri_ref, o_ref,
                qs, ks, vx, ccol, crow, qk_buf, *, seq):
    bd = bd_ref[...]
    tri = tri_ref[...]
    pb = tri.shape[0]
    hmask = [bd[h * HD:h * HD + 1, :] for h in range(NH)]
    hmask_f = [m.astype(F32) for m in hmask]
    qk_max = (HD * SCALE * NORM_SLACK) * (jnp.max(jnp.abs(gq_ref[...]), axis=-1, keepdims=True)
                                          * jnp.max(jnp.abs(gk_ref[...]), axis=-1, keepdims=True))
    carry = jnp.zeros((1, 128), F32)
    for b in range(seq // pb):
        r = slice(b * pb, (b + 1) * pb)
        lf = _log_sigmoid(ff_ref[0, r, :] + bias_ref[...])
        cb = _dot_f32_lhs01(tri, lf, 3) + carry
        carry = cb[pb - 1:pb, :]
        ccol[r, :] = cb
        crow[:, r] = cb.T[:8, :]
        q = _head_rms(p_ref[0, r, 0:GW].astype(F32), gq_ref[...], bd) * SCALE
        k = _head_rms(p_ref[0, r, GW:2 * GW].astype(F32), gk_ref[...], bd)
        qs[r, :] = q.astype(BF16)
        ks[r, :] = k.astype(BF16)
        v = p_ref[0, r, 2 * GW:3 * GW]
        for h in range(NH):
            vx[h, r, :] = v * hmask[h]

    lane = lax.broadcasted_iota(jnp.int32, (1, 128), 1)
    rel_w = (lax.broadcasted_iota(jnp.int32, (TQ, WIDE), 1) - lax.broadcasted_iota(jnp.int32, (TQ, WIDE), 0))
    nfull = (WIDE - TQ) // TQ

    def qk_all_heads(r0, start, width):
        qb = qs[pl.ds(r0, TQ), :]
        qx = jnp.concatenate([qb * m for m in hmask], axis=0)
        return _dot_nt(qx, ks[pl.ds(start, width), :])

    def wide_start(r0):
        return pl.multiple_of(jnp.maximum(r0 - (WIDE - TQ), 0), TQ)

    def wide_qk(i):
        r0 = pl.multiple_of(i * TQ, TQ)
        return qk_all_heads(r0, wide_start(r0), WIDE)

    def q_block(i, full_tile, qk_wide):
        r0 = pl.multiple_of(i * TQ, TQ)
        s0 = wide_start(r0)
        rows = pl.ds(r0, TQ)

        def mask_wide(s):
            if full_tile:
                diag = jnp.where(rel_w[:, WIDE - TQ:] <= WIDE - TQ, s[:, WIDE - TQ:], NEG_BIG)
                return jnp.concatenate([s[:, :WIDE - TQ], diag], axis=1)
            return jnp.where(rel_w <= r0 - s0, s, NEG_BIG)

        def tiles(qk, start, width):
            return [qk[h * TQ:(h + 1) * TQ] - crow[h:h + 1, pl.ds(start, width)] for h in range(NH)]

        def weighted_values(ps, start, width):
            p_all = jnp.concatenate([p.astype(BF16) for p in ps], axis=1)
            v_all = jnp.concatenate([vx[h, pl.ds(start, width), :] for h in range(NH)], axis=0)
            return _dot(p_all, v_all)

        def per_head_lanes(cols):
            out = cols[0] * hmask_f[0]
            for h in range(1, NH):
                out = out + cols[h] * hmask_f[h]
            return out

        def keep_going(s_end, ms):
            mc = jnp.zeros((TQ, 128), F32)
            for h in range(NH):
                mc = jnp.where(lane == h, ms[h], mc)
            c_last = ccol[pl.ds(jnp.maximum(s_end - 1, 0), 1), :]
            bound = (qk_max + LOGIT_SLACK) - c_last - jnp.min(mc, axis=0, keepdims=True)
            bound = jnp.where(lane < NH, bound, NEG_BIG)
            return (jnp.max(bound) > EXP_ZERO).astype(jnp.int32)

        ss = [mask_wide(s) for s in tiles(qk_wide, s0, WIDE)]
        ms = [jnp.max(s, axis=-1, keepdims=True) for s in ss]
        ps = [jnp.exp(s - m) for s, m in zip(ss, ms)]
        ls = [jnp.sum(p, axis=-1, keepdims=True) for p in ps]
        acc = weighted_values(ps, s0, WIDE)

        def emit(acc, ls):
            g = p_ref[0, rows, 3 * GW:4 * GW].astype(F32)
            o_ref[0, rows, :] = (acc * per_head_lanes([1.0 / l for l in ls]) * _silu(g)).astype(BF16)

        emit(acc, ls)

        def cond(st):
            return jnp.logical_and(st[0] > 0, st[1] > 0)

        def body(st):
            s_end, _, ms, ls, acc = st
            sb = pl.multiple_of(s_end - TQ, TQ)
            ss = tiles(qk_all_heads(r0, sb, TQ), sb, TQ)
            ms2 = [jnp.maximum(m, jnp.max(s, axis=-1, keepdims=True)) for s, m in zip(ss, ms)]
            ps = [jnp.exp(s - m) for s, m in zip(ss, ms2)]
            alphas = [jnp.exp(m - m2) for m, m2 in zip(ms, ms2)]
            ls2 = [a * l + jnp.sum(p, axis=-1, keepdims=True) for a, l, p in zip(alphas, ls, ps)]
            acc2 = acc * per_head_lanes(alphas) + weighted_values(ps, sb, TQ)
            return sb, keep_going(sb, ms2), tuple(ms2), tuple(ls2), acc2

        def rare_tail():
            @pl.when(jnp.logical_and(s0 > 0, keep_going(s0, ms) > 0))
            def _():
                _, _, _, ls_f, acc_f = lax.while_loop(cond, body, (s0, jnp.int32(1), tuple(ms), tuple(ls), acc))
                emit(acc_f, ls_f)

        return rare_tail

    def first_blocks(i, _):
        q_block(i, False, wide_qk(i))
        return 0

    lax.fori_loop(0, nfull, first_blocks, 0)

    nblk = seq // TQ
    qk_buf[0] = wide_qk(nfull)

    def block_pair(j, _):
        i = nfull + 2 * j
        qk_buf[1] = wide_qk(i + 1)
        q_block(i, True, qk_buf[0])()
        qk_buf[0] = wide_qk(jnp.minimum(i + 2, nblk - 1))
        q_block(i + 1, True, qk_buf[1])()
        return 0

    lax.fori_loop(0, (nblk - nfull) // 2, block_pair, 0)


def _fox(p3, f3, bias, gq, gk, bd, tri):
    b, s, _ = p3.shape
    assert s % tri.shape[0] == 0 and s >= WIDE and (s // TQ - (WIDE - TQ) // TQ) % 2 == 0
    kern = functools.partial(_fox_kernel, seq=s)
    c2 = lambda i: (0, 0)
    return pl.pallas_call(
        kern,
        grid=(b,),
        in_specs=[
            pl.BlockSpec((1, s, 4 * GW), lambda i: (i, 0, 0)),
            pl.BlockSpec((1, s, 128), lambda i: (i, 0, 4)),
            pl.BlockSpec((1, 128), c2),
            pl.BlockSpec((1, GW), c2),
            pl.BlockSpec((1, GW), c2),
            pl.BlockSpec((GW, GW), c2),
            pl.BlockSpec(tri.shape, c2),
        ],
        out_specs=pl.BlockSpec((1, s, GW), lambda i: (i, 0, 0)),
        out_shape=jax.ShapeDtypeStruct((b, s, GW), BF16),
        scratch_shapes=[
            pltpu.VMEM((s, GW), BF16),
            pltpu.VMEM((s, GW), BF16),
            pltpu.VMEM((NH, s, GW), BF16),
            pltpu.VMEM((s, 128), F32),
            pltpu.VMEM((8, s), F32),
            pltpu.VMEM((2, NH * TQ, WIDE), F32),
        ],
        compiler_params=pltpu.CompilerParams(
            dimension_semantics=("arbitrary",), vmem_limit_bytes=VMEM_LIMIT),
    )(p3, f3, bias, gq, gk, bd, tri)


def _sb_kernel(p_ref, to_ref, bd_ref, o_ref, vx, qk_buf, *, seq):
    to = to_ref[...]
    hmask = [bd_ref[h * HD:h * HD + 1, :] for h in range(NH)]
    qmask = [m * SCALE for m in hmask]
    pb = 256
    for b in range(seq // pb):
        r = slice(b * pb, (b + 1) * pb)
        v = p_ref[0, r, 2 * GW:3 * GW]
        for h in range(NH):
            vx[h, r, :] = v * hmask[h]

    rel_w = (lax.broadcasted_iota(jnp.int32, (TQ, WIDE), 1) - lax.broadcasted_iota(jnp.int32, (TQ, WIDE), 0))
    nsub = WIDE // TQ
    nfull = nsub - 1

    def log_one_minus_sigmoid(z):
        return -jnp.maximum(z, jnp.log(1.0 + jnp.exp(jnp.minimum(z, SOFTPLUS_LINEAR))))

    def suffix_sums(lom):
        cs = _dot(lom.astype(BF16), to)
        return cs[:, :TQ], cs[:, TQ:]

    def qk_all_heads(r0, start, width):
        qb = p_ref[0, pl.ds(r0, TQ), 0:GW]
        qx = jnp.concatenate([qb * m for m in qmask], axis=0)
        return _dot_nt(qx, p_ref[0, pl.ds(start, width), GW:2 * GW])

    def wide_start(r0):
        return pl.multiple_of(jnp.maximum(r0 - (WIDE - TQ), 0), TQ)

    def wide_qk(i):
        r0 = pl.multiple_of(i * TQ, TQ)
        return qk_all_heads(r0, wide_start(r0), WIDE)

    def q_block(i, full_tile, qk_wide):
        r0 = pl.multiple_of(i * TQ, TQ)
        s0 = wide_start(r0)
        rows = pl.ds(r0, TQ)

        def mask_wide(x):
            if full_tile:
                diag = jnp.where(rel_w[:, WIDE - TQ:] < WIDE - TQ, x[:, WIDE - TQ:], 0.0)
                return jnp.concatenate([x[:, :WIDE - TQ], diag], axis=1)
            return jnp.where(rel_w < r0 - s0, x, 0.0)

        def per_head(qk):
            return [qk[h * TQ:(h + 1) * TQ] for h in range(NH)]

        def weighted_values(ws, start, width):
            w_all = jnp.concatenate([w.astype(BF16) for w in ws], axis=1)
            v_all = jnp.concatenate([vx[h, pl.ds(start, width), :] for h in range(NH)], axis=0)
            return _dot(w_all, v_all)

        def emit(acc):
            g = p_ref[0, rows, 3 * GW:4 * GW].astype(F32)
            o_ref[0, rows, :] = (acc * _silu(g)).astype(BF16)

        zs = per_head(qk_wide)
        loms = [mask_wide(log_one_minus_sigmoid(z)) for z in zs]
        sums = [[suffix_sums(lom[:, c * TQ:(c + 1) * TQ]) for c in range(nsub)] for lom in loms]
        carries, ws = [], []
        for h in range(NH):
            between = [None] * nsub
            carry = jnp.zeros((TQ, TQ), F32)
            for c in reversed(range(nsub)):
                rc, tot = sums[h][c]
                between[c] = rc + carry
                carry = carry + tot
            ws.append(mask_wide(jnp.exp(zs[h] + loms[h] + jnp.concatenate(between, axis=1))))
            carries.append(carry)
        acc = weighted_values(ws, s0, WIDE)
        emit(acc)

        def keep_going(carries):
            cm = jnp.maximum(jnp.maximum(carries[0], carries[1]), jnp.maximum(carries[2], carries[3]))
            return (jnp.max(cm) > EXP_ZERO).astype(jnp.int32)

        def cond(st):
            return jnp.logical_and(st[0] > 0, st[1] > 0)

        def body(st):
            s_end, _, carries, acc = st
            sb = pl.multiple_of(s_end - TQ, TQ)
            zs = per_head(qk_all_heads(r0, sb, TQ))
            loms = [log_one_minus_sigmoid(z) for z in zs]
            sums = [suffix_sums(lom) for lom in loms]
            ws = [jnp.exp(z + lom + rc + cr) for z, lom, (rc, _), cr in zip(zs, loms, sums, carries)]
            c2 = [cr + tot for cr, (_, tot) in zip(carries, sums)]
            return sb, keep_going(c2), tuple(c2), acc + weighted_values(ws, sb, TQ)

        def rare_tail():
            @pl.when(jnp.logical_and(s0 > 0, keep_going(carries) > 0))
            def _():
                st = lax.while_loop(cond, body, (s0, jnp.int32(1), tuple(carries), acc))
                emit(st[3])

        return rare_tail

    def first_blocks(i, _):
        q_block(i, False, wide_qk(i))
        return 0

    lax.fori_loop(0, nfull, first_blocks, 0)

    nblk = seq // TQ
    qk_buf[0] = wide_qk(nfull)
    qk_buf[1] = wide_qk(nfull + 1)

    def block_pair(j, _):
        i = nfull + 2 * j
        rare_a = q_block(i, True, qk_buf[0])
        rare_b = q_block(i + 1, True, qk_buf[1])
        qk_buf[0] = wide_qk(jnp.minimum(i + 2, nblk - 1))
        qk_buf[1] = wide_qk(jnp.minimum(i + 3, nblk - 1))
        rare_a()
        rare_b()
        return 0

    lax.fori_loop(0, (nblk - nfull) // 2, block_pair, 0)


def _sb(p3, to, bd):
    b, s, _ = p3.shape
    assert s % 256 == 0 and s >= WIDE and (s // TQ - (WIDE - TQ) // TQ) % 2 == 0
    kern = functools.partial(_sb_kernel, seq=s)
    return pl.pallas_call(
        kern,
        grid=(b,),
        in_specs=[
            pl.BlockSpec((1, s, 4 * GW), lambda i: (i, 0, 1)),
            pl.BlockSpec((TQ, 2 * TQ), lambda i: (0, 0)),
            pl.BlockSpec((GW, GW), lambda i: (0, 0)),
        ],
        out_specs=pl.BlockSpec((1, s, GW), lambda i: (i, 0, 0)),
        out_shape=jax.ShapeDtypeStruct((b, s, GW), BF16),
        scratch_shapes=[
            pltpu.VMEM((NH, s, GW), BF16),
            pltpu.VMEM((2, NH * TQ, WIDE), F32),
        ],
        compiler_params=pltpu.CompilerParams(
            dimension_semantics=("arbitrary",), vmem_limit_bytes=VMEM_LIMIT),
    )(p3, to, bd)


def _hgrn_kernel(hq_ref, hi_ref, hg_ref, hf_ref, lbm_ref, oml_ref, gout_ref, bd_ref, a_ref,
                 mask_ref, hm_ref, o_ref, st_ref, *, seq):
    bd = bd_ref[...]
    a_all = a_ref[...]
    hm = hm_ref[...]
    lbm = lbm_ref[...]
    oml = oml_ref[...]
    c = CHUNK
    st_ref[...] = jnp.zeros((GW, GW), F32)

    n = HSB * c
    nlev = mask_ref.shape[0] - 1
    bdf = bd.astype(F32)
    odd = (lax.broadcasted_iota(jnp.int32, (n, GW), 0) & 1) == 1

    def superblock(bi, _):
        r0 = pl.multiple_of(bi * n, n)
        rows = pl.ds(r0, n)
        hf = hf_ref[0, rows, :]
        e = jnp.exp(-jnp.abs(hf))
        rr = 1.0 / (1.0 + e)
        sg = jnp.where(hf >= 0, rr, e * rr)
        sgn = jnp.where(hf >= 0, e * rr, rr)
        f = lbm + oml * sg
        g = jnp.log(f)
        kk = oml * sgn
        q = _silu(hq_ref[0, rows, :].astype(F32))
        v = hi_ref[0, rows, :]
        gh = g.astype(BF16)
        gl = (g - gh.astype(F32)).astype(BF16)
        q2 = (q * jnp.where(odd, f, 1.0)).astype(BF16)
        k2 = (kk * jnp.where(odd, 1.0 / f, 1.0)).astype(BF16)

        def scores(qf, kf, l):
            kx = jnp.concatenate([kf] * NH, axis=0) * hm
            return _dot_nt(qf, kx) * mask_ref[l]

        sls = [slice(ci * c, (ci + 1) * c) for ci in range(HSB)]
        exs = [_dot(a_all, jnp.concatenate([gh[sl], gl[sl]], axis=0)) for sl in sls]
        ps = [scores(q2[sl], k2[sl], nlev) for sl in sls]
        qb = q.astype(BF16)
        kb = kk.astype(BF16)
        for l in range(nlev):
            for ci, sl in enumerate(sls):
                x = jnp.exp(exs[ci][l * c:(l + 1) * c]).astype(BF16)
                ps[ci] = ps[ci] + scores(qb[sl] * x, kb[sl] * x, l)
        o_intra, qd, upd, dl = [], [], [], []
        for ci, sl in enumerate(sls):
            vx = jnp.concatenate([v[sl]] * NH, axis=0) * hm
            o_intra.append(_dot(ps[ci].astype(BF16), vx))
            eb = exs[ci][nlev * c:(nlev + 1) * c]
            er = exs[ci][(nlev + 1) * c:(nlev + 2) * c]
            qd.append((q[sl] * jnp.exp(eb)).astype(BF16))
            upd.append(_dot_tn(v[sl], (kk[sl] * jnp.exp(er)).astype(BF16)) * bdf)
            dl.append(jnp.exp(eb[c - 1:c, :]))

        st = st_ref[...]
        outs = []
        for ci in range(HSB):
            outs.append(o_intra[ci] + _dot_nt(qd[ci], st.astype(BF16)))
            st = st * dl[ci] + upd[ci]
        st_ref[...] = st

        o = _head_rms(jnp.concatenate(outs, axis=0), gout_ref[...], bd)
        o_ref[0, rows, :] = (o * _silu(hg_ref[0, rows, :].astype(F32))).astype(BF16)
        return 0

    lax.fori_loop(0, seq // n, superblock, 0)


def _hgrn(p3, f3, lbm, oml, gout, bd, a_all, masks, hm):
    b, s, _ = p3.shape
    assert s % (HSB * CHUNK) == 0
    kern = functools.partial(_hgrn_kernel, seq=s)
    c2 = lambda i: (0, 0)
    return pl.pallas_call(
        kern,
        grid=(b,),
        in_specs=[
            pl.BlockSpec((1, s, GW), lambda i: (i, 0, 8)),
            pl.BlockSpec((1, s, GW), lambda i: (i, 0, 9)),
            pl.BlockSpec((1, s, GW), lambda i: (i, 0, 10)),
            pl.BlockSpec((1, s, GW), lambda i: (i, 0, 0)),
            pl.BlockSpec((1, GW), c2),
            pl.BlockSpec((1, GW), c2),
            pl.BlockSpec((1, GW), c2),
            pl.BlockSpec((GW, GW), c2),
            pl.BlockSpec(a_all.shape, c2),
            pl.BlockSpec(masks.shape, lambda i: (0, 0, 0)),
            pl.BlockSpec(hm.shape, c2),
        ],
        out_specs=pl.BlockSpec((1, s, GW), lambda i: (i, 0, 0)),
        out_shape=jax.ShapeDtypeStruct((b, s, GW), BF16),
        scratch_shapes=[pltpu.VMEM((GW, GW), F32)],
        compiler_params=pltpu.CompilerParams(
            dimension_semantics=("arbitrary",), vmem_limit_bytes=VMEM_LIMIT),
    )(p3, p3, p3, f3, lbm, oml, gout, bd, a_all, masks, hm)


def _pm_kernel(pv_ref, pg_ref, mq_ref, mg_ref, mem_ref, mng_ref, wkv_ref, gmq_ref, gmk_ref, bd_ref,
               wp_ref, ps_ref, win_ref, d_ref, e_ref, ubuf, wbuf, kh, vh, *, seq, tq):
    bd = bd_ref[...]
    halo = 16
    mem = mem_ref[0]
    ms = jnp.mean(mem * mem, axis=-1, keepdims=True)
    mn = (mem * lax.rsqrt(ms + EPS) * mng_ref[...]).astype(BF16)
    kv = _dot(mn, wkv_ref[...])
    kn = _head_rms(kv[:, :GW], gmk_ref[...], bd).astype(BF16)
    vv = kv[:, GW:].astype(BF16)
    for h in range(NH):
        kh[h] = kn[:, h * HD:(h + 1) * HD]
        vh[h] = vv[:, h * HD:(h + 1) * HD]

    ubuf[0:halo, :] = jnp.zeros((halo, GW), F32)
    ubuf[halo:halo + seq, :] = pv_ref[0]
    win = win_ref[...]
    for k in range(3):
        sh = 1 << k
        wbuf[k, 0:halo, :] = jnp.zeros((halo, GW), F32)
        for b in range(seq // tq):
            base = halo + b * tq
            if k == 0:
                wbuf[k, base:base + tq, :] = ubuf[base:base + tq, :] + ubuf[base - sh:base - sh + tq, :]
            else:
                wbuf[k, base:base + tq, :] = (wbuf[k - 1, base:base + tq, :]
                                              + wbuf[k - 1, base - sh:base - sh + tq, :])

    for b in range(seq // tq):
        r = slice(b * tq, (b + 1) * tq)
        base = halo + b * tq
        u = ubuf[base:base + tq, :]
        s2 = wbuf[0, base:base + tq, :]
        s4 = wbuf[1, base:base + tq, :]
        s8 = wbuf[2, base:base + tq, :]
        s16 = s8 + wbuf[2, base - 8:base - 8 + tq, :]
        sw = jnp.where(win == 2.0, s2, jnp.where(win == 4.0, s4, jnp.where(win == 8.0, s8, s16)))
        pos = (lax.broadcasted_iota(jnp.int32, (tq, GW), 0) + (b * tq + 1)).astype(F32)
        pooled = sw / jnp.minimum(pos, win)
        y = _dot((pooled - u).astype(BF16), wp_ref[...]) * ps_ref[...]
        d_ref[0, r, :] = (y * _silu(pg_ref[0, r, :].astype(F32))).astype(BF16)

        qn = (_head_rms(mq_ref[0, r, :].astype(F32), gmq_ref[...], bd) * SCALE).astype(BF16)
        ss = [_dot_nt(qn[:, h * HD:(h + 1) * HD], kh[h]) for h in range(NH)]
        ps = [jnp.exp(s - jnp.max(s, axis=-1, keepdims=True)) for s in ss]
        ls = [jnp.sum(p, axis=-1, keepdims=True) for p in ps]
        outs = [_dot(p.astype(BF16), vh[h]) / l for h, (p, l) in enumerate(zip(ps, ls))]
        oe = jnp.concatenate(outs, axis=1)
        e_ref[0, r, :] = (oe * _silu(mg_ref[0, r, :].astype(F32))).astype(BF16)


def _pm(p3, f3, mem, mng, wkv, gmq, gmk, bd, wp, ps, win, tq):
    b, s, _ = p3.shape
    nm = mem.shape[1]
    kern = functools.partial(_pm_kernel, seq=s, tq=tq)
    c2 = lambda i: (0, 0)
    return pl.pallas_call(
        kern,
        grid=(b,),
        in_specs=[
            pl.BlockSpec((1, s, GW), lambda i: (i, 0, 1)),
            pl.BlockSpec((1, s, GW), lambda i: (i, 0, 11)),
            pl.BlockSpec((1, s, GW), lambda i: (i, 0, 12)),
            pl.BlockSpec((1, s, GW), lambda i: (i, 0, 13)),
            pl.BlockSpec((1, nm, D_MODEL), lambda i: (i, 0, 0)),
            pl.BlockSpec((1, D_MODEL), c2),
            pl.BlockSpec((D_MODEL, 2 * GW), c2),
            pl.BlockSpec((1, GW), c2),
            pl.BlockSpec((1, GW), c2),
            pl.BlockSpec((GW, GW), c2),
            pl.BlockSpec((GW, GW), c2),
            pl.BlockSpec((1, GW), c2),
            pl.BlockSpec((1, GW), c2),
        ],
        out_specs=[
            pl.BlockSpec((1, s, GW), lambda i: (i, 0, 0)),
            pl.BlockSpec((1, s, GW), lambda i: (i, 0, 0)),
        ],
        out_shape=[
            jax.ShapeDtypeStruct((b, s, GW), BF16),
            jax.ShapeDtypeStruct((b, s, GW), BF16),
        ],
        scratch_shapes=[
            pltpu.VMEM((s + 16, GW), F32),
            pltpu.VMEM((3, s + 16, GW), F32),
            pltpu.VMEM((NH, nm, HD), BF16),
            pltpu.VMEM((NH, nm, HD), BF16),
        ],
        compiler_params=pltpu.CompilerParams(
            dimension_semantics=("arbitrary",), vmem_limit_bytes=VMEM_LIMIT),
    )(f3, p3, p3, p3, mem, mng, wkv, gmq, gmk, bd, wp, ps, win)


def _out_kernel(a_ref, b_ref, c_ref, d_ref, e_ref, w_ref, x_ref, o_ref):
    mixed = jnp.concatenate([a_ref[...], b_ref[...], c_ref[...], d_ref[...], e_ref[...]], axis=1)
    o_ref[...] = x_ref[...] + _dot(mixed, w_ref[...])


def _out(parts, w, x2, tm):
    m = x2.shape[0]
    gspec = pl.BlockSpec((tm, GW), lambda i: (i, 0))
    return pl.pallas_call(
        _out_kernel,
        grid=(m // tm,),
        in_specs=[gspec] * 5 + [
            pl.BlockSpec((5 * GW, D_MODEL), lambda i: (0, 0)),
            pl.BlockSpec((tm, D_MODEL), lambda i: (i, 0)),
        ],
        out_specs=pl.BlockSpec((tm, D_MODEL), lambda i: (i, 0)),
        out_shape=jax.ShapeDtypeStruct((m, D_MODEL), F32),
        compiler_params=pltpu.CompilerParams(
            dimension_semantics=("arbitrary",), vmem_limit_bytes=VMEM_LIMIT),
    )(*parts, w, x2)


def _tile_heads(g):
    return jnp.tile(g.astype(F32), NH).reshape(1, GW)


def kernel(x, mem, norm_g, w_in, fox_f_bias, fox_q_norm, fox_k_norm, hgrn_lb_logits, hgrn_out_norm,
           pool_w, pool_scale, mem_norm_g, mem_w_kv, mem_q_norm, mem_k_norm, w_out):
    bsz, seq, _ = x.shape
    depth = w_in.shape[0]
    m = bsz * seq
    tq = 256
    tm = 512

    pr = jax.nn.softmax(hgrn_lb_logits.astype(F32), axis=0)
    lower_bounds = jnp.clip(jnp.cumsum(pr, axis=0) - pr[0:1], 0.0, 1.0 - 1e-6)

    bd_np = _bd_ones()
    bd = jnp.asarray(bd_np, BF16)
    tri = jnp.asarray(np.tril(np.ones((tq, tq), np.float32)), BF16)
    jj = np.arange(TQ)
    suffix = (jj[:, None] > jj[None, :]).astype(np.float32)
    to = jnp.asarray(np.concatenate([suffix, np.ones((TQ, TQ), np.float32)], axis=1), BF16)
    a_np, masks_np, hm_np = _hgrn_constants()
    a_all = jnp.asarray(a_np, BF16)
    masks = jnp.asarray(masks_np, F32)
    hm = jnp.asarray(hm_np, BF16)
    win = jnp.asarray(np.repeat(np.array(POOL_WINDOWS, np.float32), HD).reshape(1, GW))

    g = GW
    wb_all, wf_all = _regroup_w_in(w_in)
    x2 = x.reshape(m, D_MODEL)
    for l in range(depth):
        pb, pf = _proj(x2, norm_g[l].reshape(1, D_MODEL).astype(F32), wb_all[l], wf_all[l], tm)
        p3 = pb.reshape(bsz, seq, NP_COLS)
        f3 = pf.reshape(bsz, seq, NF_COLS)

        bias = jnp.concatenate([fox_f_bias[l].astype(F32), jnp.zeros((128 - NH,), F32)]).reshape(1, 128)
        out_a = _fox(p3, f3, bias, _tile_heads(fox_q_norm[l]), _tile_heads(fox_k_norm[l]), bd, tri)
        out_b = _sb(p3, to, bd)

        lb = lower_bounds[l].reshape(1, g)
        out_c = _hgrn(p3, f3, jnp.maximum(lb, LB_FLOOR), 1.0 - lb,
                      hgrn_out_norm[l].reshape(1, g).astype(F32), bd, a_all, masks, hm)

        wp = jax.scipy.linalg.block_diag(*[pool_w[l, i] for i in range(len(POOL_WINDOWS))]).astype(BF16)
        out_d, out_e = _pm(p3, f3, mem, mem_norm_g[l].reshape(1, D_MODEL).astype(F32),
                           mem_w_kv[l].astype(BF16), _tile_heads(mem_q_norm[l]), _tile_heads(mem_k_norm[l]),
                           bd, wp, pool_scale[l].reshape(1, g).astype(F32), win, tq)

        x2 = _out([out_a.reshape(m, g), out_b.reshape(m, g), out_c.reshape(m, g),
                   out_d.reshape(m, g), out_e.reshape(m, g)], w_out[l].astype(BF16), x2, tm)
    return x2.reshape(bsz, seq, D_MODEL)
```

```python
import functools

import numpy as np
import jax
import jax.numpy as jnp
from jax import lax
from jax.experimental import pallas as pl
from jax.experimental.pallas import tpu as pltpu

F32 = jnp.float32
BF16 = jnp.bfloat16

D_MODEL = 1024
GW = 256
NH = 4
HD = 64
CHUNK = 64
POOL_WINDOWS = (2, 4, 8, 16)
EPS = 1e-6
NEG_BIG = -1e30
LB_FLOOR = 1e-30
SCALE = HD ** -0.5

NP_COLS = 14 * GW
NF_COLS = 2 * GW + 128
VMEM_LIMIT = 56 * 1024 * 1024

TQ = 128
WIDE = 3 * TQ
EXP_ZERO = -104.0
NORM_SLACK = 1.01
LOGIT_SLACK = 0.05
HSB = 4
SOFTPLUS_LINEAR = 80.0


def _dot(a, b):
    return jnp.dot(a, b, preferred_element_type=F32)


def _dot_nt(a, b):
    return lax.dot_general(a, b, (((1,), (1,)), ((), ())), preferred_element_type=F32)


def _dot_tn(a, b):
    return lax.dot_general(a, b, (((0,), (0,)), ((), ())), preferred_element_type=F32)


def _split_bf16(x, n):
    parts = []
    r = x
    for i in range(n):
        p = r.astype(BF16)
        parts.append(p)
        if i + 1 < n:
            r = r - p.astype(F32)
    return parts


def _dot_f32_rhs01(x, m01, n=3):
    acc = None
    for p in _split_bf16(x, n):
        t = _dot(p, m01)
        acc = t if acc is None else acc + t
    return acc


def _dot_f32_lhs01(m01, x, n=3):
    acc = None
    for p in _split_bf16(x, n):
        t = _dot(m01, p)
        acc = t if acc is None else acc + t
    return acc


def _silu(x):
    return x / (1.0 + jnp.exp(-x))


def _log_sigmoid(x):
    return jnp.minimum(x, 0.0) - jnp.log(1.0 + jnp.exp(-jnp.abs(x)))


def _head_rms(x, gain, bd):
    ss = _dot((x * x).astype(BF16), bd)
    return x * lax.rsqrt(ss * (1.0 / HD) + EPS) * gain


def _bd_ones():
    h = np.arange(GW) // HD
    return (h[:, None] == h[None, :]).astype(np.float32)


def _hgrn_constants():
    c = CHUNK
    t = np.arange(c)
    j = np.arange(c)[None, :]
    mats = []
    masks = []
    for n in (64, 32, 16, 8, 4):
        blk, pos = t // n, t % n
        ref = blk * n + n // 2 - 1
        aq = (pos[:, None] >= n // 2) & (j > ref[:, None]) & (j <= t[:, None])
        ak = (pos[:, None] < n // 2) & (j > t[:, None]) & (j <= ref[:, None])
        mats.append((aq | ak).astype(np.float32))
        m = (blk[:, None] == blk[None, :]) & (pos[:, None] >= n // 2) & (pos[None, :] < n // 2)
        masks.append(m.astype(np.float32))
    m = ((t[:, None] // 2) == (t[None, :] // 2)) & (t[None, :] <= t[:, None])
    masks.append(m.astype(np.float32))
    mats.append((j <= t[:, None]).astype(np.float32))
    mats.append((j > t[:, None]).astype(np.float32))
    a_all = np.concatenate(mats, axis=0)
    a_all = np.concatenate([a_all, a_all], axis=1)
    masks = np.stack([np.tile(m, (1, NH)) for m in masks])
    total = masks[:, :, :c].sum(0)
    assert np.array_equal(total, np.tril(np.ones((c, c), np.float32)))
    hm = (np.arange(NH * c)[:, None] // c == np.arange(GW)[None, :] // HD).astype(np.float32)
    return a_all, masks, hm


def _regroup_kernel(w_ref, wb_ref, wf_ref):
    g = GW
    o_ff, o_sb, o_hg, o_pl, o_mm = 4 * g, 4 * g + NH, 8 * g + NH, 12 * g + NH, 14 * g + NH
    w = w_ref[0]
    bf_src = [0, g, 2 * g, 3 * g,
              o_sb, o_sb + g, o_sb + 2 * g, o_sb + 3 * g,
              o_hg, o_hg + 2 * g, o_hg + 3 * g,
              o_pl + g,
              o_mm, o_mm + g]
    for i, c0 in enumerate(bf_src):
        wb_ref[0, :, i * g:(i + 1) * g] = w[:, c0:c0 + g].astype(BF16)
    wf_ref[0, :, 0:g] = w[:, o_hg + g:o_hg + 2 * g].astype(BF16)
    wf_ref[0, :, g:2 * g] = w[:, o_pl:o_pl + g].astype(BF16)
    ff = jnp.concatenate([w[:, o_ff:o_ff + NH], jnp.zeros((w.shape[0], 128 - NH), F32)], axis=1)
    wf_ref[0, :, 2 * g:2 * g + 128] = ff.astype(BF16)


def _regroup_w_in(w_in):
    depth, d, n = w_in.shape
    tr = 128
    return pl.pallas_call(
        _regroup_kernel,
        grid=(depth, d // tr),
        in_specs=[pl.BlockSpec((1, tr, n), lambda l, i: (l, i, 0))],
        out_specs=[
            pl.BlockSpec((1, tr, NP_COLS), lambda l, i: (l, i, 0)),
            pl.BlockSpec((1, tr, NF_COLS), lambda l, i: (l, i, 0)),
        ],
        out_shape=[
            jax.ShapeDtypeStruct((depth, d, NP_COLS), BF16),
            jax.ShapeDtypeStruct((depth, d, NF_COLS), BF16),
        ],
        compiler_params=pltpu.CompilerParams(
            dimension_semantics=("arbitrary", "arbitrary"), vmem_limit_bytes=VMEM_LIMIT),
    )(w_in)


def _proj_kernel(x_ref, g_ref, wb_ref, wf_ref, pb_ref, pf_ref):
    x = x_ref[...]
    ms = jnp.mean(x * x, axis=-1, keepdims=True)
    h = (x * lax.rsqrt(ms + EPS) * g_ref[...]).astype(BF16)
    nb = 512
    for c0 in range(0, NP_COLS, nb):
        pb_ref[:, c0:c0 + nb] = _dot(h, wb_ref[:, c0:c0 + nb]).astype(BF16)
    pf_ref[...] = _dot(h, wf_ref[...])


def _proj(x2, g, wb, wf, tm):
    m = x2.shape[0]
    return pl.pallas_call(
        _proj_kernel,
        grid=(m // tm,),
        in_specs=[
            pl.BlockSpec((tm, D_MODEL), lambda i: (i, 0)),
            pl.BlockSpec((1, D_MODEL), lambda i: (0, 0)),
            pl.BlockSpec((D_MODEL, NP_COLS), lambda i: (0, 0)),
            pl.BlockSpec((D_MODEL, NF_COLS), lambda i: (0, 0)),
        ],
        out_specs=[
            pl.BlockSpec((tm, NP_COLS), lambda i: (i, 0)),
            pl.BlockSpec((tm, NF_COLS), lambda i: (i, 0)),
        ],
        out_shape=[
            jax.ShapeDtypeStruct((m, NP_COLS), BF16),
            jax.ShapeDtypeStruct((m, NF_COLS), F32),
        ],
        compiler_params=pltpu.CompilerParams(
            dimension_semantics=("arbitrary",), vmem_limit_bytes=VMEM_LIMIT),
    )(x2, g, wb, wf)


def _fox_kernel(p_ref, ff_ref, bias_ref, gq_ref, gk_ref, bd_ref, tri_ref, o_ref,
                qs, ks, vx, ccol, crow, qk_buf, *, seq):
    bd = bd_ref[...]
    tri = tri_ref[...]
    pb = tri.shape[0]
    hmask = [bd[h * HD:h * HD + 1, :] for h in range(NH)]
    hmask_f = [m.astype(F32) for m in hmask]
    qk_max = (HD * SCALE * NORM_SLACK) * (jnp.max(jnp.abs(gq_ref[...]), axis=-1, keepdims=True)
                                          * jnp.max(jnp.abs(gk_ref[...]), axis=-1, keepdims=True))
    carry = jnp.zeros((1, 128), F32)
    for b in range(seq // pb):
        r = slice(b * pb, (b + 1) * pb)
        lf = _log_sigmoid(ff_ref[0, r, :] + bias_ref[...])
        cb = _dot_f32_lhs01(tri, lf, 3) + carry
        carry = cb[pb - 1:pb, :]
        ccol[r, :] = cb
        crow[:, r] = cb.T[:8, :]
        q = _head_rms(p_ref[0, r, 0:GW].astype(F32), gq_ref[...], bd) * SCALE
        k = _head_rms(p_ref[0, r, GW:2 * GW].astype(F32), gk_ref[...], bd)
        qs[r, :] = q.astype(BF16)
        ks[r, :] = k.astype(BF16)
        v = p_ref[0, r, 2 * GW:3 * GW]
        for h in range(NH):
            vx[h, r, :] = v * hmask[h]

    lane = lax.broadcasted_iota(jnp.int32, (1, 128), 1)
    rel_w = (lax.broadcasted_iota(jnp.int32, (TQ, WIDE), 1) - lax.broadcasted_iota(jnp.int32, (TQ, WIDE), 0))
    nfull = (WIDE - TQ) // TQ

    def qk_all_heads(r0, start, width):
        qb = qs[pl.ds(r0, TQ), :]
        qx = jnp.concatenate([qb * m for m in hmask], axis=0)
        return _dot_nt(qx, ks[pl.ds(start, width), :])

    def wide_start(r0):
        return pl.multiple_of(jnp.maximum(r0 - (WIDE - TQ), 0), TQ)

    def wide_qk(i):
        r0 = pl.multiple_of(i * TQ, TQ)
        return qk_all_heads(r0, wide_start(r0), WIDE)

    def q_block(i, full_tile, qk_wide):
        r0 = pl.multiple_of(i * TQ, TQ)
        s0 = wide_start(r0)
        rows = pl.ds(r0, TQ)

        def mask_wide(s):
            if full_tile:
                diag = jnp.where(rel_w[:, WIDE - TQ:] <= WIDE - TQ, s[:, WIDE - TQ:], NEG_BIG)
                return jnp.concatenate([s[:, :WIDE - TQ], diag], axis=1)
            return jnp.where(rel_w <= r0 - s0, s, NEG_BIG)

        def tiles(qk, start, width):
            return [qk[h * TQ:(h + 1) * TQ] - crow[h:h + 1, pl.ds(start, width)] for h in range(NH)]

        def weighted_values(ps, start, width):
            p_all = jnp.concatenate([p.astype(BF16) for p in ps], axis=1)
            v_all = jnp.concatenate([vx[h, pl.ds(start, width), :] for h in range(NH)], axis=0)
            return _dot(p_all, v_all)

        def per_head_lanes(cols):
            out = cols[0] * hmask_f[0]
            for h in range(1, NH):
                out = out + cols[h] * hmask_f[h]
            return out

        def keep_going(s_end, ms):
            mc = jnp.zeros((TQ, 128), F32)
            for h in range(NH):
                mc = jnp.where(lane == h, ms[h], mc)
            c_last = ccol[pl.ds(jnp.maximum(s_end - 1, 0), 1), :]
            bound = (qk_max + LOGIT_SLACK) - c_last - jnp.min(mc, axis=0, keepdims=True)
            bound = jnp.where(lane < NH, bound, NEG_BIG)
            return (jnp.max(bound) > EXP_ZERO).astype(jnp.int32)

        ss = [mask_wide(s) for s in tiles(qk_wide, s0, WIDE)]
        ms = [jnp.max(s, axis=-1, keepdims=True) for s in ss]
        ps = [jnp.exp(s - m) for s, m in zip(ss, ms)]
        ls = [jnp.sum(p, axis=-1, keepdims=True) for p in ps]
        acc = weighted_values(ps, s0, WIDE)

        def emit(acc, ls):
            g = p_ref[0, rows, 3 * GW:4 * GW].astype(F32)
            o_ref[0, rows, :] = (acc * per_head_lanes([1.0 / l for l in ls]) * _silu(g)).astype(BF16)

        emit(acc, ls)

        def cond(st):
            return jnp.logical_and(st[0] > 0, st[1] > 0)

        def body(st):
            s_end, _, ms, ls, acc = st
            sb = pl.multiple_of(s_end - TQ, TQ)
            ss = tiles(qk_all_heads(r0, sb, TQ), sb, TQ)
            ms2 = [jnp.maximum(m, jnp.max(s, axis=-1, keepdims=True)) for s, m in zip(ss, ms)]
            ps = [jnp.exp(s - m) for s, m in zip(ss, ms2)]
            alphas = [jnp.exp(m - m2) for m, m2 in zip(ms, ms2)]
            ls2 = [a * l + jnp.sum(p, axis=-1, keepdims=True) for a, l, p in zip(alphas, ls, ps)]
            acc2 = acc * per_head_lanes(alphas) + weighted_values(ps, sb, TQ)
            return sb, keep_going(sb, ms2), tuple(ms2), tuple(ls2), acc2

        def rare_tail():
            @pl.when(jnp.logical_and(s0 > 0, keep_going(s0, ms) > 0))
            def _():
                _, _, _, ls_f, acc_f = lax.while_loop(cond, body, (s0, jnp.int32(1), tuple(ms), tuple(ls), acc))
                emit(acc_f, ls_f)

        return rare_tail

    def first_blocks(i, _):
        q_block(i, False, wide_qk(i))
        return 0

    lax.fori_loop(0, nfull, first_blocks, 0)

    nblk = seq // TQ
    qk_buf[0] = wide_qk(nfull)

    def block_pair(j, _):
        i = nfull + 2 * j
        qk_buf[1] = wide_qk(i + 1)
        q_block(i, True, qk_buf[0])()
        qk_buf[0] = wide_qk(jnp.minimum(i + 2, nblk - 1))
        q_block(i + 1, True, qk_buf[1])()
        return 0

    lax.fori_loop(0, (nblk - nfull) // 2, block_pair, 0)


def _fox(p3, f3, bias, gq, gk, bd, tri):
    b, s, _ = p3.shape
    assert s % tri.shape[0] == 0 and s >= WIDE and (s // TQ - (WIDE - TQ) // TQ) % 2 == 0
    kern = functools.partial(_fox_kernel, seq=s)
    c2 = lambda i: (0, 0)
    return pl.pallas_call(
        kern,
        grid=(b,),
        in_specs=[
            pl.BlockSpec((1, s, 4 * GW), lambda i: (i, 0, 0)),
            pl.BlockSpec((1, s, 128), lambda i: (i, 0, 4)),
            pl.BlockSpec((1, 128), c2),
            pl.BlockSpec((1, GW), c2),
            pl.BlockSpec((1, GW), c2),
            pl.BlockSpec((GW, GW), c2),
            pl.BlockSpec(tri.shape, c2),
        ],
        out_specs=pl.BlockSpec((1, s, GW), lambda i: (i, 0, 0)),
        out_shape=jax.ShapeDtypeStruct((b, s, GW), BF16),
        scratch_shapes=[
            pltpu.VMEM((s, GW), BF16),
            pltpu.VMEM((s, GW), BF16),
            pltpu.VMEM((NH, s, GW), BF16),
            pltpu.VMEM((s, 128), F32),
            pltpu.VMEM((8, s), F32),
            pltpu.VMEM((2, NH * TQ, WIDE), F32),
        ],
        compiler_params=pltpu.CompilerParams(
            dimension_semantics=("arbitrary",), vmem_limit_bytes=VMEM_LIMIT),
    )(p3, f3, bias, gq, gk, bd, tri)


def _sb_kernel(p_ref, to_ref, bd_ref, o_ref, vx, qk_buf, *, seq):
    to = to_ref[...]
    hmask = [bd_ref[h * HD:h * HD + 1, :] for h in range(NH)]
    qmask = [m * SCALE for m in hmask]
    pb = 256
    for b in range(seq // pb):
        r = slice(b * pb, (b + 1) * pb)
        v = p_ref[0, r, 2 * GW:3 * GW]
        for h in range(NH):
            vx[h, r, :] = v * hmask[h]

    rel_w = (lax.broadcasted_iota(jnp.int32, (TQ, WIDE), 1) - lax.broadcasted_iota(jnp.int32, (TQ, WIDE), 0))
    nsub = WIDE // TQ
    nfull = nsub - 1

    def log_one_minus_sigmoid(z):
        return -jnp.maximum(z, jnp.log(1.0 + jnp.exp(jnp.minimum(z, SOFTPLUS_LINEAR))))

    def suffix_sums(lom):
        cs = _dot(lom.astype(BF16), to)
        return cs[:, :TQ], cs[:, TQ:]

    def qk_all_heads(r0, start, width):
        qb = p_ref[0, pl.ds(r0, TQ), 0:GW]
        qx = jnp.concatenate([qb * m for m in qmask], axis=0)
        return _dot_nt(qx, p_ref[0, pl.ds(start, width), GW:2 * GW])

    def wide_start(r0):
        return pl.multiple_of(jnp.maximum(r0 - (WIDE - TQ), 0), TQ)

    def wide_qk(i):
        r0 = pl.multiple_of(i * TQ, TQ)
        return qk_all_heads(r0, wide_start(r0), WIDE)

    def q_block(i, full_tile, qk_wide):
        r0 = pl.multiple_of(i * TQ, TQ)
        s0 = wide_start(r0)
        rows = pl.ds(r0, TQ)

        def mask_wide(x):
            if full_tile:
                diag = jnp.where(rel_w[:, WIDE - TQ:] < WIDE - TQ, x[:, WIDE - TQ:], 0.0)
                return jnp.concatenate([x[:, :WIDE - TQ], diag], axis=1)
            return jnp.where(rel_w < r0 - s0, x, 0.0)

        def per_head(qk):
            return [qk[h * TQ:(h + 1) * TQ] for h in range(NH)]

        def weighted_values(ws, start, width):
            w_all = jnp.concatenate([w.astype(BF16) for w in ws], axis=1)
            v_all = jnp.concatenate([vx[h, pl.ds(start, width), :] for h in range(NH)], axis=0)
            return _dot(w_all, v_all)

        def emit(acc):
            g = p_ref[0, rows, 3 * GW:4 * GW].astype(F32)
            o_ref[0, rows, :] = (acc * _silu(g)).astype(BF16)

        zs = per_head(qk_wide)
        loms = [mask_wide(log_one_minus_sigmoid(z)) for z in zs]
        sums = [[suffix_sums(lom[:, c * TQ:(c + 1) * TQ]) for c in range(nsub)] for lom in loms]
        carries, ws = [], []
        for h in range(NH):
            between = [None] * nsub
            carry = jnp.zeros((TQ, TQ), F32)
            for c in reversed(range(nsub)):
                rc, tot = sums[h][c]
                between[c] = rc + carry
                carry = carry + tot
            ws.append(mask_wide(jnp.exp(zs[h] + loms[h] + jnp.concatenate(between, axis=1))))
            carries.append(carry)
        acc = weighted_values(ws, s0, WIDE)
        emit(acc)

        def keep_going(carries):
            cm = jnp.maximum(jnp.maximum(carries[0], carries[1]), jnp.maximum(carries[2], carries[3]))
            return (jnp.max(cm) > EXP_ZERO).astype(jnp.int32)

        def cond(st):
            return jnp.logical_and(st[0] > 0, st[1] > 0)

        def body(st):
            s_end, _, carries, acc = st
            sb = pl.multiple_of(s_end - TQ, TQ)
            zs = per_head(qk_all_heads(r0, sb, TQ))
            loms = [log_one_minus_sigmoid(z) for z in zs]
            sums = [suffix_sums(lom) for lom in loms]
            ws = [jnp.exp(z + lom + rc + cr) for z, lom, (rc, _), cr in zip(zs, loms, sums, carries)]
            c2 = [cr + tot for cr, (_, tot) in zip(carries, sums)]
            return sb, keep_going(c2), tuple(c2), acc + weighted_values(ws, sb, TQ)

        def rare_tail():
            @pl.when(jnp.logical_and(s0 > 0, keep_going(carries) > 0))
            def _():
                st = lax.while_loop(cond, body, (s0, jnp.int32(1), tuple(carries), acc))
                emit(st[3])

        return rare_tail

    def first_blocks(i, _):
        q_block(i, False, wide_qk(i))
        return 0

    lax.fori_loop(0, nfull, first_blocks, 0)

    nblk = seq // TQ
    qk_buf[0] = wide_qk(nfull)
    qk_buf[1] = wide_qk(nfull + 1)

    def block_pair(j, _):
        i = nfull + 2 * j
        rare_a = q_block(i, True, qk_buf[0])
        rare_b = q_block(i + 1, True, qk_buf[1])
        qk_buf[0] = wide_qk(jnp.minimum(i + 2, nblk - 1))
        qk_buf[1] = wide_qk(jnp.minimum(i + 3, nblk - 1))
        rare_a()
        rare_b()
        return 0

    lax.fori_loop(0, (nblk - nfull) // 2, block_pair, 0)


def _sb(p3, to, bd):
    b, s, _ = p3.shape
    assert s % 256 == 0 and s >= WIDE and (s // TQ - (WIDE - TQ) // TQ) % 2 == 0
    kern = functools.partial(_sb_kernel, seq=s)
    return pl.pallas_call(
        kern,
        grid=(b,),
        in_specs=[
            pl.BlockSpec((1, s, 4 * GW), lambda i: (i, 0, 1)),
            pl.BlockSpec((TQ, 2 * TQ), lambda i: (0, 0)),
            pl.BlockSpec((GW, GW), lambda i: (0, 0)),
        ],
        out_specs=pl.BlockSpec((1, s, GW), lambda i: (i, 0, 0)),
        out_shape=jax.ShapeDtypeStruct((b, s, GW), BF16),
        scratch_shapes=[
            pltpu.VMEM((NH, s, GW), BF16),
            pltpu.VMEM((2, NH * TQ, WIDE), F32),
        ],
        compiler_params=pltpu.CompilerParams(
            dimension_semantics=("arbitrary",), vmem_limit_bytes=VMEM_LIMIT),
    )(p3, to, bd)


def _hgrn_kernel(hq_ref, hi_ref, hg_ref, hf_ref, lbm_ref, oml_ref, gout_ref, bd_ref, a_ref,
                 mask_ref, hm_ref, o_ref, st_ref, *, seq):
    bd = bd_ref[...]
    a_all = a_ref[...]
    hm = hm_ref[...]
    lbm = lbm_ref[...]
    oml = oml_ref[...]
    c = CHUNK
    st_ref[...] = jnp.zeros((GW, GW), F32)

    n = HSB * c
    nlev = mask_ref.shape[0] - 1
    odd =(lax.broadcasted_iota(jnp.int32, (n, GW), 0) & 1) == 1

    def superblock(bi, _):
        r0 = pl.multiple_of(bi * n, n)
        rows = pl.ds(r0, n)
        hf = hf_ref[0, rows, :]
        e = jnp.exp(-jnp.abs(hf))
        rr = 1.0 / (1.0 + e)
        sg = jnp.where(hf >= 0, rr, e * rr)
        sgn = jnp.where(hf >= 0, e * rr, rr)
        f = lbm + oml * sg
        g = jnp.log(f)
        kk = oml * sgn
        q = _silu(hq_ref[0, rows, :].astype(F32))
        v = hi_ref[0, rows, :]
        gh = g.astype(BF16)
        gl = (g - gh.astype(F32)).astype(BF16)
        q2 = (q * jnp.where(odd, f, 1.0)).astype(BF16)
        k2 = (kk * jnp.where(odd, 1.0 / f, 1.0)).astype(BF16)

        def scores(qf, kf, l):
            kx = jnp.concatenate([kf] * NH, axis=0) * hm
            return _dot_nt(qf, kx) * mask_ref[l]

        sls = [slice(ci * c, (ci + 1) * c) for ci in range(HSB)]
        exs = [_dot(a_all, jnp.concatenate([gh[sl], gl[sl]], axis=0)) for sl in sls]
        ps = [scores(q2[sl], k2[sl], nlev) for sl in sls]
        qb = q.astype(BF16)
        kb = kk.astype(BF16)
        for l in range(nlev):
            for ci, sl in enumerate(sls):
                x = jnp.exp(exs[ci][l * c:(l + 1) * c]).astype(BF16)
                ps[ci] = ps[ci] + scores(qb[sl] * x, kb[sl] * x, l)
        o_intra, qd, upd, dl = [], [], [], []
        for ci, sl in enumerate(sls):
            vx = jnp.concatenate([v[sl]] * NH, axis=0) * hm
            o_intra.append(_dot(ps[ci].astype(BF16), vx))
            eb = exs[ci][nlev * c:(nlev + 1) * c]
            er = exs[ci][(nlev + 1) * c:(nlev + 2) * c]
            qd.append((q[sl] * jnp.exp(eb)).astype(BF16))
            upd.append(_dot_tn(v[sl], (kk[sl] * jnp.exp(er)).astype(BF16)))
            dl.append(jnp.exp(eb[c - 1:c, :]))

        st = st_ref[...]
        outs = []
        for ci in range(HSB):
            outs.append(o_intra[ci] + _dot_nt(qd[ci], st.astype(BF16) * bd))
            st = st * dl[ci] + upd[ci]
        st_ref[...] = st

        o = _head_rms(jnp.concatenate(outs, axis=0), gout_ref[...], bd)
        o_ref[0, rows, :] = (o * _silu(hg_ref[0, rows, :].astype(F32))).astype(BF16)
        return 0

    lax.fori_loop(0, seq // n, superblock, 0)


def _hgrn(p3, f3, lbm, oml, gout, bd, a_all, masks, hm):
    b, s, _ = p3.shape
    assert s % (HSB * CHUNK) == 0
    kern = functools.partial(_hgrn_kernel, seq=s)
    c2 = lambda i: (0, 0)
    return pl.pallas_call(
        kern,
        grid=(b,),
        in_specs=[
            pl.BlockSpec((1, s, GW), lambda i: (i, 0, 8)),
            pl.BlockSpec((1, s, GW), lambda i: (i, 0, 9)),
            pl.BlockSpec((1, s, GW), lambda i: (i, 0, 10)),
            pl.BlockSpec((1, s, GW), lambda i: (i, 0, 0)),
            pl.BlockSpec((1, GW), c2),
            pl.BlockSpec((1, GW), c2),
            pl.BlockSpec((1, GW), c2),
            pl.BlockSpec((GW, GW), c2),
            pl.BlockSpec(a_all.shape, c2),
            pl.BlockSpec(masks.shape, lambda i: (0, 0, 0)),
            pl.BlockSpec(hm.shape, c2),
        ],
        out_specs=pl.BlockSpec((1, s, GW), lambda i: (i, 0, 0)),
        out_shape=jax.ShapeDtypeStruct((b, s, GW), BF16),
        scratch_shapes=[pltpu.VMEM((GW, GW), F32)],
        compiler_params=pltpu.CompilerParams(
            dimension_semantics=("arbitrary",), vmem_limit_bytes=VMEM_LIMIT),
    )(p3, p3, p3, f3, lbm, oml, gout, bd, a_all, masks, hm)


def _pm_kernel(pv_ref, pg_ref, mq_ref, mg_ref, mem_ref, mng_ref, wkv_ref, gmq_ref, gmk_ref, bd_ref,
               wp_ref, ps_ref, win_ref, d_ref, e_ref, ubuf, wbuf, kmem, vxm, *, seq, tq):
    bd = bd_ref[...]
    hmask = [bd[h * HD:h * HD + 1, :] for h in range(NH)]
    hmask_f = [m.astype(F32) for m in hmask]
    halo = 16
    mem = mem_ref[0]
    ms = jnp.mean(mem * mem, axis=-1, keepdims=True)
    mn = (mem * lax.rsqrt(ms + EPS) * mng_ref[...]).astype(BF16)
    kv = _dot(mn, wkv_ref[...])
    kn = _head_rms(kv[:, :GW], gmk_ref[...], bd).astype(BF16)
    vv = kv[:, GW:].astype(BF16)
    nm = kn.shape[0]
    kmem[...] = kn
    for h in range(NH):
        vxm[h * nm:(h + 1) * nm, :] = vv * hmask[h]

    ubuf[0:halo, :] = jnp.zeros((halo, GW), F32)
    ubuf[halo:halo + seq, :] = pv_ref[0]
    win = win_ref[...]
    for k in range(3):
        sh = 1 << k
        wbuf[k, 0:halo, :] = jnp.zeros((halo, GW), F32)
        for b in range(seq // tq):
            base = halo + b * tq
            if k == 0:
                wbuf[k, base:base + tq, :] = ubuf[base:base + tq, :] + ubuf[base - sh:base - sh + tq, :]
            else:
                wbuf[k, base:base + tq, :] = (wbuf[k - 1, base:base + tq, :]
                                              + wbuf[k - 1, base - sh:base - sh + tq, :])

    for b in range(seq // tq):
        r = slice(b * tq, (b + 1) * tq)
        base = halo + b * tq
        u = ubuf[base:base + tq, :]
        s2 = wbuf[0, base:base + tq, :]
        s4 = wbuf[1, base:base + tq, :]
        s8 = wbuf[2, base:base + tq, :]
        s16 = s8 + wbuf[2, base - 8:base - 8 + tq, :]
        sw = jnp.where(win == 2.0, s2, jnp.where(win == 4.0, s4, jnp.where(win == 8.0, s8, s16)))
        pos = (lax.broadcasted_iota(jnp.int32, (tq, GW), 0) + (b * tq + 1)).astype(F32)
        pooled = sw / jnp.minimum(pos, win)
        y = _dot((pooled - u).astype(BF16), wp_ref[...]) * ps_ref[...]
        d_ref[0, r, :] = (y * _silu(pg_ref[0, r, :].astype(F32))).astype(BF16)

        qn = (_head_rms(mq_ref[0, r, :].astype(F32), gmq_ref[...], bd) * SCALE).astype(BF16)
        s_all = _dot_nt(jnp.concatenate([qn * m for m in hmask], axis=0), kmem[...])
        ss = [s_all[h * tq:(h + 1) * tq] for h in range(NH)]
        ps = [jnp.exp(s - jnp.max(s, axis=-1, keepdims=True)) for s in ss]
        ls = [jnp.sum(p, axis=-1, keepdims=True) for p in ps]
        inv_l = (1.0 / ls[0]) * hmask_f[0]
        for h in range(1, NH):
            inv_l = inv_l + (1.0 / ls[h]) * hmask_f[h]
        oe = _dot(jnp.concatenate([p.astype(BF16) for p in ps], axis=1), vxm[...]) * inv_l
        e_ref[0, r, :] = (oe * _silu(mg_ref[0, r, :].astype(F32))).astype(BF16)


def _pm(p3, f3, mem, mng, wkv, gmq, gmk, bd, wp, ps, win, tq):
    b, s, _ = p3.shape
    nm = mem.shape[1]
    kern = functools.partial(_pm_kernel, seq=s, tq=tq)
    c2 = lambda i: (0, 0)
    return pl.pallas_call(
        kern,
        grid=(b,),
        in_specs=[
            pl.BlockSpec((1, s, GW), lambda i: (i, 0, 1)),
            pl.BlockSpec((1, s, GW), lambda i: (i, 0, 11)),
            pl.BlockSpec((1, s, GW), lambda i: (i, 0, 12)),
            pl.BlockSpec((1, s, GW), lambda i: (i, 0, 13)),
            pl.BlockSpec((1, nm, D_MODEL), lambda i: (i, 0, 0)),
            pl.BlockSpec((1, D_MODEL), c2),
            pl.BlockSpec((D_MODEL, 2 * GW), c2),
            pl.BlockSpec((1, GW), c2),
            pl.BlockSpec((1, GW), c2),
            pl.BlockSpec((GW, GW), c2),
            pl.BlockSpec((GW, GW), c2),
            pl.BlockSpec((1, GW), c2),
            pl.BlockSpec((1, GW), c2),
        ],
        out_specs=[
            pl.BlockSpec((1, s, GW), lambda i: (i, 0, 0)),
            pl.BlockSpec((1, s, GW), lambda i: (i, 0, 0)),
        ],
        out_shape=[
            jax.ShapeDtypeStruct((b, s, GW), BF16),
            jax.ShapeDtypeStruct((b, s, GW), BF16),
        ],
        scratch_shapes=[
            pltpu.VMEM((s + 16, GW), F32),
            pltpu.VMEM((3, s + 16, GW), F32),
            pltpu.VMEM((nm, GW), BF16),
            pltpu.VMEM((NH * nm, GW), BF16),
        ],
        compiler_params=pltpu.CompilerParams(
            dimension_semantics=("arbitrary",), vmem_limit_bytes=VMEM_LIMIT),
    )(f3, p3, p3, p3, mem, mng, wkv, gmq, gmk, bd, wp, ps, win)


def _out_kernel(a_ref, b_ref, c_ref, d_ref, e_ref, w_ref, x_ref, o_ref):
    mixed = jnp.concatenate([a_ref[...], b_ref[...], c_ref[...], d_ref[...], e_ref[...]], axis=1)
    o_ref[...] = x_ref[...] + _dot(mixed, w_ref[...])


def _out(parts, w, x2, tm):
    m = x2.shape[0]
    gspec = pl.BlockSpec((tm, GW), lambda i: (i, 0))
    return pl.pallas_call(
        _out_kernel,
        grid=(m // tm,),
        in_specs=[gspec] * 5 + [
            pl.BlockSpec((5 * GW, D_MODEL), lambda i: (0, 0)),
            pl.BlockSpec((tm, D_MODEL), lambda i: (i, 0)),
        ],
        out_specs=pl.BlockSpec((tm, D_MODEL), lambda i: (i, 0)),
        out_shape=jax.ShapeDtypeStruct((m, D_MODEL), F32),
        compiler_params=pltpu.CompilerParams(
            dimension_semantics=("arbitrary",), vmem_limit_bytes=VMEM_LIMIT),
    )(*parts, w, x2)


def _out_proj_kernel(a_ref, b_ref, c_ref, d_ref, e_ref, w_ref, x_ref, g_ref, wb_ref, wf_ref,
                     o_ref, pb_ref, pf_ref):
    _out_kernel(a_ref, b_ref, c_ref, d_ref, e_ref, w_ref, x_ref, o_ref)
    _proj_kernel(o_ref, g_ref, wb_ref, wf_ref, pb_ref, pf_ref)


def _out_proj(parts, w, x2, g, wb, wf, tm):
    m = x2.shape[0]
    gspec = pl.BlockSpec((tm, GW), lambda i: (i, 0))
    c2 = lambda i: (0, 0)
    return pl.pallas_call(
        _out_proj_kernel,
        grid=(m // tm,),
        in_specs=[gspec] * 5 + [
            pl.BlockSpec((5 * GW, D_MODEL), c2),
            pl.BlockSpec((tm, D_MODEL), lambda i: (i, 0)),
            pl.BlockSpec((1, D_MODEL), c2),
            pl.BlockSpec((D_MODEL, NP_COLS), c2),
            pl.BlockSpec((D_MODEL, NF_COLS), c2),
        ],
        out_specs=[
            pl.BlockSpec((tm, D_MODEL), lambda i: (i, 0)),
            pl.BlockSpec((tm, NP_COLS), lambda i: (i, 0)),
            pl.BlockSpec((tm, NF_COLS), lambda i: (i, 0)),
        ],
        out_shape=[
            jax.ShapeDtypeStruct((m, D_MODEL), F32),
            jax.ShapeDtypeStruct((m, NP_COLS), BF16),
            jax.ShapeDtypeStruct((m, NF_COLS), F32),
        ],
        compiler_params=pltpu.CompilerParams(
            dimension_semantics=("arbitrary",), vmem_limit_bytes=VMEM_LIMIT),
    )(*parts, w, x2, g, wb, wf)


def _tile_heads(g):
    return jnp.tile(g.astype(F32), NH).reshape(1, GW)


def kernel(x, mem, norm_g, w_in, fox_f_bias, fox_q_norm, fox_k_norm, hgrn_lb_logits, hgrn_out_norm,
           pool_w, pool_scale, mem_norm_g, mem_w_kv, mem_q_norm, mem_k_norm, w_out):
    bsz, seq, _ = x.shape
    depth = w_in.shape[0]
    m = bsz * seq
    tq = 256
    tm = 512

    pr = jax.nn.softmax(hgrn_lb_logits.astype(F32), axis=0)
    lower_bounds = jnp.clip(jnp.cumsum(pr, axis=0) - pr[0:1], 0.0, 1.0 - 1e-6)

    bd_np = _bd_ones()
    bd = jnp.asarray(bd_np, BF16)
    tri = jnp.asarray(np.tril(np.ones((tq, tq), np.float32)), BF16)
    jj = np.arange(TQ)
    suffix = (jj[:, None] > jj[None, :]).astype(np.float32)
    to = jnp.asarray(np.concatenate([suffix, np.ones((TQ, TQ), np.float32)], axis=1), BF16)
    a_np, masks_np, hm_np = _hgrn_constants()
    a_all = jnp.asarray(a_np, BF16)
    masks = jnp.asarray(masks_np, F32)
    hm = jnp.asarray(hm_np, BF16)
    win = jnp.asarray(np.repeat(np.array(POOL_WINDOWS, np.float32), HD).reshape(1, GW))

    g = GW
    wb_all, wf_all = _regroup_w_in(w_in)
    x2 = x.reshape(m, D_MODEL)
    norm_gs = [norm_g[l].reshape(1, D_MODEL).astype(F32) for l in range(depth)]
    pb, pf = _proj(x2, norm_gs[0], wb_all[0], wf_all[0], tm)
    for l in range(depth):
        p3 = pb.reshape(bsz, seq, NP_COLS)
        f3 = pf.reshape(bsz, seq, NF_COLS)

        bias = jnp.concatenate([fox_f_bias[l].astype(F32), jnp.zeros((128 - NH,), F32)]).reshape(1, 128)
        out_a = _fox(p3, f3, bias, _tile_heads(fox_q_norm[l]), _tile_heads(fox_k_norm[l]), bd, tri)
        out_b = _sb(p3, to, bd)

        lb = lower_bounds[l].reshape(1, g)
        out_c = _hgrn(p3, f3, jnp.maximum(lb, LB_FLOOR), 1.0 - lb,
                      hgrn_out_norm[l].reshape(1, g).astype(F32), bd, a_all, masks, hm)

        wp = jax.scipy.linalg.block_diag(*[pool_w[l, i] for i in range(len(POOL_WINDOWS))]).astype(BF16)
        out_d, out_e = _pm(p3, f3, mem, mem_norm_g[l].reshape(1, D_MODEL).astype(F32),
                           mem_w_kv[l].astype(BF16), _tile_heads(mem_q_norm[l]), _tile_heads(mem_k_norm[l]),
                           bd, wp, pool_scale[l].reshape(1, g).astype(F32), win, tq)

        parts = [o.reshape(m, g) for o in (out_a, out_b, out_c, out_d, out_e)]
        if l + 1 < depth:
            x2, pb, pf = _out_proj(parts, w_out[l].astype(BF16), x2, norm_gs[l + 1],
                                   wb_all[l + 1], wf_all[l + 1], tm)
        else:
            x2 = _out(parts, w_out[l].astype(BF16), x2, tm)
    return x2.reshape(bsz, seq, D_MODEL)
```

```python
import functools

import numpy as np
import jax
import jax.numpy as jnp
from jax import lax
from jax.experimental import pallas as pl
from jax.experimental.pallas import tpu as pltpu

F32 = jnp.float32
BF16 = jnp.bfloat16

D_MODEL = 1024
GW = 256
NH = 4
HD = 64
CHUNK = 64
POOL_WINDOWS = (2, 4, 8, 16)
EPS = 1e-6
NEG_BIG = -1e30
LB_FLOOR = 1e-30
SCALE = HD ** -0.5

NP_COLS = 14 * GW
NF_COLS = 2 * GW + 128
VMEM_LIMIT = 56 * 1024 * 1024

TQ = 128
WIDE = 3 * TQ
EXP_ZERO = -104.0
NORM_SLACK = 1.01
LOGIT_SLACK = 0.05
HSB = 8
SOFTPLUS_LINEAR = 80.0


def _dot(a, b):
    return jnp.dot(a, b, preferred_element_type=F32)


def _dot_nt(a, b):
    return lax.dot_general(a, b, (((1,), (1,)), ((), ())), preferred_element_type=F32)


def _dot_tn(a, b):
    return lax.dot_general(a, b, (((0,), (0,)), ((), ())), preferred_element_type=F32)


def _split_bf16(x, n):
    parts = []
    r = x
    for i in range(n):
        p = r.astype(BF16)
        parts.append(p)
        if i + 1 < n:
            r = r - p.astype(F32)
    return parts


def _dot_f32_rhs01(x, m01, n=3):
    acc = None
    for p in _split_bf16(x, n):
        t = _dot(p, m01)
        acc = t if acc is None else acc + t
    return acc


def _silu(x):
    return x / (1.0 + jnp.exp(-x))


def _log_sigmoid(x):
    return jnp.minimum(x, 0.0) - jnp.log(1.0 + jnp.exp(-jnp.abs(x)))


def _head_rms(x, gain, bd):
    ss = _dot((x * x).astype(BF16), bd)
    return x * lax.rsqrt(ss * (1.0 / HD) + EPS) * gain


def _bd_ones():
    h = np.arange(GW) // HD
    return (h[:, None] == h[None, :]).astype(np.float32)


def _hgrn_constants():
    c = CHUNK
    t = np.arange(c)
    j = np.arange(c)[None, :]
    mats = []
    masks = []
    for n in (64, 32, 16, 8, 4):
        blk, pos = t // n, t % n
        ref = blk * n + n // 2 - 1
        aq = (pos[:, None] >= n // 2) & (j > ref[:, None]) & (j <= t[:, None])
        ak = (pos[:, None] < n // 2) & (j > t[:, None]) & (j <= ref[:, None])
        mats.append((aq | ak).astype(np.float32))
        m = (blk[:, None] == blk[None, :]) & (pos[:, None] >= n // 2) & (pos[None, :] < n // 2)
        masks.append(m.astype(np.float32))
    m = ((t[:, None] // 2) == (t[None, :] // 2)) & (t[None, :] <= t[:, None])
    masks.append(m.astype(np.float32))
    mats.append((j <= t[:, None]).astype(np.float32))
    mats.append((j > t[:, None]).astype(np.float32))
    a_all = np.concatenate(mats, axis=0)
    a_all = np.concatenate([a_all, a_all], axis=1)
    masks = np.stack([np.tile(m, (1, NH)) for m in masks])
    total = masks[:, :, :c].sum(0)
    assert np.array_equal(total, np.tril(np.ones((c, c), np.float32)))
    hm = (np.arange(NH * c)[:, None] // c == np.arange(GW)[None, :] // HD).astype(np.float32)
    return a_all, masks, hm


def _regroup_kernel(w_ref, wb_ref, wf_ref):
    g = GW
    o_ff, o_sb, o_hg, o_pl, o_mm = 4 * g, 4 * g + NH, 8 * g + NH, 12 * g + NH, 14 * g + NH
    w = w_ref[0]
    bf_src = [0, g, 2 * g, 3 * g,
              o_sb, o_sb + g, o_sb + 2 * g, o_sb + 3 * g,
              o_hg, o_hg + 2 * g, o_hg + 3 * g,
              o_pl + g,
              o_mm, o_mm + g]
    for i, c0 in enumerate(bf_src):
        wb_ref[0, :, i * g:(i + 1) * g] = w[:, c0:c0 + g].astype(BF16)
    wf_ref[0, :, 0:g] = w[:, o_hg + g:o_hg + 2 * g].astype(BF16)
    wf_ref[0, :, g:2 * g] = w[:, o_pl:o_pl + g].astype(BF16)
    ff = jnp.concatenate([w[:, o_ff:o_ff + NH], jnp.zeros((w.shape[0], 128 - NH), F32)], axis=1)
    wf_ref[0, :, 2 * g:2 * g + 128] = ff.astype(BF16)


def _regroup_w_in(w_in):
    depth, d, n = w_in.shape
    tr = 128
    return pl.pallas_call(
        _regroup_kernel,
        grid=(depth, d // tr),
        in_specs=[pl.BlockSpec((1, tr, n), lambda l, i: (l, i, 0))],
        out_specs=[
            pl.BlockSpec((1, tr, NP_COLS), lambda l, i: (l, i, 0)),
            pl.BlockSpec((1, tr, NF_COLS), lambda l, i: (l, i, 0)),
        ],
        out_shape=[
            jax.ShapeDtypeStruct((depth, d, NP_COLS), BF16),
            jax.ShapeDtypeStruct((depth, d, NF_COLS), BF16),
        ],
        compiler_params=pltpu.CompilerParams(
            dimension_semantics=("arbitrary", "arbitrary"), vmem_limit_bytes=VMEM_LIMIT),
    )(w_in)


def _proj_kernel(x_ref, g_ref, wb_ref, wf_ref, pb_ref, pf_ref):
    x = x_ref[...]
    ms = jnp.mean(x * x, axis=-1, keepdims=True)
    h = (x * lax.rsqrt(ms + EPS) * g_ref[...]).astype(BF16)
    nb = 512
    for c0 in range(0, NP_COLS, nb):
        pb_ref[:, c0:c0 + nb] = _dot(h, wb_ref[:, c0:c0 + nb]).astype(BF16)
    pf_ref[...] = _dot(h, wf_ref[...])


def _proj(x2, g, wb, wf, tm):
    m = x2.shape[0]
    return pl.pallas_call(
        _proj_kernel,
        grid=(m // tm,),
        in_specs=[
            pl.BlockSpec((tm, D_MODEL), lambda i: (i, 0)),
            pl.BlockSpec((1, D_MODEL), lambda i: (0, 0)),
            pl.BlockSpec((D_MODEL, NP_COLS), lambda i: (0, 0)),
            pl.BlockSpec((D_MODEL, NF_COLS), lambda i: (0, 0)),
        ],
        out_specs=[
            pl.BlockSpec((tm, NP_COLS), lambda i: (i, 0)),
            pl.BlockSpec((tm, NF_COLS), lambda i: (i, 0)),
        ],
        out_shape=[
            jax.ShapeDtypeStruct((m, NP_COLS), BF16),
            jax.ShapeDtypeStruct((m, NF_COLS), F32),
        ],
        compiler_params=pltpu.CompilerParams(
            dimension_semantics=("arbitrary",), vmem_limit_bytes=VMEM_LIMIT),
    )(x2, g, wb, wf)


def _fox_kernel(p_ref, ff_ref, bias_ref, gq_ref, gk_ref, bd_ref, tri_ref, o_ref,
                qs, ks, vx, crow, qk_buf, *, seq):
    bd = bd_ref[...]
    tri = tri_ref[...]
    pb = tri.shape[0]
    hmask = [bd[h * HD:h * HD + 1, :] for h in range(NH)]
    hmask_f = [m.astype(F32) for m in hmask]
    qk_max = (HD * SCALE * NORM_SLACK) * (jnp.max(jnp.abs(gq_ref[...]), axis=-1, keepdims=True)
                                          * jnp.max(jnp.abs(gk_ref[...]), axis=-1, keepdims=True))
    carry = jnp.zeros((8, 1), F32)
    for b in range(seq // pb):
        r = slice(b * pb, (b + 1) * pb)
        lf = _log_sigmoid(ff_ref[0, r, :].T[:8, :] + bias_ref[...])
        cb = _dot_f32_rhs01(lf, tri, 3) + carry
        carry = cb[:, pb - 1:pb]
        crow[:, r] = cb
        q =_head_rms(p_ref[0, r, 0:GW].astype(F32), gq_ref[...], bd) * SCALE
        k = _head_rms(p_ref[0, r, GW:2 * GW].astype(F32), gk_ref[...], bd)
        qs[r, :] = q.astype(BF16)
        ks[r, :] = k.astype(BF16)
        v = p_ref[0, r, 2 * GW:3 * GW]
        for h in range(NH):
            vx[h, r, :] = v * hmask[h]

    rel_w = (lax.broadcasted_iota(jnp.int32, (TQ, WIDE), 1) - lax.broadcasted_iota(jnp.int32, (TQ, WIDE), 0))
    nfull = (WIDE - TQ) // TQ

    def qk_all_heads(r0, start, width):
        qb = qs[pl.ds(r0, TQ), :]
        qx = jnp.concatenate([qb * m for m in hmask], axis=0)
        return _dot_nt(qx, ks[pl.ds(start, width), :])

    def wide_start(r0):
        return pl.multiple_of(jnp.maximum(r0 - (WIDE - TQ), 0), TQ)

    def wide_qk(i):
        r0 = pl.multiple_of(i * TQ, TQ)
        return qk_all_heads(r0, wide_start(r0), WIDE)

    def q_block(i, full_tile, qk_wide):
        r0 = pl.multiple_of(i * TQ, TQ)
        s0 = wide_start(r0)
        rows = pl.ds(r0, TQ)

        def mask_wide(s):
            if full_tile:
                diag = jnp.where(rel_w[:, WIDE - TQ:] <= WIDE - TQ, s[:, WIDE - TQ:], NEG_BIG)
                return jnp.concatenate([s[:, :WIDE - TQ], diag], axis=1)
            return jnp.where(rel_w <= r0 - s0, s, NEG_BIG)

        def tiles(qk, start, width):
            return [qk[h * TQ:(h + 1) * TQ] - crow[h:h + 1, pl.ds(start, width)] for h in range(NH)]

        def weighted_values(ps, start, width):
            p_all = jnp.concatenate([p.astype(BF16) for p in ps], axis=1)
            v_all = jnp.concatenate([vx[h, pl.ds(start, width), :] for h in range(NH)], axis=0)
            return _dot(p_all, v_all)

        def per_head_lanes(cols):
            out = cols[0] * hmask_f[0]
            for h in range(1, NH):
                out = out + cols[h] * hmask_f[h]
            return out

        def keep_going(s_end, ms):
            last = pl.multiple_of(jnp.maximum(s_end - TQ, 0), TQ)
            bound = None
            for h in range(NH):
                c_last = crow[h:h + 1, pl.ds(last, TQ)][:, TQ - 1:TQ]
                b_h = (qk_max + LOGIT_SLACK) - c_last - jnp.min(ms[h], axis=0, keepdims=True)
                bound = b_h if bound is None else jnp.maximum(bound, b_h)
            return (jnp.max(bound) > EXP_ZERO).astype(jnp.int32)

        ss = [mask_wide(s) for s in tiles(qk_wide, s0, WIDE)]
        ms = [jnp.max(s, axis=-1, keepdims=True) for s in ss]
        ps = [jnp.exp(s - m) for s, m in zip(ss, ms)]
        ls = [jnp.sum(p, axis=-1, keepdims=True) for p in ps]
        acc = weighted_values(ps, s0, WIDE)

        def emit(acc, ls):
            g = p_ref[0, rows, 3 * GW:4 * GW].astype(F32)
            o_ref[0, rows, :] = (acc * per_head_lanes([1.0 / l for l in ls]) * _silu(g)).astype(BF16)

        emit(acc, ls)

        def cond(st):
            return jnp.logical_and(st[0] > 0, st[1] > 0)

        def body(st):
            s_end, _, ms, ls, acc = st
            sb = pl.multiple_of(s_end - TQ, TQ)
            ss = tiles(qk_all_heads(r0, sb, TQ), sb, TQ)
            ms2 = [jnp.maximum(m, jnp.max(s, axis=-1, keepdims=True)) for s, m in zip(ss, ms)]
            ps = [jnp.exp(s - m) for s, m in zip(ss, ms2)]
            alphas = [jnp.exp(m - m2) for m, m2 in zip(ms, ms2)]
            ls2 = [a * l + jnp.sum(p, axis=-1, keepdims=True) for a, l, p in zip(alphas, ls, ps)]
            acc2 = acc * per_head_lanes(alphas) + weighted_values(ps, sb, TQ)
            return sb, keep_going(sb, ms2), tuple(ms2), tuple(ls2), acc2

        def rare_tail():
            @pl.when(jnp.logical_and(s0 > 0, keep_going(s0, ms) > 0))
            def _():
                _, _, _, ls_f, acc_f = lax.while_loop(cond, body, (s0, jnp.int32(1), tuple(ms), tuple(ls), acc))
                emit(acc_f, ls_f)

        return rare_tail

    def first_blocks(i, _):
        q_block(i, False, wide_qk(i))
        return 0

    lax.fori_loop(0, nfull, first_blocks, 0)

    nblk = seq // TQ
    qk_buf[0] = wide_qk(nfull)

    def block_pair(j, _):
        i = nfull + 2 * j
        qk_buf[1] = wide_qk(i + 1)
        q_block(i, True, qk_buf[0])()
        qk_buf[0] = wide_qk(jnp.minimum(i + 2, nblk - 1))
        q_block(i + 1, True, qk_buf[1])()
        return 0

    lax.fori_loop(0, (nblk - nfull) // 2, block_pair, 0)


def _fox(p3, f3, bias, gq, gk, bd, tri):
    b, s, _ = p3.shape
    assert s % tri.shape[0] == 0 and s >= WIDE and (s // TQ - (WIDE - TQ) // TQ) % 2 == 0
    kern = functools.partial(_fox_kernel, seq=s)
    c2 = lambda i: (0, 0)
    return pl.pallas_call(
        kern,
        grid=(b,),
        in_specs=[
            pl.BlockSpec((1, s, 4 * GW), lambda i: (i, 0, 0)),
            pl.BlockSpec((1, s, 128), lambda i: (i, 0, 4)),
            pl.BlockSpec((8, tri.shape[0]), c2),
            pl.BlockSpec((1, GW), c2),
            pl.BlockSpec((1, GW), c2),
            pl.BlockSpec((GW, GW), c2),
            pl.BlockSpec(tri.shape, c2),
        ],
        out_specs=pl.BlockSpec((1, s, GW), lambda i: (i, 0, 0)),
        out_shape=jax.ShapeDtypeStruct((b, s, GW), BF16),
        scratch_shapes=[
            pltpu.VMEM((s, GW), BF16),
            pltpu.VMEM((s, GW), BF16),
            pltpu.VMEM((NH, s, GW), BF16),
            pltpu.VMEM((8, s), F32),
            pltpu.VMEM((2, NH * TQ, WIDE), F32),
        ],
        compiler_params=pltpu.CompilerParams(
            dimension_semantics=("arbitrary",), vmem_limit_bytes=VMEM_LIMIT),
    )(p3, f3, bias, gq, gk, bd, tri)


def _sb_kernel(p_ref, to_ref, bd_ref, o_ref, vx, qk_buf, *, seq):
    to = to_ref[...]
    hmask = [bd_ref[h * HD:h * HD + 1, :] for h in range(NH)]
    qmask = [m * SCALE for m in hmask]
    pb = 256
    for b in range(seq // pb):
        r = slice(b * pb, (b + 1) * pb)
        v = p_ref[0, r, 2 * GW:3 * GW]
        for h in range(NH):
            vx[h, r, :] = v * hmask[h]

    rel_w = (lax.broadcasted_iota(jnp.int32, (TQ, WIDE), 1) - lax.broadcasted_iota(jnp.int32, (TQ, WIDE), 0))
    nsub = WIDE // TQ
    nfull = nsub - 1

    def log_one_minus_sigmoid(z):
        return -jnp.maximum(z, jnp.log(1.0 + jnp.exp(jnp.minimum(z, SOFTPLUS_LINEAR))))

    def suffix_sums(lom):
        cs = _dot(lom.astype(BF16), to)
        return cs[:, :TQ], cs[:, TQ:]

    def qk_all_heads(r0, start, width):
        qb = p_ref[0, pl.ds(r0, TQ), 0:GW]
        qx = jnp.concatenate([qb * m for m in qmask], axis=0)
        return _dot_nt(qx, p_ref[0, pl.ds(start, width), GW:2 * GW])

    def wide_start(r0):
        return pl.multiple_of(jnp.maximum(r0 - (WIDE - TQ), 0), TQ)

    def wide_qk(i):
        r0 = pl.multiple_of(i * TQ, TQ)
        return qk_all_heads(r0, wide_start(r0), WIDE)

    def q_block(i, full_tile, qk_wide):
        r0 = pl.multiple_of(i * TQ, TQ)
        s0 = wide_start(r0)
        rows = pl.ds(r0, TQ)

        def mask_wide(x):
            if full_tile:
                diag = jnp.where(rel_w[:, WIDE - TQ:] < WIDE - TQ, x[:, WIDE - TQ:], 0.0)
                return jnp.concatenate([x[:, :WIDE - TQ], diag], axis=1)
            return jnp.where(rel_w < r0 - s0, x, 0.0)

        def per_head(qk):
            return [qk[h * TQ:(h + 1) * TQ] for h in range(NH)]

        def weighted_values(ws, start, width):
            w_all = jnp.concatenate([w.astype(BF16) for w in ws], axis=1)
            v_all = jnp.concatenate([vx[h, pl.ds(start, width), :] for h in range(NH)], axis=0)
            return _dot(w_all, v_all)

        def emit(acc):
            g = p_ref[0, rows, 3 * GW:4 * GW].astype(F32)
            o_ref[0, rows, :] = (acc * _silu(g)).astype(BF16)

        zs = per_head(qk_wide)
        loms = [mask_wide(log_one_minus_sigmoid(z)) for z in zs]
        sums = [[suffix_sums(lom[:, c * TQ:(c + 1) * TQ]) for c in range(nsub)] for lom in loms]
        carries, ws = [], []
        for h in range(NH):
            between = [None] * nsub
            carry = jnp.zeros((TQ, TQ), F32)
            for c in reversed(range(nsub)):
                rc, tot = sums[h][c]
                between[c] = rc + carry
                carry = carry + tot
            ws.append(mask_wide(jnp.exp(zs[h] + loms[h] + jnp.concatenate(between, axis=1))))
            carries.append(carry)
        acc = weighted_values(ws, s0, WIDE)
        emit(acc)

        def keep_going(carries):
            cm = jnp.maximum(jnp.maximum(carries[0], carries[1]), jnp.maximum(carries[2], carries[3]))
            return (jnp.max(cm) > EXP_ZERO).astype(jnp.int32)

        def cond(st):
            return jnp.logical_and(st[0] > 0, st[1] > 0)

        def body(st):
            s_end, _, carries, acc = st
            sb = pl.multiple_of(s_end - TQ, TQ)
            zs = per_head(qk_all_heads(r0, sb, TQ))
            loms = [log_one_minus_sigmoid(z) for z in zs]
            sums = [suffix_sums(lom) for lom in loms]
            ws = [jnp.exp(z + lom + rc + cr) for z, lom, (rc, _), cr in zip(zs, loms, sums, carries)]
            c2 = [cr + tot for cr, (_, tot) in zip(carries, sums)]
            return sb, keep_going(c2), tuple(c2), acc + weighted_values(ws, sb, TQ)

        def rare_tail():
            @pl.when(jnp.logical_and(s0 > 0, keep_going(carries) > 0))
            def _():
                st = lax.while_loop(cond, body, (s0, jnp.int32(1), tuple(carries), acc))
                emit(st[3])

        return rare_tail

    def first_blocks(i, _):
        q_block(i, False, wide_qk(i))
        return 0

    lax.fori_loop(0, nfull, first_blocks, 0)

    nblk = seq // TQ
    qk_buf[0] = wide_qk(nfull)
    qk_buf[1] = wide_qk(nfull + 1)

    def block_pair(j, _):
        i = nfull + 2 * j
        rare_a = q_block(i, True, qk_buf[0])
        rare_b = q_block(i + 1, True, qk_buf[1])
        qk_buf[0] = wide_qk(jnp.minimum(i + 2, nblk - 1))
        qk_buf[1] = wide_qk(jnp.minimum(i + 3, nblk - 1))
        rare_a()
        rare_b()
        return 0

    lax.fori_loop(0, (nblk - nfull) // 2, block_pair, 0)


def _sb(p3, to, bd):
    b, s, _ = p3.shape
    assert s % 256 == 0 and s >= WIDE and (s // TQ - (WIDE - TQ) // TQ) % 2 == 0
    kern = functools.partial(_sb_kernel, seq=s)
    return pl.pallas_call(
        kern,
        grid=(b,),
        in_specs=[
            pl.BlockSpec((1, s, 4 * GW), lambda i: (i, 0, 1)),
            pl.BlockSpec((TQ, 2 * TQ), lambda i: (0, 0)),
            pl.BlockSpec((GW, GW), lambda i: (0, 0)),
        ],
        out_specs=pl.BlockSpec((1, s, GW), lambda i: (i, 0, 0)),
        out_shape=jax.ShapeDtypeStruct((b, s, GW), BF16),
        scratch_shapes=[
            pltpu.VMEM((NH, s, GW), BF16),
            pltpu.VMEM((2, NH * TQ, WIDE), F32),
        ],
        compiler_params=pltpu.CompilerParams(
            dimension_semantics=("arbitrary",), vmem_limit_bytes=VMEM_LIMIT),
    )(p3, to, bd)


def _hgrn_kernel(hq_ref, hi_ref, hg_ref, hf_ref, lbm_ref, oml_ref, gout_ref, bd_ref, a_ref,
                 mask_ref, hm_ref, o_ref, st_ref, *, seq):
    bd = bd_ref[...]
    a_all = a_ref[...]
    hm = hm_ref[...]
    lbm = lbm_ref[...]
    oml = oml_ref[...]
    c = CHUNK
    st_ref[...] = jnp.zeros((GW, GW), F32)

    n = HSB * c
    nlev = mask_ref.shape[0] - 1
    odd =(lax.broadcasted_iota(jnp.int32, (n, GW), 0) & 1) == 1

    def superblock(bi, _):
        r0 = pl.multiple_of(bi * n, n)
        rows = pl.ds(r0, n)
        hf = hf_ref[0, rows, :]
        sg = 1.0 / (1.0 + jnp.exp(-hf))
        f = lbm + oml * sg
        g = jnp.log(f)
        kk = oml * (1.0 - sg)
        q = _silu(hq_ref[0, rows, :].astype(F32))
        v = hi_ref[0, rows, :]
        gh = g.astype(BF16)
        gl = (g - gh.astype(F32)).astype(BF16)
        q2 = (q * jnp.where(odd, f, 1.0)).astype(BF16)
        k2 = (kk * jnp.where(odd, 1.0 / f, 1.0)).astype(BF16)

        def scores(qf, kf, l):
            kx = jnp.concatenate([kf] * NH, axis=0) * hm
            return _dot_nt(qf, kx) * mask_ref[l]

        sls = [slice(ci * c, (ci + 1) * c) for ci in range(HSB)]
        exs = [_dot(a_all, jnp.concatenate([gh[sl], gl[sl]], axis=0)) for sl in sls]
        ps = [scores(q2[sl], k2[sl], nlev) for sl in sls]
        qb = q.astype(BF16)
        kb = kk.astype(BF16)
        for l in range(nlev):
            for ci, sl in enumerate(sls):
                x = jnp.exp(exs[ci][l * c:(l + 1) * c]).astype(BF16)
                ps[ci] = ps[ci] + scores(qb[sl] * x, kb[sl] * x, l)
        o_intra, qd, upd, dl = [], [], [], []
        for ci, sl in enumerate(sls):
            vx = jnp.concatenate([v[sl]] * NH, axis=0) * hm
            o_intra.append(_dot(ps[ci].astype(BF16), vx))
            eb = exs[ci][nlev * c:(nlev + 1) * c]
            er = exs[ci][(nlev + 1) * c:(nlev + 2) * c]
            qd.append((q[sl] * jnp.exp(eb)).astype(BF16))
            upd.append(_dot_tn(v[sl], (kk[sl] * jnp.exp(er)).astype(BF16)))
            dl.append(jnp.exp(eb[c - 1:c, :]))

        st = st_ref[...]
        outs = []
        for ci in range(HSB):
            outs.append(o_intra[ci] + _dot_nt(qd[ci], st.astype(BF16) * bd))
            st = st * dl[ci] + upd[ci]
        st_ref[...] = st

        o = _head_rms(jnp.concatenate(outs, axis=0), gout_ref[...], bd)
        o_ref[0, rows, :] = (o * _silu(hg_ref[0, rows, :].astype(F32))).astype(BF16)
        return 0

    lax.fori_loop(0, seq // n, superblock, 0)


def _hgrn(p3, f3, lbm, oml, gout, bd, a_all, masks, hm):
    b, s, _ = p3.shape
    assert s % (HSB * CHUNK) == 0
    kern = functools.partial(_hgrn_kernel, seq=s)
    c2 = lambda i: (0, 0)
    return pl.pallas_call(
        kern,
        grid=(b,),
        in_specs=[
            pl.BlockSpec((1, s, GW), lambda i: (i, 0, 8)),
            pl.BlockSpec((1, s, GW), lambda i: (i, 0, 9)),
            pl.BlockSpec((1, s, GW), lambda i: (i, 0, 10)),
            pl.BlockSpec((1, s, GW), lambda i: (i, 0, 0)),
            pl.BlockSpec((1, GW), c2),
            pl.BlockSpec((1, GW), c2),
            pl.BlockSpec((1, GW), c2),
            pl.BlockSpec((GW, GW), c2),
            pl.BlockSpec(a_all.shape, c2),
            pl.BlockSpec(masks.shape, lambda i: (0, 0, 0)),
            pl.BlockSpec(hm.shape, c2),
        ],
        out_specs=pl.BlockSpec((1, s, GW), lambda i: (i, 0, 0)),
        out_shape=jax.ShapeDtypeStruct((b, s, GW), BF16),
        scratch_shapes=[pltpu.VMEM((GW, GW), F32)],
        compiler_params=pltpu.CompilerParams(
            dimension_semantics=("arbitrary",), vmem_limit_bytes=VMEM_LIMIT),
    )(p3, p3, p3, f3, lbm, oml, gout, bd, a_all, masks, hm)


def _pm_kernel(pv_ref, pg_ref, mq_ref, mg_ref, mem_ref, mng_ref, wkv_ref, gmq_ref, gmk_ref, bd_ref,
               wp_ref, ps_ref, win_ref, d_ref, e_ref, ubuf, wbuf, kmem, vxm, *, seq, tq):
    bd = bd_ref[...]
    hmask = [bd[h * HD:h * HD + 1, :] for h in range(NH)]
    hmask_f = [m.astype(F32) for m in hmask]
    halo = 16
    mem = mem_ref[0]
    ms = jnp.mean(mem * mem, axis=-1, keepdims=True)
    mn = (mem * lax.rsqrt(ms + EPS) * mng_ref[...]).astype(BF16)
    kv = _dot(mn, wkv_ref[...])
    kn = _head_rms(kv[:, :GW], gmk_ref[...], bd).astype(BF16)
    vv = kv[:, GW:].astype(BF16)
    nm = kn.shape[0]
    kmem[...] = kn
    for h in range(NH):
        vxm[h * nm:(h + 1) * nm, :] = vv * hmask[h]

    ubuf[0:halo, :] = jnp.zeros((halo, GW), F32)
    ubuf[halo:halo + seq, :] = pv_ref[0]
    win = win_ref[...]
    inv_win = 1.0 / win
    for k in range(3):
        sh = 1 << k
        wbuf[k, 0:halo, :] = jnp.zeros((halo, GW), F32)
        for b in range(seq // tq):
            base = halo + b * tq
            if k == 0:
                wbuf[k, base:base + tq, :] = ubuf[base:base + tq, :] + ubuf[base - sh:base - sh + tq, :]
            else:
                wbuf[k, base:base + tq, :] = (wbuf[k - 1, base:base + tq, :]
                                              + wbuf[k - 1, base - sh:base - sh + tq, :])

    for b in range(seq // tq):
        r = slice(b * tq, (b + 1) * tq)
        base = halo + b * tq
        u = ubuf[base:base + tq, :]
        s2 = wbuf[0, base:base + tq, :]
        s4 = wbuf[1, base:base + tq, :]
        s8 = wbuf[2, base:base + tq, :]
        s16 = s8 + wbuf[2, base - 8:base - 8 + tq, :]
        sw = jnp.where(win == 2.0, s2, jnp.where(win == 4.0, s4, jnp.where(win == 8.0, s8, s16)))
        if b == 0:
            pos = (lax.broadcasted_iota(jnp.int32, (tq, GW), 0) + 1).astype(F32)
            pooled = sw / jnp.minimum(pos, win)
        else:
            pooled = sw * inv_win
        y = _dot((pooled - u).astype(BF16), wp_ref[...]) * ps_ref[...]
        d_ref[0, r, :] = (y * _silu(pg_ref[0, r, :].astype(F32))).astype(BF16)

        qn = (_head_rms(mq_ref[0, r, :].astype(F32), gmq_ref[...], bd) * SCALE).astype(BF16)
        s_all = _dot_nt(jnp.concatenate([qn * m for m in hmask], axis=0), kmem[...])
        ss = [s_all[h * tq:(h + 1) * tq] for h in range(NH)]
        ps = [jnp.exp(s - jnp.max(s, axis=-1, keepdims=True)) for s in ss]
        ls = [jnp.sum(p, axis=-1, keepdims=True) for p in ps]
        inv_l = (1.0 / ls[0]) * hmask_f[0]
        for h in range(1, NH):
            inv_l = inv_l + (1.0 / ls[h]) * hmask_f[h]
        oe = _dot(jnp.concatenate([p.astype(BF16) for p in ps], axis=1), vxm[...]) * inv_l
        e_ref[0, r, :] = (oe * _silu(mg_ref[0, r, :].astype(F32))).astype(BF16)


def _pm(p3, f3, mem, mng, wkv, gmq, gmk, bd, wp, ps, win, tq):
    b, s, _ = p3.shape
    nm = mem.shape[1]
    kern = functools.partial(_pm_kernel, seq=s, tq=tq)
    c2 = lambda i: (0, 0)
    return pl.pallas_call(
        kern,
        grid=(b,),
        in_specs=[
            pl.BlockSpec((1, s, GW), lambda i: (i, 0, 1)),
            pl.BlockSpec((1, s, GW), lambda i: (i, 0, 11)),
            pl.BlockSpec((1, s, GW), lambda i: (i, 0, 12)),
            pl.BlockSpec((1, s, GW), lambda i: (i, 0, 13)),
            pl.BlockSpec((1, nm, D_MODEL), lambda i: (i, 0, 0)),
            pl.BlockSpec((1, D_MODEL), c2),
            pl.BlockSpec((D_MODEL, 2 * GW), c2),
            pl.BlockSpec((1, GW), c2),
            pl.BlockSpec((1, GW), c2),
            pl.BlockSpec((GW, GW), c2),
            pl.BlockSpec((GW, GW), c2),
            pl.BlockSpec((1, GW), c2),
            pl.BlockSpec((1, GW), c2),
        ],
        out_specs=[
            pl.BlockSpec((1, s, GW), lambda i: (i, 0, 0)),
            pl.BlockSpec((1, s, GW), lambda i: (i, 0, 0)),
        ],
        out_shape=[
            jax.ShapeDtypeStruct((b, s, GW), BF16),
            jax.ShapeDtypeStruct((b, s, GW), BF16),
        ],
        scratch_shapes=[
            pltpu.VMEM((s + 16, GW), F32),
            pltpu.VMEM((3, s + 16, GW), F32),
            pltpu.VMEM((nm, GW), BF16),
            pltpu.VMEM((NH * nm, GW), BF16),
        ],
        compiler_params=pltpu.CompilerParams(
            dimension_semantics=("arbitrary",), vmem_limit_bytes=VMEM_LIMIT),
    )(f3, p3, p3, p3, mem, mng, wkv, gmq, gmk, bd, wp, ps, win)


def _out_kernel(a_ref, b_ref, c_ref, d_ref, e_ref, w_ref, x_ref, o_ref):
    mixed = jnp.concatenate([a_ref[...], b_ref[...], c_ref[...], d_ref[...], e_ref[...]], axis=1)
    o_ref[...] = x_ref[...] + _dot(mixed, w_ref[...])


def _out(parts, w, x2, tm):
    m = x2.shape[0]
    gspec = pl.BlockSpec((tm, GW), lambda i: (i, 0))
    return pl.pallas_call(
        _out_kernel,
        grid=(m // tm,),
        in_specs=[gspec] * 5 + [
            pl.BlockSpec((5 * GW, D_MODEL), lambda i: (0, 0)),
            pl.BlockSpec((tm, D_MODEL), lambda i: (i, 0)),
        ],
        out_specs=pl.BlockSpec((tm, D_MODEL), lambda i: (i, 0)),
        out_shape=jax.ShapeDtypeStruct((m, D_MODEL), F32),
        compiler_params=pltpu.CompilerParams(
            dimension_semantics=("arbitrary",), vmem_limit_bytes=VMEM_LIMIT),
    )(*parts, w, x2)


def _out_proj_kernel(a_ref, b_ref, c_ref, d_ref, e_ref, w_ref, x_ref, g_ref, wb_ref, wf_ref,
                     o_ref, pb_ref, pf_ref):
    _out_kernel(a_ref, b_ref, c_ref, d_ref, e_ref, w_ref, x_ref, o_ref)
    _proj_kernel(o_ref, g_ref, wb_ref, wf_ref, pb_ref, pf_ref)


def _out_proj(parts, w, x2, g, wb, wf, tm):
    m = x2.shape[0]
    gspec = pl.BlockSpec((tm, GW), lambda i: (i, 0))
    c2 = lambda i: (0, 0)
    return pl.pallas_call(
        _out_proj_kernel,
        grid=(m // tm,),
        in_specs=[gspec] * 5 + [
            pl.BlockSpec((5 * GW, D_MODEL), c2),
            pl.BlockSpec((tm, D_MODEL), lambda i: (i, 0)),
            pl.BlockSpec((1, D_MODEL), c2),
            pl.BlockSpec((D_MODEL, NP_COLS), c2),
            pl.BlockSpec((D_MODEL, NF_COLS), c2),
        ],
        out_specs=[
            pl.BlockSpec((tm, D_MODEL), lambda i: (i, 0)),
            pl.BlockSpec((tm, NP_COLS), lambda i: (i, 0)),
            pl.BlockSpec((tm, NF_COLS), lambda i: (i, 0)),
        ],
        out_shape=[
            jax.ShapeDtypeStruct((m, D_MODEL), F32),
            jax.ShapeDtypeStruct((m, NP_COLS), BF16),
            jax.ShapeDtypeStruct((m, NF_COLS), F32),
        ],
        compiler_params=pltpu.CompilerParams(
            dimension_semantics=("arbitrary",), vmem_limit_bytes=VMEM_LIMIT),
    )(*parts, w, x2, g, wb, wf)


def _tile_heads(g):
    return jnp.tile(g.astype(F32), NH).reshape(1, GW)


def kernel(x, mem, norm_g, w_in, fox_f_bias, fox_q_norm, fox_k_norm, hgrn_lb_logits, hgrn_out_norm,
           pool_w, pool_scale, mem_norm_g, mem_w_kv, mem_q_norm, mem_k_norm, w_out):
    bsz, seq, _ = x.shape
    depth = w_in.shape[0]
    m = bsz * seq
    tq = 256
    tm = 512

    pr = jax.nn.softmax(hgrn_lb_logits.astype(F32), axis=0)
    lower_bounds = jnp.clip(jnp.cumsum(pr, axis=0) - pr[0:1], 0.0, 1.0 - 1e-6)

    bd_np = _bd_ones()
    bd = jnp.asarray(bd_np, BF16)
    tri = jnp.asarray(np.triu(np.ones((tq, tq), np.float32)), BF16)
    jj = np.arange(TQ)
    suffix = (jj[:, None] > jj[None, :]).astype(np.float32)
    to = jnp.asarray(np.concatenate([suffix, np.ones((TQ, TQ), np.float32)], axis=1), BF16)
    a_np, masks_np, hm_np = _hgrn_constants()
    a_all = jnp.asarray(a_np, BF16)
    masks = jnp.asarray(masks_np, F32)
    hm = jnp.asarray(hm_np, BF16)
    win = jnp.asarray(np.repeat(np.array(POOL_WINDOWS, np.float32), HD).reshape(1, GW))

    g = GW
    wb_all, wf_all = _regroup_w_in(w_in)
    x2 = x.reshape(m, D_MODEL)
    norm_gs = [norm_g[l].reshape(1, D_MODEL).astype(F32) for l in range(depth)]
    pb, pf = _proj(x2, norm_gs[0], wb_all[0], wf_all[0], tm)
    for l in range(depth):
        p3 = pb.reshape(bsz, seq, NP_COLS)
        f3 = pf.reshape(bsz, seq, NF_COLS)

        bias = jnp.broadcast_to(jnp.pad(fox_f_bias[l].astype(F32), (0, 8 - NH))[:, None], (8, tq))
        out_a = _fox(p3, f3, bias, _tile_heads(fox_q_norm[l]), _tile_heads(fox_k_norm[l]), bd, tri)
        out_b = _sb(p3, to, bd)

        lb = lower_bounds[l].reshape(1, g)
        out_c = _hgrn(p3, f3, jnp.maximum(lb, LB_FLOOR), 1.0 - lb,
                      hgrn_out_norm[l].reshape(1, g).astype(F32), bd, a_all, masks, hm)

        wp = jax.scipy.linalg.block_diag(*[pool_w[l, i] for i in range(len(POOL_WINDOWS))]).astype(BF16)
        out_d, out_e = _pm(p3, f3, mem, mem_norm_g[l].reshape(1, D_MODEL).astype(F32),
                           mem_w_kv[l].astype(BF16), _tile_heads(mem_q_norm[l]), _tile_heads(mem_k_norm[l]),
                           bd, wp, pool_scale[l].reshape(1, g).astype(F32), win, tq)

        parts = [o.reshape(m, g) for o in (out_a, out_b, out_c, out_d, out_e)]
        if l + 1 < depth:
            x2, pb, pf = _out_proj(parts, w_out[l].astype(BF16), x2, norm_gs[l + 1],
                                   wb_all[l + 1], wf_all[l + 1], tm)
        else:
            x2 = _out(parts, w_out[l].astype(BF16), x2, tm)
    return x2.reshape(bsz, seq, D_MODEL)
```

```python
import functools

import numpy as np
import jax
import jax.numpy as jnp
from jax import lax
from jax.experimental import pallas as pl
from jax.experimental.pallas import tpu as pltpu

F32 = jnp.float32
BF16 = jnp.bfloat16

D_MODEL = 1024
GW = 256
NH = 4
HD = 64
CHUNK = 64
POOL_WINDOWS = (2, 4, 8, 16)
EPS = 1e-6
NEG_BIG = -1e30
LB_FLOOR = 1e-30
SCALE = HD ** -0.5

NP_COLS = 14 * GW
NF_COLS = 2 * GW + 128
VMEM_LIMIT = 56 * 1024 * 1024

TQ = 128
WIDE = 3 * TQ
EXP_ZERO = -104.0
NORM_SLACK = 1.01
LOGIT_SLACK = 0.05
FOX_UNROLL = 2
HSB = 8
SB_TQ = 128
SB_WIDE = 384
SB_CW = 128
SB_UNROLL = 2
SOFTPLUS_LINEAR = 80.0


def _dot(a, b):
    return jnp.dot(a, b, preferred_element_type=F32)


def _dot_nt(a, b):
    return lax.dot_general(a, b, (((1,), (1,)), ((), ())), preferred_element_type=F32)


def _dot_tn(a, b):
    return lax.dot_general(a, b, (((0,), (0,)), ((), ())), preferred_element_type=F32)


def _aligned(x, m):
    return x if isinstance(x, int) else pl.multiple_of(x, m)


def _split_bf16(x, n):
    parts = []
    r = x
    for i in range(n):
        p = r.astype(BF16)
        parts.append(p)
        if i + 1 < n:
            r = r - p.astype(F32)
    return parts


def _dot_f32_rhs01(x, m01, n=3):
    acc = None
    for p in _split_bf16(x, n):
        t = _dot(p, m01)
        acc = t if acc is None else acc + t
    return acc


def _silu(x):
    return x / (1.0 + jnp.exp(-x))


def _log_sigmoid(x):
    return jnp.minimum(x, 0.0) - jnp.log(1.0 + jnp.exp(-jnp.abs(x)))


def _head_rms(x, gain, bd):
    ss = _dot((x * x).astype(BF16), bd)
    return x * lax.rsqrt(ss * (1.0 / HD) + EPS) * gain


def _bd_ones():
    h = np.arange(GW) // HD
    return (h[:, None] == h[None, :]).astype(np.float32)


def _hgrn_constants():
    c = CHUNK
    t = np.arange(c)
    j = np.arange(c)[None, :]
    mats = []
    masks = []
    for n in (64, 32, 16, 8, 4):
        blk, pos = t // n, t % n
        ref = blk * n + n // 2 - 1
        aq = (pos[:, None] >= n // 2) & (j > ref[:, None]) & (j <= t[:, None])
        ak = (pos[:, None] < n // 2) & (j > t[:, None]) & (j <= ref[:, None])
        mats.append((aq | ak).astype(np.float32))
        m = (blk[:, None] == blk[None, :]) & (pos[:, None] >= n // 2) & (pos[None, :] < n // 2)
        masks.append(m.astype(np.float32))
    m = ((t[:, None] // 2) == (t[None, :] // 2)) & (t[None, :] <= t[:, None])
    masks.append(m.astype(np.float32))
    mats.append((j <= t[:, None]).astype(np.float32))
    mats.append((j > t[:, None]).astype(np.float32))
    a_all = np.concatenate(mats, axis=0)
    a_all = np.concatenate([a_all, a_all], axis=1)
    masks = np.stack([np.tile(m, (1, NH)) for m in masks])
    total = masks[:, :, :c].sum(0)
    assert np.array_equal(total, np.tril(np.ones((c, c), np.float32)))
    hm = (np.arange(NH * c)[:, None] // c == np.arange(GW)[None, :] // HD).astype(np.float32)
    return a_all, masks, hm


def _regroup_kernel(w_ref, wb_ref, wf_ref):
    g = GW
    o_ff, o_sb, o_hg, o_pl, o_mm = 4 * g, 4 * g + NH, 8 * g + NH, 12 * g + NH, 14 * g + NH
    w = w_ref[0]
    bf_src = [0, g, 2 * g, 3 * g,
              o_sb, o_sb + g, o_sb + 2 * g, o_sb + 3 * g,
              o_hg, o_hg + 2 * g, o_hg + 3 * g,
              o_pl + g,
              o_mm, o_mm + g]
    for i, c0 in enumerate(bf_src):
        wb_ref[0, :, i * g:(i + 1) * g] = w[:, c0:c0 + g].astype(BF16)
    wf_ref[0, :, 0:g] = w[:, o_hg + g:o_hg + 2 * g].astype(BF16)
    wf_ref[0, :, g:2 * g] = w[:, o_pl:o_pl + g].astype(BF16)
    ff = jnp.concatenate([w[:, o_ff:o_ff + NH], jnp.zeros((w.shape[0], 128 - NH), F32)], axis=1)
    wf_ref[0, :, 2 * g:2 * g + 128] = ff.astype(BF16)


def _regroup_w_in(w_in):
    depth, d, n = w_in.shape
    tr = 128
    return pl.pallas_call(
        _regroup_kernel,
        grid=(depth, d // tr),
        in_specs=[pl.BlockSpec((1, tr, n), lambda l, i: (l, i, 0))],
        out_specs=[
            pl.BlockSpec((1, tr, NP_COLS), lambda l, i: (l, i, 0)),
            pl.BlockSpec((1, tr, NF_COLS), lambda l, i: (l, i, 0)),
        ],
        out_shape=[
            jax.ShapeDtypeStruct((depth, d, NP_COLS), BF16),
            jax.ShapeDtypeStruct((depth, d, NF_COLS), BF16),
        ],
        compiler_params=pltpu.CompilerParams(
            dimension_semantics=("arbitrary", "arbitrary"), vmem_limit_bytes=VMEM_LIMIT),
    )(w_in)


def _proj_kernel(x_ref, g_ref, wb_ref, wf_ref, pb_ref, pf_ref):
    x = x_ref[...]
    ms = jnp.mean(x * x, axis=-1, keepdims=True)
    h = (x * lax.rsqrt(ms + EPS) * g_ref[...]).astype(BF16)
    nb = 512
    for c0 in range(0, NP_COLS, nb):
        pb_ref[:, c0:c0 + nb] = _dot(h, wb_ref[:, c0:c0 + nb]).astype(BF16)
    pf_ref[...] = _dot(h, wf_ref[...])


def _proj(x2, g, wb, wf, tm):
    m = x2.shape[0]
    return pl.pallas_call(
        _proj_kernel,
        grid=(m // tm,),
        in_specs=[
            pl.BlockSpec((tm, D_MODEL), lambda i: (i, 0)),
            pl.BlockSpec((1, D_MODEL), lambda i: (0, 0)),
            pl.BlockSpec((D_MODEL, NP_COLS), lambda i: (0, 0)),
            pl.BlockSpec((D_MODEL, NF_COLS), lambda i: (0, 0)),
        ],
        out_specs=[
            pl.BlockSpec((tm, NP_COLS), lambda i: (i, 0)),
            pl.BlockSpec((tm, NF_COLS), lambda i: (i, 0)),
        ],
        out_shape=[
            jax.ShapeDtypeStruct((m, NP_COLS), BF16),
            jax.ShapeDtypeStruct((m, NF_COLS), F32),
        ],
        compiler_params=pltpu.CompilerParams(
            dimension_semantics=("arbitrary",), vmem_limit_bytes=VMEM_LIMIT),
    )(x2, g, wb, wf)


def _fox_kernel(p_ref, ff_ref, bias_ref, gq_ref, gk_ref, bd_ref, tri_ref, o_ref,
                qs, ks, vx, crow, qk_buf, worst_ref, *, seq):
    bd = bd_ref[...]
    tri = tri_ref[...]
    pb = tri.shape[0]
    hmask = [bd[h * HD:h * HD + 1, :] for h in range(NH)]
    hmask_f = [m.astype(F32) for m in hmask]
    qk_max = (HD * SCALE * NORM_SLACK) * (jnp.max(jnp.abs(gq_ref[...]), axis=-1, keepdims=True)
                                          * jnp.max(jnp.abs(gk_ref[...]), axis=-1, keepdims=True))
    carry = jnp.zeros((8, 1), F32)
    for b in range(seq // pb):
        r = slice(b * pb, (b + 1) * pb)
        lf = _log_sigmoid(ff_ref[0, r, :].T[:8, :] + bias_ref[...])
        cb = _dot_f32_rhs01(lf, tri, 3) + carry
        carry = cb[:, pb - 1:pb]
        crow[:, r] = cb
        q =_head_rms(p_ref[0, r, 0:GW].astype(F32), gq_ref[...], bd) * SCALE
        k = _head_rms(p_ref[0, r, GW:2 * GW].astype(F32), gk_ref[...], bd)
        qs[r, :] = q.astype(BF16)
        ks[r, :] = k.astype(BF16)
        v = p_ref[0, r, 2 * GW:3 * GW]
        for h in range(NH):
            vx[h, r, :] = v * hmask[h]

    rel_w = (lax.broadcasted_iota(jnp.int32, (TQ, WIDE), 1) - lax.broadcasted_iota(jnp.int32, (TQ, WIDE), 0))
    nfull = (WIDE - TQ) // TQ

    def qk_all_heads(r0, start, width):
        qb = qs[pl.ds(r0, TQ), :]
        qx = jnp.concatenate([qb * m for m in hmask], axis=0)
        return _dot_nt(qx, ks[pl.ds(start, width), :])

    def wide_start(r0):
        if isinstance(r0, int):
            return max(r0 - (WIDE - TQ), 0)
        return pl.multiple_of(jnp.maximum(r0 - (WIDE - TQ), 0), TQ)

    def wide_qk(i):
        r0 = _aligned(i * TQ, TQ)
        return qk_all_heads(r0, wide_start(r0), WIDE)

    def q_block(i, full_tile, qk_wide):
        r0 = _aligned(i * TQ, TQ)
        s0 = wide_start(r0)
        rows = pl.ds(r0, TQ)

        def mask_wide(s):
            if full_tile:
                diag = jnp.where(rel_w[:, WIDE - TQ:] <= WIDE - TQ, s[:, WIDE - TQ:], NEG_BIG)
                return jnp.concatenate([s[:, :WIDE - TQ], diag], axis=1)
            return jnp.where(rel_w <= r0 - s0, s, NEG_BIG)

        def tiles(qk, start, width):
            return [qk[h * TQ:(h + 1) * TQ] - crow[h:h + 1, pl.ds(start, width)] for h in range(NH)]

        def weighted_values(ps, start, width):
            p_all = jnp.concatenate([p.astype(BF16) for p in ps], axis=1)
            v_all = jnp.concatenate([vx[h, pl.ds(start, width), :] for h in range(NH)], axis=0)
            return _dot(p_all, v_all)

        def per_head_lanes(cols):
            out = cols[0] * hmask_f[0]
            for h in range(1, NH):
                out = out + cols[h] * hmask_f[h]
            return out

        def weight_bound(s_end, ms):
            last = pl.multiple_of(jnp.maximum(s_end - TQ, 0), TQ)
            bound = None
            for h in range(NH):
                c_last = crow[h:h + 1, pl.ds(last, TQ)][:, TQ - 1:TQ]
                b_h = (qk_max + LOGIT_SLACK) - c_last - jnp.min(ms[h], axis=0, keepdims=True)
                bound = b_h if bound is None else jnp.maximum(bound, b_h)
            return bound

        def keep_going(s_end, ms):
            return (jnp.max(weight_bound(s_end, ms)) > EXP_ZERO).astype(jnp.int32)

        ss = [mask_wide(s) for s in tiles(qk_wide, s0, WIDE)]
        ms = [jnp.max(s, axis=-1, keepdims=True) for s in ss]
        ps = [jnp.exp(s - m) for s, m in zip(ss, ms)]
        ls = [jnp.sum(p, axis=-1, keepdims=True) for p in ps]
        acc = weighted_values(ps, s0, WIDE)

        def emit(acc, ls):
            g = p_ref[0, rows, 3 * GW:4 * GW].astype(F32)
            o_ref[0, rows, :] = (acc * per_head_lanes([1.0 / l for l in ls]) * _silu(g)).astype(BF16)

        emit(acc, ls)
        if full_tile:
            worst_ref[...] = jnp.maximum(worst_ref[...], jnp.where(s0 > 0, weight_bound(s0, ms), NEG_BIG))

        def cond(st):
            return jnp.logical_and(st[0] > 0, st[1] > 0)

        def body(st):
            s_end, _, ms, ls, acc = st
            sb = pl.multiple_of(s_end - TQ, TQ)
            ss = tiles(qk_all_heads(r0, sb, TQ), sb, TQ)
            ms2 = [jnp.maximum(m, jnp.max(s, axis=-1, keepdims=True)) for s, m in zip(ss, ms)]
            ps = [jnp.exp(s - m) for s, m in zip(ss, ms2)]
            alphas = [jnp.exp(m - m2) for m, m2 in zip(ms, ms2)]
            ls2 = [a * l + jnp.sum(p, axis=-1, keepdims=True) for a, l, p in zip(alphas, ls, ps)]
            acc2 = acc * per_head_lanes(alphas) + weighted_values(ps, sb, TQ)
            return sb, keep_going(sb, ms2), tuple(ms2), tuple(ls2), acc2

        def rare_tail():
            @pl.when(jnp.logical_and(s0 > 0, keep_going(s0, ms) > 0))
            def _():
                _, _, _, ls_f, acc_f = lax.while_loop(cond, body, (s0, jnp.int32(1), tuple(ms), tuple(ls), acc))
                emit(acc_f, ls_f)

        return rare_tail

    nblk = seq // TQ
    grp = FOX_UNROLL
    lead = nfull + (nblk - nfull) % (2 * grp)
    worst_ref[...] = jnp.full(worst_ref.shape, NEG_BIG, F32)
    for i in range(lead):
        q_block(i, i >= nfull, wide_qk(i))
    for k in range(grp):
        qk_buf[k] = wide_qk(lead + k)

    def group(first, cur, nxt):
        for k in range(grp):
            qk_buf[nxt * grp + k] = wide_qk(jnp.minimum(first + grp + k, nblk - 1))
        for k in range(grp):
            q_block(first + k, True, qk_buf[cur * grp + k])

    def two_groups(j, _):
        i = lead + 2 * grp * j
        group(i, 0, 1)
        group(i + grp, 1, 0)
        return 0

    lax.fori_loop(0, (nblk - lead) // (2 * grp), two_groups, 0)

    @pl.when(jnp.max(worst_ref[...]) > EXP_ZERO)
    def _():
        def redo(i, _):
            q_block(i, True, wide_qk(i))()
            return 0

        lax.fori_loop(nfull, nblk, redo, 0)


def _fox(p3, f3, bias, gq, gk, bd, tri):
    b, s, _ = p3.shape
    assert s % tri.shape[0] == 0 and s >= WIDE + 2 * FOX_UNROLL * TQ
    kern = functools.partial(_fox_kernel, seq=s)
    c2 = lambda i: (0, 0)
    return pl.pallas_call(
        kern,
        grid=(b,),
        in_specs=[
            pl.BlockSpec((1, s, 4 * GW), lambda i: (i, 0, 0)),
            pl.BlockSpec((1, s, 128), lambda i: (i, 0, 4)),
            pl.BlockSpec((8, tri.shape[0]), c2),
            pl.BlockSpec((1, GW), c2),
            pl.BlockSpec((1, GW), c2),
            pl.BlockSpec((GW, GW), c2),
            pl.BlockSpec(tri.shape, c2),
        ],
        out_specs=pl.BlockSpec((1, s, GW), lambda i: (i, 0, 0)),
        out_shape=jax.ShapeDtypeStruct((b, s, GW), BF16),
        scratch_shapes=[
            pltpu.VMEM((s, GW), BF16),
            pltpu.VMEM((s, GW), BF16),
            pltpu.VMEM((NH, s, GW), BF16),
            pltpu.VMEM((8, s), F32),
            pltpu.VMEM((2 * FOX_UNROLL, NH * TQ, WIDE), F32),
            pltpu.VMEM((8, 128), F32),
        ],
        compiler_params=pltpu.CompilerParams(
            dimension_semantics=("arbitrary",), vmem_limit_bytes=VMEM_LIMIT),
    )(p3, f3, bias, gq, gk, bd, tri)


def _sb_kernel(p_ref, to_ref, tof_ref, bd_ref, o_ref, vx, qk_buf, worst_ref, *, seq):
    to = to_ref[...]
    tq, wide, cw = SB_TQ, SB_WIDE, SB_CW
    hmask = [bd_ref[h * HD:h * HD + 1, :] for h in range(NH)]
    qmask = [m * SCALE for m in hmask]
    pb = 256
    for b in range(seq // pb):
        r = slice(b * pb, (b + 1) * pb)
        v = p_ref[0, r, 2 * GW:3 * GW]
        for h in range(NH):
            vx[h, r, :] = v * hmask[h]

    rel_w = (lax.broadcasted_iota(jnp.int32, (tq, wide), 1) - lax.broadcasted_iota(jnp.int32, (tq, wide), 0))
    nsub = wide // cw
    nfull = -(-(wide - tq) // tq)

    def log_one_minus_sigmoid(z):
        return -jnp.maximum(z, jnp.log(1.0 + jnp.exp(jnp.minimum(z, SOFTPLUS_LINEAR))))

    def suffix_sums(lom, mat):
        w = lom.shape[1]
        cs = _dot(lom.astype(BF16), mat)
        return cs[:, :w], cs[:, w:]

    def qk_all_heads(r0, start, width):
        qb = p_ref[0, pl.ds(r0, tq), 0:GW]
        qx = jnp.concatenate([qb * m for m in qmask], axis=0)
        return _dot_nt(qx, p_ref[0, pl.ds(start, width), GW:2 * GW])

    def wide_start(r0):
        if isinstance(r0, int):
            return max(r0 - (wide - tq), 0)
        return pl.multiple_of(jnp.maximum(r0 - (wide - tq), 0), tq)

    def wide_qk(i):
        r0 = _aligned(i * tq, tq)
        return qk_all_heads(r0, wide_start(r0), wide)

    def q_block(i, full_tile, qk_wide):
        r0 = _aligned(i * tq, tq)
        s0 = wide_start(r0)
        rows = pl.ds(r0, tq)

        def mask_wide(x):
            if full_tile:
                diag = jnp.where(rel_w[:, wide - cw:] < wide - tq, x[:, wide - cw:], 0.0)
                return jnp.concatenate([x[:, :wide - cw], diag], axis=1)
            return jnp.where(rel_w < r0 - s0, x, 0.0)

        def per_head(qk):
            return [qk[h * tq:(h + 1) * tq] for h in range(NH)]

        def weighted_values(ws, start, width):
            w_all = jnp.concatenate([w.astype(BF16) for w in ws], axis=1)
            v_all = jnp.concatenate([vx[h, pl.ds(start, width), :] for h in range(NH)], axis=0)
            return _dot(w_all, v_all)

        def emit(acc):
            g = p_ref[0, rows, 3 * GW:4 * GW].astype(F32)
            o_ref[0, rows, :] = (acc * _silu(g)).astype(BF16)

        zs = per_head(qk_wide)
        loms = [log_one_minus_sigmoid(z) for z in zs]
        log_betas = [z + lom for z, lom in zip(zs, loms)]
        sums = [[suffix_sums(mask_wide(lom)[:, c * cw:(c + 1) * cw], to) for c in range(nsub)] for lom in loms]
        carries, ws = [], []
        for h in range(NH):
            between = [None] * nsub
            carry = jnp.zeros((tq, cw), F32)
            for c in reversed(range(nsub)):
                rc, tot = sums[h][c]
                between[c] = rc + carry
                carry = carry + tot
            ws.append(mask_wide(jnp.exp(log_betas[h] + jnp.concatenate(between, axis=1))))
            carries.append(carry)
        acc = weighted_values(ws, s0, wide)
        emit(acc)

        def worst_carry(carries):
            cm = jnp.maximum(jnp.maximum(carries[0], carries[1]), jnp.maximum(carries[2], carries[3]))
            return jnp.max(cm.reshape(tq // 8, 8, cw), axis=0)

        def keep_going(carries):
            return (jnp.max(worst_carry(carries)) > EXP_ZERO).astype(jnp.int32)

        if full_tile:
            worst_ref[0] = jnp.maximum(worst_ref[0], jnp.where(s0 > 0, worst_carry(carries), NEG_BIG))

        def cond(st):
            return jnp.logical_and(st[0] > 0, st[1] > 0)

        def body(st):
            s_end, _, carries, acc = st
            sb = pl.multiple_of(s_end - tq, tq)
            zs = per_head(qk_all_heads(r0, sb, tq))
            loms = [log_one_minus_sigmoid(z) for z in zs]
            sums = [suffix_sums(lom, tof_ref[...]) for lom in loms]
            ws = [jnp.exp(z + lom + rc + cr[:, :tq]) for z, lom, (rc, _), cr in zip(zs, loms, sums, carries)]
            c2 = [cr + tot for cr, (_, tot) in zip(carries, sums)]
            return sb, keep_going(c2), tuple(c2), acc + weighted_values(ws, sb, tq)

        def rare_tail():
            @pl.when(jnp.logical_and(s0 > 0, keep_going(carries) > 0))
            def _():
                st = lax.while_loop(cond, body, (s0, jnp.int32(1), tuple(carries), acc))
                emit(st[3])

        return rare_tail

    nblk = seq // tq
    grp = SB_UNROLL
    lead = nfull + (nblk - nfull) % (2 * grp)
    worst_ref[0] = jnp.full((8, cw), NEG_BIG, F32)
    for i in range(lead):
        q_block(i, i >= nfull, wide_qk(i))
    for k in range(grp):
        qk_buf[k] = wide_qk(lead + k)

    def group(first, cur, nxt):
        for k in range(grp):
            qk_buf[nxt * grp + k] = wide_qk(jnp.minimum(first + grp + k, nblk - 1))
        for k in range(grp):
            q_block(first + k, True, qk_buf[cur * grp + k])

    def two_groups(j, _):
        i = lead + 2 * grp * j
        group(i, 0, 1)
        group(i + grp, 1, 0)
        return 0

    lax.fori_loop(0, (nblk - lead) // (2 * grp), two_groups, 0)

    @pl.when(jnp.max(worst_ref[0]) > EXP_ZERO)
    def _():
        def redo(i, _):
            q_block(i, True, wide_qk(i))()
            return 0

        lax.fori_loop(nfull, nblk, redo, 0)


def _sb(p3, to, tof, bd):
    b, s, _ = p3.shape
    assert s % 256 == 0 and s >= SB_WIDE + 2 * SB_UNROLL * SB_TQ
    kern = functools.partial(_sb_kernel, seq=s)
    return pl.pallas_call(
        kern,
        grid=(b,),
        in_specs=[
            pl.BlockSpec((1, s, 4 * GW), lambda i: (i, 0, 1)),
            pl.BlockSpec(to.shape, lambda i: (0, 0)),
            pl.BlockSpec(tof.shape, lambda i: (0, 0)),
            pl.BlockSpec((GW, GW), lambda i: (0, 0)),
        ],
        out_specs=pl.BlockSpec((1, s, GW), lambda i: (i, 0, 0)),
        out_shape=jax.ShapeDtypeStruct((b, s, GW), BF16),
        scratch_shapes=[
            pltpu.VMEM((NH, s, GW), BF16),
            pltpu.VMEM((2 * SB_UNROLL, NH * SB_TQ, SB_WIDE), F32),
            pltpu.VMEM((1, 8, SB_CW), F32),
        ],
        compiler_params=pltpu.CompilerParams(
            dimension_semantics=("arbitrary",), vmem_limit_bytes=VMEM_LIMIT),
    )(p3, to, tof, bd)


def _hgrn_kernel(hq_ref, hi_ref, hg_ref, hf_ref, lbm_ref, oml_ref, gout_ref, bd_ref, a_ref,
                 mask_ref, hm_ref, o_ref, st_ref, *, seq):
    bd = bd_ref[...]
    a_all = a_ref[...]
    hm = hm_ref[...]
    lbm = lbm_ref[...]
    oml = oml_ref[...]
    c = CHUNK
    st_ref[...] = jnp.zeros((GW, GW), F32)

    n = HSB * c
    nlev = mask_ref.shape[0] - 1
    odd =(lax.broadcasted_iota(jnp.int32, (n, GW), 0) & 1) == 1

    def superblock(bi, _):
        r0 = pl.multiple_of(bi * n, n)
        rows = pl.ds(r0, n)
        hf = hf_ref[0, rows, :]
        sg = 1.0 / (1.0 + jnp.exp(-hf))
        f = lbm + oml * sg
        g = jnp.log(f)
        kk = oml * (1.0 - sg)
        q = _silu(hq_ref[0, rows, :].astype(F32))
        v = hi_ref[0, rows, :]
        gh = g.astype(BF16)
        gl = (g - gh.astype(F32)).astype(BF16)
        q2 = (q * jnp.where(odd, f, 1.0)).astype(BF16)
        k2 = (kk * jnp.where(odd, 1.0 / f, 1.0)).astype(BF16)

        def scores(qf, kf, l):
            kx = jnp.concatenate([kf] * NH, axis=0) * hm
            return _dot_nt(qf, kx) * mask_ref[l]

        sls = [slice(ci * c, (ci + 1) * c) for ci in range(HSB)]
        exs = [_dot(a_all, jnp.concatenate([gh[sl], gl[sl]], axis=0)) for sl in sls]
        ps = [scores(q2[sl], k2[sl], nlev) for sl in sls]
        qb = q.astype(BF16)
        kb = kk.astype(BF16)
        for l in range(nlev):
            for ci, sl in enumerate(sls):
                x = jnp.exp(exs[ci][l * c:(l + 1) * c]).astype(BF16)
                ps[ci] = ps[ci] + scores(qb[sl] * x, kb[sl] * x, l)
        o_intra, qd, upd, dl = [], [], [], []
        for ci, sl in enumerate(sls):
            vx = jnp.concatenate([v[sl]] * NH, axis=0) * hm
            o_intra.append(_dot(ps[ci].astype(BF16), vx))
            eb = exs[ci][nlev * c:(nlev + 1) * c]
            er = exs[ci][(nlev + 1) * c:(nlev + 2) * c]
            qd.append((q[sl] * jnp.exp(eb)).astype(BF16))
            upd.append(_dot_tn(v[sl], (kk[sl] * jnp.exp(er)).astype(BF16)))
            dl.append(jnp.exp(eb[c - 1:c, :]))

        st = st_ref[...]
        outs = []
        for ci in range(HSB):
            outs.append(o_intra[ci] + _dot_nt(qd[ci], st.astype(BF16) * bd))
            st = st * dl[ci] + upd[ci]
        st_ref[...] = st

        o = _head_rms(jnp.concatenate(outs, axis=0), gout_ref[...], bd)
        o_ref[0, rows, :] = (o * _silu(hg_ref[0, rows, :].astype(F32))).astype(BF16)
        return 0

    lax.fori_loop(0, seq // n, superblock, 0)


def _hgrn(p3, f3, lbm, oml, gout, bd, a_all, masks, hm):
    b, s, _ = p3.shape
    assert s % (HSB * CHUNK) == 0
    kern = functools.partial(_hgrn_kernel, seq=s)
    c2 = lambda i: (0, 0)
    return pl.pallas_call(
        kern,
        grid=(b,),
        in_specs=[
            pl.BlockSpec((1, s, GW), lambda i: (i, 0, 8)),
            pl.BlockSpec((1, s, GW), lambda i: (i, 0, 9)),
            pl.BlockSpec((1, s, GW), lambda i: (i, 0, 10)),
            pl.BlockSpec((1, s, GW), lambda i: (i, 0, 0)),
            pl.BlockSpec((1, GW), c2),
            pl.BlockSpec((1, GW), c2),
            pl.BlockSpec((1, GW), c2),
            pl.BlockSpec((GW, GW), c2),
            pl.BlockSpec(a_all.shape, c2),
            pl.BlockSpec(masks.shape, lambda i: (0, 0, 0)),
            pl.BlockSpec(hm.shape, c2),
        ],
        out_specs=pl.BlockSpec((1, s, GW), lambda i: (i, 0, 0)),
        out_shape=jax.ShapeDtypeStruct((b, s, GW), BF16),
        scratch_shapes=[pltpu.VMEM((GW, GW), F32)],
        compiler_params=pltpu.CompilerParams(
            dimension_semantics=("arbitrary",), vmem_limit_bytes=VMEM_LIMIT),
    )(p3, p3, p3, f3, lbm, oml, gout, bd, a_all, masks, hm)


def _pm_kernel(pv_ref, pg_ref, mq_ref, mg_ref, mem_ref, mng_ref, wkv_ref, gmq_ref, gmk_ref, bd_ref,
               wp_ref, ps_ref, win_ref, d_ref, e_ref, ubuf, wbuf, kmem, vxm, *, seq, tq):
    bd = bd_ref[...]
    hmask = [bd[h * HD:h * HD + 1, :] for h in range(NH)]
    hmask_f = [m.astype(F32) for m in hmask]
    halo = 16
    mem = mem_ref[0]
    ms = jnp.mean(mem * mem, axis=-1, keepdims=True)
    mn = (mem * lax.rsqrt(ms + EPS) * mng_ref[...]).astype(BF16)
    kv = _dot(mn, wkv_ref[...])
    kn = _head_rms(kv[:, :GW], gmk_ref[...], bd).astype(BF16)
    vv = kv[:, GW:].astype(BF16)
    nm = kn.shape[0]
    kmem[...] = kn
    for h in range(NH):
        vxm[h * nm:(h + 1) * nm, :] = vv * hmask[h]

    ubuf[0:halo, :] = jnp.zeros((halo, GW), F32)
    ubuf[halo:halo + seq, :] = pv_ref[0]
    win = win_ref[...]
    inv_win = 1.0 / win
    for k in range(3):
        sh = 1 << k
        wbuf[k, 0:halo, :] = jnp.zeros((halo, GW), F32)
        for b in range(seq // tq):
            base = halo + b * tq
            if k == 0:
                wbuf[k, base:base + tq, :] = ubuf[base:base + tq, :] + ubuf[base - sh:base - sh + tq, :]
            else:
                wbuf[k, base:base + tq, :] = (wbuf[k - 1, base:base + tq, :]
                                              + wbuf[k - 1, base - sh:base - sh + tq, :])

    for b in range(seq // tq):
        r = slice(b * tq, (b + 1) * tq)
        base = halo + b * tq
        u = ubuf[base:base + tq, :]
        s2 = wbuf[0, base:base + tq, :]
        s4 = wbuf[1, base:base + tq, :]
        s8 = wbuf[2, base:base + tq, :]
        s16 = s8 + wbuf[2, base - 8:base - 8 + tq, :]
        sw = jnp.where(win == 2.0, s2, jnp.where(win == 4.0, s4, jnp.where(win == 8.0, s8, s16)))
        if b == 0:
            pos = (lax.broadcasted_iota(jnp.int32, (tq, GW), 0) + 1).astype(F32)
            pooled = sw / jnp.minimum(pos, win)
        else:
            pooled = sw * inv_win
        y = _dot((pooled - u).astype(BF16), wp_ref[...]) * ps_ref[...]
        d_ref[0, r, :] = (y * _silu(pg_ref[0, r, :].astype(F32))).astype(BF16)

        qn = (_head_rms(mq_ref[0, r, :].astype(F32), gmq_ref[...], bd) * SCALE).astype(BF16)
        s_all = _dot_nt(jnp.concatenate([qn * m for m in hmask], axis=0), kmem[...])
        ss = [s_all[h * tq:(h + 1) * tq] for h in range(NH)]
        ps = [jnp.exp(s - jnp.max(s, axis=-1, keepdims=True)) for s in ss]
        ls = [jnp.sum(p, axis=-1, keepdims=True) for p in ps]
        inv_l = (1.0 / ls[0]) * hmask_f[0]
        for h in range(1, NH):
            inv_l = inv_l + (1.0 / ls[h]) * hmask_f[h]
        oe = _dot(jnp.concatenate([p.astype(BF16) for p in ps], axis=1), vxm[...]) * inv_l
        e_ref[0, r, :] = (oe * _silu(mg_ref[0, r, :].astype(F32))).astype(BF16)


def _pm(p3, f3, mem, mng, wkv, gmq, gmk, bd, wp, ps, win, tq):
    b, s, _ = p3.shape
    nm = mem.shape[1]
    kern = functools.partial(_pm_kernel, seq=s, tq=tq)
    c2 = lambda i: (0, 0)
    return pl.pallas_call(
        kern,
        grid=(b,),
        in_specs=[
            pl.BlockSpec((1, s, GW), lambda i: (i, 0, 1)),
            pl.BlockSpec((1, s, GW), lambda i: (i, 0, 11)),
            pl.BlockSpec((1, s, GW), lambda i: (i, 0, 12)),
            pl.BlockSpec((1, s, GW), lambda i: (i, 0, 13)),
            pl.BlockSpec((1, nm, D_MODEL), lambda i: (i, 0, 0)),
            pl.BlockSpec((1, D_MODEL), c2),
            pl.BlockSpec((D_MODEL, 2 * GW), c2),
            pl.BlockSpec((1, GW), c2),
            pl.BlockSpec((1, GW), c2),
            pl.BlockSpec((GW, GW), c2),
            pl.BlockSpec((GW, GW), c2),
            pl.BlockSpec((1, GW), c2),
            pl.BlockSpec((1, GW), c2),
        ],
        out_specs=[
            pl.BlockSpec((1, s, GW), lambda i: (i, 0, 0)),
            pl.BlockSpec((1, s, GW), lambda i: (i, 0, 0)),
        ],
        out_shape=[
            jax.ShapeDtypeStruct((b, s, GW), BF16),
            jax.ShapeDtypeStruct((b, s, GW), BF16),
        ],
        scratch_shapes=[
            pltpu.VMEM((s + 16, GW), F32),
            pltpu.VMEM((3, s + 16, GW), F32),
            pltpu.VMEM((nm, GW), BF16),
            pltpu.VMEM((NH * nm, GW), BF16),
        ],
        compiler_params=pltpu.CompilerParams(
            dimension_semantics=("arbitrary",), vmem_limit_bytes=VMEM_LIMIT),
    )(f3, p3, p3, p3, mem, mng, wkv, gmq, gmk, bd, wp, ps, win)


def _out_kernel(a_ref, b_ref, c_ref, d_ref, e_ref, w_ref, x_ref, o_ref):
    mixed = jnp.concatenate([a_ref[...], b_ref[...], c_ref[...], d_ref[...], e_ref[...]], axis=1)
    o_ref[...] = x_ref[...] + _dot(mixed, w_ref[...])


def _out(parts, w, x2, tm):
    m = x2.shape[0]
    gspec = pl.BlockSpec((tm, GW), lambda i: (i, 0))
    return pl.pallas_call(
        _out_kernel,
        grid=(m // tm,),
        in_specs=[gspec] * 5 + [
            pl.BlockSpec((5 * GW, D_MODEL), lambda i: (0, 0)),
            pl.BlockSpec((tm, D_MODEL), lambda i: (i, 0)),
        ],
        out_specs=pl.BlockSpec((tm, D_MODEL), lambda i: (i, 0)),
        out_shape=jax.ShapeDtypeStruct((m, D_MODEL), F32),
        compiler_params=pltpu.CompilerParams(
            dimension_semantics=("arbitrary",), vmem_limit_bytes=VMEM_LIMIT),
    )(*parts, w, x2)


def _out_proj_kernel(a_ref, b_ref, c_ref, d_ref, e_ref, w_ref, x_ref, g_ref, wb_ref, wf_ref,
                     o_ref, pb_ref, pf_ref):
    _out_kernel(a_ref, b_ref, c_ref, d_ref, e_ref, w_ref, x_ref, o_ref)
    _proj_kernel(o_ref, g_ref, wb_ref, wf_ref, pb_ref, pf_ref)


def _out_proj(parts, w, x2, g, wb, wf, tm):
    m = x2.shape[0]
    gspec = pl.BlockSpec((tm, GW), lambda i: (i, 0))
    c2 = lambda i: (0, 0)
    return pl.pallas_call(
        _out_proj_kernel,
        grid=(m // tm,),
        in_specs=[gspec] * 5 + [
            pl.BlockSpec((5 * GW, D_MODEL), c2),
            pl.BlockSpec((tm, D_MODEL), lambda i: (i, 0)),
            pl.BlockSpec((1, D_MODEL), c2),
            pl.BlockSpec((D_MODEL, NP_COLS), c2),
            pl.BlockSpec((D_MODEL, NF_COLS), c2),
        ],
        out_specs=[
            pl.BlockSpec((tm, D_MODEL), lambda i: (i, 0)),
            pl.BlockSpec((tm, NP_COLS), lambda i: (i, 0)),
            pl.BlockSpec((tm, NF_COLS), lambda i: (i, 0)),
        ],
        out_shape=[
            jax.ShapeDtypeStruct((m, D_MODEL), F32),
            jax.ShapeDtypeStruct((m, NP_COLS), BF16),
            jax.ShapeDtypeStruct((m, NF_COLS), F32),
        ],
        compiler_params=pltpu.CompilerParams(
            dimension_semantics=("arbitrary",), vmem_limit_bytes=VMEM_LIMIT),
    )(*parts, w, x2, g, wb, wf)


def _tile_heads(g):
    return jnp.tile(g.astype(F32), NH).reshape(1, GW)


def kernel(x, mem, norm_g, w_in, fox_f_bias, fox_q_norm, fox_k_norm, hgrn_lb_logits, hgrn_out_norm,
           pool_w, pool_scale, mem_norm_g, mem_w_kv, mem_q_norm, mem_k_norm, w_out):
    bsz, seq, _ = x.shape
    depth = w_in.shape[0]
    m = bsz * seq
    tq = 256
    tm = 512

    pr = jax.nn.softmax(hgrn_lb_logits.astype(F32), axis=0)
    lower_bounds = jnp.clip(jnp.cumsum(pr, axis=0) - pr[0:1], 0.0, 1.0 - 1e-6)

    bd_np = _bd_ones()
    bd = jnp.asarray(bd_np, BF16)
    tri = jnp.asarray(np.triu(np.ones((tq, tq), np.float32)), BF16)

    def suffix_and_ones(w):
        jj = np.arange(w)
        suffix = (jj[:, None] > jj[None, :]).astype(np.float32)
        return jnp.asarray(np.concatenate([suffix, np.ones((w, SB_CW), np.float32)], axis=1), BF16)

    to = suffix_and_ones(SB_CW)
    tof = suffix_and_ones(SB_TQ)
    a_np, masks_np, hm_np = _hgrn_constants()
    a_all = jnp.asarray(a_np, BF16)
    masks = jnp.asarray(masks_np, F32)
    hm = jnp.asarray(hm_np, BF16)
    win = jnp.asarray(np.repeat(np.array(POOL_WINDOWS, np.float32), HD).reshape(1, GW))

    g = GW
    wb_all, wf_all = _regroup_w_in(w_in)
    x2 = x.reshape(m, D_MODEL)
    norm_gs = [norm_g[l].reshape(1, D_MODEL).astype(F32) for l in range(depth)]
    pb, pf = _proj(x2, norm_gs[0], wb_all[0], wf_all[0], tm)
    for l in range(depth):
        p3 = pb.reshape(bsz, seq, NP_COLS)
        f3 = pf.reshape(bsz, seq, NF_COLS)

        bias = jnp.broadcast_to(jnp.pad(fox_f_bias[l].astype(F32), (0, 8 - NH))[:, None], (8, tq))
        out_a = _fox(p3, f3, bias, _tile_heads(fox_q_norm[l]), _tile_heads(fox_k_norm[l]), bd, tri)
        out_b = _sb(p3, to, tof, bd)

        lb = lower_bounds[l].reshape(1, g)
        out_c = _hgrn(p3, f3, jnp.maximum(lb, LB_FLOOR), 1.0 - lb,
                      hgrn_out_norm[l].reshape(1, g).astype(F32), bd, a_all, masks, hm)

        wp = jax.scipy.linalg.block_diag(*[pool_w[l, i] for i in range(len(POOL_WINDOWS))]).astype(BF16)
        out_d, out_e = _pm(p3, f3, mem, mem_norm_g[l].reshape(1, D_MODEL).astype(F32),
                           mem_w_kv[l].astype(BF16), _tile_heads(mem_q_norm[l]), _tile_heads(mem_k_norm[l]),
                           bd, wp, pool_scale[l].reshape(1, g).astype(F32), win, tq)

        parts = [o.reshape(m, g) for o in (out_a, out_b, out_c, out_d, out_e)]
        if l + 1 < depth:
            x2, pb, pf = _out_proj(parts, w_out[l].astype(BF16), x2, norm_gs[l + 1],
                                   wb_all[l + 1], wf_all[l + 1], tm)
        else:
            x2 = _out(parts, w_out[l].astype(BF16), x2, tm)
    return x2.reshape(bsz, seq, D_MODEL)
```

```python
import functools

import numpy as np
import jax
import jax.numpy as jnp
from jax import lax
from jax.experimental import pallas as pl
from jax.experimental.pallas import tpu as pltpu

F32 = jnp.float32
BF16 = jnp.bfloat16

D_MODEL = 1024
GW = 256
NH = 4
HD = 64
CHUNK = 64
POOL_WINDOWS = (2, 4, 8, 16)
EPS = 1e-6
NEG_BIG = -1e30
LB_FLOOR = 1e-30
SCALE = HD ** -0.5

NP_COLS = 14 * GW
P_FQ, P_FK, P_FV, P_FG, P_SQ, P_SK, P_SV, P_SG, P_HQ, P_HI, P_HG, P_PG, P_MQ, P_MG = range(14)
NF_COLS = 2 * GW + 128
VMEM_LIMIT = 56 * 1024 * 1024

TQ = 128
WIDE = 3 * TQ
EXP_ZERO = -104.0
NORM_SLACK = 1.01
LOGIT_SLACK = 0.05
FOX_UNROLL = 2
HSB = 8
SB_TQ = 128
SB_WIDE = 384
SB_CW = 128
SB_UNROLL = 2
SOFTPLUS_LINEAR = 80.0


def _dot(a, b):
    return jnp.dot(a, b, preferred_element_type=F32)


def _dot_nt(a, b):
    return lax.dot_general(a, b, (((1,), (1,)), ((), ())), preferred_element_type=F32)


def _dot_tn(a, b):
    return lax.dot_general(a, b, (((0,), (0,)), ((), ())), preferred_element_type=F32)


def _aligned(x, m):
    return x if isinstance(x, int) else pl.multiple_of(x, m)


def _split_bf16(x, n):
    parts = []
    r = x
    for i in range(n):
        p = r.astype(BF16)
        parts.append(p)
        if i + 1 < n:
            r = r - p.astype(F32)
    return parts


def _dot_f32_rhs01(x, m01, n=3):
    acc = None
    for p in _split_bf16(x, n):
        t = _dot(p, m01)
        acc = t if acc is None else acc + t
    return acc


def _silu(x):
    return x / (1.0 + jnp.exp(-x))


def _log_sigmoid(x):
    return jnp.minimum(x, 0.0) - jnp.log(1.0 + jnp.exp(-jnp.abs(x)))


def _head_rms(x, gain, bd):
    ss = _dot((x * x).astype(BF16), bd)
    return x * lax.rsqrt(ss * (1.0 / HD) + EPS) * gain


def _bd_ones():
    h = np.arange(GW) // HD
    return (h[:, None] == h[None, :]).astype(np.float32)


def _hgrn_constants():
    c = CHUNK
    t = np.arange(c)
    j = np.arange(c)[None, :]
    mats = []
    masks = []
    for n in (64, 32, 16, 8, 4):
        blk, pos = t // n, t % n
        ref = blk * n + n // 2 - 1
        aq = (pos[:, None] >= n // 2) & (j > ref[:, None]) & (j <= t[:, None])
        ak = (pos[:, None] < n // 2) & (j > t[:, None]) & (j <= ref[:, None])
        mats.append((aq | ak).astype(np.float32))
        m = (blk[:, None] == blk[None, :]) & (pos[:, None] >= n // 2) & (pos[None, :] < n // 2)
        masks.append(m.astype(np.float32))
    m = ((t[:, None] // 2) == (t[None, :] // 2)) & (t[None, :] <= t[:, None])
    masks.append(m.astype(np.float32))
    mats.append((j <= t[:, None]).astype(np.float32))
    mats.append((j > t[:, None]).astype(np.float32))
    a_all = np.concatenate(mats, axis=0)
    a_all = np.concatenate([a_all, a_all], axis=1)
    masks = np.stack([np.tile(m, (1, NH)) for m in masks])
    total = masks[:, :, :c].sum(0)
    assert np.array_equal(total, np.tril(np.ones((c, c), np.float32)))
    hm = (np.arange(NH * c)[:, None] // c == np.arange(GW)[None, :] // HD).astype(np.float32)
    return a_all, masks, hm


def _regroup_kernel(w_ref, wb_ref, wf_ref):
    g = GW
    o_ff, o_sb, o_hg, o_pl, o_mm = 4 * g, 4 * g + NH, 8 * g + NH, 12 * g + NH, 14 * g + NH
    w = w_ref[0]
    bf_src = [0, g, 2 * g, 3 * g,
              o_sb, o_sb + g, o_sb + 2 * g, o_sb + 3 * g,
              o_hg, o_hg + 2 * g, o_hg + 3 * g,
              o_pl + g,
              o_mm, o_mm + g]
    for i, c0 in enumerate(bf_src):
        wb_ref[0, :, i * g:(i + 1) * g] = w[:, c0:c0 + g].astype(BF16)
    wf_ref[0, :, 0:g] = w[:, o_hg + g:o_hg + 2 * g].astype(BF16)
    wf_ref[0, :, g:2 * g] = w[:, o_pl:o_pl + g].astype(BF16)
    ff = jnp.concatenate([w[:, o_ff:o_ff + NH], jnp.zeros((w.shape[0], 128 - NH), F32)], axis=1)
    wf_ref[0, :, 2 * g:2 * g + 128] = ff.astype(BF16)


def _regroup_w_in(w_in):
    depth, d, n = w_in.shape
    tr = 128
    return pl.pallas_call(
        _regroup_kernel,
        grid=(depth, d // tr),
        in_specs=[pl.BlockSpec((1, tr, n), lambda l, i: (l, i, 0))],
        out_specs=[
            pl.BlockSpec((1, tr, NP_COLS), lambda l, i: (l, i, 0)),
            pl.BlockSpec((1, tr, NF_COLS), lambda l, i: (l, i, 0)),
        ],
        out_shape=[
            jax.ShapeDtypeStruct((depth, d, NP_COLS), BF16),
            jax.ShapeDtypeStruct((depth, d, NF_COLS), BF16),
        ],
        compiler_params=pltpu.CompilerParams(
            dimension_semantics=("arbitrary", "arbitrary"), vmem_limit_bytes=VMEM_LIMIT),
    )(w_in)


def _proj_kernel(x_ref, g_ref, wb_ref, wf_ref, hgain_ref, bd_ref, pb_ref, pf_ref):
    x = x_ref[...]
    ms = jnp.mean(x * x, axis=-1, keepdims=True)
    h = (x * lax.rsqrt(ms + EPS) * g_ref[...]).astype(BF16)
    gains = hgain_ref[...]
    bd = bd_ref[...]
    normed = {P_FQ: (0, SCALE), P_FK: (1, 1.0), P_MQ: (2, SCALE)}
    post = {P_SQ: lambda r: r * SCALE}
    post.update({gi: _silu for gi in (P_FG, P_SG, P_HQ, P_HG, P_PG, P_MG)})

    def cols(gi):
        return slice(gi * GW, (gi + 1) * GW)

    raw = {gi: _dot(h, wb_ref[:, cols(gi)]) for gi in normed}
    for gi in range(NP_COLS // GW):
        if gi not in normed:
            r = _dot(h, wb_ref[:, cols(gi)])
            pb_ref[:, cols(gi)] = post.get(gi, lambda r: r)(r).astype(BF16)
    pf_ref[...] = _dot(h, wf_ref[...])
    for gi, (row, scale) in normed.items():
        pb_ref[:, cols(gi)] = (_head_rms(raw[gi], gains[row:row + 1], bd) * scale).astype(BF16)


def _proj(x2, g, wb, wf, hgain, bd, tm):
    m = x2.shape[0]
    return pl.pallas_call(
        _proj_kernel,
        grid=(m // tm,),
        in_specs=[
            pl.BlockSpec((tm, D_MODEL), lambda i: (i, 0)),
            pl.BlockSpec((1, D_MODEL), lambda i: (0, 0)),
            pl.BlockSpec((D_MODEL, NP_COLS), lambda i: (0, 0)),
            pl.BlockSpec((D_MODEL, NF_COLS), lambda i: (0, 0)),
            pl.BlockSpec((8, GW), lambda i: (0, 0)),
            pl.BlockSpec((GW, GW), lambda i: (0, 0)),
        ],
        out_specs=[
            pl.BlockSpec((tm, NP_COLS), lambda i: (i, 0)),
            pl.BlockSpec((tm, NF_COLS), lambda i: (i, 0)),
        ],
        out_shape=[
            jax.ShapeDtypeStruct((m, NP_COLS), BF16),
            jax.ShapeDtypeStruct((m, NF_COLS), F32),
        ],
        compiler_params=pltpu.CompilerParams(
            dimension_semantics=("arbitrary",), vmem_limit_bytes=VMEM_LIMIT),
    )(x2, g, wb, wf, hgain, bd)


def _fox_kernel(p_ref, ff_ref, bias_ref, gq_ref, gk_ref, bd_ref, tri_ref, o_ref,
                vx, crow, qk_buf, worst_ref, *, seq):
    bd = bd_ref[...]
    tri = tri_ref[...]
    pb = tri.shape[0]
    hmask = [bd[h * HD:h * HD + 1, :] for h in range(NH)]
    hmask_f = [m.astype(F32) for m in hmask]
    qk_max = (HD * SCALE * NORM_SLACK) * (jnp.max(jnp.abs(gq_ref[...]), axis=-1, keepdims=True)
                                          * jnp.max(jnp.abs(gk_ref[...]), axis=-1, keepdims=True))
    carry = jnp.zeros((8, 1), F32)
    for b in range(seq // pb):
        r = slice(b * pb, (b + 1) * pb)
        lf = _log_sigmoid(ff_ref[0, r, :].T[:8, :] + bias_ref[...])
        cb = _dot_f32_rhs01(lf, tri, 3) + carry
        carry = cb[:, pb - 1:pb]
        crow[:, r] = cb
        v = p_ref[0, r, 2 * GW:3 * GW]
        for h in range(NH):
            vx[h, r, :] = v * hmask[h]

    rel_w = (lax.broadcasted_iota(jnp.int32, (TQ, WIDE), 1) - lax.broadcasted_iota(jnp.int32, (TQ, WIDE), 0))
    nfull = (WIDE - TQ) // TQ

    def qk_all_heads(r0, start, width):
        qb = p_ref[0, pl.ds(r0, TQ), 0:GW]
        qx = jnp.concatenate([qb * m for m in hmask], axis=0)
        return _dot_nt(qx, p_ref[0, pl.ds(start, width), GW:2 * GW])

    def wide_start(r0):
        if isinstance(r0, int):
            return max(r0 - (WIDE - TQ), 0)
        return pl.multiple_of(jnp.maximum(r0 - (WIDE - TQ), 0), TQ)

    def wide_qk(i):
        r0 = _aligned(i * TQ, TQ)
        return qk_all_heads(r0, wide_start(r0), WIDE)

    def q_block(i, full_tile, qk_wide):
        r0 = _aligned(i * TQ, TQ)
        s0 = wide_start(r0)
        rows = pl.ds(r0, TQ)

        def mask_wide(s):
            if full_tile:
                diag = jnp.where(rel_w[:, WIDE - TQ:] <= WIDE - TQ, s[:, WIDE - TQ:], NEG_BIG)
                return jnp.concatenate([s[:, :WIDE - TQ], diag], axis=1)
            return jnp.where(rel_w <= r0 - s0, s, NEG_BIG)

        def tiles(qk, start, width):
            return [qk[h * TQ:(h + 1) * TQ] - crow[h:h + 1, pl.ds(start, width)] for h in range(NH)]

        def weighted_values(ps, start, width):
            p_all = jnp.concatenate([p.astype(BF16) for p in ps], axis=1)
            v_all = jnp.concatenate([vx[h, pl.ds(start, width), :] for h in range(NH)], axis=0)
            return _dot(p_all, v_all)

        def per_head_lanes(cols):
            out = cols[0] * hmask_f[0]
            for h in range(1, NH):
                out = out + cols[h] * hmask_f[h]
            return out

        def weight_bound(s_end, ms):
            last = pl.multiple_of(jnp.maximum(s_end - TQ, 0), TQ)
            bound = None
            for h in range(NH):
                c_last = crow[h:h + 1, pl.ds(last, TQ)][:, TQ - 1:TQ]
                b_h = (qk_max + LOGIT_SLACK) - c_last - jnp.min(ms[h], axis=0, keepdims=True)
                bound = b_h if bound is None else jnp.maximum(bound, b_h)
            return bound

        def keep_going(s_end, ms):
            return (jnp.max(weight_bound(s_end, ms)) > EXP_ZERO).astype(jnp.int32)

        ss = [mask_wide(s) for s in tiles(qk_wide, s0, WIDE)]
        ms = [jnp.max(s, axis=-1, keepdims=True) for s in ss]
        ps = [jnp.exp(s - m) for s, m in zip(ss, ms)]
        ls = [jnp.sum(p, axis=-1, keepdims=True) for p in ps]
        acc = weighted_values(ps, s0, WIDE)

        def emit(acc, ls):
            g = p_ref[0, rows, 3 * GW:4 * GW].astype(F32)
            o_ref[0, rows, :] = (acc * per_head_lanes([1.0 / l for l in ls]) * g).astype(BF16)

        emit(acc, ls)
        if full_tile:
            worst_ref[...] = jnp.maximum(worst_ref[...], jnp.where(s0 > 0, weight_bound(s0, ms), NEG_BIG))

        def cond(st):
            return jnp.logical_and(st[0] > 0, st[1] > 0)

        def body(st):
            s_end, _, ms, ls, acc = st
            sb = pl.multiple_of(s_end - TQ, TQ)
            ss = tiles(qk_all_heads(r0, sb, TQ), sb, TQ)
            ms2 = [jnp.maximum(m, jnp.max(s, axis=-1, keepdims=True)) for s, m in zip(ss, ms)]
            ps = [jnp.exp(s - m) for s, m in zip(ss, ms2)]
            alphas = [jnp.exp(m - m2) for m, m2 in zip(ms, ms2)]
            ls2 = [a * l + jnp.sum(p, axis=-1, keepdims=True) for a, l, p in zip(alphas, ls, ps)]
            acc2 = acc * per_head_lanes(alphas) + weighted_values(ps, sb, TQ)
            return sb, keep_going(sb, ms2), tuple(ms2), tuple(ls2), acc2

        def rare_tail():
            @pl.when(jnp.logical_and(s0 > 0, keep_going(s0, ms) > 0))
            def _():
                _, _, _, ls_f, acc_f = lax.while_loop(cond, body, (s0, jnp.int32(1), tuple(ms), tuple(ls), acc))
                emit(acc_f, ls_f)

        return rare_tail

    nblk = seq // TQ
    grp = FOX_UNROLL
    lead = nfull + (nblk - nfull) % (2 * grp)
    worst_ref[...] = jnp.full(worst_ref.shape, NEG_BIG, F32)
    for i in range(lead):
        q_block(i, i >= nfull, wide_qk(i))
    for k in range(grp):
        qk_buf[k] = wide_qk(lead + k)

    def group(first, cur, nxt):
        for k in range(grp):
            qk_buf[nxt * grp + k] = wide_qk(jnp.minimum(first + grp + k, nblk - 1))
        for k in range(grp):
            q_block(first + k, True, qk_buf[cur * grp + k])

    def two_groups(j, _):
        i = lead + 2 * grp * j
        group(i, 0, 1)
        group(i + grp, 1, 0)
        return 0

    lax.fori_loop(0, (nblk - lead) // (2 * grp), two_groups, 0)

    @pl.when(jnp.max(worst_ref[...]) > EXP_ZERO)
    def _():
        def redo(i, _):
            q_block(i, True, wide_qk(i))()
            return 0

        lax.fori_loop(nfull, nblk, redo, 0)


def _fox(p3, f3, bias, gq, gk, bd, tri):
    b, s, _ = p3.shape
    assert s % tri.shape[0] == 0 and s >= WIDE + 2 * FOX_UNROLL * TQ
    kern = functools.partial(_fox_kernel, seq=s)
    c2 = lambda i: (0, 0)
    return pl.pallas_call(
        kern,
        grid=(b,),
        in_specs=[
            pl.BlockSpec((1, s, 4 * GW), lambda i: (i, 0, 0)),
            pl.BlockSpec((1, s, 128), lambda i: (i, 0, 4)),
            pl.BlockSpec((8, tri.shape[0]), c2),
            pl.BlockSpec((1, GW), c2),
            pl.BlockSpec((1, GW), c2),
            pl.BlockSpec((GW, GW), c2),
            pl.BlockSpec(tri.shape, c2),
        ],
        out_specs=pl.BlockSpec((1, s, GW), lambda i: (i, 0, 0)),
        out_shape=jax.ShapeDtypeStruct((b, s, GW), BF16),
        scratch_shapes=[
            pltpu.VMEM((NH, s, GW), BF16),
            pltpu.VMEM((8, s), F32),
            pltpu.VMEM((2 * FOX_UNROLL, NH * TQ, WIDE), F32),
            pltpu.VMEM((8, 128), F32),
        ],
        compiler_params=pltpu.CompilerParams(
            dimension_semantics=("arbitrary",), vmem_limit_bytes=VMEM_LIMIT),
    )(p3, f3, bias, gq, gk, bd, tri)


def _sb_kernel(p_ref, to_ref, tof_ref, bd_ref, o_ref, vx, qk_buf, worst_ref, *, seq):
    to = to_ref[...]
    tq, wide, cw = SB_TQ, SB_WIDE, SB_CW
    hmask = [bd_ref[h * HD:h * HD + 1, :] for h in range(NH)]
    pb = 256
    for b in range(seq // pb):
        r = slice(b * pb, (b + 1) * pb)
        v = p_ref[0, r, 2 * GW:3 * GW]
        for h in range(NH):
            vx[h, r, :] = v * hmask[h]

    rel_w = (lax.broadcasted_iota(jnp.int32, (tq, wide), 1) - lax.broadcasted_iota(jnp.int32, (tq, wide), 0))
    nsub = wide // cw
    nfull = -(-(wide - tq) // tq)

    def log_one_minus_sigmoid(z):
        return -jnp.maximum(z, jnp.log(1.0 + jnp.exp(jnp.minimum(z, SOFTPLUS_LINEAR))))

    def suffix_sums(lom, mat):
        w = lom.shape[1]
        cs = _dot(lom.astype(BF16), mat)
        return cs[:, :w], cs[:, w:]

    def qk_all_heads(r0, start, width):
        qb = p_ref[0, pl.ds(r0, tq), 0:GW]
        qx = jnp.concatenate([qb * m for m in hmask], axis=0)
        return _dot_nt(qx, p_ref[0, pl.ds(start, width), GW:2 * GW])

    def wide_start(r0):
        if isinstance(r0, int):
            return max(r0 - (wide - tq), 0)
        return pl.multiple_of(jnp.maximum(r0 - (wide - tq), 0), tq)

    def wide_qk(i):
        r0 = _aligned(i * tq, tq)
        return qk_all_heads(r0, wide_start(r0), wide)

    def q_block(i, full_tile, qk_wide):
        r0 = _aligned(i * tq, tq)
        s0 = wide_start(r0)
        rows = pl.ds(r0, tq)

        def mask_wide(x):
            if full_tile:
                diag = jnp.where(rel_w[:, wide - cw:] < wide - tq, x[:, wide - cw:], 0.0)
                return jnp.concatenate([x[:, :wide - cw], diag], axis=1)
            return jnp.where(rel_w < r0 - s0, x, 0.0)

        def per_head(qk):
            return [qk[h * tq:(h + 1) * tq] for h in range(NH)]

        def weighted_values(ws, start, width):
            w_all = jnp.concatenate([w.astype(BF16) for w in ws], axis=1)
            v_all = jnp.concatenate([vx[h, pl.ds(start, width), :] for h in range(NH)], axis=0)
            return _dot(w_all, v_all)

        def emit(acc):
            g = p_ref[0, rows, 3 * GW:4 * GW].astype(F32)
            o_ref[0, rows, :] = (acc * g).astype(BF16)

        zs = per_head(qk_wide)
        loms = [log_one_minus_sigmoid(z) for z in zs]
        log_betas = [z + lom for z, lom in zip(zs, loms)]
        sums = [[suffix_sums(mask_wide(lom)[:, c * cw:(c + 1) * cw], to) for c in range(nsub)] for lom in loms]
        carries, ws = [], []
        for h in range(NH):
            between = [None] * nsub
            carry = jnp.zeros((tq, cw), F32)
            for c in reversed(range(nsub)):
                rc, tot = sums[h][c]
                between[c] = rc + carry
                carry = carry + tot
            ws.append(mask_wide(jnp.exp(log_betas[h] + jnp.concatenate(between, axis=1))))
            carries.append(carry)
        acc = weighted_values(ws, s0, wide)
        emit(acc)

        def worst_carry(carries):
            cm = jnp.maximum(jnp.maximum(carries[0], carries[1]), jnp.maximum(carries[2], carries[3]))
            return jnp.max(cm.reshape(tq // 8, 8, cw), axis=0)

        def keep_going(carries):
            return (jnp.max(worst_carry(carries)) > EXP_ZERO).astype(jnp.int32)

        if full_tile:
            worst_ref[0] = jnp.maximum(worst_ref[0], jnp.where(s0 > 0, worst_carry(carries), NEG_BIG))

        def cond(st):
            return jnp.logical_and(st[0] > 0, st[1] > 0)

        def body(st):
            s_end, _, carries, acc = st
            sb = pl.multiple_of(s_end - tq, tq)
            zs = per_head(qk_all_heads(r0, sb, tq))
            loms = [log_one_minus_sigmoid(z) for z in zs]
            sums = [suffix_sums(lom, tof_ref[...]) for lom in loms]
            ws = [jnp.exp(z + lom + rc + cr[:, :tq]) for z, lom, (rc, _), cr in zip(zs, loms, sums, carries)]
            c2 = [cr + tot for cr, (_, tot) in zip(carries, sums)]
            return sb, keep_going(c2), tuple(c2), acc + weighted_values(ws, sb, tq)

        def rare_tail():
            @pl.when(jnp.logical_and(s0 > 0, keep_going(carries) > 0))
            def _():
                st = lax.while_loop(cond, body, (s0, jnp.int32(1), tuple(carries), acc))
                emit(st[3])

        return rare_tail

    nblk = seq // tq
    grp = SB_UNROLL
    lead = nfull + (nblk - nfull) % (2 * grp)
    worst_ref[0] = jnp.full((8, cw), NEG_BIG, F32)
    for i in range(lead):
        q_block(i, i >= nfull, wide_qk(i))
    for k in range(grp):
        qk_buf[k] = wide_qk(lead + k)

    def group(first, cur, nxt):
        for k in range(grp):
            qk_buf[nxt * grp + k] = wide_qk(jnp.minimum(first + grp + k, nblk - 1))
        for k in range(grp):
            q_block(first + k, True, qk_buf[cur * grp + k])

    def two_groups(j, _):
        i = lead + 2 * grp * j
        group(i, 0, 1)
        group(i + grp, 1, 0)
        return 0

    lax.fori_loop(0, (nblk - lead) // (2 * grp), two_groups, 0)

    @pl.when(jnp.max(worst_ref[0]) > EXP_ZERO)
    def _():
        def redo(i, _):
            q_block(i, True, wide_qk(i))()
            return 0

        lax.fori_loop(nfull, nblk, redo, 0)


def _sb(p3, to, tof, bd):
    b, s, _ = p3.shape
    assert s % 256 == 0 and s >= SB_WIDE + 2 * SB_UNROLL * SB_TQ
    kern = functools.partial(_sb_kernel, seq=s)
    return pl.pallas_call(
        kern,
        grid=(b,),
        in_specs=[
            pl.BlockSpec((1, s, 4 * GW), lambda i: (i, 0, 1)),
            pl.BlockSpec(to.shape, lambda i: (0, 0)),
            pl.BlockSpec(tof.shape, lambda i: (0, 0)),
            pl.BlockSpec((GW, GW), lambda i: (0, 0)),
        ],
        out_specs=pl.BlockSpec((1, s, GW), lambda i: (i, 0, 0)),
        out_shape=jax.ShapeDtypeStruct((b, s, GW), BF16),
        scratch_shapes=[
            pltpu.VMEM((NH, s, GW), BF16),
            pltpu.VMEM((2 * SB_UNROLL, NH * SB_TQ, SB_WIDE), F32),
            pltpu.VMEM((1, 8, SB_CW), F32),
        ],
        compiler_params=pltpu.CompilerParams(
            dimension_semantics=("arbitrary",), vmem_limit_bytes=VMEM_LIMIT),
    )(p3, to, tof, bd)


def _hgrn_kernel(hq_ref, hi_ref, hg_ref, hf_ref, lbm_ref, oml_ref, gout_ref, bd_ref, a_ref,
                 mask_ref, hm_ref, o_ref, st_ref, *, seq):
    bd = bd_ref[...]
    a_all = a_ref[...]
    hm = hm_ref[...]
    lbm = lbm_ref[...]
    oml = oml_ref[...]
    c = CHUNK
    st_ref[...] = jnp.zeros((GW, GW), F32)

    n = HSB * c
    nlev = mask_ref.shape[0] - 1
    odd =(lax.broadcasted_iota(jnp.int32, (n, GW), 0) & 1) == 1

    def superblock(bi, _):
        r0 = pl.multiple_of(bi * n, n)
        rows = pl.ds(r0, n)
        hf = hf_ref[0, rows, :]
        sg = 1.0 / (1.0 + jnp.exp(-hf))
        f = lbm + oml * sg
        g = jnp.log(f)
        kk = oml * (1.0 - sg)
        q = hq_ref[0, rows, :].astype(F32)
        v = hi_ref[0, rows, :]
        gh = g.astype(BF16)
        gl = (g - gh.astype(F32)).astype(BF16)
        q2 = (q * jnp.where(odd, f, 1.0)).astype(BF16)
        k2 = (kk * jnp.where(odd, 1.0 / f, 1.0)).astype(BF16)

        def scores(qf, kf, l):
            kx = jnp.concatenate([kf] * NH, axis=0) * hm
            return _dot_nt(qf, kx) * mask_ref[l]

        sls = [slice(ci * c, (ci + 1) * c) for ci in range(HSB)]
        exs = [_dot(a_all, jnp.concatenate([gh[sl], gl[sl]], axis=0)) for sl in sls]

        st = st_ref[...]
        o_inter = []
        for ci, sl in enumerate(sls):
            eb = exs[ci][nlev * c:(nlev + 1) * c]
            er = exs[ci][(nlev + 1) * c:(nlev + 2) * c]
            qd = (q[sl] * jnp.exp(eb)).astype(BF16)
            o_inter.append(_dot_nt(qd, st.astype(BF16) * bd))
            upd = _dot_tn(v[sl], (kk[sl] * jnp.exp(er)).astype(BF16))
            st = st * jnp.exp(eb[c - 1:c, :]) + upd
        st_ref[...] = st

        ps = [scores(q2[sl], k2[sl], nlev) for sl in sls]
        qb = q.astype(BF16)
        kb = kk.astype(BF16)
        for l in range(nlev):
            for ci, sl in enumerate(sls):
                x = jnp.exp(exs[ci][l * c:(l + 1) * c]).astype(BF16)
                ps[ci] = ps[ci] + scores(qb[sl] * x, kb[sl] * x, l)
        outs = []
        for ci, sl in enumerate(sls):
            vx = jnp.concatenate([v[sl]] * NH, axis=0) * hm
            outs.append(o_inter[ci] + _dot(ps[ci].astype(BF16), vx))

        o = _head_rms(jnp.concatenate(outs, axis=0), gout_ref[...], bd)
        o_ref[0, rows, :] = (o * hg_ref[0, rows, :].astype(F32)).astype(BF16)
        return 0

    lax.fori_loop(0, seq // n, superblock, 0)


def _hgrn(p3, f3, lbm, oml, gout, bd, a_all, masks, hm):
    b, s, _ = p3.shape
    assert s % (HSB * CHUNK) == 0
    kern = functools.partial(_hgrn_kernel, seq=s)
    c2 = lambda i: (0, 0)
    return pl.pallas_call(
        kern,
        grid=(b,),
        in_specs=[
            pl.BlockSpec((1, s, GW), lambda i: (i, 0, 8)),
            pl.BlockSpec((1, s, GW), lambda i: (i, 0, 9)),
            pl.BlockSpec((1, s, GW), lambda i: (i, 0, 10)),
            pl.BlockSpec((1, s, GW), lambda i: (i, 0, 0)),
            pl.BlockSpec((1, GW), c2),
            pl.BlockSpec((1, GW), c2),
            pl.BlockSpec((1, GW), c2),
            pl.BlockSpec((GW, GW), c2),
            pl.BlockSpec(a_all.shape, c2),
            pl.BlockSpec(masks.shape, lambda i: (0, 0, 0)),
            pl.BlockSpec(hm.shape, c2),
        ],
        out_specs=pl.BlockSpec((1, s, GW), lambda i: (i, 0, 0)),
        out_shape=jax.ShapeDtypeStruct((b, s, GW), BF16),
        scratch_shapes=[pltpu.VMEM((GW, GW), F32)],
        compiler_params=pltpu.CompilerParams(
            dimension_semantics=("arbitrary",), vmem_limit_bytes=VMEM_LIMIT),
    )(p3, p3, p3, f3, lbm, oml, gout, bd, a_all, masks, hm)


def _pm_kernel(pv_ref, pg_ref, mq_ref, mg_ref, mem_ref, mng_ref, wkv_ref, gmk_ref, bd_ref,
               wp_ref, ps_ref, win_ref, d_ref, e_ref, ubuf, wbuf, kmem, vxm, *, seq, tq):
    bd = bd_ref[...]
    hmask = [bd[h * HD:h * HD + 1, :] for h in range(NH)]
    hmask_f = [m.astype(F32) for m in hmask]
    halo = 16
    mem = mem_ref[0]
    ms = jnp.mean(mem * mem, axis=-1, keepdims=True)
    mn = (mem * lax.rsqrt(ms + EPS) * mng_ref[...]).astype(BF16)
    kv = _dot(mn, wkv_ref[...])
    kn = _head_rms(kv[:, :GW], gmk_ref[...], bd).astype(BF16)
    vv = kv[:, GW:].astype(BF16)
    nm = kn.shape[0]
    kmem[...] = kn
    for h in range(NH):
        vxm[h * nm:(h + 1) * nm, :] = vv * hmask[h]

    ubuf[0:halo, :] = jnp.zeros((halo, GW), F32)
    ubuf[halo:halo + seq, :] = pv_ref[0]
    win = win_ref[...]
    inv_win = 1.0 / win
    for k in range(3):
        sh = 1 << k
        wbuf[k, 0:halo, :] = jnp.zeros((halo, GW), F32)
        for b in range(seq // tq):
            base = halo + b * tq
            if k == 0:
                wbuf[k, base:base + tq, :] = ubuf[base:base + tq, :] + ubuf[base - sh:base - sh + tq, :]
            else:
                wbuf[k, base:base + tq, :] = (wbuf[k - 1, base:base + tq, :]
                                              + wbuf[k - 1, base - sh:base - sh + tq, :])

    for b in range(seq // tq):
        r = slice(b * tq, (b + 1) * tq)
        base = halo + b * tq
        u = ubuf[base:base + tq, :]
        s2 = wbuf[0, base:base + tq, :]
        s4 = wbuf[1, base:base + tq, :]
        s8 = wbuf[2, base:base + tq, :]
        s16 = s8 + wbuf[2, base - 8:base - 8 + tq, :]
        sw = jnp.where(win == 2.0, s2, jnp.where(win == 4.0, s4, jnp.where(win == 8.0, s8, s16)))
        if b == 0:
            pos = (lax.broadcasted_iota(jnp.int32, (tq, GW), 0) + 1).astype(F32)
            pooled = sw / jnp.minimum(pos, win)
        else:
            pooled = sw * inv_win
        y = _dot((pooled - u).astype(BF16), wp_ref[...]) * ps_ref[...]
        d_ref[0, r, :] = (y * pg_ref[0, r, :].astype(F32)).astype(BF16)

        qn = mq_ref[0, r, :]
        s_all = _dot_nt(jnp.concatenate([qn * m for m in hmask], axis=0), kmem[...])
        ss = [s_all[h * tq:(h + 1) * tq] for h in range(NH)]
        ps = [jnp.exp(s - jnp.max(s, axis=-1, keepdims=True)) for s in ss]
        ls = [jnp.sum(p, axis=-1, keepdims=True) for p in ps]
        inv_l = (1.0 / ls[0]) * hmask_f[0]
        for h in range(1, NH):
            inv_l = inv_l + (1.0 / ls[h]) * hmask_f[h]
        oe = _dot(jnp.concatenate([p.astype(BF16) for p in ps], axis=1), vxm[...]) * inv_l
        e_ref[0, r, :] = (oe * mg_ref[0, r, :].astype(F32)).astype(BF16)


def _pm(p3, f3, mem, mng, wkv, gmk, bd, wp, ps, win, tq):
    b, s, _ = p3.shape
    nm = mem.shape[1]
    kern = functools.partial(_pm_kernel, seq=s, tq=tq)
    c2 = lambda i: (0, 0)
    return pl.pallas_call(
        kern,
        grid=(b,),
        in_specs=[
            pl.BlockSpec((1, s, GW), lambda i: (i, 0, 1)),
            pl.BlockSpec((1, s, GW), lambda i: (i, 0, 11)),
            pl.BlockSpec((1, s, GW), lambda i: (i, 0, 12)),
            pl.BlockSpec((1, s, GW), lambda i: (i, 0, 13)),
            pl.BlockSpec((1, nm, D_MODEL), lambda i: (i, 0, 0)),
            pl.BlockSpec((1, D_MODEL), c2),
            pl.BlockSpec((D_MODEL, 2 * GW), c2),
            pl.BlockSpec((1, GW), c2),
            pl.BlockSpec((GW, GW), c2),
            pl.BlockSpec((GW, GW), c2),
            pl.BlockSpec((1, GW), c2),
            pl.BlockSpec((1, GW), c2),
        ],
        out_specs=[
            pl.BlockSpec((1, s, GW), lambda i: (i, 0, 0)),
            pl.BlockSpec((1, s, GW), lambda i: (i, 0, 0)),
        ],
        out_shape=[
            jax.ShapeDtypeStruct((b, s, GW), BF16),
            jax.ShapeDtypeStruct((b, s, GW), BF16),
        ],
        scratch_shapes=[
            pltpu.VMEM((s + 16, GW), F32),
            pltpu.VMEM((3, s + 16, GW), F32),
            pltpu.VMEM((nm, GW), BF16),
            pltpu.VMEM((NH * nm, GW), BF16),
        ],
        compiler_params=pltpu.CompilerParams(
            dimension_semantics=("arbitrary",), vmem_limit_bytes=VMEM_LIMIT),
    )(f3, p3, p3, p3, mem, mng, wkv, gmk, bd, wp, ps, win)


def _out_kernel(a_ref, b_ref, c_ref, d_ref, e_ref, w_ref, x_ref, o_ref):
    mixed = jnp.concatenate([a_ref[...], b_ref[...], c_ref[...], d_ref[...], e_ref[...]], axis=1)
    o_ref[...] = x_ref[...] + _dot(mixed, w_ref[...])


def _out(parts, w, x2, tm):
    m = x2.shape[0]
    gspec = pl.BlockSpec((tm, GW), lambda i: (i, 0))
    return pl.pallas_call(
        _out_kernel,
        grid=(m // tm,),
        in_specs=[gspec] * 5 + [
            pl.BlockSpec((5 * GW, D_MODEL), lambda i: (0, 0)),
            pl.BlockSpec((tm, D_MODEL), lambda i: (i, 0)),
        ],
        out_specs=pl.BlockSpec((tm, D_MODEL), lambda i: (i, 0)),
        out_shape=jax.ShapeDtypeStruct((m, D_MODEL), F32),
        compiler_params=pltpu.CompilerParams(
            dimension_semantics=("arbitrary",), vmem_limit_bytes=VMEM_LIMIT),
    )(*parts, w, x2)


def _out_proj_kernel(a_ref, b_ref, c_ref, d_ref, e_ref, w_ref, x_ref, g_ref, wb_ref, wf_ref, hgain_ref, bd_ref,
                     o_ref, pb_ref, pf_ref):
    _out_kernel(a_ref, b_ref, c_ref, d_ref, e_ref, w_ref, x_ref, o_ref)
    _proj_kernel(o_ref, g_ref, wb_ref, wf_ref, hgain_ref, bd_ref, pb_ref, pf_ref)


def _out_proj(parts, w, x2, g, wb, wf, hgain, bd, tm):
    m = x2.shape[0]
    gspec = pl.BlockSpec((tm, GW), lambda i: (i, 0))
    c2 = lambda i: (0, 0)
    return pl.pallas_call(
        _out_proj_kernel,
        grid=(m // tm,),
        in_specs=[gspec] * 5 + [
            pl.BlockSpec((5 * GW, D_MODEL), c2),
            pl.BlockSpec((tm, D_MODEL), lambda i: (i, 0)),
            pl.BlockSpec((1, D_MODEL), c2),
            pl.BlockSpec((D_MODEL, NP_COLS), c2),
            pl.BlockSpec((D_MODEL, NF_COLS), c2),
            pl.BlockSpec((8, GW), c2),
            pl.BlockSpec((GW, GW), c2),
        ],
        out_specs=[
            pl.BlockSpec((tm, D_MODEL), lambda i: (i, 0)),
            pl.BlockSpec((tm, NP_COLS), lambda i: (i, 0)),
            pl.BlockSpec((tm, NF_COLS), lambda i: (i, 0)),
        ],
        out_shape=[
            jax.ShapeDtypeStruct((m, D_MODEL), F32),
            jax.ShapeDtypeStruct((m, NP_COLS), BF16),
            jax.ShapeDtypeStruct((m, NF_COLS), F32),
        ],
        compiler_params=pltpu.CompilerParams(
            dimension_semantics=("arbitrary",), vmem_limit_bytes=VMEM_LIMIT),
    )(*parts, w, x2, g, wb, wf, hgain, bd)


def _tile_heads(g):
    return jnp.tile(g.astype(F32), NH).reshape(1, GW)


def kernel(x, mem, norm_g, w_in, fox_f_bias, fox_q_norm, fox_k_norm, hgrn_lb_logits, hgrn_out_norm,
           pool_w, pool_scale, mem_norm_g, mem_w_kv, mem_q_norm, mem_k_norm, w_out):
    bsz, seq, _ = x.shape
    depth = w_in.shape[0]
    m = bsz * seq
    tq = 256
    tm = 512

    pr = jax.nn.softmax(hgrn_lb_logits.astype(F32), axis=0)
    lower_bounds = jnp.clip(jnp.cumsum(pr, axis=0) - pr[0:1], 0.0, 1.0 - 1e-6)

    bd_np = _bd_ones()
    bd = jnp.asarray(bd_np, BF16)
    tri = jnp.asarray(np.triu(np.ones((tq, tq), np.float32)), BF16)

    def suffix_and_ones(w):
        jj = np.arange(w)
        suffix = (jj[:, None] > jj[None, :]).astype(np.float32)
        return jnp.asarray(np.concatenate([suffix, np.ones((w, SB_CW), np.float32)], axis=1), BF16)

    to = suffix_and_ones(SB_CW)
    tof = suffix_and_ones(SB_TQ)
    a_np, masks_np, hm_np = _hgrn_constants()
    a_all = jnp.asarray(a_np, BF16)
    masks = jnp.asarray(masks_np, F32)
    hm = jnp.asarray(hm_np, BF16)
    win = jnp.asarray(np.repeat(np.array(POOL_WINDOWS, np.float32), HD).reshape(1, GW))

    g = GW
    wb_all, wf_all = _regroup_w_in(w_in)
    x2 = x.reshape(m, D_MODEL)
    norm_gs = [norm_g[l].reshape(1, D_MODEL).astype(F32) for l in range(depth)]
    head_gains = [jnp.concatenate([_tile_heads(fox_q_norm[l]), _tile_heads(fox_k_norm[l]),
                                   _tile_heads(mem_q_norm[l]), jnp.zeros((5, GW), F32)], axis=0)
                  for l in range(depth)]
    pb, pf = _proj(x2, norm_gs[0], wb_all[0], wf_all[0], head_gains[0], bd, tm)
    for l in range(depth):
        p3 = pb.reshape(bsz, seq, NP_COLS)
        f3 = pf.reshape(bsz, seq, NF_COLS)

        bias = jnp.broadcast_to(jnp.pad(fox_f_bias[l].astype(F32), (0, 8 - NH))[:, None], (8, tq))
        out_a = _fox(p3, f3, bias, _tile_heads(fox_q_norm[l]), _tile_heads(fox_k_norm[l]), bd, tri)
        out_b = _sb(p3, to, tof, bd)

        lb = lower_bounds[l].reshape(1, g)
        out_c = _hgrn(p3, f3, jnp.maximum(lb, LB_FLOOR), 1.0 - lb,
                      hgrn_out_norm[l].reshape(1, g).astype(F32), bd, a_all, masks, hm)

        wp = jax.scipy.linalg.block_diag(*[pool_w[l, i] for i in range(len(POOL_WINDOWS))]).astype(BF16)
        out_d, out_e = _pm(p3, f3, mem, mem_norm_g[l].reshape(1, D_MODEL).astype(F32),
                           mem_w_kv[l].astype(BF16), _tile_heads(mem_k_norm[l]),
                           bd, wp, pool_scale[l].reshape(1, g).astype(F32), win, tq)

        parts = [o.reshape(m, g) for o in (out_a, out_b, out_c, out_d, out_e)]
        if l + 1 < depth:
            x2, pb, pf = _out_proj(parts, w_out[l].astype(BF16), x2, norm_gs[l + 1],
                                   wb_all[l + 1], wf_all[l + 1], head_gains[l + 1], bd, tm)
        else:
            x2 = _out(parts, w_out[l].astype(BF16), x2, tm)
    return x2.reshape(bsz, seq, D_MODEL)
```

```python
import functools

import numpy as np
import jax
import jax.numpy as jnp
from jax import lax
from jax.experimental import pallas as pl
from jax.experimental.pallas import tpu as pltpu

F32 = jnp.float32
BF16 = jnp.bfloat16

D_MODEL = 1024
GW = 256
NH = 4
HD = 64
CHUNK = 64
POOL_WINDOWS = (2, 4, 8, 16)
EPS = 1e-6
NEG_BIG = -1e30
LB_FLOOR = 1e-30
SCALE = HD ** -0.5

NP_COLS = 14 * GW
P_FQ, P_FK, P_FV, P_FG, P_SQ, P_SK, P_SV, P_SG, P_HQ, P_HI, P_HG, P_PG, P_MQ, P_MG = range(14)
NF_COLS = 2 * GW + 128
F_HF, F_PV, F_FF = range(3)
VMEM_LIMIT = 56 * 1024 * 1024

TQ = 128
WIDE = 3 * TQ
EXP_ZERO = -104.0
NORM_SLACK = 1.01
LOGIT_SLACK = 0.05
FOX_UNROLL = 2
HSB = 8
SB_TQ = 128
SB_WIDE = 384
SB_CW = 128
SB_UNROLL = 2
SOFTPLUS_LINEAR = 80.0


def _dot(a, b):
    return jnp.dot(a, b, preferred_element_type=F32)


def _dot_nt(a, b):
    return lax.dot_general(a, b, (((1,), (1,)), ((), ())), preferred_element_type=F32)


def _dot_tn(a, b):
    return lax.dot_general(a, b, (((0,), (0,)), ((), ())), preferred_element_type=F32)


def _aligned(x, m):
    return x if isinstance(x, int) else pl.multiple_of(x, m)


def _split_bf16(x, n):
    parts = []
    r = x
    for i in range(n):
        p = r.astype(BF16)
        parts.append(p)
        if i + 1 < n:
            r = r - p.astype(F32)
    return parts


def _dot_f32_rhs01(x, m01, n=3):
    acc = None
    for p in _split_bf16(x, n):
        t = _dot(p, m01)
        acc = t if acc is None else acc + t
    return acc


def _silu(x):
    return x / (1.0 + jnp.exp(-x))


def _log_sigmoid(x):
    return jnp.minimum(x, 0.0) - jnp.log(1.0 + jnp.exp(-jnp.abs(x)))


def _head_rms(x, gain, bd):
    ss = _dot((x * x).astype(BF16), bd)
    return x * lax.rsqrt(ss * (1.0 / HD) + EPS) * gain


def _bd_ones():
    h = np.arange(GW) // HD
    return (h[:, None] == h[None, :]).astype(np.float32)


def _hgrn_constants():
    c = CHUNK
    t = np.arange(c)
    j = np.arange(c)[None, :]
    mats = []
    masks = []
    for n in (64, 32, 16, 8, 4):
        blk, pos = t // n, t % n
        ref = blk * n + n // 2 - 1
        aq = (pos[:, None] >= n // 2) & (j > ref[:, None]) & (j <= t[:, None])
        ak = (pos[:, None] < n // 2) & (j > t[:, None]) & (j <= ref[:, None])
        mats.append((aq | ak).astype(np.float32))
        m = (blk[:, None] == blk[None, :]) & (pos[:, None] >= n // 2) & (pos[None, :] < n // 2)
        masks.append(m.astype(np.float32))
    m = ((t[:, None] // 2) == (t[None, :] // 2)) & (t[None, :] <= t[:, None])
    masks.append(m.astype(np.float32))
    mats.append((j <= t[:, None]).astype(np.float32))
    mats.append((j > t[:, None]).astype(np.float32))
    a_all = np.concatenate(mats, axis=0)
    a_all = np.concatenate([a_all, a_all], axis=1)
    masks = np.stack([np.tile(m, (1, NH)) for m in masks])
    total = masks[:, :, :c].sum(0)
    assert np.array_equal(total, np.tril(np.ones((c, c), np.float32)))
    hm = (np.arange(NH * c)[:, None] // c == np.arange(GW)[None, :] // HD).astype(np.float32)
    return a_all, masks, hm


def _regroup_kernel(w_ref, wb_ref, wf_ref):
    g = GW
    o_ff, o_sb, o_hg, o_pl, o_mm = 4 * g, 4 * g + NH, 8 * g + NH, 12 * g + NH, 14 * g + NH
    w = w_ref[0]
    bf_src = {P_FQ: 0, P_FK: g, P_FV: 2 * g, P_FG: 3 * g,
              P_SQ: o_sb, P_SK: o_sb + g, P_SV: o_sb + 2 * g, P_SG: o_sb + 3 * g,
              P_HQ: o_hg, P_HI: o_hg + 2 * g, P_HG: o_hg + 3 * g,
              P_PG: o_pl + g, P_MQ: o_mm, P_MG: o_mm + g}
    f_src = {F_HF: o_hg + g, F_PV: o_pl}
    for dst, c0 in bf_src.items():
        wb_ref[0, :, dst * g:(dst + 1) * g] = w[:, c0:c0 + g].astype(BF16)
    for dst, c0 in f_src.items():
        wf_ref[0, :, dst * g:(dst + 1) * g] = w[:, c0:c0 + g].astype(BF16)
    ff = jnp.concatenate([w[:, o_ff:o_ff + NH], jnp.zeros((w.shape[0], 128 - NH), F32)], axis=1)
    wf_ref[0, :, F_FF * g:F_FF * g + 128] = ff.astype(BF16)


def _regroup_w_in(w_in):
    depth, d, n = w_in.shape
    tr = 128
    return pl.pallas_call(
        _regroup_kernel,
        grid=(depth, d // tr),
        in_specs=[pl.BlockSpec((1, tr, n), lambda l, i: (l, i, 0))],
        out_specs=[
            pl.BlockSpec((1, tr, NP_COLS), lambda l, i: (l, i, 0)),
            pl.BlockSpec((1, tr, NF_COLS), lambda l, i: (l, i, 0)),
        ],
        out_shape=[
            jax.ShapeDtypeStruct((depth, d, NP_COLS), BF16),
            jax.ShapeDtypeStruct((depth, d, NF_COLS), BF16),
        ],
        compiler_params=pltpu.CompilerParams(
            dimension_semantics=("arbitrary", "arbitrary"), vmem_limit_bytes=VMEM_LIMIT),
    )(w_in)


def _proj_kernel(x_ref, g_ref, wb_ref, wf_ref, hgain_ref, bd_ref, pb_ref, pf_ref):
    x = x_ref[...]
    ms = jnp.mean(x * x, axis=-1, keepdims=True)
    h = (x * lax.rsqrt(ms + EPS) * g_ref[...]).astype(BF16)
    gains = hgain_ref[...]
    bd = bd_ref[...]
    normed = {P_FQ: (0, SCALE), P_FK: (1, 1.0), P_MQ: (2, SCALE)}
    post = {P_SQ: lambda r: r * SCALE}
    post.update({gi: _silu for gi in (P_FG, P_SG, P_HQ, P_HG, P_PG, P_MG)})

    def cols(gi):
        return slice(gi * GW, (gi + 1) * GW)

    raw = {gi: _dot(h, wb_ref[:, cols(gi)]) for gi in normed}
    for gi in range(NP_COLS // GW):
        if gi not in normed:
            r = _dot(h, wb_ref[:, cols(gi)])
            pb_ref[:, cols(gi)] = post.get(gi, lambda r: r)(r).astype(BF16)
    pf_ref[...] = _dot(h, wf_ref[...])
    for gi, (row, scale) in normed.items():
        pb_ref[:, cols(gi)] = (_head_rms(raw[gi], gains[row:row + 1], bd) * scale).astype(BF16)


def _proj(x2, g, wb, wf, hgain, bd, tm):
    m = x2.shape[0]
    return pl.pallas_call(
        _proj_kernel,
        grid=(m // tm,),
        in_specs=[
            pl.BlockSpec((tm, D_MODEL), lambda i: (i, 0)),
            pl.BlockSpec((1, D_MODEL), lambda i: (0, 0)),
            pl.BlockSpec((D_MODEL, NP_COLS), lambda i: (0, 0)),
            pl.BlockSpec((D_MODEL, NF_COLS), lambda i: (0, 0)),
            pl.BlockSpec((8, GW), lambda i: (0, 0)),
            pl.BlockSpec((GW, GW), lambda i: (0, 0)),
        ],
        out_specs=[
            pl.BlockSpec((tm, NP_COLS), lambda i: (i, 0)),
            pl.BlockSpec((tm, NF_COLS), lambda i: (i, 0)),
        ],
        out_shape=[
            jax.ShapeDtypeStruct((m, NP_COLS), BF16),
            jax.ShapeDtypeStruct((m, NF_COLS), F32),
        ],
        compiler_params=pltpu.CompilerParams(
            dimension_semantics=("arbitrary",), vmem_limit_bytes=VMEM_LIMIT),
    )(x2, g, wb, wf, hgain, bd)


def _fox_kernel(p_ref, ff_ref, bias_ref, gq_ref, gk_ref, bd_ref, tri_ref, o_ref,
                vx, crow, qk_buf, worst_ref, *, seq):
    bd = bd_ref[...]
    tri = tri_ref[...]
    pb = tri.shape[0]
    hmask = [bd[h * HD:h * HD + 1, :] for h in range(NH)]
    hmask_f = [m.astype(F32) for m in hmask]
    qk_max = (HD * SCALE * NORM_SLACK) * (jnp.max(jnp.abs(gq_ref[...]), axis=-1, keepdims=True)
                                          * jnp.max(jnp.abs(gk_ref[...]), axis=-1, keepdims=True))
    carry = jnp.zeros((8, 1), F32)
    for b in range(seq // pb):
        r = slice(b * pb, (b + 1) * pb)
        lf = _log_sigmoid(ff_ref[0, r, :].T[:8, :] + bias_ref[...])
        cb = _dot_f32_rhs01(lf, tri, 3) + carry
        carry = cb[:, pb - 1:pb]
        crow[:, r] = cb
        v = p_ref[0, r, 2 * GW:3 * GW]
        for h in range(NH):
            vx[h, r, :] = v * hmask[h]

    rel_w = (lax.broadcasted_iota(jnp.int32, (TQ, WIDE), 1) - lax.broadcasted_iota(jnp.int32, (TQ, WIDE), 0))
    nfull = (WIDE - TQ) // TQ

    def qk_all_heads(r0, start, width):
        qb = p_ref[0, pl.ds(r0, TQ), 0:GW]
        qx = jnp.concatenate([qb * m for m in hmask], axis=0)
        return _dot_nt(qx, p_ref[0, pl.ds(start, width), GW:2 * GW])

    def wide_start(r0):
        if isinstance(r0, int):
            return max(r0 - (WIDE - TQ), 0)
        return pl.multiple_of(jnp.maximum(r0 - (WIDE - TQ), 0), TQ)

    def wide_qk(i):
        r0 = _aligned(i * TQ, TQ)
        return qk_all_heads(r0, wide_start(r0), WIDE)

    def q_block(i, full_tile, qk_wide):
        r0 = _aligned(i * TQ, TQ)
        s0 = wide_start(r0)
        rows = pl.ds(r0, TQ)

        def mask_wide(s):
            if full_tile:
                diag = jnp.where(rel_w[:, WIDE - TQ:] <= WIDE - TQ, s[:, WIDE - TQ:], NEG_BIG)
                return jnp.concatenate([s[:, :WIDE - TQ], diag], axis=1)
            return jnp.where(rel_w <= r0 - s0, s, NEG_BIG)

        def tiles(qk, start, width):
            return [qk[h * TQ:(h + 1) * TQ] - crow[h:h + 1, pl.ds(start, width)] for h in range(NH)]

        def weighted_values(ps, start, width):
            p_all = jnp.concatenate([p.astype(BF16) for p in ps], axis=1)
            v_all = jnp.concatenate([vx[h, pl.ds(start, width), :] for h in range(NH)], axis=0)
            return _dot(p_all, v_all)

        def per_head_lanes(cols):
            out = cols[0] * hmask_f[0]
            for h in range(1, NH):
                out = out + cols[h] * hmask_f[h]
            return out

        def weight_bound(s_end, ms):
            last = pl.multiple_of(jnp.maximum(s_end - TQ, 0), TQ)
            bound = None
            for h in range(NH):
                c_last = crow[h:h + 1, pl.ds(last, TQ)][:, TQ - 1:TQ]
                b_h = (qk_max + LOGIT_SLACK) - c_last - jnp.min(ms[h], axis=0, keepdims=True)
                bound = b_h if bound is None else jnp.maximum(bound, b_h)
            return bound

        def keep_going(s_end, ms):
            return (jnp.max(weight_bound(s_end, ms)) > EXP_ZERO).astype(jnp.int32)

        ss = [mask_wide(s) for s in tiles(qk_wide, s0, WIDE)]
        ms = [jnp.max(s, axis=-1, keepdims=True) for s in ss]
        ps = [jnp.exp(s - m) for s, m in zip(ss, ms)]
        ls = [jnp.sum(p, axis=-1, keepdims=True) for p in ps]
        acc = weighted_values(ps, s0, WIDE)

        def emit(acc, ls):
            g = p_ref[0, rows, 3 * GW:4 * GW].astype(F32)
            o_ref[0, rows, :] = (acc * per_head_lanes([1.0 / l for l in ls]) * g).astype(BF16)

        emit(acc, ls)
        if full_tile:
            worst_ref[...] = jnp.maximum(worst_ref[...], jnp.where(s0 > 0, weight_bound(s0, ms), NEG_BIG))

        def cond(st):
            return jnp.logical_and(st[0] > 0, st[1] > 0)

        def body(st):
            s_end, _, ms, ls, acc = st
            sb = pl.multiple_of(s_end - TQ, TQ)
            ss = tiles(qk_all_heads(r0, sb, TQ), sb, TQ)
            ms2 = [jnp.maximum(m, jnp.max(s, axis=-1, keepdims=True)) for s, m in zip(ss, ms)]
            ps = [jnp.exp(s - m) for s, m in zip(ss, ms2)]
            alphas = [jnp.exp(m - m2) for m, m2 in zip(ms, ms2)]
            ls2 = [a * l + jnp.sum(p, axis=-1, keepdims=True) for a, l, p in zip(alphas, ls, ps)]
            acc2 = acc * per_head_lanes(alphas) + weighted_values(ps, sb, TQ)
            return sb, keep_going(sb, ms2), tuple(ms2), tuple(ls2), acc2

        def rare_tail():
            @pl.when(jnp.logical_and(s0 > 0, keep_going(s0, ms) > 0))
            def _():
                _, _, _, ls_f, acc_f = lax.while_loop(cond, body, (s0, jnp.int32(1), tuple(ms), tuple(ls), acc))
                emit(acc_f, ls_f)

        return rare_tail

    nblk = seq // TQ
    grp = FOX_UNROLL
    lead = nfull + (nblk - nfull) % (2 * grp)
    worst_ref[...] = jnp.full(worst_ref.shape, NEG_BIG, F32)
    for i in range(lead):
        q_block(i, i >= nfull, wide_qk(i))
    for k in range(grp):
        qk_buf[k] = wide_qk(lead + k)

    def group(first, cur, nxt):
        for k in range(grp):
            qk_buf[nxt * grp + k] = wide_qk(jnp.minimum(first + grp + k, nblk - 1))
        for k in range(grp):
            q_block(first + k, True, qk_buf[cur * grp + k])

    def two_groups(j, _):
        i = lead + 2 * grp * j
        group(i, 0, 1)
        group(i + grp, 1, 0)
        return 0

    lax.fori_loop(0, (nblk - lead) // (2 * grp), two_groups, 0)

    @pl.when(jnp.max(worst_ref[...]) > EXP_ZERO)
    def _():
        def redo(i, _):
            q_block(i, True, wide_qk(i))()
            return 0

        lax.fori_loop(nfull, nblk, redo, 0)


def _fox(p3, f3, bias, gq, gk, bd, tri):
    b, s, _ = p3.shape
    assert s % tri.shape[0] == 0 and s >= WIDE + 2 * FOX_UNROLL * TQ
    kern = functools.partial(_fox_kernel, seq=s)
    c2 = lambda i: (0, 0)
    return pl.pallas_call(
        kern,
        grid=(b,),
        in_specs=[
            pl.BlockSpec((1, s, 4 * GW), lambda i: (i, 0, P_FQ // 4)),
            pl.BlockSpec((1, s, 128), lambda i: (i, 0, F_FF * GW // 128)),
            pl.BlockSpec((8, tri.shape[0]), c2),
            pl.BlockSpec((1, GW), c2),
            pl.BlockSpec((1, GW), c2),
            pl.BlockSpec((GW, GW), c2),
            pl.BlockSpec(tri.shape, c2),
        ],
        out_specs=pl.BlockSpec((1, s, GW), lambda i: (i, 0, 0)),
        out_shape=jax.ShapeDtypeStruct((b, s, GW), BF16),
        scratch_shapes=[
            pltpu.VMEM((NH, s, GW), BF16),
            pltpu.VMEM((8, s), F32),
            pltpu.VMEM((2 * FOX_UNROLL, NH * TQ, WIDE), F32),
            pltpu.VMEM((8, 128), F32),
        ],
        compiler_params=pltpu.CompilerParams(
            dimension_semantics=("arbitrary",), vmem_limit_bytes=VMEM_LIMIT),
    )(p3, f3, bias, gq, gk, bd, tri)


def _sb_kernel(p_ref, to_ref, tof_ref, bd_ref, o_ref, vx, qk_buf, worst_ref, *, seq):
    to = to_ref[...]
    tq, wide, cw = SB_TQ, SB_WIDE, SB_CW
    hmask = [bd_ref[h * HD:h * HD + 1, :] for h in range(NH)]
    pb = 256
    for b in range(seq // pb):
        r = slice(b * pb, (b + 1) * pb)
        v = p_ref[0, r, 2 * GW:3 * GW]
        for h in range(NH):
            vx[h, r, :] = v * hmask[h]

    rel_w = (lax.broadcasted_iota(jnp.int32, (tq, wide), 1) - lax.broadcasted_iota(jnp.int32, (tq, wide), 0))
    nsub = wide // cw
    nfull = -(-(wide - tq) // tq)

    def log_one_minus_sigmoid(z):
        return -jnp.maximum(z, jnp.log(1.0 + jnp.exp(jnp.minimum(z, SOFTPLUS_LINEAR))))

    def suffix_sums(lom, mat):
        w = lom.shape[1]
        cs = _dot(lom.astype(BF16), mat)
        return cs[:, :w], cs[:, w:]

    def qk_all_heads(r0, start, width):
        qb = p_ref[0, pl.ds(r0, tq), 0:GW]
        qx = jnp.concatenate([qb * m for m in hmask], axis=0)
        return _dot_nt(qx, p_ref[0, pl.ds(start, width), GW:2 * GW])

    def wide_start(r0):
        if isinstance(r0, int):
            return max(r0 - (wide - tq), 0)
        return pl.multiple_of(jnp.maximum(r0 - (wide - tq), 0), tq)

    def wide_qk(i):
        r0 = _aligned(i * tq, tq)
        return qk_all_heads(r0, wide_start(r0), wide)

    def q_block(i, full_tile, qk_wide):
        r0 = _aligned(i * tq, tq)
        s0 = wide_start(r0)
        rows = pl.ds(r0, tq)

        def mask_wide(x):
            if full_tile:
                diag = jnp.where(rel_w[:, wide - cw:] < wide - tq, x[:, wide - cw:], 0.0)
                return jnp.concatenate([x[:, :wide - cw], diag], axis=1)
            return jnp.where(rel_w < r0 - s0, x, 0.0)

        def per_head(qk):
            return [qk[h * tq:(h + 1) * tq] for h in range(NH)]

        def weighted_values(ws, start, width):
            w_all = jnp.concatenate([w.astype(BF16) for w in ws], axis=1)
            v_all = jnp.concatenate([vx[h, pl.ds(start, width), :] for h in range(NH)], axis=0)
            return _dot(w_all, v_all)

        def emit(acc):
            g = p_ref[0, rows, 3 * GW:4 * GW].astype(F32)
            o_ref[0, rows, :] = (acc * g).astype(BF16)

        zs = per_head(qk_wide)
        loms = [log_one_minus_sigmoid(z) for z in zs]
        log_betas = [z + lom for z, lom in zip(zs, loms)]
        sums = [[suffix_sums(mask_wide(lom)[:, c * cw:(c + 1) * cw], to) for c in range(nsub)] for lom in loms]
        carries, ws = [], []
        for h in range(NH):
            between = [None] * nsub
            carry = jnp.zeros((tq, cw), F32)
            for c in reversed(range(nsub)):
                rc, tot = sums[h][c]
                between[c] = rc + carry
                carry = carry + tot
            ws.append(mask_wide(jnp.exp(log_betas[h] + jnp.concatenate(between, axis=1))))
            carries.append(carry)
        acc = weighted_values(ws, s0, wide)
        emit(acc)

        def worst_carry(carries):
            cm = jnp.maximum(jnp.maximum(carries[0], carries[1]), jnp.maximum(carries[2], carries[3]))
            return jnp.max(cm.reshape(tq // 8, 8, cw), axis=0)

        def keep_going(carries):
            return (jnp.max(worst_carry(carries)) > EXP_ZERO).astype(jnp.int32)

        if full_tile:
            worst_ref[0] = jnp.maximum(worst_ref[0], jnp.where(s0 > 0, worst_carry(carries), NEG_BIG))

        def cond(st):
            return jnp.logical_and(st[0] > 0, st[1] > 0)

        def body(st):
            s_end, _, carries, acc = st
            sb = pl.multiple_of(s_end - tq, tq)
            zs = per_head(qk_all_heads(r0, sb, tq))
            loms = [log_one_minus_sigmoid(z) for z in zs]
            sums = [suffix_sums(lom, tof_ref[...]) for lom in loms]
            ws = [jnp.exp(z + lom + rc + cr[:, :tq]) for z, lom, (rc, _), cr in zip(zs, loms, sums, carries)]
            c2 = [cr + tot for cr, (_, tot) in zip(carries, sums)]
            return sb, keep_going(c2), tuple(c2), acc + weighted_values(ws, sb, tq)

        def rare_tail():
            @pl.when(jnp.logical_and(s0 > 0, keep_going(carries) > 0))
            def _():
                st = lax.while_loop(cond, body, (s0, jnp.int32(1), tuple(carries), acc))
                emit(st[3])

        return rare_tail

    nblk = seq // tq
    grp = SB_UNROLL
    lead = nfull + (nblk - nfull) % (2 * grp)
    worst_ref[0] = jnp.full((8, cw), NEG_BIG, F32)
    for i in range(lead):
        q_block(i, i >= nfull, wide_qk(i))
    for k in range(grp):
        qk_buf[k] = wide_qk(lead + k)

    def group(first, cur, nxt):
        for k in range(grp):
            qk_buf[nxt * grp + k] = wide_qk(jnp.minimum(first + grp + k, nblk - 1))
        for k in range(grp):
            q_block(first + k, True, qk_buf[cur * grp + k])

    def two_groups(j, _):
        i = lead + 2 * grp * j
        group(i, 0, 1)
        group(i + grp, 1, 0)
        return 0

    lax.fori_loop(0, (nblk - lead) // (2 * grp), two_groups, 0)

    @pl.when(jnp.max(worst_ref[0]) > EXP_ZERO)
    def _():
        def redo(i, _):
            q_block(i, True, wide_qk(i))()
            return 0

        lax.fori_loop(nfull, nblk, redo, 0)


def _sb(p3, to, tof, bd):
    b, s, _ = p3.shape
    assert s % 256 == 0 and s >= SB_WIDE + 2 * SB_UNROLL * SB_TQ
    kern = functools.partial(_sb_kernel, seq=s)
    return pl.pallas_call(
        kern,
        grid=(b,),
        in_specs=[
            pl.BlockSpec((1, s, 4 * GW), lambda i: (i, 0, P_SQ // 4)),
            pl.BlockSpec(to.shape, lambda i: (0, 0)),
            pl.BlockSpec(tof.shape, lambda i: (0, 0)),
            pl.BlockSpec((GW, GW), lambda i: (0, 0)),
        ],
        out_specs=pl.BlockSpec((1, s, GW), lambda i: (i, 0, 0)),
        out_shape=jax.ShapeDtypeStruct((b, s, GW), BF16),
        scratch_shapes=[
            pltpu.VMEM((NH, s, GW), BF16),
            pltpu.VMEM((2 * SB_UNROLL, NH * SB_TQ, SB_WIDE), F32),
            pltpu.VMEM((1, 8, SB_CW), F32),
        ],
        compiler_params=pltpu.CompilerParams(
            dimension_semantics=("arbitrary",), vmem_limit_bytes=VMEM_LIMIT),
    )(p3, to, tof, bd)


def _hgrn_kernel(hq_ref, hi_ref, hg_ref, hf_ref, lbm_ref, oml_ref, gout_ref, bd_ref, a_ref,
                 mask_ref, hm_ref, o_ref, st_ref, *, seq):
    bd = bd_ref[...]
    a_all = a_ref[...]
    hm = hm_ref[...]
    lbm = lbm_ref[...]
    oml = oml_ref[...]
    c = CHUNK
    st_ref[...] = jnp.zeros((GW, GW), F32)

    n = HSB * c
    nlev = mask_ref.shape[0] - 1
    odd =(lax.broadcasted_iota(jnp.int32, (n, GW), 0) & 1) == 1

    def superblock(bi, _):
        r0 = pl.multiple_of(bi * n, n)
        rows = pl.ds(r0, n)
        hf = hf_ref[0, rows, :]
        sg = 1.0 / (1.0 + jnp.exp(-hf))
        f = lbm + oml * sg
        g = jnp.log(f)
        kk = oml * (1.0 - sg)
        q = hq_ref[0, rows, :].astype(F32)
        v = hi_ref[0, rows, :]
        gh = g.astype(BF16)
        gl = (g - gh.astype(F32)).astype(BF16)
        q2 = (q * jnp.where(odd, f, 1.0)).astype(BF16)
        k2 = (kk * jnp.where(odd, 1.0 / f, 1.0)).astype(BF16)

        def scores(qf, kf, l):
            kx = jnp.concatenate([kf] * NH, axis=0) * hm
            return _dot_nt(qf, kx) * mask_ref[l]

        sls = [slice(ci * c, (ci + 1) * c) for ci in range(HSB)]
        exs = [_dot(a_all, jnp.concatenate([gh[sl], gl[sl]], axis=0)) for sl in sls]

        st = st_ref[...]
        o_inter = []
        for ci, sl in enumerate(sls):
            eb = exs[ci][nlev * c:(nlev + 1) * c]
            er = exs[ci][(nlev + 1) * c:(nlev + 2) * c]
            qd = (q[sl] * jnp.exp(eb)).astype(BF16)
            o_inter.append(_dot_nt(qd, st.astype(BF16) * bd))
            upd = _dot_tn(v[sl], (kk[sl] * jnp.exp(er)).astype(BF16))
            st = st * jnp.exp(eb[c - 1:c, :]) + upd
        st_ref[...] = st

        ps = [scores(q2[sl], k2[sl], nlev) for sl in sls]
        qb = q.astype(BF16)
        kb = kk.astype(BF16)
        for l in range(nlev):
            for ci, sl in enumerate(sls):
                x = jnp.exp(exs[ci][l * c:(l + 1) * c]).astype(BF16)
                ps[ci] = ps[ci] + scores(qb[sl] * x, kb[sl] * x, l)
        outs = []
        for ci, sl in enumerate(sls):
            vx = jnp.concatenate([v[sl]] * NH, axis=0) * hm
            outs.append(o_inter[ci] + _dot(ps[ci].astype(BF16), vx))

        o = _head_rms(jnp.concatenate(outs, axis=0), gout_ref[...], bd)
        o_ref[0, rows, :] = (o * hg_ref[0, rows, :].astype(F32)).astype(BF16)
        return 0

    lax.fori_loop(0, seq // n, superblock, 0)


def _hgrn(p3, f3, lbm, oml, gout, bd, a_all, masks, hm):
    b, s, _ = p3.shape
    assert s % (HSB * CHUNK) == 0
    kern = functools.partial(_hgrn_kernel, seq=s)
    c2 = lambda i: (0, 0)
    return pl.pallas_call(
        kern,
        grid=(b,),
        in_specs=[
            pl.BlockSpec((1, s, GW), lambda i: (i, 0, P_HQ)),
            pl.BlockSpec((1, s, GW), lambda i: (i, 0, P_HI)),
            pl.BlockSpec((1, s, GW), lambda i: (i, 0, P_HG)),
            pl.BlockSpec((1, s, GW), lambda i: (i, 0, F_HF)),
            pl.BlockSpec((1, GW), c2),
            pl.BlockSpec((1, GW), c2),
            pl.BlockSpec((1, GW), c2),
            pl.BlockSpec((GW, GW), c2),
            pl.BlockSpec(a_all.shape, c2),
            pl.BlockSpec(masks.shape, lambda i: (0, 0, 0)),
            pl.BlockSpec(hm.shape, c2),
        ],
        out_specs=pl.BlockSpec((1, s, GW), lambda i: (i, 0, 0)),
        out_shape=jax.ShapeDtypeStruct((b, s, GW), BF16),
        scratch_shapes=[pltpu.VMEM((GW, GW), F32)],
        compiler_params=pltpu.CompilerParams(
            dimension_semantics=("arbitrary",), vmem_limit_bytes=VMEM_LIMIT),
    )(p3, p3, p3, f3, lbm, oml, gout, bd, a_all, masks, hm)


def _pm_kernel(pv_ref, pg_ref, mq_ref, mg_ref, mem_ref, mng_ref, wkv_ref, gmk_ref, bd_ref,
               wp_ref, ps_ref, win_ref, d_ref, e_ref, ubuf, wbuf, kmem, vxm, *, seq, tq):
    bd = bd_ref[...]
    hmask = [bd[h * HD:h * HD + 1, :] for h in range(NH)]
    hmask_f = [m.astype(F32) for m in hmask]
    halo = 16
    mem = mem_ref[0]
    ms = jnp.mean(mem * mem, axis=-1, keepdims=True)
    mn = (mem * lax.rsqrt(ms + EPS) * mng_ref[...]).astype(BF16)
    kv = _dot(mn, wkv_ref[...])
    kn = _head_rms(kv[:, :GW], gmk_ref[...], bd).astype(BF16)
    vv = kv[:, GW:].astype(BF16)
    nm = kn.shape[0]
    kmem[...] = kn
    for h in range(NH):
        vxm[h * nm:(h + 1) * nm, :] = vv * hmask[h]

    ubuf[0:halo, :] = jnp.zeros((halo, GW), F32)
    ubuf[halo:halo + seq, :] = pv_ref[0]
    win = win_ref[...]
    inv_win = 1.0 / win
    for k in range(3):
        sh = 1 << k
        wbuf[k, 0:halo, :] = jnp.zeros((halo, GW), F32)
        for b in range(seq // tq):
            base = halo + b * tq
            if k == 0:
                wbuf[k, base:base + tq, :] = ubuf[base:base + tq, :] + ubuf[base - sh:base - sh + tq, :]
            else:
                wbuf[k, base:base + tq, :] = (wbuf[k - 1, base:base + tq, :]
                                              + wbuf[k - 1, base - sh:base - sh + tq, :])

    for b in range(seq // tq):
        r = slice(b * tq, (b + 1) * tq)
        base = halo + b * tq
        u = ubuf[base:base + tq, :]
        s2 = wbuf[0, base:base + tq, :]
        s4 = wbuf[1, base:base + tq, :]
        s8 = wbuf[2, base:base + tq, :]
        s16 = s8 + wbuf[2, base - 8:base - 8 + tq, :]
        sw = jnp.where(win == 2.0, s2, jnp.where(win == 4.0, s4, jnp.where(win == 8.0, s8, s16)))
        if b == 0:
            pos = (lax.broadcasted_iota(jnp.int32, (tq, GW), 0) + 1).astype(F32)
            pooled = sw / jnp.minimum(pos, win)
        else:
            pooled = sw * inv_win
        y = _dot((pooled - u).astype(BF16), wp_ref[...]) * ps_ref[...]
        d_ref[0, r, :] = (y * pg_ref[0, r, :].astype(F32)).astype(BF16)

        qn = mq_ref[0, r, :]
        s_all = _dot_nt(jnp.concatenate([qn * m for m in hmask], axis=0), kmem[...])
        ss = [s_all[h * tq:(h + 1) * tq] for h in range(NH)]
        ps = [jnp.exp(s - jnp.max(s, axis=-1, keepdims=True)) for s in ss]
        ls = [jnp.sum(p, axis=-1, keepdims=True) for p in ps]
        inv_l = (1.0 / ls[0]) * hmask_f[0]
        for h in range(1, NH):
            inv_l = inv_l + (1.0 / ls[h]) * hmask_f[h]
        oe = _dot(jnp.concatenate([p.astype(BF16) for p in ps], axis=1), vxm[...]) * inv_l
        e_ref[0, r, :] = (oe * mg_ref[0, r, :].astype(F32)).astype(BF16)


def _pm(p3, f3, mem, mng, wkv, gmk, bd, wp, ps, win, tq):
    b, s, _ = p3.shape
    nm = mem.shape[1]
    kern = functools.partial(_pm_kernel, seq=s, tq=tq)
    c2 = lambda i: (0, 0)
    return pl.pallas_call(
        kern,
        grid=(b,),
        in_specs=[
            pl.BlockSpec((1, s, GW), lambda i: (i, 0, F_PV)),
            pl.BlockSpec((1, s, GW), lambda i: (i, 0, P_PG)),
            pl.BlockSpec((1, s, GW), lambda i: (i, 0, P_MQ)),
            pl.BlockSpec((1, s, GW), lambda i: (i, 0, P_MG)),
            pl.BlockSpec((1, nm, D_MODEL), lambda i: (i, 0, 0)),
            pl.BlockSpec((1, D_MODEL), c2),
            pl.BlockSpec((D_MODEL, 2 * GW), c2),
            pl.BlockSpec((1, GW), c2),
            pl.BlockSpec((GW, GW), c2),
            pl.BlockSpec((GW, GW), c2),
            pl.BlockSpec((1, GW), c2),
            pl.BlockSpec((1, GW), c2),
        ],
        out_specs=[
            pl.BlockSpec((1, s, GW), lambda i: (i, 0, 0)),
            pl.BlockSpec((1, s, GW), lambda i: (i, 0, 0)),
        ],
        out_shape=[
            jax.ShapeDtypeStruct((b, s, GW), BF16),
            jax.ShapeDtypeStruct((b, s, GW), BF16),
        ],
        scratch_shapes=[
            pltpu.VMEM((s + 16, GW), F32),
            pltpu.VMEM((3, s + 16, GW), F32),
            pltpu.VMEM((nm, GW), BF16),
            pltpu.VMEM((NH * nm, GW), BF16),
        ],
        compiler_params=pltpu.CompilerParams(
            dimension_semantics=("arbitrary",), vmem_limit_bytes=VMEM_LIMIT),
    )(f3, p3, p3, p3, mem, mng, wkv, gmk, bd, wp, ps, win)


def _out_kernel(a_ref, b_ref, c_ref, d_ref, e_ref, w_ref, x_ref, o_ref):
    mixed = jnp.concatenate([a_ref[...], b_ref[...], c_ref[...], d_ref[...], e_ref[...]], axis=1)
    o_ref[...] = x_ref[...] + _dot(mixed, w_ref[...])


def _out(parts, w, x2, tm):
    m = x2.shape[0]
    gspec = pl.BlockSpec((tm, GW), lambda i: (i, 0))
    return pl.pallas_call(
        _out_kernel,
        grid=(m // tm,),
        in_specs=[gspec] * 5 + [
            pl.BlockSpec((5 * GW, D_MODEL), lambda i: (0, 0)),
            pl.BlockSpec((tm, D_MODEL), lambda i: (i, 0)),
        ],
        out_specs=pl.BlockSpec((tm, D_MODEL), lambda i: (i, 0)),
        out_shape=jax.ShapeDtypeStruct((m, D_MODEL), F32),
        compiler_params=pltpu.CompilerParams(
            dimension_semantics=("arbitrary",), vmem_limit_bytes=VMEM_LIMIT),
    )(*parts, w, x2)


def _out_proj_kernel(a_ref, b_ref, c_ref, d_ref, e_ref, w_ref, x_ref, g_ref, wb_ref, wf_ref, hgain_ref, bd_ref,
                     o_ref, pb_ref, pf_ref):
    _out_kernel(a_ref, b_ref, c_ref, d_ref, e_ref, w_ref, x_ref, o_ref)
    _proj_kernel(o_ref, g_ref, wb_ref, wf_ref, hgain_ref, bd_ref, pb_ref, pf_ref)


def _out_proj(parts, w, x2, g, wb, wf, hgain, bd, tm):
    m = x2.shape[0]
    gspec = pl.BlockSpec((tm, GW), lambda i: (i, 0))
    c2 = lambda i: (0, 0)
    return pl.pallas_call(
        _out_proj_kernel,
        grid=(m // tm,),
        in_specs=[gspec] * 5 + [
            pl.BlockSpec((5 * GW, D_MODEL), c2),
            pl.BlockSpec((tm, D_MODEL), lambda i: (i, 0)),
            pl.BlockSpec((1, D_MODEL), c2),
            pl.BlockSpec((D_MODEL, NP_COLS), c2),
            pl.BlockSpec((D_MODEL, NF_COLS), c2),
            pl.BlockSpec((8, GW), c2),
            pl.BlockSpec((GW, GW), c2),
        ],
        out_specs=[
            pl.BlockSpec((tm, D_MODEL), lambda i: (i, 0)),
            pl.BlockSpec((tm, NP_COLS), lambda i: (i, 0)),
            pl.BlockSpec((tm, NF_COLS), lambda i: (i, 0)),
        ],
        out_shape=[
            jax.ShapeDtypeStruct((m, D_MODEL), F32),
            jax.ShapeDtypeStruct((m, NP_COLS), BF16),
            jax.ShapeDtypeStruct((m, NF_COLS), F32),
        ],
        compiler_params=pltpu.CompilerParams(
            dimension_semantics=("arbitrary",), vmem_limit_bytes=VMEM_LIMIT),
    )(*parts, w, x2, g, wb, wf, hgain, bd)


def _tile_heads(g):
    return jnp.tile(g.astype(F32), NH).reshape(1, GW)


def kernel(x, mem, norm_g, w_in, fox_f_bias, fox_q_norm, fox_k_norm, hgrn_lb_logits, hgrn_out_norm,
           pool_w, pool_scale, mem_norm_g, mem_w_kv, mem_q_norm, mem_k_norm, w_out):
    bsz, seq, _ = x.shape
    depth = w_in.shape[0]
    m = bsz * seq
    tq = 256
    tm = 512

    pr = jax.nn.softmax(hgrn_lb_logits.astype(F32), axis=0)
    lower_bounds = jnp.clip(jnp.cumsum(pr, axis=0) - pr[0:1], 0.0, 1.0 - 1e-6)

    bd_np = _bd_ones()
    bd = jnp.asarray(bd_np, BF16)
    tri = jnp.asarray(np.triu(np.ones((tq, tq), np.float32)), BF16)

    def suffix_and_ones(w):
        jj = np.arange(w)
        suffix = (jj[:, None] > jj[None, :]).astype(np.float32)
        return jnp.asarray(np.concatenate([suffix, np.ones((w, SB_CW), np.float32)], axis=1), BF16)

    to = suffix_and_ones(SB_CW)
    tof = suffix_and_ones(SB_TQ)
    a_np, masks_np, hm_np = _hgrn_constants()
    a_all = jnp.asarray(a_np, BF16)
    masks = jnp.asarray(masks_np, F32)
    hm = jnp.asarray(hm_np, BF16)
    win = jnp.asarray(np.repeat(np.array(POOL_WINDOWS, np.float32), HD).reshape(1, GW))

    g = GW
    wb_all, wf_all = _regroup_w_in(w_in)
    x2 = x.reshape(m, D_MODEL)
    norm_gs = [norm_g[l].reshape(1, D_MODEL).astype(F32) for l in range(depth)]
    head_gains = [jnp.concatenate([_tile_heads(fox_q_norm[l]), _tile_heads(fox_k_norm[l]),
                                   _tile_heads(mem_q_norm[l]), jnp.zeros((5, GW), F32)], axis=0)
                  for l in range(depth)]
    pb, pf = _proj(x2, norm_gs[0], wb_all[0], wf_all[0], head_gains[0], bd, tm)
    for l in range(depth):
        p3 = pb.reshape(bsz, seq, NP_COLS)
        f3 = pf.reshape(bsz, seq, NF_COLS)

        bias = jnp.broadcast_to(jnp.pad(fox_f_bias[l].astype(F32), (0, 8 - NH))[:, None], (8, tq))
        out_a = _fox(p3, f3, bias, _tile_heads(fox_q_norm[l]), _tile_heads(fox_k_norm[l]), bd, tri)
        out_b = _sb(p3, to, tof, bd)

        lb = lower_bounds[l].reshape(1, g)
        out_c = _hgrn(p3, f3, jnp.maximum(lb, LB_FLOOR), 1.0 - lb,
                      hgrn_out_norm[l].reshape(1, g).astype(F32), bd, a_all, masks, hm)

        wp = jax.scipy.linalg.block_diag(*[pool_w[l, i] for i in range(len(POOL_WINDOWS))]).astype(BF16)
        out_d, out_e = _pm(p3, f3, mem, mem_norm_g[l].reshape(1, D_MODEL).astype(F32),
                           mem_w_kv[l].astype(BF16), _tile_heads(mem_k_norm[l]),
                           bd, wp, pool_scale[l].reshape(1, g).astype(F32), win, tq)

        parts = [o.reshape(m, g) for o in (out_a, out_b, out_c, out_d, out_e)]
        if l + 1 < depth:
            x2, pb, pf = _out_proj(parts, w_out[l].astype(BF16), x2, norm_gs[l + 1],
                                   wb_all[l + 1], wf_all[l + 1], head_gains[l + 1], bd, tm)
        else:
            x2 = _out(parts, w_out[l].astype(BF16), x2, tm)
    return x2.reshape(bsz, seq, D_MODEL)
```

```python
import functools

import numpy as np
import jax
import jax.numpy as jnp
from jax import lax
from jax.experimental import pallas as pl
from jax.experimental.pallas import tpu as pltpu

F32 = jnp.float32
BF16 = jnp.bfloat16

D_MODEL = 1024
GW = 256
NH = 4
HD = 64
CHUNK = 64
POOL_WINDOWS = (2, 4, 8, 16)
EPS = 1e-6
NEG_BIG = -1e30
LB_FLOOR = 1e-30
SCALE = HD ** -0.5

NP_COLS = 14 * GW
P_FQ, P_FK, P_FV, P_FG, P_SQ, P_SK, P_SV, P_SG, P_HQ, P_HI, P_HG, P_PG, P_MQ, P_MG = range(14)
NF_COLS = 2 * GW + 128
F_HF, F_PV, F_FF = range(3)
VMEM_LIMIT = 56 * 1024 * 1024

TQ = 128
WIDE = 3 * TQ
EXP_ZERO = -104.0
NORM_SLACK = 1.01
LOGIT_SLACK = 0.05
FOX_UNROLL = 2
HSB = 8
SB_TQ = 128
SB_WIDE = 384
SB_CW = 128
SB_UNROLL = 2
SOFTPLUS_LINEAR = 80.0


def _dot(a, b):
    return jnp.dot(a, b, preferred_element_type=F32)


def _dot_nt(a, b):
    return lax.dot_general(a, b, (((1,), (1,)), ((), ())), preferred_element_type=F32)


def _dot_tn(a, b):
    return lax.dot_general(a, b, (((0,), (0,)), ((), ())), preferred_element_type=F32)


def _aligned(x, m):
    return x if isinstance(x, int) else pl.multiple_of(x, m)


def _split_bf16(x, n):
    parts = []
    r = x
    for i in range(n):
        p = r.astype(BF16)
        parts.append(p)
        if i + 1 < n:
            r = r - p.astype(F32)
    return parts


def _dot_f32_rhs01(x, m01, n=3):
    acc = None
    for p in _split_bf16(x, n):
        t = _dot(p, m01)
        acc = t if acc is None else acc + t
    return acc


def _silu(x):
    return x / (1.0 + jnp.exp(-x))


def _log_sigmoid(x):
    return jnp.minimum(x, 0.0) - jnp.log(1.0 + jnp.exp(-jnp.abs(x)))


def _head_rms(x, gain, bd):
    ss = _dot((x * x).astype(BF16), bd)
    return x * lax.rsqrt(ss * (1.0 / HD) + EPS) * gain


def _bd_ones():
    h = np.arange(GW) // HD
    return (h[:, None] == h[None, :]).astype(np.float32)


def _hgrn_constants():
    c = CHUNK
    t = np.arange(c)
    j = np.arange(c)[None, :]
    mats = []
    masks = []
    for n in (64, 32, 16, 8, 4):
        blk, pos = t // n, t % n
        ref = blk * n + n // 2 - 1
        aq = (pos[:, None] >= n // 2) & (j > ref[:, None]) & (j <= t[:, None])
        ak = (pos[:, None] < n // 2) & (j > t[:, None]) & (j <= ref[:, None])
        mats.append((aq | ak).astype(np.float32))
        m = (blk[:, None] == blk[None, :]) & (pos[:, None] >= n // 2) & (pos[None, :] < n // 2)
        masks.append(m.astype(np.float32))
    m = ((t[:, None] // 2) == (t[None, :] // 2)) & (t[None, :] <= t[:, None])
    masks.append(m.astype(np.float32))
    mats.append((j <= t[:, None]).astype(np.float32))
    mats.append((j > t[:, None]).astype(np.float32))
    a_all = np.concatenate(mats, axis=0)
    a_all = np.concatenate([a_all, a_all], axis=1)
    masks = np.stack([np.tile(m, (1, NH)) for m in masks])
    total = masks[:, :, :c].sum(0)
    assert np.array_equal(total, np.tril(np.ones((c, c), np.float32)))
    hm = (np.arange(NH * c)[:, None] // c == np.arange(GW)[None, :] // HD).astype(np.float32)
    return a_all, masks, hm


def _regroup_kernel(w_ref, wb_ref, wf_ref):
    g = GW
    o_ff, o_sb, o_hg, o_pl, o_mm = 4 * g, 4 * g + NH, 8 * g + NH, 12 * g + NH, 14 * g + NH
    w = w_ref[0]
    bf_src = {P_FQ: 0, P_FK: g, P_FV: 2 * g, P_FG: 3 * g,
              P_SQ: o_sb, P_SK: o_sb + g, P_SV: o_sb + 2 * g, P_SG: o_sb + 3 * g,
              P_HQ: o_hg, P_HI: o_hg + 2 * g, P_HG: o_hg + 3 * g,
              P_PG: o_pl + g, P_MQ: o_mm, P_MG: o_mm + g}
    f_src = {F_HF: o_hg + g, F_PV: o_pl}
    for dst, c0 in bf_src.items():
        wb_ref[0, :, dst * g:(dst + 1) * g] = w[:, c0:c0 + g].astype(BF16)
    for dst, c0 in f_src.items():
        wf_ref[0, :, dst * g:(dst + 1) * g] = w[:, c0:c0 + g].astype(BF16)
    ff = jnp.concatenate([w[:, o_ff:o_ff + NH], jnp.zeros((w.shape[0], 128 - NH), F32)], axis=1)
    wf_ref[0, :, F_FF * g:F_FF * g + 128] = ff.astype(BF16)


def _regroup_w_in(w_in):
    depth, d, n = w_in.shape
    tr = 128
    return pl.pallas_call(
        _regroup_kernel,
        grid=(depth, d // tr),
        in_specs=[pl.BlockSpec((1, tr, n), lambda l, i: (l, i, 0))],
        out_specs=[
            pl.BlockSpec((1, tr, NP_COLS), lambda l, i: (l, i, 0)),
            pl.BlockSpec((1, tr, NF_COLS), lambda l, i: (l, i, 0)),
        ],
        out_shape=[
            jax.ShapeDtypeStruct((depth, d, NP_COLS), BF16),
            jax.ShapeDtypeStruct((depth, d, NF_COLS), BF16),
        ],
        compiler_params=pltpu.CompilerParams(
            dimension_semantics=("arbitrary", "arbitrary"), vmem_limit_bytes=VMEM_LIMIT),
    )(w_in)


def _proj_kernel(x_ref, g_ref, wb_ref, wf_ref, hgain_ref, bd_ref, pb_ref, pf_ref):
    x = x_ref[...]
    ms = jnp.mean(x * x, axis=-1, keepdims=True)
    h = (x * lax.rsqrt(ms + EPS) * g_ref[...]).astype(BF16)
    gains = hgain_ref[...]
    bd = bd_ref[...]
    normed = {P_FQ: (0, SCALE), P_FK: (1, 1.0), P_MQ: (2, SCALE)}
    post = {P_SQ: lambda r: r * SCALE}
    post.update({gi: _silu for gi in (P_FG, P_SG, P_HQ, P_HG, P_PG, P_MG)})

    def cols(gi):
        return slice(gi * GW, (gi + 1) * GW)

    raw = {gi: _dot(h, wb_ref[:, cols(gi)]) for gi in normed}
    for gi in range(NP_COLS // GW):
        if gi not in normed:
            r = _dot(h, wb_ref[:, cols(gi)])
            pb_ref[:, cols(gi)] = post.get(gi, lambda r: r)(r).astype(BF16)
    pf_ref[...] = _dot(h, wf_ref[...])
    for gi, (row, scale) in normed.items():
        pb_ref[:, cols(gi)] = (_head_rms(raw[gi], gains[row:row + 1], bd) * scale).astype(BF16)


def _proj(x2, g, wb, wf, hgain, bd, tm):
    m = x2.shape[0]
    return pl.pallas_call(
        _proj_kernel,
        grid=(m // tm,),
        in_specs=[
            pl.BlockSpec((tm, D_MODEL), lambda i: (i, 0)),
            pl.BlockSpec((1, D_MODEL), lambda i: (0, 0)),
            pl.BlockSpec((D_MODEL, NP_COLS), lambda i: (0, 0)),
            pl.BlockSpec((D_MODEL, NF_COLS), lambda i: (0, 0)),
            pl.BlockSpec((8, GW), lambda i: (0, 0)),
            pl.BlockSpec((GW, GW), lambda i: (0, 0)),
        ],
        out_specs=[
            pl.BlockSpec((tm, NP_COLS), lambda i: (i, 0)),
            pl.BlockSpec((tm, NF_COLS), lambda i: (i, 0)),
        ],
        out_shape=[
            jax.ShapeDtypeStruct((m, NP_COLS), BF16),
            jax.ShapeDtypeStruct((m, NF_COLS), F32),
        ],
        compiler_params=pltpu.CompilerParams(
            dimension_semantics=("arbitrary",), vmem_limit_bytes=VMEM_LIMIT),
    )(x2, g, wb, wf, hgain, bd)


def _fox_kernel(p_ref, ff_ref, bias_ref, gq_ref, gk_ref, bd_ref, tri_ref, o_ref,
                vx, crow, qk_buf, worst_ref, *, seq):
    bd = bd_ref[...]
    tri = tri_ref[...]
    pb = tri.shape[0]
    hmask = [bd[h * HD:h * HD + 1, :] for h in range(NH)]
    hmask_f = [m.astype(F32) for m in hmask]
    qk_max = (HD * SCALE * NORM_SLACK) * (jnp.max(jnp.abs(gq_ref[...]), axis=-1, keepdims=True)
                                          * jnp.max(jnp.abs(gk_ref[...]), axis=-1, keepdims=True))
    carry = jnp.zeros((8, 1), F32)
    for b in range(seq // pb):
        r = slice(b * pb, (b + 1) * pb)
        lf = _log_sigmoid(ff_ref[0, r, :].T[:8, :] + bias_ref[...])
        cb = _dot_f32_rhs01(lf, tri, 3) + carry
        carry = cb[:, pb - 1:pb]
        crow[:, r] = cb
        v = p_ref[0, r, 2 * GW:3 * GW]
        for h in range(NH):
            vx[h, r, :] = v * hmask[h]

    rel_w = (lax.broadcasted_iota(jnp.int32, (TQ, WIDE), 1) - lax.broadcasted_iota(jnp.int32, (TQ, WIDE), 0))
    nfull = (WIDE - TQ) // TQ

    def qk_all_heads(r0, start, width):
        qb = p_ref[0, pl.ds(r0, TQ), 0:GW]
        qx = jnp.concatenate([qb * m for m in hmask], axis=0)
        return _dot_nt(qx, p_ref[0, pl.ds(start, width), GW:2 * GW])

    def wide_start(r0):
        if isinstance(r0, int):
            return max(r0 - (WIDE - TQ), 0)
        return pl.multiple_of(jnp.maximum(r0 - (WIDE - TQ), 0), TQ)

    def wide_qk(i):
        r0 = _aligned(i * TQ, TQ)
        return qk_all_heads(r0, wide_start(r0), WIDE)

    def q_block(i, full_tile, qk_wide):
        r0 = _aligned(i * TQ, TQ)
        s0 = wide_start(r0)
        rows = pl.ds(r0, TQ)

        def mask_wide(s):
            if full_tile:
                diag = jnp.where(rel_w[:, WIDE - TQ:] <= WIDE - TQ, s[:, WIDE - TQ:], NEG_BIG)
                return jnp.concatenate([s[:, :WIDE - TQ], diag], axis=1)
            return jnp.where(rel_w <= r0 - s0, s, NEG_BIG)

        def tiles(qk, start, width):
            return [qk[h * TQ:(h + 1) * TQ] - crow[h:h + 1, pl.ds(start, width)] for h in range(NH)]

        def weighted_values(ps, start, width):
            p_all = jnp.concatenate([p.astype(BF16) for p in ps], axis=1)
            v_all = jnp.concatenate([vx[h, pl.ds(start, width), :] for h in range(NH)], axis=0)
            return _dot(p_all, v_all)

        def per_head_lanes(cols):
            out = cols[0] * hmask_f[0]
            for h in range(1, NH):
                out = out + cols[h] * hmask_f[h]
            return out

        def weight_bound(s_end, ms):
            last = pl.multiple_of(jnp.maximum(s_end - TQ, 0), TQ)
            bound = None
            for h in range(NH):
                c_last = crow[h:h + 1, pl.ds(last, TQ)][:, TQ - 1:TQ]
                b_h = (qk_max + LOGIT_SLACK) - c_last - jnp.min(ms[h], axis=0, keepdims=True)
                bound = b_h if bound is None else jnp.maximum(bound, b_h)
            return bound

        def keep_going(s_end, ms):
            return (jnp.max(weight_bound(s_end, ms)) > EXP_ZERO).astype(jnp.int32)

        ss = [mask_wide(s) for s in tiles(qk_wide, s0, WIDE)]
        ms = [jnp.max(s, axis=-1, keepdims=True) for s in ss]
        ps = [jnp.exp(s - m) for s, m in zip(ss, ms)]
        ls = [jnp.sum(p, axis=-1, keepdims=True) for p in ps]
        acc = weighted_values(ps, s0, WIDE)

        def emit(acc, ls):
            g = p_ref[0, rows, 3 * GW:4 * GW].astype(F32)
            o_ref[0, rows, :] = (acc * per_head_lanes([1.0 / l for l in ls]) * g).astype(BF16)

        emit(acc, ls)
        if full_tile:
            worst_ref[...] = jnp.maximum(worst_ref[...], jnp.where(s0 > 0, weight_bound(s0, ms), NEG_BIG))

        def cond(st):
            return jnp.logical_and(st[0] > 0, st[1] > 0)

        def body(st):
            s_end, _, ms, ls, acc = st
            sb = pl.multiple_of(s_end - TQ, TQ)
            ss = tiles(qk_all_heads(r0, sb, TQ), sb, TQ)
            ms2 = [jnp.maximum(m, jnp.max(s, axis=-1, keepdims=True)) for s, m in zip(ss, ms)]
            ps = [jnp.exp(s - m) for s, m in zip(ss, ms2)]
            alphas = [jnp.exp(m - m2) for m, m2 in zip(ms, ms2)]
            ls2 = [a * l + jnp.sum(p, axis=-1, keepdims=True) for a, l, p in zip(alphas, ls, ps)]
            acc2 = acc * per_head_lanes(alphas) + weighted_values(ps, sb, TQ)
            return sb, keep_going(sb, ms2), tuple(ms2), tuple(ls2), acc2

        def rare_tail():
            @pl.when(jnp.logical_and(s0 > 0, keep_going(s0, ms) > 0))
            def _():
                _, _, _, ls_f, acc_f = lax.while_loop(cond, body, (s0, jnp.int32(1), tuple(ms), tuple(ls), acc))
                emit(acc_f, ls_f)

        return rare_tail

    nblk = seq // TQ
    grp = FOX_UNROLL
    lead = nfull + (nblk - nfull) % (2 * grp)
    worst_ref[...] = jnp.full(worst_ref.shape, NEG_BIG, F32)
    for i in range(lead):
        q_block(i, i >= nfull, wide_qk(i))
    for k in range(grp):
        qk_buf[k] = wide_qk(lead + k)

    def group(first, cur, nxt):
        for k in range(grp):
            qk_buf[nxt * grp + k] = wide_qk(jnp.minimum(first + grp + k, nblk - 1))
        for k in range(grp):
            q_block(first + k, True, qk_buf[cur * grp + k])

    def two_groups(j, _):
        i = lead + 2 * grp * j
        group(i, 0, 1)
        group(i + grp, 1, 0)
        return 0

    lax.fori_loop(0, (nblk - lead) // (2 * grp), two_groups, 0)

    @pl.when(jnp.max(worst_ref[...]) > EXP_ZERO)
    def _():
        def redo(i, _):
            q_block(i, True, wide_qk(i))()
            return 0

        lax.fori_loop(nfull, nblk, redo, 0)


def _fox(p3, f3, bias, gq, gk, bd, tri):
    b, s, _ = p3.shape
    assert s % tri.shape[0] == 0 and s >= WIDE + 2 * FOX_UNROLL * TQ
    kern = functools.partial(_fox_kernel, seq=s)
    c2 = lambda i: (0, 0)
    return pl.pallas_call(
        kern,
        grid=(b,),
        in_specs=[
            pl.BlockSpec((1, s, 4 * GW), lambda i: (i, 0, P_FQ // 4)),
            pl.BlockSpec((1, s, 128), lambda i: (i, 0, F_FF * GW // 128)),
            pl.BlockSpec((8, tri.shape[0]), c2),
            pl.BlockSpec((1, GW), c2),
            pl.BlockSpec((1, GW), c2),
            pl.BlockSpec((GW, GW), c2),
            pl.BlockSpec(tri.shape, c2),
        ],
        out_specs=pl.BlockSpec((1, s, GW), lambda i: (i, 0, 0)),
        out_shape=jax.ShapeDtypeStruct((b, s, GW), BF16),
        scratch_shapes=[
            pltpu.VMEM((NH, s, GW), BF16),
            pltpu.VMEM((8, s), F32),
            pltpu.VMEM((2 * FOX_UNROLL, NH * TQ, WIDE), F32),
            pltpu.VMEM((8, 128), F32),
        ],
        compiler_params=pltpu.CompilerParams(
            dimension_semantics=("arbitrary",), vmem_limit_bytes=VMEM_LIMIT),
    )(p3, f3, bias, gq, gk, bd, tri)


def _sb_kernel(p_ref, to_ref, tof_ref, bd_ref, o_ref, vx, qk_buf, worst_ref, *, seq):
    to = to_ref[...]
    tq, wide, cw = SB_TQ, SB_WIDE, SB_CW
    hmask = [bd_ref[h * HD:h * HD + 1, :] for h in range(NH)]
    pb = 256
    for b in range(seq // pb):
        r = slice(b * pb, (b + 1) * pb)
        v = p_ref[0, r, 2 * GW:3 * GW]
        for h in range(NH):
            vx[h, r, :] = v * hmask[h]

    rel_w = (lax.broadcasted_iota(jnp.int32, (tq, wide), 1) - lax.broadcasted_iota(jnp.int32, (tq, wide), 0))
    nsub = wide // cw
    nfull = -(-(wide - tq) // tq)

    def log_one_minus_sigmoid(z):
        return -jnp.maximum(z, jnp.log(1.0 + jnp.exp(jnp.minimum(z, SOFTPLUS_LINEAR))))

    def suffix_sums(lom, mat):
        w = lom.shape[1]
        cs = _dot(lom.astype(BF16), mat)
        return cs[:, :w], cs[:, w:]

    def qk_all_heads(r0, start, width):
        qb = p_ref[0, pl.ds(r0, tq), 0:GW]
        qx = jnp.concatenate([qb * m for m in hmask], axis=0)
        return _dot_nt(qx, p_ref[0, pl.ds(start, width), GW:2 * GW])

    def wide_start(r0):
        if isinstance(r0, int):
            return max(r0 - (wide - tq), 0)
        return pl.multiple_of(jnp.maximum(r0 - (wide - tq), 0), tq)

    def wide_qk(i):
        r0 = _aligned(i * tq, tq)
        return qk_all_heads(r0, wide_start(r0), wide)

    def q_block(i, full_tile, qk_wide):
        r0 = _aligned(i * tq, tq)
        s0 = wide_start(r0)
        rows = pl.ds(r0, tq)

        def mask_wide(x):
            if full_tile:
                diag = jnp.where(rel_w[:, wide - cw:] < wide - tq, x[:, wide - cw:], 0.0)
                return jnp.concatenate([x[:, :wide - cw], diag], axis=1)
            return jnp.where(rel_w < r0 - s0, x, 0.0)

        def per_head(qk):
            return [qk[h * tq:(h + 1) * tq] for h in range(NH)]

        def weighted_values(ws, start, width):
            w_all = jnp.concatenate([w.astype(BF16) for w in ws], axis=1)
            v_all = jnp.concatenate([vx[h, pl.ds(start, width), :] for h in range(NH)], axis=0)
            return _dot(w_all, v_all)

        def emit(acc):
            g = p_ref[0, rows, 3 * GW:4 * GW].astype(F32)
            o_ref[0, rows, :] = (acc * g).astype(BF16)

        zs = per_head(qk_wide)
        loms = [log_one_minus_sigmoid(z) for z in zs]
        log_betas = [z + lom for z, lom in zip(zs, loms)]
        sums = [[suffix_sums(mask_wide(lom)[:, c * cw:(c + 1) * cw], to) for c in range(nsub)] for lom in loms]
        carries, ws = [], []
        for h in range(NH):
            between = [None] * nsub
            carry = jnp.zeros((tq, cw), F32)
            for c in reversed(range(nsub)):
                rc, tot = sums[h][c]
                between[c] = rc + carry
                carry = carry + tot
            ws.append(mask_wide(jnp.exp(log_betas[h] + jnp.concatenate(between, axis=1))))
            carries.append(carry)
        acc = weighted_values(ws, s0, wide)
        emit(acc)

        def worst_carry(carries):
            cm = jnp.maximum(jnp.maximum(carries[0], carries[1]), jnp.maximum(carries[2], carries[3]))
            return jnp.max(cm.reshape(tq // 8, 8, cw), axis=0)

        def keep_going(carries):
            return (jnp.max(worst_carry(carries)) > EXP_ZERO).astype(jnp.int32)

        if full_tile:
            worst_ref[0] = jnp.maximum(worst_ref[0], jnp.where(s0 > 0, worst_carry(carries), NEG_BIG))

        def cond(st):
            return jnp.logical_and(st[0] > 0, st[1] > 0)

        def body(st):
            s_end, _, carries, acc = st
            sb = pl.multiple_of(s_end - tq, tq)
            zs = per_head(qk_all_heads(r0, sb, tq))
            loms = [log_one_minus_sigmoid(z) for z in zs]
            sums = [suffix_sums(lom, tof_ref[...]) for lom in loms]
            ws = [jnp.exp(z + lom + rc + cr[:, :tq]) for z, lom, (rc, _), cr in zip(zs, loms, sums, carries)]
            c2 = [cr + tot for cr, (_, tot) in zip(carries, sums)]
            return sb, keep_going(c2), tuple(c2), acc + weighted_values(ws, sb, tq)

        def rare_tail():
            @pl.when(jnp.logical_and(s0 > 0, keep_going(carries) > 0))
            def _():
                st = lax.while_loop(cond, body, (s0, jnp.int32(1), tuple(carries), acc))
                emit(st[3])

        return rare_tail

    nblk = seq // tq
    grp = SB_UNROLL
    lead = nfull + (nblk - nfull) % (2 * grp)
    worst_ref[0] = jnp.full((8, cw), NEG_BIG, F32)
    for i in range(lead):
        q_block(i, i >= nfull, wide_qk(i))
    for k in range(grp):
        qk_buf[k] = wide_qk(lead + k)

    def group(first, cur, nxt):
        for k in range(grp):
            qk_buf[nxt * grp + k] = wide_qk(jnp.minimum(first + grp + k, nblk - 1))
        for k in range(grp):
            q_block(first + k, True, qk_buf[cur * grp + k])

    def two_groups(j, _):
        i = lead + 2 * grp * j
        group(i, 0, 1)
        group(i + grp, 1, 0)
        return 0

    lax.fori_loop(0, (nblk - lead) // (2 * grp), two_groups, 0)

    @pl.when(jnp.max(worst_ref[0]) > EXP_ZERO)
    def _():
        def redo(i, _):
            q_block(i, True, wide_qk(i))()
            return 0

        lax.fori_loop(nfull, nblk, redo, 0)


def _sb(p3, to, tof, bd):
    b, s, _ = p3.shape
    assert s % 256 == 0 and s >= SB_WIDE + 2 * SB_UNROLL * SB_TQ
    kern = functools.partial(_sb_kernel, seq=s)
    return pl.pallas_call(
        kern,
        grid=(b,),
        in_specs=[
            pl.BlockSpec((1, s, 4 * GW), lambda i: (i, 0, P_SQ // 4)),
            pl.BlockSpec(to.shape, lambda i: (0, 0)),
            pl.BlockSpec(tof.shape, lambda i: (0, 0)),
            pl.BlockSpec((GW, GW), lambda i: (0, 0)),
        ],
        out_specs=pl.BlockSpec((1, s, GW), lambda i: (i, 0, 0)),
        out_shape=jax.ShapeDtypeStruct((b, s, GW), BF16),
        scratch_shapes=[
            pltpu.VMEM((NH, s, GW), BF16),
            pltpu.VMEM((2 * SB_UNROLL, NH * SB_TQ, SB_WIDE), F32),
            pltpu.VMEM((1, 8, SB_CW), F32),
        ],
        compiler_params=pltpu.CompilerParams(
            dimension_semantics=("arbitrary",), vmem_limit_bytes=VMEM_LIMIT),
    )(p3, to, tof, bd)


def _hgrn_kernel(hq_ref, hi_ref, hg_ref, hf_ref, lbm_ref, oml_ref, gout_ref, bd_ref, a_ref,
                 mask_ref, hm_ref, o_ref, st_ref, *, seq):
    bd = bd_ref[...]
    a_all = a_ref[...]
    hm = hm_ref[...]
    lbm = lbm_ref[...]
    oml = oml_ref[...]
    c = CHUNK
    st_ref[...] = jnp.zeros((GW, GW), F32)

    n = HSB * c
    nlev = mask_ref.shape[0] - 1
    odd =(lax.broadcasted_iota(jnp.int32, (n, GW), 0) & 1) == 1

    def superblock(bi, _):
        r0 = pl.multiple_of(bi * n, n)
        rows = pl.ds(r0, n)
        hf = hf_ref[0, rows, :]
        sg = 1.0 / (1.0 + jnp.exp(-hf))
        f = lbm + oml * sg
        g = jnp.log(f)
        kk = oml * (1.0 - sg)
        q = hq_ref[0, rows, :].astype(F32)
        v = hi_ref[0, rows, :]
        gh = g.astype(BF16)
        gl = (g - gh.astype(F32)).astype(BF16)
        q2 = (q * jnp.where(odd, f, 1.0)).astype(BF16)
        k2 = (kk * jnp.where(odd, 1.0 / f, 1.0)).astype(BF16)

        def scores(qf, kf, l):
            kx = jnp.concatenate([kf] * NH, axis=0) * hm
            return _dot_nt(qf, kx) * mask_ref[l]

        sls = [slice(ci * c, (ci + 1) * c) for ci in range(HSB)]
        exs = [_dot(a_all, jnp.concatenate([gh[sl], gl[sl]], axis=0)) for sl in sls]

        st = st_ref[...]
        o_inter = []
        for ci, sl in enumerate(sls):
            eb = exs[ci][nlev * c:(nlev + 1) * c]
            er = exs[ci][(nlev + 1) * c:(nlev + 2) * c]
            qd = (q[sl] * jnp.exp(eb)).astype(BF16)
            o_inter.append(_dot_nt(qd, st.astype(BF16) * bd))
            upd = _dot_tn(v[sl], (kk[sl] * jnp.exp(er)).astype(BF16))
            st = st * jnp.exp(eb[c - 1:c, :]) + upd
        st_ref[...] = st

        ps = [scores(q2[sl], k2[sl], nlev) for sl in sls]
        qb = q.astype(BF16)
        kb = kk.astype(BF16)
        for l in range(nlev):
            for ci, sl in enumerate(sls):
                x = jnp.exp(exs[ci][l * c:(l + 1) * c]).astype(BF16)
                ps[ci] = ps[ci] + scores(qb[sl] * x, kb[sl] * x, l)
        outs = []
        for ci, sl in enumerate(sls):
            vx = jnp.concatenate([v[sl]] * NH, axis=0) * hm
            outs.append(o_inter[ci] + _dot(ps[ci].astype(BF16), vx))

        o = _head_rms(jnp.concatenate(outs, axis=0), gout_ref[...], bd)
        o_ref[0, rows, :] = (o * hg_ref[0, rows, :].astype(F32)).astype(BF16)
        return 0

    lax.fori_loop(0, seq // n, superblock, 0)


def _hgrn(p3, f3, lbm, oml, gout, bd, a_all, masks, hm):
    b, s, _ = p3.shape
    assert s % (HSB * CHUNK) == 0
    kern = functools.partial(_hgrn_kernel, seq=s)
    c2 = lambda i: (0, 0)
    return pl.pallas_call(
        kern,
        grid=(b,),
        in_specs=[
            pl.BlockSpec((1, s, GW), lambda i: (i, 0, P_HQ)),
            pl.BlockSpec((1, s, GW), lambda i: (i, 0, P_HI)),
            pl.BlockSpec((1, s, GW), lambda i: (i, 0, P_HG)),
            pl.BlockSpec((1, s, GW), lambda i: (i, 0, F_HF)),
            pl.BlockSpec((1, GW), c2),
            pl.BlockSpec((1, GW), c2),
            pl.BlockSpec((1, GW), c2),
            pl.BlockSpec((GW, GW), c2),
            pl.BlockSpec(a_all.shape, c2),
            pl.BlockSpec(masks.shape, lambda i: (0, 0, 0)),
            pl.BlockSpec(hm.shape, c2),
        ],
        out_specs=pl.BlockSpec((1, s, GW), lambda i: (i, 0, 0)),
        out_shape=jax.ShapeDtypeStruct((b, s, GW), BF16),
        scratch_shapes=[pltpu.VMEM((GW, GW), F32)],
        compiler_params=pltpu.CompilerParams(
            dimension_semantics=("arbitrary",), vmem_limit_bytes=VMEM_LIMIT),
    )(p3, p3, p3, f3, lbm, oml, gout, bd, a_all, masks, hm)


def _pm_kernel(pv_ref, pg_ref, mq_ref, mg_ref, mem_ref, mng_ref, wkv_ref, gmk_ref, bd_ref,
               wp_ref, ps_ref, win_ref, d_ref, e_ref, ubuf, wbuf, kmem, vxm, *, seq, tq):
    bd = bd_ref[...]
    hmask = [bd[h * HD:h * HD + 1, :] for h in range(NH)]
    hmask_f = [m.astype(F32) for m in hmask]
    halo = 16
    mem = mem_ref[0]
    ms = jnp.mean(mem * mem, axis=-1, keepdims=True)
    mn = (mem * lax.rsqrt(ms + EPS) * mng_ref[...]).astype(BF16)
    kv = _dot(mn, wkv_ref[...])
    kn = _head_rms(kv[:, :GW], gmk_ref[...], bd).astype(BF16)
    vv = kv[:, GW:].astype(BF16)
    nm = kn.shape[0]
    kmem[...] = kn
    for h in range(NH):
        vxm[h * nm:(h + 1) * nm, :] = vv * hmask[h]

    ubuf[0:halo, :] = jnp.zeros((halo, GW), F32)
    ubuf[halo:halo + seq, :] = pv_ref[0]
    win = win_ref[...]
    inv_win = 1.0 / win
    for k in range(3):
        sh = 1 << k
        wbuf[k, 0:halo, :] = jnp.zeros((halo, GW), F32)
        for b in range(seq // tq):
            base = halo + b * tq
            if k == 0:
                wbuf[k, base:base + tq, :] = ubuf[base:base + tq, :] + ubuf[base - sh:base - sh + tq, :]
            else:
                wbuf[k, base:base + tq, :] = (wbuf[k - 1, base:base + tq, :]
                                              + wbuf[k - 1, base - sh:base - sh + tq, :])

    for b in range(seq // tq):
        r = slice(b * tq, (b + 1) * tq)
        base = halo + b * tq
        u = ubuf[base:base + tq, :]
        s2 = wbuf[0, base:base + tq, :]
        s4 = wbuf[1, base:base + tq, :]
        s8 = wbuf[2, base:base + tq, :]
        s16 = s8 + wbuf[2, base - 8:base - 8 + tq, :]
        sw = jnp.where(win == 2.0, s2, jnp.where(win == 4.0, s4, jnp.where(win == 8.0, s8, s16)))
        if b == 0:
            pos = (lax.broadcasted_iota(jnp.int32, (tq, GW), 0) + 1).astype(F32)
            pooled = sw / jnp.minimum(pos, win)
        else:
            pooled = sw * inv_win
        y = _dot((pooled - u).astype(BF16), wp_ref[...]) * ps_ref[...]
        d_ref[0, r, :] = (y * pg_ref[0, r, :].astype(F32)).astype(BF16)

        qn = mq_ref[0, r, :]
        s_all = _dot_nt(jnp.concatenate([qn * m for m in hmask], axis=0), kmem[...])
        ss = [s_all[h * tq:(h + 1) * tq] for h in range(NH)]
        ps = [jnp.exp(s - jnp.max(s, axis=-1, keepdims=True)) for s in ss]
        ls = [jnp.sum(p, axis=-1, keepdims=True) for p in ps]
        inv_l = (1.0 / ls[0]) * hmask_f[0]
        for h in range(1, NH):
            inv_l = inv_l + (1.0 / ls[h]) * hmask_f[h]
        oe = _dot(jnp.concatenate([p.astype(BF16) for p in ps], axis=1), vxm[...]) * inv_l
        e_ref[0, r, :] = (oe * mg_ref[0, r, :].astype(F32)).astype(BF16)


def _pm(p3, f3, mem, mng, wkv, gmk, bd, wp, ps, win, tq):
    b, s, _ = p3.shape
    nm = mem.shape[1]
    kern = functools.partial(_pm_kernel, seq=s, tq=tq)
    c2 = lambda i: (0, 0)
    return pl.pallas_call(
        kern,
        grid=(b,),
        in_specs=[
            pl.BlockSpec((1, s, GW), lambda i: (i, 0, F_PV)),
            pl.BlockSpec((1, s, GW), lambda i: (i, 0, P_PG)),
            pl.BlockSpec((1, s, GW), lambda i: (i, 0, P_MQ)),
            pl.BlockSpec((1, s, GW), lambda i: (i, 0, P_MG)),
            pl.BlockSpec((1, nm, D_MODEL), lambda i: (i, 0, 0)),
            pl.BlockSpec((1, D_MODEL), c2),
            pl.BlockSpec((D_MODEL, 2 * GW), c2),
            pl.BlockSpec((1, GW), c2),
            pl.BlockSpec((GW, GW), c2),
            pl.BlockSpec((GW, GW), c2),
            pl.BlockSpec((1, GW), c2),
            pl.BlockSpec((1, GW), c2),
        ],
        out_specs=[
            pl.BlockSpec((1, s, GW), lambda i: (i, 0, 0)),
            pl.BlockSpec((1, s, GW), lambda i: (i, 0, 0)),
        ],
        out_shape=[
            jax.ShapeDtypeStruct((b, s, GW), BF16),
            jax.ShapeDtypeStruct((b, s, GW), BF16),
        ],
        scratch_shapes=[
            pltpu.VMEM((s + 16, GW), F32),
            pltpu.VMEM((3, s + 16, GW), F32),
            pltpu.VMEM((nm, GW), BF16),
            pltpu.VMEM((NH * nm, GW), BF16),
        ],
        compiler_params=pltpu.CompilerParams(
            dimension_semantics=("arbitrary",), vmem_limit_bytes=VMEM_LIMIT),
    )(f3, p3, p3, p3, mem, mng, wkv, gmk, bd, wp, ps, win)


def _out_kernel(a_ref, b_ref, c_ref, d_ref, e_ref, w_ref, x_ref, o_ref):
    mixed = jnp.concatenate([a_ref[...], b_ref[...], c_ref[...], d_ref[...], e_ref[...]], axis=1)
    o_ref[...] = x_ref[...] + _dot(mixed, w_ref[...])


def _out(parts, w, x2, tm):
    m = x2.shape[0]
    gspec = pl.BlockSpec((tm, GW), lambda i: (i, 0))
    return pl.pallas_call(
        _out_kernel,
        grid=(m // tm,),
        in_specs=[gspec] * 5 + [
            pl.BlockSpec((5 * GW, D_MODEL), lambda i: (0, 0)),
            pl.BlockSpec((tm, D_MODEL), lambda i: (i, 0)),
        ],
        out_specs=pl.BlockSpec((tm, D_MODEL), lambda i: (i, 0)),
        out_shape=jax.ShapeDtypeStruct((m, D_MODEL), F32),
        compiler_params=pltpu.CompilerParams(
            dimension_semantics=("arbitrary",), vmem_limit_bytes=VMEM_LIMIT),
    )(*parts, w, x2)


def _out_proj_kernel(a_ref, b_ref, c_ref, d_ref, e_ref, w_ref, x_ref, g_ref, wb_ref, wf_ref, hgain_ref, bd_ref,
                     o_ref, pb_ref, pf_ref):
    _out_kernel(a_ref, b_ref, c_ref, d_ref, e_ref, w_ref, x_ref, o_ref)
    _proj_kernel(o_ref, g_ref, wb_ref, wf_ref, hgain_ref, bd_ref, pb_ref, pf_ref)


def _out_proj(parts, w, x2, g, wb, wf, hgain, bd, tm):
    m = x2.shape[0]
    gspec = pl.BlockSpec((tm, GW), lambda i: (i, 0))
    c2 = lambda i: (0, 0)
    return pl.pallas_call(
        _out_proj_kernel,
        grid=(m // tm,),
        in_specs=[gspec] * 5 + [
            pl.BlockSpec((5 * GW, D_MODEL), c2),
            pl.BlockSpec((tm, D_MODEL), lambda i: (i, 0)),
            pl.BlockSpec((1, D_MODEL), c2),
            pl.BlockSpec((D_MODEL, NP_COLS), c2),
            pl.BlockSpec((D_MODEL, NF_COLS), c2),
            pl.BlockSpec((8, GW), c2),
            pl.BlockSpec((GW, GW), c2),
        ],
        out_specs=[
            pl.BlockSpec((tm, D_MODEL), lambda i: (i, 0)),
            pl.BlockSpec((tm, NP_COLS), lambda i: (i, 0)),
            pl.BlockSpec((tm, NF_COLS), lambda i: (i, 0)),
        ],
        out_shape=[
            jax.ShapeDtypeStruct((m, D_MODEL), F32),
            jax.ShapeDtypeStruct((m, NP_COLS), BF16),
            jax.ShapeDtypeStruct((m, NF_COLS), F32),
        ],
        compiler_params=pltpu.CompilerParams(
            dimension_semantics=("arbitrary",), vmem_limit_bytes=VMEM_LIMIT),
    )(*parts, w, x2, g, wb, wf, hgain, bd)


def _tile_heads(g):
    return jnp.tile(g.astype(F32), NH).reshape(1, GW)


def kernel(x, mem, norm_g, w_in, fox_f_bias, fox_q_norm, fox_k_norm, hgrn_lb_logits, hgrn_out_norm,
           pool_w, pool_scale, mem_norm_g, mem_w_kv, mem_q_norm, mem_k_norm, w_out):
    bsz, seq, _ = x.shape
    depth = w_in.shape[0]
    m = bsz * seq
    tq = 256
    tm = 512
    tm_out = 1024

    pr = jax.nn.softmax(hgrn_lb_logits.astype(F32), axis=0)
    lower_bounds = jnp.clip(jnp.cumsum(pr, axis=0) - pr[0:1], 0.0, 1.0 - 1e-6)

    bd_np = _bd_ones()
    bd = jnp.asarray(bd_np, BF16)
    tri = jnp.asarray(np.triu(np.ones((tq, tq), np.float32)), BF16)

    def suffix_and_ones(w):
        jj = np.arange(w)
        suffix = (jj[:, None] > jj[None, :]).astype(np.float32)
        return jnp.asarray(np.concatenate([suffix, np.ones((w, SB_CW), np.float32)], axis=1), BF16)

    to = suffix_and_ones(SB_CW)
    tof = suffix_and_ones(SB_TQ)
    a_np, masks_np, hm_np = _hgrn_constants()
    a_all = jnp.asarray(a_np, BF16)
    masks = jnp.asarray(masks_np, F32)
    hm = jnp.asarray(hm_np, BF16)
    win = jnp.asarray(np.repeat(np.array(POOL_WINDOWS, np.float32), HD).reshape(1, GW))

    g = GW
    wb_all, wf_all = _regroup_w_in(w_in)
    x2 = x.reshape(m, D_MODEL)
    norm_gs = [norm_g[l].reshape(1, D_MODEL).astype(F32) for l in range(depth)]
    head_gains = [jnp.concatenate([_tile_heads(fox_q_norm[l]), _tile_heads(fox_k_norm[l]),
                                   _tile_heads(mem_q_norm[l]), jnp.zeros((5, GW), F32)], axis=0)
                  for l in range(depth)]
    pb, pf = _proj(x2, norm_gs[0], wb_all[0], wf_all[0], head_gains[0], bd, tm)
    for l in range(depth):
        p3 = pb.reshape(bsz, seq, NP_COLS)
        f3 = pf.reshape(bsz, seq, NF_COLS)

        bias = jnp.broadcast_to(jnp.pad(fox_f_bias[l].astype(F32), (0, 8 - NH))[:, None], (8, tq))
        out_a = _fox(p3, f3, bias, _tile_heads(fox_q_norm[l]), _tile_heads(fox_k_norm[l]), bd, tri)
        out_b = _sb(p3, to, tof, bd)

        lb = lower_bounds[l].reshape(1, g)
        out_c = _hgrn(p3, f3, jnp.maximum(lb, LB_FLOOR), 1.0 - lb,
                      hgrn_out_norm[l].reshape(1, g).astype(F32), bd, a_all, masks, hm)

        wp = jax.scipy.linalg.block_diag(*[pool_w[l, i] for i in range(len(POOL_WINDOWS))]).astype(BF16)
        out_d, out_e = _pm(p3, f3, mem, mem_norm_g[l].reshape(1, D_MODEL).astype(F32),
                           mem_w_kv[l].astype(BF16), _tile_heads(mem_k_norm[l]),
                           bd, wp, pool_scale[l].reshape(1, g).astype(F32), win, tq)

        parts = [o.reshape(m, g) for o in (out_a, out_b, out_c, out_d, out_e)]
        if l + 1 < depth:
            x2, pb, pf = _out_proj(parts, w_out[l].astype(BF16), x2, norm_gs[l + 1],
                                   wb_all[l + 1], wf_all[l + 1], head_gains[l + 1], bd, tm)
        else:
            x2 = _out(parts, w_out[l].astype(BF16), x2, tm_out)
    return x2.reshape(bsz, seq, D_MODEL)
```

```python
import functools

import numpy as np
import jax
import jax.numpy as jnp
from jax import lax
from jax.experimental import pallas as pl
from jax.experimental.pallas import tpu as pltpu

F32 = jnp.float32
BF16 = jnp.bfloat16

D_MODEL = 1024
GW = 256
NH = 4
HD = 64
CHUNK = 64
POOL_WINDOWS = (2, 4, 8, 16)
EPS = 1e-6
NEG_BIG = -1e30
LB_FLOOR = 1e-30
SCALE = HD ** -0.5

NP_COLS = 14 * GW
P_FQ, P_FK, P_FV, P_FG, P_SQ, P_SK, P_SV, P_SG, P_HQ, P_HI, P_HG, P_PG, P_MQ, P_MG = range(14)
NF_COLS = 2 * GW + 128
F_HF, F_PV, F_FF = range(3)
VMEM_LIMIT = 56 * 1024 * 1024

TQ = 128
WIDE = 3 * TQ
EXP_ZERO = -104.0
NORM_SLACK = 1.01
LOGIT_SLACK = 0.05
FOX_UNROLL = 2
HSB = 8
SB_TQ = 128
SB_WIDE = 384
SB_CW = 128
SB_UNROLL = 2
SOFTPLUS_LINEAR = 80.0


def _dot(a, b):
    return jnp.dot(a, b, preferred_element_type=F32)


def _dot_nt(a, b):
    return lax.dot_general(a, b, (((1,), (1,)), ((), ())), preferred_element_type=F32)


def _dot_tn(a, b):
    return lax.dot_general(a, b, (((0,), (0,)), ((), ())), preferred_element_type=F32)


def _aligned(x, m):
    return x if isinstance(x, int) else pl.multiple_of(x, m)


def _split_bf16(x, n):
    parts = []
    r = x
    for i in range(n):
        p = r.astype(BF16)
        parts.append(p)
        if i + 1 < n:
            r = r - p.astype(F32)
    return parts


def _dot_f32_rhs01(x, m01, n=3):
    acc = None
    for p in _split_bf16(x, n):
        t = _dot(p, m01)
        acc = t if acc is None else acc + t
    return acc


def _silu(x):
    return x / (1.0 + jnp.exp(-x))


def _log_sigmoid(x):
    return jnp.minimum(x, 0.0) - jnp.log(1.0 + jnp.exp(-jnp.abs(x)))


def _head_rms(x, gain, bd):
    ss = _dot((x * x).astype(BF16), bd)
    return x * lax.rsqrt(ss * (1.0 / HD) + EPS) * gain


def _bd_ones():
    h = np.arange(GW) // HD
    return (h[:, None] == h[None, :]).astype(np.float32)


def _hgrn_constants():
    c = CHUNK
    t = np.arange(c)
    j = np.arange(c)[None, :]
    mats = []
    masks = []
    for n in (64, 32, 16, 8, 4):
        blk, pos = t // n, t % n
        ref = blk * n + n // 2 - 1
        aq = (pos[:, None] >= n // 2) & (j > ref[:, None]) & (j <= t[:, None])
        ak = (pos[:, None] < n // 2) & (j > t[:, None]) & (j <= ref[:, None])
        mats.append((aq | ak).astype(np.float32))
        m = (blk[:, None] == blk[None, :]) & (pos[:, None] >= n // 2) & (pos[None, :] < n // 2)
        masks.append(m.astype(np.float32))
    m = ((t[:, None] // 2) == (t[None, :] // 2)) & (t[None, :] <= t[:, None])
    masks.append(m.astype(np.float32))
    mats.append((j <= t[:, None]).astype(np.float32))
    mats.append((j > t[:, None]).astype(np.float32))
    a_all = np.concatenate(mats, axis=0)
    a_all = np.concatenate([a_all, a_all], axis=1)
    masks = np.stack([np.tile(m, (1, NH)) for m in masks])
    total = masks[:, :, :c].sum(0)
    assert np.array_equal(total, np.tril(np.ones((c, c), np.float32)))
    hm = (np.arange(NH * c)[:, None] // c == np.arange(GW)[None, :] // HD).astype(np.float32)
    return a_all, masks, hm


def _regroup_kernel(w_ref, wb_ref, wf_ref):
    g = GW
    o_ff, o_sb, o_hg, o_pl, o_mm = 4 * g, 4 * g + NH, 8 * g + NH, 12 * g + NH, 14 * g + NH
    w = w_ref[0]
    bf_src = {P_FQ: 0, P_FK: g, P_FV: 2 * g, P_FG: 3 * g,
              P_SQ: o_sb, P_SK: o_sb + g, P_SV: o_sb + 2 * g, P_SG: o_sb + 3 * g,
              P_HQ: o_hg, P_HI: o_hg + 2 * g, P_HG: o_hg + 3 * g,
              P_PG: o_pl + g, P_MQ: o_mm, P_MG: o_mm + g}
    f_src = {F_HF: o_hg + g, F_PV: o_pl}
    for dst, c0 in bf_src.items():
        wb_ref[0, :, dst * g:(dst + 1) * g] = w[:, c0:c0 + g].astype(BF16)
    for dst, c0 in f_src.items():
        wf_ref[0, :, dst * g:(dst + 1) * g] = w[:, c0:c0 + g].astype(BF16)
    ff = jnp.concatenate([w[:, o_ff:o_ff + NH], jnp.zeros((w.shape[0], 128 - NH), F32)], axis=1)
    wf_ref[0, :, F_FF * g:F_FF * g + 128] = ff.astype(BF16)


def _regroup_w_in(w_in):
    depth, d, n = w_in.shape
    tr = 128
    return pl.pallas_call(
        _regroup_kernel,
        grid=(depth, d // tr),
        in_specs=[pl.BlockSpec((1, tr, n), lambda l, i: (l, i, 0))],
        out_specs=[
            pl.BlockSpec((1, tr, NP_COLS), lambda l, i: (l, i, 0)),
            pl.BlockSpec((1, tr, NF_COLS), lambda l, i: (l, i, 0)),
        ],
        out_shape=[
            jax.ShapeDtypeStruct((depth, d, NP_COLS), BF16),
            jax.ShapeDtypeStruct((depth, d, NF_COLS), BF16),
        ],
        compiler_params=pltpu.CompilerParams(
            dimension_semantics=("arbitrary", "arbitrary"), vmem_limit_bytes=VMEM_LIMIT),
    )(w_in)


def _proj_kernel(x_ref, g_ref, wb_ref, wf_ref, hgain_ref, bd_ref, pb_ref, pf_ref):
    x = x_ref[...]
    ms = jnp.mean(x * x, axis=-1, keepdims=True)
    h = (x * lax.rsqrt(ms + EPS) * g_ref[...]).astype(BF16)
    gains = hgain_ref[...]
    bd = bd_ref[...]
    normed = {P_FQ: (0, SCALE), P_FK: (1, 1.0), P_MQ: (2, SCALE)}
    post = {P_SQ: lambda r: r * SCALE}
    post.update({gi: _silu for gi in (P_FG, P_SG, P_HQ, P_HG, P_PG, P_MG)})

    def cols(gi):
        return slice(gi * GW, (gi + 1) * GW)

    raw = {gi: _dot(h, wb_ref[:, cols(gi)]) for gi in normed}
    for gi in range(NP_COLS // GW):
        if gi not in normed:
            r = _dot(h, wb_ref[:, cols(gi)])
            pb_ref[:, cols(gi)] = post.get(gi, lambda r: r)(r).astype(BF16)
    pf_ref[...] = _dot(h, wf_ref[...])
    for gi, (row, scale) in normed.items():
        pb_ref[:, cols(gi)] = (_head_rms(raw[gi], gains[row:row + 1], bd) * scale).astype(BF16)


def _proj(x2, g, wb, wf, hgain, bd, tm):
    m = x2.shape[0]
    return pl.pallas_call(
        _proj_kernel,
        grid=(m // tm,),
        in_specs=[
            pl.BlockSpec((tm, D_MODEL), lambda i: (i, 0)),
            pl.BlockSpec((1, D_MODEL), lambda i: (0, 0)),
            pl.BlockSpec((D_MODEL, NP_COLS), lambda i: (0, 0)),
            pl.BlockSpec((D_MODEL, NF_COLS), lambda i: (0, 0)),
            pl.BlockSpec((8, GW), lambda i: (0, 0)),
            pl.BlockSpec((GW, GW), lambda i: (0, 0)),
        ],
        out_specs=[
            pl.BlockSpec((tm, NP_COLS), lambda i: (i, 0)),
            pl.BlockSpec((tm, NF_COLS), lambda i: (i, 0)),
        ],
        out_shape=[
            jax.ShapeDtypeStruct((m, NP_COLS), BF16),
            jax.ShapeDtypeStruct((m, NF_COLS), F32),
        ],
        compiler_params=pltpu.CompilerParams(
            dimension_semantics=("arbitrary",), vmem_limit_bytes=VMEM_LIMIT),
    )(x2, g, wb, wf, hgain, bd)


def _fox_kernel(p_ref, ff_ref, bias_ref, gq_ref, gk_ref, bd_ref, tri_ref, o_ref,
                vx, crow, qk_buf, worst_ref, *, seq):
    bd = bd_ref[...]
    tri = tri_ref[...]
    pb = tri.shape[0]
    hmask = [bd[h * HD:h * HD + 1, :] for h in range(NH)]
    hmask_f = [m.astype(F32) for m in hmask]
    qk_max = (HD * SCALE * NORM_SLACK) * (jnp.max(jnp.abs(gq_ref[...]), axis=-1, keepdims=True)
                                          * jnp.max(jnp.abs(gk_ref[...]), axis=-1, keepdims=True))
    carry = jnp.zeros((8, 1), F32)
    for b in range(seq // pb):
        r = slice(b * pb, (b + 1) * pb)
        lf = _log_sigmoid(ff_ref[0, r, :].T[:8, :] + bias_ref[...])
        cb = _dot_f32_rhs01(lf, tri, 3) + carry
        carry = cb[:, pb - 1:pb]
        crow[:, r] = cb
        v = p_ref[0, r, 2 * GW:3 * GW]
        for h in range(NH):
            vx[h, r, :] = v * hmask[h]

    rel_w = (lax.broadcasted_iota(jnp.int32, (TQ, WIDE), 1) - lax.broadcasted_iota(jnp.int32, (TQ, WIDE), 0))
    nfull = (WIDE - TQ) // TQ

    def qk_all_heads(r0, start, width):
        qb = p_ref[0, pl.ds(r0, TQ), 0:GW]
        qx = jnp.concatenate([qb * m for m in hmask], axis=0)
        return _dot_nt(qx, p_ref[0, pl.ds(start, width), GW:2 * GW])

    def wide_start(r0):
        if isinstance(r0, int):
            return max(r0 - (WIDE - TQ), 0)
        return pl.multiple_of(jnp.maximum(r0 - (WIDE - TQ), 0), TQ)

    def wide_qk(i):
        r0 = _aligned(i * TQ, TQ)
        return qk_all_heads(r0, wide_start(r0), WIDE)

    def q_block(i, full_tile, qk_wide):
        r0 = _aligned(i * TQ, TQ)
        s0 = wide_start(r0)
        rows = pl.ds(r0, TQ)

        def mask_wide(s):
            if full_tile:
                diag = jnp.where(rel_w[:, WIDE - TQ:] <= WIDE - TQ, s[:, WIDE - TQ:], NEG_BIG)
                return jnp.concatenate([s[:, :WIDE - TQ], diag], axis=1)
            return jnp.where(rel_w <= r0 - s0, s, NEG_BIG)

        def tiles(qk, start, width):
            return [qk[h * TQ:(h + 1) * TQ] - crow[h:h + 1, pl.ds(start, width)] for h in range(NH)]

        def weighted_values(ps, start, width):
            p_all = jnp.concatenate([p.astype(BF16) for p in ps], axis=1)
            v_all = jnp.concatenate([vx[h, pl.ds(start, width), :] for h in range(NH)], axis=0)
            return _dot(p_all, v_all)

        def per_head_lanes(cols):
            out = cols[0] * hmask_f[0]
            for h in range(1, NH):
                out = out + cols[h] * hmask_f[h]
            return out

        def weight_bound(s_end, ms):
            last = pl.multiple_of(jnp.maximum(s_end - TQ, 0), TQ)
            bound = None
            for h in range(NH):
                c_last = crow[h:h + 1, pl.ds(last, TQ)][:, TQ - 1:TQ]
                b_h = (qk_max + LOGIT_SLACK) - c_last - jnp.min(ms[h], axis=0, keepdims=True)
                bound = b_h if bound is None else jnp.maximum(bound, b_h)
            return bound

        def keep_going(s_end, ms):
            return (jnp.max(weight_bound(s_end, ms)) > EXP_ZERO).astype(jnp.int32)

        ss = [mask_wide(s) for s in tiles(qk_wide, s0, WIDE)]
        ms = [jnp.max(s, axis=-1, keepdims=True) for s in ss]
        ps = [jnp.exp(s - m) for s, m in zip(ss, ms)]
        ls = [jnp.sum(p, axis=-1, keepdims=True) for p in ps]
        acc = weighted_values(ps, s0, WIDE)

        def emit(acc, ls):
            g = p_ref[0, rows, 3 * GW:4 * GW].astype(F32)
            o_ref[0, rows, :] = (acc * per_head_lanes([1.0 / l for l in ls]) * g).astype(BF16)

        emit(acc, ls)
        if full_tile:
            worst_ref[...] = jnp.maximum(worst_ref[...], jnp.where(s0 > 0, weight_bound(s0, ms), NEG_BIG))

        def cond(st):
            return jnp.logical_and(st[0] > 0, st[1] > 0)

        def body(st):
            s_end, _, ms, ls, acc = st
            sb = pl.multiple_of(s_end - TQ, TQ)
            ss = tiles(qk_all_heads(r0, sb, TQ), sb, TQ)
            ms2 = [jnp.maximum(m, jnp.max(s, axis=-1, keepdims=True)) for s, m in zip(ss, ms)]
            ps = [jnp.exp(s - m) for s, m in zip(ss, ms2)]
            alphas = [jnp.exp(m - m2) for m, m2 in zip(ms, ms2)]
            ls2 = [a * l + jnp.sum(p, axis=-1, keepdims=True) for a, l, p in zip(alphas, ls, ps)]
            acc2 = acc * per_head_lanes(alphas) + weighted_values(ps, sb, TQ)
            return sb, keep_going(sb, ms2), tuple(ms2), tuple(ls2), acc2

        def rare_tail():
            @pl.when(jnp.logical_and(s0 > 0, keep_going(s0, ms) > 0))
            def _():
                _, _, _, ls_f, acc_f = lax.while_loop(cond, body, (s0, jnp.int32(1), tuple(ms), tuple(ls), acc))
                emit(acc_f, ls_f)

        return rare_tail

    nblk = seq // TQ
    grp = FOX_UNROLL
    lead = nfull + (nblk - nfull) % (2 * grp)
    worst_ref[...] = jnp.full(worst_ref.shape, NEG_BIG, F32)
    for i in range(lead):
        q_block(i, i >= nfull, wide_qk(i))
    for k in range(grp):
        qk_buf[k] = wide_qk(lead + k)

    def group(first, cur, nxt):
        for k in range(grp):
            qk_buf[nxt * grp + k] = wide_qk(jnp.minimum(first + grp + k, nblk - 1))
        for k in range(grp):
            q_block(first + k, True, qk_buf[cur * grp + k])

    def two_groups(j, _):
        i = lead + 2 * grp * j
        group(i, 0, 1)
        group(i + grp, 1, 0)
        return 0

    lax.fori_loop(0, (nblk - lead) // (2 * grp), two_groups, 0)

    @pl.when(jnp.max(worst_ref[...]) > EXP_ZERO)
    def _():
        def redo(i, _):
            q_block(i, True, wide_qk(i))()
            return 0

        lax.fori_loop(nfull, nblk, redo, 0)


def _fox(p3, f3, bias, gq, gk, bd, tri):
    b, s, _ = p3.shape
    assert s % tri.shape[0] == 0 and s >= WIDE + 2 * FOX_UNROLL * TQ
    kern = functools.partial(_fox_kernel, seq=s)
    c2 = lambda i: (0, 0)
    return pl.pallas_call(
        kern,
        grid=(b,),
        in_specs=[
            pl.BlockSpec((1, s, 4 * GW), lambda i: (i, 0, P_FQ // 4)),
            pl.BlockSpec((1, s, 128), lambda i: (i, 0, F_FF * GW // 128)),
            pl.BlockSpec((8, tri.shape[0]), c2),
            pl.BlockSpec((1, GW), c2),
            pl.BlockSpec((1, GW), c2),
            pl.BlockSpec((GW, GW), c2),
            pl.BlockSpec(tri.shape, c2),
        ],
        out_specs=pl.BlockSpec((1, s, GW), lambda i: (i, 0, 0)),
        out_shape=jax.ShapeDtypeStruct((b, s, GW), BF16),
        scratch_shapes=[
            pltpu.VMEM((NH, s, GW), BF16),
            pltpu.VMEM((8, s), F32),
            pltpu.VMEM((2 * FOX_UNROLL, NH * TQ, WIDE), F32),
            pltpu.VMEM((8, 128), F32),
        ],
        compiler_params=pltpu.CompilerParams(
            dimension_semantics=("arbitrary",), vmem_limit_bytes=VMEM_LIMIT),
    )(p3, f3, bias, gq, gk, bd, tri)


def _sb_kernel(p_ref, to_ref, tof_ref, bd_ref, o_ref, vx, qk_buf, worst_ref, *, seq):
    to = to_ref[...]
    tq, wide, cw = SB_TQ, SB_WIDE, SB_CW
    hmask = [bd_ref[h * HD:h * HD + 1, :] for h in range(NH)]
    pb = 256
    for b in range(seq // pb):
        r = slice(b * pb, (b + 1) * pb)
        v = p_ref[0, r, 2 * GW:3 * GW]
        for h in range(NH):
            vx[h, r, :] = v * hmask[h]

    rel_w = (lax.broadcasted_iota(jnp.int32, (tq, wide), 1) - lax.broadcasted_iota(jnp.int32, (tq, wide), 0))
    nsub = wide // cw
    nfull = -(-(wide - tq) // tq)

    def log_one_minus_sigmoid(z):
        return -jnp.maximum(z, jnp.log(1.0 + jnp.exp(jnp.minimum(z, SOFTPLUS_LINEAR))))

    def suffix_sums(lom, mat):
        w = lom.shape[1]
        cs = _dot(lom.astype(BF16), mat)
        return cs[:, :w], cs[:, w:]

    def qk_all_heads(r0, start, width):
        qb = p_ref[0, pl.ds(r0, tq), 0:GW]
        qx = jnp.concatenate([qb * m for m in hmask], axis=0)
        return _dot_nt(qx, p_ref[0, pl.ds(start, width), GW:2 * GW])

    def wide_start(r0):
        if isinstance(r0, int):
            return max(r0 - (wide - tq), 0)
        return pl.multiple_of(jnp.maximum(r0 - (wide - tq), 0), tq)

    def wide_qk(i):
        r0 = _aligned(i * tq, tq)
        return qk_all_heads(r0, wide_start(r0), wide)

    def q_block(i, full_tile, qk_wide):
        r0 = _aligned(i * tq, tq)
        s0 = wide_start(r0)
        rows = pl.ds(r0, tq)

        def mask_wide(x):
            if full_tile:
                diag = jnp.where(rel_w[:, wide - cw:] < wide - tq, x[:, wide - cw:], 0.0)
                return jnp.concatenate([x[:, :wide - cw], diag], axis=1)
            return jnp.where(rel_w < r0 - s0, x, 0.0)

        def per_head(qk):
            return [qk[h * tq:(h + 1) * tq] for h in range(NH)]

        def weighted_values(ws, start, width):
            w_all = jnp.concatenate([w.astype(BF16) for w in ws], axis=1)
            v_all = jnp.concatenate([vx[h, pl.ds(start, width), :] for h in range(NH)], axis=0)
            return _dot(w_all, v_all)

        def emit(acc):
            g = p_ref[0, rows, 3 * GW:4 * GW].astype(F32)
            o_ref[0, rows, :] = (acc * g).astype(BF16)

        zs = per_head(qk_wide)
        loms = [log_one_minus_sigmoid(z) for z in zs]
        log_betas = [z + lom for z, lom in zip(zs, loms)]
        sums = [[suffix_sums(mask_wide(lom)[:, c * cw:(c + 1) * cw], to) for c in range(nsub)] for lom in loms]
        carries, ws = [], []
        for h in range(NH):
            between = [None] * nsub
            carry = jnp.zeros((tq, cw), F32)
            for c in reversed(range(nsub)):
                rc, tot = sums[h][c]
                between[c] = rc + carry
                carry = carry + tot
            ws.append(mask_wide(jnp.exp(log_betas[h] + jnp.concatenate(between, axis=1))))
            carries.append(carry)
        acc = weighted_values(ws, s0, wide)
        emit(acc)

        def worst_carry(carries):
            cm = jnp.maximum(jnp.maximum(carries[0], carries[1]), jnp.maximum(carries[2], carries[3]))
            return jnp.max(cm.reshape(tq // 8, 8, cw), axis=0)

        def keep_going(carries):
            return (jnp.max(worst_carry(carries)) > EXP_ZERO).astype(jnp.int32)

        if full_tile:
            worst_ref[0] = jnp.maximum(worst_ref[0], jnp.where(s0 > 0, worst_carry(carries), NEG_BIG))

        def cond(st):
            return jnp.logical_and(st[0] > 0, st[1] > 0)

        def body(st):
            s_end, _, carries, acc = st
            sb = pl.multiple_of(s_end - tq, tq)
            zs = per_head(qk_all_heads(r0, sb, tq))
            loms = [log_one_minus_sigmoid(z) for z in zs]
            sums = [suffix_sums(lom, tof_ref[...]) for lom in loms]
            ws = [jnp.exp(z + lom + rc + cr[:, :tq]) for z, lom, (rc, _), cr in zip(zs, loms, sums, carries)]
            c2 = [cr + tot for cr, (_, tot) in zip(carries, sums)]
            return sb, keep_going(c2), tuple(c2), acc + weighted_values(ws, sb, tq)

        def rare_tail():
            @pl.when(jnp.logical_and(s0 > 0, keep_going(carries) > 0))
            def _():
                st = lax.while_loop(cond, body, (s0, jnp.int32(1), tuple(carries), acc))
                emit(st[3])

        return rare_tail

    nblk = seq // tq
    grp = SB_UNROLL
    lead = nfull + (nblk - nfull) % (2 * grp)
    worst_ref[0] = jnp.full((8, cw), NEG_BIG, F32)
    for i in range(lead):
        q_block(i, i >= nfull, wide_qk(i))
    for k in range(grp):
        qk_buf[k] = wide_qk(lead + k)

    def group(first, cur, nxt):
        for k in range(grp):
            qk_buf[nxt * grp + k] = wide_qk(jnp.minimum(first + grp + k, nblk - 1))
        for k in range(grp):
            q_block(first + k, True, qk_buf[cur * grp + k])

    def two_groups(j, _):
        i = lead + 2 * grp * j
        group(i, 0, 1)
        group(i + grp, 1, 0)
        return 0

    lax.fori_loop(0, (nblk - lead) // (2 * grp), two_groups, 0)

    @pl.when(jnp.max(worst_ref[0]) > EXP_ZERO)
    def _():
        def redo(i, _):
            q_block(i, True, wide_qk(i))()
            return 0

        lax.fori_loop(nfull, nblk, redo, 0)


def _sb(p3, to, tof, bd):
    b, s, _ = p3.shape
    assert s % 256 == 0 and s >= SB_WIDE + 2 * SB_UNROLL * SB_TQ
    kern = functools.partial(_sb_kernel, seq=s)
    return pl.pallas_call(
        kern,
        grid=(b,),
        in_specs=[
            pl.BlockSpec((1, s, 4 * GW), lambda i: (i, 0, P_SQ // 4)),
            pl.BlockSpec(to.shape, lambda i: (0, 0)),
            pl.BlockSpec(tof.shape, lambda i: (0, 0)),
            pl.BlockSpec((GW, GW), lambda i: (0, 0)),
        ],
        out_specs=pl.BlockSpec((1, s, GW), lambda i: (i, 0, 0)),
        out_shape=jax.ShapeDtypeStruct((b, s, GW), BF16),
        scratch_shapes=[
            pltpu.VMEM((NH, s, GW), BF16),
            pltpu.VMEM((2 * SB_UNROLL, NH * SB_TQ, SB_WIDE), F32),
            pltpu.VMEM((1, 8, SB_CW), F32),
        ],
        compiler_params=pltpu.CompilerParams(
            dimension_semantics=("arbitrary",), vmem_limit_bytes=VMEM_LIMIT),
    )(p3, to, tof, bd)


def _hgrn_kernel(hq_ref, hi_ref, hg_ref, hf_ref, lbm_ref, oml_ref, gout_ref, bd_ref, a_ref,
                 mask_ref, hm_ref, o_ref, st_ref, *, seq):
    bd = bd_ref[...]
    a_all = a_ref[...]
    hm = hm_ref[...]
    lbm = lbm_ref[...]
    oml = oml_ref[...]
    c = CHUNK
    st_ref[...] = jnp.zeros((GW, GW), F32)

    n = HSB * c
    nlev = mask_ref.shape[0] - 1
    odd =(lax.broadcasted_iota(jnp.int32, (n, GW), 0) & 1) == 1

    def superblock(bi, _):
        r0 = pl.multiple_of(bi * n, n)
        rows = pl.ds(r0, n)
        hf = hf_ref[0, rows, :]
        sg = 1.0 / (1.0 + jnp.exp(-hf))
        f = lbm + oml * sg
        g = jnp.log(f)
        kk = oml * (1.0 - sg)
        q = hq_ref[0, rows, :].astype(F32)
        v = hi_ref[0, rows, :]
        gh = g.astype(BF16)
        gl = (g - gh.astype(F32)).astype(BF16)
        q2 = (q * jnp.where(odd, f, 1.0)).astype(BF16)
        k2 = (kk * jnp.where(odd, 1.0 / f, 1.0)).astype(BF16)

        def scores(qf, kf, l):
            kx = jnp.concatenate([kf] * NH, axis=0) * hm
            return _dot_nt(qf, kx) * mask_ref[l]

        sls = [slice(ci * c, (ci + 1) * c) for ci in range(HSB)]
        exs = [_dot(a_all, jnp.concatenate([gh[sl], gl[sl]], axis=0)) for sl in sls]

        st = st_ref[...]
        o_inter = []
        for ci, sl in enumerate(sls):
            eb = exs[ci][nlev * c:(nlev + 1) * c]
            er = exs[ci][(nlev + 1) * c:(nlev + 2) * c]
            qd = (q[sl] * jnp.exp(eb)).astype(BF16)
            o_inter.append(_dot_nt(qd, st.astype(BF16) * bd))
            upd = _dot_tn(v[sl], (kk[sl] * jnp.exp(er)).astype(BF16))
            st = st * jnp.exp(eb[c - 1:c, :]) + upd
        st_ref[...] = st

        ps = [scores(q2[sl], k2[sl], nlev) for sl in sls]
        qb = q.astype(BF16)
        kb = kk.astype(BF16)
        for l in range(nlev):
            for ci, sl in enumerate(sls):
                x = jnp.exp(exs[ci][l * c:(l + 1) * c]).astype(BF16)
                ps[ci] = ps[ci] + scores(qb[sl] * x, kb[sl] * x, l)
        outs = []
        for ci, sl in enumerate(sls):
            vx = jnp.concatenate([v[sl]] * NH, axis=0) * hm
            outs.append(o_inter[ci] + _dot(ps[ci].astype(BF16), vx))

        o = _head_rms(jnp.concatenate(outs, axis=0), gout_ref[...], bd)
        o_ref[0, rows, :] = (o * hg_ref[0, rows, :].astype(F32)).astype(BF16)
        return 0

    lax.fori_loop(0, seq // n, superblock, 0)


def _hgrn(p3, f3, lbm, oml, gout, bd, a_all, masks, hm):
    b, s, _ = p3.shape
    assert s % (HSB * CHUNK) == 0
    kern = functools.partial(_hgrn_kernel, seq=s)
    c2 = lambda i: (0, 0)
    return pl.pallas_call(
        kern,
        grid=(b,),
        in_specs=[
            pl.BlockSpec((1, s, GW), lambda i: (i, 0, P_HQ)),
            pl.BlockSpec((1, s, GW), lambda i: (i, 0, P_HI)),
            pl.BlockSpec((1, s, GW), lambda i: (i, 0, P_HG)),
            pl.BlockSpec((1, s, GW), lambda i: (i, 0, F_HF)),
            pl.BlockSpec((1, GW), c2),
            pl.BlockSpec((1, GW), c2),
            pl.BlockSpec((1, GW), c2),
            pl.BlockSpec((GW, GW), c2),
            pl.BlockSpec(a_all.shape, c2),
            pl.BlockSpec(masks.shape, lambda i: (0, 0, 0)),
            pl.BlockSpec(hm.shape, c2),
        ],
        out_specs=pl.BlockSpec((1, s, GW), lambda i: (i, 0, 0)),
        out_shape=jax.ShapeDtypeStruct((b, s, GW), BF16),
        scratch_shapes=[pltpu.VMEM((GW, GW), F32)],
        compiler_params=pltpu.CompilerParams(
            dimension_semantics=("arbitrary",), vmem_limit_bytes=VMEM_LIMIT),
    )(p3, p3, p3, f3, lbm, oml, gout, bd, a_all, masks, hm)


def _pm_kernel(pv_ref, pg_ref, mq_ref, mg_ref, mem_ref, mng_ref, wkv_ref, gmk_ref, bd_ref,
               wp_ref, ps_ref, win_ref, d_ref, e_ref, ubuf, wbuf, kmem, vxm, *, seq, tq):
    bd = bd_ref[...]
    hmask = [bd[h * HD:h * HD + 1, :] for h in range(NH)]
    hmask_f = [m.astype(F32) for m in hmask]
    halo = 16
    mem = mem_ref[0]
    ms = jnp.mean(mem * mem, axis=-1, keepdims=True)
    mn = (mem * lax.rsqrt(ms + EPS) * mng_ref[...]).astype(BF16)
    kv = _dot(mn, wkv_ref[...])
    kn = _head_rms(kv[:, :GW], gmk_ref[...], bd).astype(BF16)
    vv = kv[:, GW:].astype(BF16)
    nm = kn.shape[0]
    kmem[...] = kn
    for h in range(NH):
        vxm[h * nm:(h + 1) * nm, :] = vv * hmask[h]

    ubuf[0:halo, :] = jnp.zeros((halo, GW), F32)
    ubuf[halo:halo + seq, :] = pv_ref[0]
    win = win_ref[...]
    inv_win = 1.0 / win
    for k in range(3):
        sh = 1 << k
        wbuf[k, 0:halo, :] = jnp.zeros((halo, GW), F32)
        for b in range(seq // tq):
            base = halo + b * tq
            if k == 0:
                wbuf[k, base:base + tq, :] = ubuf[base:base + tq, :] + ubuf[base - sh:base - sh + tq, :]
            else:
                wbuf[k, base:base + tq, :] = (wbuf[k - 1, base:base + tq, :]
                                              + wbuf[k - 1, base - sh:base - sh + tq, :])

    for b in range(seq // tq):
        r = slice(b * tq, (b + 1) * tq)
        base = halo + b * tq
        u = ubuf[base:base + tq, :]
        s2 = wbuf[0, base:base + tq, :]
        s4 = wbuf[1, base:base + tq, :]
        s8 = wbuf[2, base:base + tq, :]
        s16 = s8 + wbuf[2, base - 8:base - 8 + tq, :]
        sw = jnp.where(win == 2.0, s2, jnp.where(win == 4.0, s4, jnp.where(win == 8.0, s8, s16)))
        if b == 0:
            pos = (lax.broadcasted_iota(jnp.int32, (tq, GW), 0) + 1).astype(F32)
            pooled = sw / jnp.minimum(pos, win)
        else:
            pooled = sw * inv_win
        y = _dot((pooled - u).astype(BF16), wp_ref[...]) * ps_ref[...]
        d_ref[0, r, :] = (y * pg_ref[0, r, :].astype(F32)).astype(BF16)

        qn = mq_ref[0, r, :]
        s_all = _dot_nt(jnp.concatenate([qn * m for m in hmask], axis=0), kmem[...])
        ss = [s_all[h * tq:(h + 1) * tq] for h in range(NH)]
        ps = [jnp.exp(s - jnp.max(s, axis=-1, keepdims=True)) for s in ss]
        ls = [jnp.sum(p, axis=-1, keepdims=True) for p in ps]
        inv_l = (1.0 / ls[0]) * hmask_f[0]
        for h in range(1, NH):
            inv_l = inv_l + (1.0 / ls[h]) * hmask_f[h]
        oe = _dot(jnp.concatenate([p.astype(BF16) for p in ps], axis=1), vxm[...]) * inv_l
        e_ref[0, r, :] = (oe * mg_ref[0, r, :].astype(F32)).astype(BF16)


def _pm(p3, f3, mem, mng, wkv, gmk, bd, wp, ps, win, tq):
    b, s, _ = p3.shape
    nm = mem.shape[1]
    kern = functools.partial(_pm_kernel, seq=s, tq=tq)
    c2 = lambda i: (0, 0)
    return pl.pallas_call(
        kern,
        grid=(b,),
        in_specs=[
            pl.BlockSpec((1, s, GW), lambda i: (i, 0, F_PV)),
            pl.BlockSpec((1, s, GW), lambda i: (i, 0, P_PG)),
            pl.BlockSpec((1, s, GW), lambda i: (i, 0, P_MQ)),
            pl.BlockSpec((1, s, GW), lambda i: (i, 0, P_MG)),
            pl.BlockSpec((1, nm, D_MODEL), lambda i: (i, 0, 0)),
            pl.BlockSpec((1, D_MODEL), c2),
            pl.BlockSpec((D_MODEL, 2 * GW), c2),
            pl.BlockSpec((1, GW), c2),
            pl.BlockSpec((GW, GW), c2),
            pl.BlockSpec((GW, GW), c2),
            pl.BlockSpec((1, GW), c2),
            pl.BlockSpec((1, GW), c2),
        ],
        out_specs=[
            pl.BlockSpec((1, s, GW), lambda i: (i, 0, 0)),
            pl.BlockSpec((1, s, GW), lambda i: (i, 0, 0)),
        ],
        out_shape=[
            jax.ShapeDtypeStruct((b, s, GW), BF16),
            jax.ShapeDtypeStruct((b, s, GW), BF16),
        ],
        scratch_shapes=[
            pltpu.VMEM((s + 16, GW), F32),
            pltpu.VMEM((3, s + 16, GW), F32),
            pltpu.VMEM((nm, GW), BF16),
            pltpu.VMEM((NH * nm, GW), BF16),
        ],
        compiler_params=pltpu.CompilerParams(
            dimension_semantics=("arbitrary",), vmem_limit_bytes=VMEM_LIMIT),
    )(f3, p3, p3, p3, mem, mng, wkv, gmk, bd, wp, ps, win)


def _out_kernel(a_ref, b_ref, c_ref, d_ref, e_ref, w_ref, x_ref, o_ref):
    mixed = jnp.concatenate([a_ref[...], b_ref[...], c_ref[...], d_ref[...], e_ref[...]], axis=1)
    o_ref[...] = x_ref[...] + _dot(mixed, w_ref[...])


def _out(parts, w, x2, tm):
    m = x2.shape[0]
    gspec = pl.BlockSpec((tm, GW), lambda i: (i, 0))
    return pl.pallas_call(
        _out_kernel,
        grid=(m // tm,),
        in_specs=[gspec] * 5 + [
            pl.BlockSpec((5 * GW, D_MODEL), lambda i: (0, 0)),
            pl.BlockSpec((tm, D_MODEL), lambda i: (i, 0)),
        ],
        out_specs=pl.BlockSpec((tm, D_MODEL), lambda i: (i, 0)),
        out_shape=jax.ShapeDtypeStruct((m, D_MODEL), F32),
        compiler_params=pltpu.CompilerParams(
            dimension_semantics=("arbitrary",), vmem_limit_bytes=VMEM_LIMIT),
    )(*parts, w, x2)


def _out_proj_kernel(a_ref, b_ref, c_ref, d_ref, e_ref, w_ref, x_ref, g_ref, wb_ref, wf_ref, hgain_ref, bd_ref,
                     o_ref, pb_ref, pf_ref):
    _out_kernel(a_ref, b_ref, c_ref, d_ref, e_ref, w_ref, x_ref, o_ref)
    _proj_kernel(o_ref, g_ref, wb_ref, wf_ref, hgain_ref, bd_ref, pb_ref, pf_ref)


def _out_proj(parts, w, x2, g, wb, wf, hgain, bd, tm):
    m = x2.shape[0]
    gspec = pl.BlockSpec((tm, GW), lambda i: (i, 0))
    c2 = lambda i: (0, 0)
    return pl.pallas_call(
        _out_proj_kernel,
        grid=(m // tm,),
        in_specs=[gspec] * 5 + [
            pl.BlockSpec((5 * GW, D_MODEL), c2),
            pl.BlockSpec((tm, D_MODEL), lambda i: (i, 0)),
            pl.BlockSpec((1, D_MODEL), c2),
            pl.BlockSpec((D_MODEL, NP_COLS), c2),
            pl.BlockSpec((D_MODEL, NF_COLS), c2),
            pl.BlockSpec((8, GW), c2),
            pl.BlockSpec((GW, GW), c2),
        ],
        out_specs=[
            pl.BlockSpec((tm, D_MODEL), lambda i: (i, 0)),
            pl.BlockSpec((tm, NP_COLS), lambda i: (i, 0)),
            pl.BlockSpec((tm, NF_COLS), lambda i: (i, 0)),
        ],
        out_shape=[
            jax.ShapeDtypeStruct((m, D_MODEL), F32),
            jax.ShapeDtypeStruct((m, NP_COLS), BF16),
            jax.ShapeDtypeStruct((m, NF_COLS), F32),
        ],
        compiler_params=pltpu.CompilerParams(
            dimension_semantics=("arbitrary",), vmem_limit_bytes=VMEM_LIMIT),
    )(*parts, w, x2, g, wb, wf, hgain, bd)


def _tile_heads(g):
    return jnp.tile(g.astype(F32), NH).reshape(1, GW)


def kernel(x, mem, norm_g, w_in, fox_f_bias, fox_q_norm, fox_k_norm, hgrn_lb_logits, hgrn_out_norm,
           pool_w, pool_scale, mem_norm_g, mem_w_kv, mem_q_norm, mem_k_norm, w_out):
    bsz, seq, _ = x.shape
    depth = w_in.shape[0]
    m = bsz * seq
    tq = 256
    tm = 512
    tm_out = 2048

    pr = jax.nn.softmax(hgrn_lb_logits.astype(F32), axis=0)
    lower_bounds = jnp.clip(jnp.cumsum(pr, axis=0) - pr[0:1], 0.0, 1.0 - 1e-6)

    bd_np = _bd_ones()
    bd = jnp.asarray(bd_np, BF16)
    tri = jnp.asarray(np.triu(np.ones((tq, tq), np.float32)), BF16)

    def suffix_and_ones(w):
        jj = np.arange(w)
        suffix = (jj[:, None] > jj[None, :]).astype(np.float32)
        return jnp.asarray(np.concatenate([suffix, np.ones((w, SB_CW), np.float32)], axis=1), BF16)

    to = suffix_and_ones(SB_CW)
    tof = suffix_and_ones(SB_TQ)
    a_np, masks_np, hm_np = _hgrn_constants()
    a_all = jnp.asarray(a_np, BF16)
    masks = jnp.asarray(masks_np, F32)
    hm = jnp.asarray(hm_np, BF16)
    win = jnp.asarray(np.repeat(np.array(POOL_WINDOWS, np.float32), HD).reshape(1, GW))

    g = GW
    wb_all, wf_all = _regroup_w_in(w_in)
    x2 = x.reshape(m, D_MODEL)
    norm_gs = [norm_g[l].reshape(1, D_MODEL).astype(F32) for l in range(depth)]
    head_gains = [jnp.concatenate([_tile_heads(fox_q_norm[l]), _tile_heads(fox_k_norm[l]),
                                   _tile_heads(mem_q_norm[l]), jnp.zeros((5, GW), F32)], axis=0)
                  for l in range(depth)]
    pb, pf = _proj(x2, norm_gs[0], wb_all[0], wf_all[0], head_gains[0], bd, tm)
    for l in range(depth):
        p3 = pb.reshape(bsz, seq, NP_COLS)
        f3 = pf.reshape(bsz, seq, NF_COLS)

        bias = jnp.broadcast_to(jnp.pad(fox_f_bias[l].astype(F32), (0, 8 - NH))[:, None], (8, tq))
        out_a = _fox(p3, f3, bias, _tile_heads(fox_q_norm[l]), _tile_heads(fox_k_norm[l]), bd, tri)
        out_b = _sb(p3, to, tof, bd)

        lb = lower_bounds[l].reshape(1, g)
        out_c = _hgrn(p3, f3, jnp.maximum(lb, LB_FLOOR), 1.0 - lb,
                      hgrn_out_norm[l].reshape(1, g).astype(F32), bd, a_all, masks, hm)

        wp = jax.scipy.linalg.block_diag(*[pool_w[l, i] for i in range(len(POOL_WINDOWS))]).astype(BF16)
        out_d, out_e = _pm(p3, f3, mem, mem_norm_g[l].reshape(1, D_MODEL).astype(F32),
                           mem_w_kv[l].astype(BF16), _tile_heads(mem_k_norm[l]),
                           bd, wp, pool_scale[l].reshape(1, g).astype(F32), win, tq)

        parts = [o.reshape(m, g) for o in (out_a, out_b, out_c, out_d, out_e)]
        if l + 1 < depth:
            x2, pb, pf = _out_proj(parts, w_out[l].astype(BF16), x2, norm_gs[l + 1],
                                   wb_all[l + 1], wf_all[l + 1], head_gains[l + 1], bd, tm)
        else:
            x2 = _out(parts, w_out[l].astype(BF16), x2, tm_out)
    return x2.reshape(bsz, seq, D_MODEL)
```

```python
import functools

import numpy as np
import jax
import jax.numpy as jnp
from jax import lax
from jax.experimental import pallas as pl
from jax.experimental.pallas import tpu as pltpu

F32 = jnp.float32
BF16 = jnp.bfloat16

D_MODEL = 1024
GW = 256
NH = 4
HD = 64
CHUNK = 64
POOL_WINDOWS = (2, 4, 8, 16)
EPS = 1e-6
NEG_BIG = -1e30
LB_FLOOR = 1e-30
SCALE = HD ** -0.5

NP_COLS = 14 * GW
P_FQ, P_FK, P_FV, P_FG, P_SQ, P_SK, P_SV, P_SG, P_HQ, P_HI, P_HG, P_PG, P_MQ, P_MG = range(14)
NF_COLS = 2 * GW + 128
F_HF, F_PV, F_FF = range(3)
VMEM_LIMIT = 56 * 1024 * 1024

TQ = 128
WIDE = 3 * TQ
EXP_ZERO = -104.0
NORM_SLACK = 1.01
LOGIT_SLACK = 0.05
FOX_UNROLL = 2
NCOARSE = 3
COARSE_STARTS = ((0,), (0, 32), (0, 16, 32, 48))
COARSE_RUN = (32, 16, 8)
HSB = 8
SB_TQ = 128
SB_WIDE = 384
SB_CW = 128
SB_UNROLL = 2
SOFTPLUS_LINEAR = 80.0


def _dot(a, b):
    return jnp.dot(a, b, preferred_element_type=F32)


def _dot_nt(a, b):
    return lax.dot_general(a, b, (((1,), (1,)), ((), ())), preferred_element_type=F32)


def _dot_tn(a, b):
    return lax.dot_general(a, b, (((0,), (0,)), ((), ())), preferred_element_type=F32)


def _aligned(x, m):
    return x if isinstance(x, int) else pl.multiple_of(x, m)


def _split_bf16(x, n):
    parts = []
    r = x
    for i in range(n):
        p = r.astype(BF16)
        parts.append(p)
        if i + 1 < n:
            r = r - p.astype(F32)
    return parts


def _dot_f32_rhs01(x, m01, n=3):
    acc = None
    for p in _split_bf16(x, n):
        t = _dot(p, m01)
        acc = t if acc is None else acc + t
    return acc


def _silu(x):
    return x / (1.0 + jnp.exp(-x))


def _log_sigmoid(x):
    return jnp.minimum(x, 0.0) - jnp.log(1.0 + jnp.exp(-jnp.abs(x)))


def _head_rms(x, gain, bd):
    ss = _dot((x * x).astype(BF16), bd)
    return x * lax.rsqrt(ss * (1.0 / HD) + EPS) * gain


def _bd_ones():
    h = np.arange(GW) // HD
    return (h[:, None] == h[None, :]).astype(np.float32)


def _hgrn_constants():
    c = CHUNK
    t = np.arange(c)
    j = np.arange(c)[None, :]
    mats = []
    masks = []
    for n in (64, 32, 16, 8, 4):
        blk, pos = t // n, t % n
        ref = blk * n + n // 2 - 1
        aq = (pos[:, None] >= n // 2) & (j > ref[:, None]) & (j <= t[:, None])
        ak = (pos[:, None] < n // 2) & (j > t[:, None]) & (j <= ref[:, None])
        mats.append((aq | ak).astype(np.float32))
        m = (blk[:, None] == blk[None, :]) & (pos[:, None] >= n // 2) & (pos[None, :] < n // 2)
        masks.append(m.astype(np.float32))
    m = ((t[:, None] // 2) == (t[None, :] // 2)) & (t[None, :] <= t[:, None])
    masks.append(m.astype(np.float32))
    mats.append((j <= t[:, None]).astype(np.float32))
    mats.append((j > t[:, None]).astype(np.float32))
    a_all = np.concatenate(mats, axis=0)
    a_all = np.concatenate([a_all, a_all], axis=1)
    masks = np.stack([np.tile(m, (1, NH)) for m in masks])
    total = masks[:, :, :c].sum(0)
    assert np.array_equal(total, np.tril(np.ones((c, c), np.float32)))
    hm = (np.arange(NH * c)[:, None] // c == np.arange(GW)[None, :] // HD).astype(np.float32)
    maskc = []
    for l, n in enumerate((64, 32, 16, 8, 4)[:NCOARSE]):
        s = np.concatenate([np.arange(lo, lo + COARSE_RUN[l]) for lo in COARSE_STARTS[l]])
        m = (t[:, None] // n == s[None, :] // n) & (t[:, None] % n >= n // 2)
        maskc.append(np.tile(m.astype(np.float32), (1, NH)))
    hmc = (np.arange(NH * (c // 2))[:, None] // (c // 2) == np.arange(GW)[None, :] // HD).astype(np.float32)
    return a_all, masks, hm, np.stack(maskc), hmc


def _regroup_kernel(w_ref, wb_ref, wf_ref):
    g = GW
    o_ff, o_sb, o_hg, o_pl, o_mm = 4 * g, 4 * g + NH, 8 * g + NH, 12 * g + NH, 14 * g + NH
    w = w_ref[0]
    bf_src = {P_FQ: 0, P_FK: g, P_FV: 2 * g, P_FG: 3 * g,
              P_SQ: o_sb, P_SK: o_sb + g, P_SV: o_sb + 2 * g, P_SG: o_sb + 3 * g,
              P_HQ: o_hg, P_HI: o_hg + 2 * g, P_HG: o_hg + 3 * g,
              P_PG: o_pl + g, P_MQ: o_mm, P_MG: o_mm + g}
    f_src = {F_HF: o_hg + g, F_PV: o_pl}
    for dst, c0 in bf_src.items():
        wb_ref[0, :, dst * g:(dst + 1) * g] = w[:, c0:c0 + g].astype(BF16)
    for dst, c0 in f_src.items():
        wf_ref[0, :, dst * g:(dst + 1) * g] = w[:, c0:c0 + g].astype(BF16)
    ff = jnp.concatenate([w[:, o_ff:o_ff + NH], jnp.zeros((w.shape[0], 128 - NH), F32)], axis=1)
    wf_ref[0, :, F_FF * g:F_FF * g + 128] = ff.astype(BF16)


def _regroup_w_in(w_in):
    depth, d, n = w_in.shape
    tr = 128
    return pl.pallas_call(
        _regroup_kernel,
        grid=(depth, d // tr),
        in_specs=[pl.BlockSpec((1, tr, n), lambda l, i: (l, i, 0))],
        out_specs=[
            pl.BlockSpec((1, tr, NP_COLS), lambda l, i: (l, i, 0)),
            pl.BlockSpec((1, tr, NF_COLS), lambda l, i: (l, i, 0)),
        ],
        out_shape=[
            jax.ShapeDtypeStruct((depth, d, NP_COLS), BF16),
            jax.ShapeDtypeStruct((depth, d, NF_COLS), BF16),
        ],
        compiler_params=pltpu.CompilerParams(
            dimension_semantics=("arbitrary", "arbitrary"), vmem_limit_bytes=VMEM_LIMIT),
    )(w_in)


def _proj_kernel(x_ref, g_ref, wb_ref, wf_ref, hgain_ref, bd_ref, pb_ref, pf_ref):
    x = x_ref[...]
    ms = jnp.mean(x * x, axis=-1, keepdims=True)
    h = (x * lax.rsqrt(ms + EPS) * g_ref[...]).astype(BF16)
    gains = hgain_ref[...]
    bd = bd_ref[...]
    normed = {P_FQ: (0, SCALE), P_FK: (1, 1.0), P_MQ: (2, SCALE)}
    post = {P_SQ: lambda r: r * SCALE}
    post.update({gi: _silu for gi in (P_FG, P_SG, P_HQ, P_HG, P_PG, P_MG)})

    def cols(gi):
        return slice(gi * GW, (gi + 1) * GW)

    raw = {gi: _dot(h, wb_ref[:, cols(gi)]) for gi in normed}
    for gi in range(NP_COLS // GW):
        if gi not in normed:
            r = _dot(h, wb_ref[:, cols(gi)])
            pb_ref[:, cols(gi)] = post.get(gi, lambda r: r)(r).astype(BF16)
    pf_ref[...] = _dot(h, wf_ref[...])
    for gi, (row, scale) in normed.items():
        pb_ref[:, cols(gi)] = (_head_rms(raw[gi], gains[row:row + 1], bd) * scale).astype(BF16)


def _proj(x2, g, wb, wf, hgain, bd, tm):
    m = x2.shape[0]
    return pl.pallas_call(
        _proj_kernel,
        grid=(m // tm,),
        in_specs=[
            pl.BlockSpec((tm, D_MODEL), lambda i: (i, 0)),
            pl.BlockSpec((1, D_MODEL), lambda i: (0, 0)),
            pl.BlockSpec((D_MODEL, NP_COLS), lambda i: (0, 0)),
            pl.BlockSpec((D_MODEL, NF_COLS), lambda i: (0, 0)),
            pl.BlockSpec((8, GW), lambda i: (0, 0)),
            pl.BlockSpec((GW, GW), lambda i: (0, 0)),
        ],
        out_specs=[
            pl.BlockSpec((tm, NP_COLS), lambda i: (i, 0)),
            pl.BlockSpec((tm, NF_COLS), lambda i: (i, 0)),
        ],
        out_shape=[
            jax.ShapeDtypeStruct((m, NP_COLS), BF16),
            jax.ShapeDtypeStruct((m, NF_COLS), F32),
        ],
        compiler_params=pltpu.CompilerParams(
            dimension_semantics=("arbitrary",), vmem_limit_bytes=VMEM_LIMIT),
    )(x2, g, wb, wf, hgain, bd)


def _fox_kernel(p_ref, ff_ref, bias_ref, gq_ref, gk_ref, bd_ref, tri_ref, o_ref,
                vx, crow, qk_buf, worst_ref, *, seq):
    bd = bd_ref[...]
    tri = tri_ref[...]
    pb = tri.shape[0]
    hmask = [bd[h * HD:h * HD + 1, :] for h in range(NH)]
    hmask_f = [m.astype(F32) for m in hmask]
    qk_max = (HD * SCALE * NORM_SLACK) * (jnp.max(jnp.abs(gq_ref[...]), axis=-1, keepdims=True)
                                          * jnp.max(jnp.abs(gk_ref[...]), axis=-1, keepdims=True))
    carry = jnp.zeros((8, 1), F32)
    for b in range(seq // pb):
        r = slice(b * pb, (b + 1) * pb)
        lf = _log_sigmoid(ff_ref[0, r, :].T[:8, :] + bias_ref[...])
        cb = _dot_f32_rhs01(lf, tri, 3) + carry
        carry = cb[:, pb - 1:pb]
        crow[:, r] = cb
        v = p_ref[0, r, 2 * GW:3 * GW]
        for h in range(NH):
            vx[h, r, :] = v * hmask[h]

    rel_w = (lax.broadcasted_iota(jnp.int32, (TQ, WIDE), 1) - lax.broadcasted_iota(jnp.int32, (TQ, WIDE), 0))
    nfull = (WIDE - TQ) // TQ

    def qk_all_heads(r0, start, width):
        qb = p_ref[0, pl.ds(r0, TQ), 0:GW]
        qx = jnp.concatenate([qb * m for m in hmask], axis=0)
        return _dot_nt(qx, p_ref[0, pl.ds(start, width), GW:2 * GW])

    def wide_start(r0):
        if isinstance(r0, int):
            return max(r0 - (WIDE - TQ), 0)
        return pl.multiple_of(jnp.maximum(r0 - (WIDE - TQ), 0), TQ)

    def wide_qk(i):
        r0 = _aligned(i * TQ, TQ)
        return qk_all_heads(r0, wide_start(r0), WIDE)

    def q_block(i, full_tile, qk_wide):
        r0 = _aligned(i * TQ, TQ)
        s0 = wide_start(r0)
        rows = pl.ds(r0, TQ)

        def mask_wide(s):
            if full_tile:
                diag = jnp.where(rel_w[:, WIDE - TQ:] <= WIDE - TQ, s[:, WIDE - TQ:], NEG_BIG)
                return jnp.concatenate([s[:, :WIDE - TQ], diag], axis=1)
            return jnp.where(rel_w <= r0 - s0, s, NEG_BIG)

        def tiles(qk, start, width):
            return [qk[h * TQ:(h + 1) * TQ] - crow[h:h + 1, pl.ds(start, width)] for h in range(NH)]

        def weighted_values(ps, start, width):
            p_all = jnp.concatenate([p.astype(BF16) for p in ps], axis=1)
            v_all = jnp.concatenate([vx[h, pl.ds(start, width), :] for h in range(NH)], axis=0)
            return _dot(p_all, v_all)

        def per_head_lanes(cols):
            out = cols[0] * hmask_f[0]
            for h in range(1, NH):
                out = out + cols[h] * hmask_f[h]
            return out

        def weight_bound(s_end, ms):
            last = pl.multiple_of(jnp.maximum(s_end - TQ, 0), TQ)
            bound = None
            for h in range(NH):
                c_last = crow[h:h + 1, pl.ds(last, TQ)][:, TQ - 1:TQ]
                b_h = (qk_max + LOGIT_SLACK) - c_last - jnp.min(ms[h], axis=0, keepdims=True)
                bound = b_h if bound is None else jnp.maximum(bound, b_h)
            return bound

        def keep_going(s_end, ms):
            return (jnp.max(weight_bound(s_end, ms)) > EXP_ZERO).astype(jnp.int32)

        ss = [mask_wide(s) for s in tiles(qk_wide, s0, WIDE)]
        ms = [jnp.max(s, axis=-1, keepdims=True) for s in ss]
        ps = [jnp.exp(s - m) for s, m in zip(ss, ms)]
        ls = [jnp.sum(p, axis=-1, keepdims=True) for p in ps]
        acc = weighted_values(ps, s0, WIDE)

        def emit(acc, ls):
            g = p_ref[0, rows, 3 * GW:4 * GW].astype(F32)
            o_ref[0, rows, :] = (acc * per_head_lanes([1.0 / l for l in ls]) * g).astype(BF16)

        emit(acc, ls)
        if full_tile:
            worst_ref[...] = jnp.maximum(worst_ref[...], jnp.where(s0 > 0, weight_bound(s0, ms), NEG_BIG))

        def cond(st):
            return jnp.logical_and(st[0] > 0, st[1] > 0)

        def body(st):
            s_end, _, ms, ls, acc = st
            sb = pl.multiple_of(s_end - TQ, TQ)
            ss = tiles(qk_all_heads(r0, sb, TQ), sb, TQ)
            ms2 = [jnp.maximum(m, jnp.max(s, axis=-1, keepdims=True)) for s, m in zip(ss, ms)]
            ps = [jnp.exp(s - m) for s, m in zip(ss, ms2)]
            alphas = [jnp.exp(m - m2) for m, m2 in zip(ms, ms2)]
            ls2 = [a * l + jnp.sum(p, axis=-1, keepdims=True) for a, l, p in zip(alphas, ls, ps)]
            acc2 = acc * per_head_lanes(alphas) + weighted_values(ps, sb, TQ)
            return sb, keep_going(sb, ms2), tuple(ms2), tuple(ls2), acc2

        def rare_tail():
            @pl.when(jnp.logical_and(s0 > 0, keep_going(s0, ms) > 0))
            def _():
                _, _, _, ls_f, acc_f = lax.while_loop(cond, body, (s0, jnp.int32(1), tuple(ms), tuple(ls), acc))
                emit(acc_f, ls_f)

        return rare_tail

    nblk = seq // TQ
    grp = FOX_UNROLL
    lead = nfull + (nblk - nfull) % (2 * grp)
    worst_ref[...] = jnp.full(worst_ref.shape, NEG_BIG, F32)
    for i in range(lead):
        q_block(i, i >= nfull, wide_qk(i))
    for k in range(grp):
        qk_buf[k] = wide_qk(lead + k)

    def group(first, cur, nxt):
        for k in range(grp):
            qk_buf[nxt * grp + k] = wide_qk(jnp.minimum(first + grp + k, nblk - 1))
        for k in range(grp):
            q_block(first + k, True, qk_buf[cur * grp + k])

    def two_groups(j, _):
        i = lead + 2 * grp * j
        group(i, 0, 1)
        group(i + grp, 1, 0)
        return 0

    lax.fori_loop(0, (nblk - lead) // (2 * grp), two_groups, 0)

    @pl.when(jnp.max(worst_ref[...]) > EXP_ZERO)
    def _():
        def redo(i, _):
            q_block(i, True, wide_qk(i))()
            return 0

        lax.fori_loop(nfull, nblk, redo, 0)


def _fox(p3, f3, bias, gq, gk, bd, tri):
    b, s, _ = p3.shape
    assert s % tri.shape[0] == 0 and s >= WIDE + 2 * FOX_UNROLL * TQ
    kern = functools.partial(_fox_kernel, seq=s)
    c2 = lambda i: (0, 0)
    return pl.pallas_call(
        kern,
        grid=(b,),
        in_specs=[
            pl.BlockSpec((1, s, 4 * GW), lambda i: (i, 0, P_FQ // 4)),
            pl.BlockSpec((1, s, 128), lambda i: (i, 0, F_FF * GW // 128)),
            pl.BlockSpec((8, tri.shape[0]), c2),
            pl.BlockSpec((1, GW), c2),
            pl.BlockSpec((1, GW), c2),
            pl.BlockSpec((GW, GW), c2),
            pl.BlockSpec(tri.shape, c2),
        ],
        out_specs=pl.BlockSpec((1, s, GW), lambda i: (i, 0, 0)),
        out_shape=jax.ShapeDtypeStruct((b, s, GW), BF16),
        scratch_shapes=[
            pltpu.VMEM((NH, s, GW), BF16),
            pltpu.VMEM((8, s), F32),
            pltpu.VMEM((2 * FOX_UNROLL, NH * TQ, WIDE), F32),
            pltpu.VMEM((8, 128), F32),
        ],
        compiler_params=pltpu.CompilerParams(
            dimension_semantics=("arbitrary",), vmem_limit_bytes=VMEM_LIMIT),
    )(p3, f3, bias, gq, gk, bd, tri)


def _sb_kernel(p_ref, to_ref, tof_ref, bd_ref, o_ref, vx, qk_buf, worst_ref, *, seq):
    to = to_ref[...]
    tq, wide, cw = SB_TQ, SB_WIDE, SB_CW
    hmask = [bd_ref[h * HD:h * HD + 1, :] for h in range(NH)]
    pb = 256
    for b in range(seq // pb):
        r = slice(b * pb, (b + 1) * pb)
        v = p_ref[0, r, 2 * GW:3 * GW]
        for h in range(NH):
            vx[h, r, :] = v * hmask[h]

    rel_w = (lax.broadcasted_iota(jnp.int32, (tq, wide), 1) - lax.broadcasted_iota(jnp.int32, (tq, wide), 0))
    nsub = wide // cw
    nfull = -(-(wide - tq) // tq)

    def log_one_minus_sigmoid(z):
        return -jnp.maximum(z, jnp.log(1.0 + jnp.exp(jnp.minimum(z, SOFTPLUS_LINEAR))))

    def suffix_sums(lom, mat):
        w = lom.shape[1]
        cs = _dot(lom.astype(BF16), mat)
        return cs[:, :w], cs[:, w:]

    def qk_all_heads(r0, start, width):
        qb = p_ref[0, pl.ds(r0, tq), 0:GW]
        qx = jnp.concatenate([qb * m for m in hmask], axis=0)
        return _dot_nt(qx, p_ref[0, pl.ds(start, width), GW:2 * GW])

    def wide_start(r0):
        if isinstance(r0, int):
            return max(r0 - (wide - tq), 0)
        return pl.multiple_of(jnp.maximum(r0 - (wide - tq), 0), tq)

    def wide_qk(i):
        r0 = _aligned(i * tq, tq)
        return qk_all_heads(r0, wide_start(r0), wide)

    def q_block(i, full_tile, qk_wide):
        r0 = _aligned(i * tq, tq)
        s0 = wide_start(r0)
        rows = pl.ds(r0, tq)

        def mask_wide(x):
            if full_tile:
                diag = jnp.where(rel_w[:, wide - cw:] < wide - tq, x[:, wide - cw:], 0.0)
                return jnp.concatenate([x[:, :wide - cw], diag], axis=1)
            return jnp.where(rel_w < r0 - s0, x, 0.0)

        def per_head(qk):
            return [qk[h * tq:(h + 1) * tq] for h in range(NH)]

        def weighted_values(ws, start, width):
            w_all = jnp.concatenate([w.astype(BF16) for w in ws], axis=1)
            v_all = jnp.concatenate([vx[h, pl.ds(start, width), :] for h in range(NH)], axis=0)
            return _dot(w_all, v_all)

        def emit(acc):
            g = p_ref[0, rows, 3 * GW:4 * GW].astype(F32)
            o_ref[0, rows, :] = (acc * g).astype(BF16)

        zs = per_head(qk_wide)
        loms = [log_one_minus_sigmoid(z) for z in zs]
        log_betas = [z + lom for z, lom in zip(zs, loms)]
        sums = [[suffix_sums(mask_wide(lom)[:, c * cw:(c + 1) * cw], to) for c in range(nsub)] for lom in loms]
        carries, ws = [], []
        for h in range(NH):
            between = [None] * nsub
            carry = jnp.zeros((tq, cw), F32)
            for c in reversed(range(nsub)):
                rc, tot = sums[h][c]
                between[c] = rc + carry
                carry = carry + tot
            ws.append(mask_wide(jnp.exp(log_betas[h] + jnp.concatenate(between, axis=1))))
            carries.append(carry)
        acc = weighted_values(ws, s0, wide)
        emit(acc)

        def worst_carry(carries):
            cm = jnp.maximum(jnp.maximum(carries[0], carries[1]), jnp.maximum(carries[2], carries[3]))
            return jnp.max(cm.reshape(tq // 8, 8, cw), axis=0)

        def keep_going(carries):
            return (jnp.max(worst_carry(carries)) > EXP_ZERO).astype(jnp.int32)

        if full_tile:
            worst_ref[0] = jnp.maximum(worst_ref[0], jnp.where(s0 > 0, worst_carry(carries), NEG_BIG))

        def cond(st):
            return jnp.logical_and(st[0] > 0, st[1] > 0)

        def body(st):
            s_end, _, carries, acc = st
            sb = pl.multiple_of(s_end - tq, tq)
            zs = per_head(qk_all_heads(r0, sb, tq))
            loms = [log_one_minus_sigmoid(z) for z in zs]
            sums = [suffix_sums(lom, tof_ref[...]) for lom in loms]
            ws = [jnp.exp(z + lom + rc + cr[:, :tq]) for z, lom, (rc, _), cr in zip(zs, loms, sums, carries)]
            c2 = [cr + tot for cr, (_, tot) in zip(carries, sums)]
            return sb, keep_going(c2), tuple(c2), acc + weighted_values(ws, sb, tq)

        def rare_tail():
            @pl.when(jnp.logical_and(s0 > 0, keep_going(carries) > 0))
            def _():
                st = lax.while_loop(cond, body, (s0, jnp.int32(1), tuple(carries), acc))
                emit(st[3])

        return rare_tail

    nblk = seq // tq
    grp = SB_UNROLL
    lead = nfull + (nblk - nfull) % (2 * grp)
    worst_ref[0] = jnp.full((8, cw), NEG_BIG, F32)
    for i in range(lead):
        q_block(i, i >= nfull, wide_qk(i))
    for k in range(grp):
        qk_buf[k] = wide_qk(lead + k)

    def group(first, cur, nxt):
        for k in range(grp):
            qk_buf[nxt * grp + k] = wide_qk(jnp.minimum(first + grp + k, nblk - 1))
        for k in range(grp):
            q_block(first + k, True, qk_buf[cur * grp + k])

    def two_groups(j, _):
        i = lead + 2 * grp * j
        group(i, 0, 1)
        group(i + grp, 1, 0)
        return 0

    lax.fori_loop(0, (nblk - lead) // (2 * grp), two_groups, 0)

    @pl.when(jnp.max(worst_ref[0]) > EXP_ZERO)
    def _():
        def redo(i, _):
            q_block(i, True, wide_qk(i))()
            return 0

        lax.fori_loop(nfull, nblk, redo, 0)


def _sb(p3, to, tof, bd):
    b, s, _ = p3.shape
    assert s % 256 == 0 and s >= SB_WIDE + 2 * SB_UNROLL * SB_TQ
    kern = functools.partial(_sb_kernel, seq=s)
    return pl.pallas_call(
        kern,
        grid=(b,),
        in_specs=[
            pl.BlockSpec((1, s, 4 * GW), lambda i: (i, 0, P_SQ // 4)),
            pl.BlockSpec(to.shape, lambda i: (0, 0)),
            pl.BlockSpec(tof.shape, lambda i: (0, 0)),
            pl.BlockSpec((GW, GW), lambda i: (0, 0)),
        ],
        out_specs=pl.BlockSpec((1, s, GW), lambda i: (i, 0, 0)),
        out_shape=jax.ShapeDtypeStruct((b, s, GW), BF16),
        scratch_shapes=[
            pltpu.VMEM((NH, s, GW), BF16),
            pltpu.VMEM((2 * SB_UNROLL, NH * SB_TQ, SB_WIDE), F32),
            pltpu.VMEM((1, 8, SB_CW), F32),
        ],
        compiler_params=pltpu.CompilerParams(
            dimension_semantics=("arbitrary",), vmem_limit_bytes=VMEM_LIMIT),
    )(p3, to, tof, bd)


def _hgrn_kernel(hq_ref, hi_ref, hg_ref, hf_ref, lbm_ref, oml_ref, gout_ref, bd_ref, a_ref,
                 mask_ref, hm_ref, maskc_ref, hmc_ref, o_ref, st_ref, *, seq):
    bd = bd_ref[...]
    a_all = a_ref[...]
    hm = hm_ref[...]
    lbm = lbm_ref[...]
    oml = oml_ref[...]
    c = CHUNK
    st_ref[...] = jnp.zeros((GW, GW), F32)

    n = HSB * c
    nlev = mask_ref.shape[0] - 1
    odd =(lax.broadcasted_iota(jnp.int32, (n, GW), 0) & 1) == 1

    def superblock(bi, _):
        r0 = pl.multiple_of(bi * n, n)
        rows = pl.ds(r0, n)
        hf = hf_ref[0, rows, :]
        sg = 1.0 / (1.0 + jnp.exp(-hf))
        f = lbm + oml * sg
        g = jnp.log(f)
        kk = oml * (1.0 - sg)
        q = hq_ref[0, rows, :].astype(F32)
        v = hi_ref[0, rows, :]
        gh = g.astype(BF16)
        gl = (g - gh.astype(F32)).astype(BF16)
        q2 = (q * jnp.where(odd, f, 1.0)).astype(BF16)
        k2 = (kk * jnp.where(odd, 1.0 / f, 1.0)).astype(BF16)

        def scores(qf, kf, l):
            kx = jnp.concatenate([kf] * NH, axis=0) * hm
            return _dot_nt(qf, kx) * mask_ref[l]

        sls = [slice(ci * c, (ci + 1) * c) for ci in range(HSB)]
        exs = [_dot(a_all, jnp.concatenate([gh[sl], gl[sl]], axis=0)) for sl in sls]

        st = st_ref[...]
        o_inter = []
        for ci, sl in enumerate(sls):
            eb = exs[ci][nlev * c:(nlev + 1) * c]
            er = exs[ci][(nlev + 1) * c:(nlev + 2) * c]
            qd = (q[sl] * jnp.exp(eb)).astype(BF16)
            o_inter.append(_dot_nt(qd, st.astype(BF16) * bd))
            upd = _dot_tn(v[sl], (kk[sl] * jnp.exp(er)).astype(BF16))
            st = st * jnp.exp(eb[c - 1:c, :]) + upd
        st_ref[...] = st

        ps = [scores(q2[sl], k2[sl], nlev) for sl in sls]
        qb = q.astype(BF16)
        kb = kk.astype(BF16)
        hm_c = hmc_ref[...]

        def key_rows(a, l):
            return jnp.concatenate([a[lo:lo + COARSE_RUN[l]] for lo in COARSE_STARTS[l]], axis=0)

        pcs = [[] for _ in sls]
        for l in range(NCOARSE):
            for ci, sl in enumerate(sls):
                x = jnp.exp(exs[ci][l * c:(l + 1) * c]).astype(BF16)
                kx = jnp.concatenate([key_rows(kb[sl] * x, l)] * NH, axis=0) * hm_c
                pcs[ci].append(_dot_nt(qb[sl] * x, kx) * maskc_ref[l])
        for l in range(NCOARSE, nlev):
            for ci, sl in enumerate(sls):
                x = jnp.exp(exs[ci][l * c:(l + 1) * c]).astype(BF16)
                ps[ci] = ps[ci] + scores(qb[sl] * x, kb[sl] * x, l)
        outs = []
        for ci, sl in enumerate(sls):
            vc = v[sl]
            vxs = [jnp.concatenate([key_rows(vc, l)] * NH, axis=0) * hm_c for l in range(NCOARSE)]
            vx = jnp.concatenate(vxs + [jnp.concatenate([vc] * NH, axis=0) * hm], axis=0)
            p_all = jnp.concatenate(pcs[ci] + [ps[ci]], axis=1).astype(BF16)
            outs.append(o_inter[ci] + _dot(p_all, vx))

        o = _head_rms(jnp.concatenate(outs, axis=0), gout_ref[...], bd)
        o_ref[0, rows, :] = (o * hg_ref[0, rows, :].astype(F32)).astype(BF16)
        return 0

    lax.fori_loop(0, seq // n, superblock, 0)


def _hgrn(p3, f3, lbm, oml, gout, bd, a_all, masks, hm, maskc, hmc):
    b, s, _ = p3.shape
    assert s % (HSB * CHUNK) == 0
    kern = functools.partial(_hgrn_kernel, seq=s)
    c2 = lambda i: (0, 0)
    return pl.pallas_call(
        kern,
        grid=(b,),
        in_specs=[
            pl.BlockSpec((1, s, GW), lambda i: (i, 0, P_HQ)),
            pl.BlockSpec((1, s, GW), lambda i: (i, 0, P_HI)),
            pl.BlockSpec((1, s, GW), lambda i: (i, 0, P_HG)),
            pl.BlockSpec((1, s, GW), lambda i: (i, 0, F_HF)),
            pl.BlockSpec((1, GW), c2),
            pl.BlockSpec((1, GW), c2),
            pl.BlockSpec((1, GW), c2),
            pl.BlockSpec((GW, GW), c2),
            pl.BlockSpec(a_all.shape, c2),
            pl.BlockSpec(masks.shape, lambda i: (0, 0, 0)),
            pl.BlockSpec(hm.shape, c2),
            pl.BlockSpec(maskc.shape, lambda i: (0, 0, 0)),
            pl.BlockSpec(hmc.shape, c2),
        ],
        out_specs=pl.BlockSpec((1, s, GW), lambda i: (i, 0, 0)),
        out_shape=jax.ShapeDtypeStruct((b, s, GW), BF16),
        scratch_shapes=[pltpu.VMEM((GW, GW), F32)],
        compiler_params=pltpu.CompilerParams(
            dimension_semantics=("arbitrary",), vmem_limit_bytes=VMEM_LIMIT),
    )(p3, p3, p3, f3, lbm, oml, gout, bd, a_all, masks, hm, maskc, hmc)


def _pm_kernel(pv_ref, pg_ref, mq_ref, mg_ref, mem_ref, mng_ref, wkv_ref, gmk_ref, bd_ref,
               wp_ref, ps_ref, win_ref, d_ref, e_ref, ubuf, wbuf, kmem, vxm, *, seq, tq):
    bd = bd_ref[...]
    hmask = [bd[h * HD:h * HD + 1, :] for h in range(NH)]
    hmask_f = [m.astype(F32) for m in hmask]
    halo = 16
    mem = mem_ref[0]
    ms = jnp.mean(mem * mem, axis=-1, keepdims=True)
    mn = (mem * lax.rsqrt(ms + EPS) * mng_ref[...]).astype(BF16)
    kv = _dot(mn, wkv_ref[...])
    kn = _head_rms(kv[:, :GW], gmk_ref[...], bd).astype(BF16)
    vv = kv[:, GW:].astype(BF16)
    nm = kn.shape[0]
    kmem[...] = kn
    for h in range(NH):
        vxm[h * nm:(h + 1) * nm, :] = vv * hmask[h]

    ubuf[0:halo, :] = jnp.zeros((halo, GW), F32)
    ubuf[halo:halo + seq, :] = pv_ref[0]
    win = win_ref[...]
    inv_win = 1.0 / win
    for k in range(3):
        sh = 1 << k
        wbuf[k, 0:halo, :] = jnp.zeros((halo, GW), F32)
        for b in range(seq // tq):
            base = halo + b * tq
            if k == 0:
                wbuf[k, base:base + tq, :] = ubuf[base:base + tq, :] + ubuf[base - sh:base - sh + tq, :]
            else:
                wbuf[k, base:base + tq, :] = (wbuf[k - 1, base:base + tq, :]
                                              + wbuf[k - 1, base - sh:base - sh + tq, :])

    for b in range(seq // tq):
        r = slice(b * tq, (b + 1) * tq)
        base = halo + b * tq
        u = ubuf[base:base + tq, :]
        s2 = wbuf[0, base:base + tq, :]
        s4 = wbuf[1, base:base + tq, :]
        s8 = wbuf[2, base:base + tq, :]
        s16 = s8 + wbuf[2, base - 8:base - 8 + tq, :]
        sw = jnp.where(win == 2.0, s2, jnp.where(win == 4.0, s4, jnp.where(win == 8.0, s8, s16)))
        if b == 0:
            pos = (lax.broadcasted_iota(jnp.int32, (tq, GW), 0) + 1).astype(F32)
            pooled = sw / jnp.minimum(pos, win)
        else:
            pooled = sw * inv_win
        y = _dot((pooled - u).astype(BF16), wp_ref[...]) * ps_ref[...]
        d_ref[0, r, :] = (y * pg_ref[0, r, :].astype(F32)).astype(BF16)

        qn = mq_ref[0, r, :]
        s_all = _dot_nt(jnp.concatenate([qn * m for m in hmask], axis=0), kmem[...])
        ss = [s_all[h * tq:(h + 1) * tq] for h in range(NH)]
        ps = [jnp.exp(s - jnp.max(s, axis=-1, keepdims=True)) for s in ss]
        ls = [jnp.sum(p, axis=-1, keepdims=True) for p in ps]
        inv_l = (1.0 / ls[0]) * hmask_f[0]
        for h in range(1, NH):
            inv_l = inv_l + (1.0 / ls[h]) * hmask_f[h]
        oe = _dot(jnp.concatenate([p.astype(BF16) for p in ps], axis=1), vxm[...]) * inv_l
        e_ref[0, r, :] = (oe * mg_ref[0, r, :].astype(F32)).astype(BF16)


def _pm(p3, f3, mem, mng, wkv, gmk, bd, wp, ps, win, tq):
    b, s, _ = p3.shape
    nm = mem.shape[1]
    kern = functools.partial(_pm_kernel, seq=s, tq=tq)
    c2 = lambda i: (0, 0)
    return pl.pallas_call(
        kern,
        grid=(b,),
        in_specs=[
            pl.BlockSpec((1, s, GW), lambda i: (i, 0, F_PV)),
            pl.BlockSpec((1, s, GW), lambda i: (i, 0, P_PG)),
            pl.BlockSpec((1, s, GW), lambda i: (i, 0, P_MQ)),
            pl.BlockSpec((1, s, GW), lambda i: (i, 0, P_MG)),
            pl.BlockSpec((1, nm, D_MODEL), lambda i: (i, 0, 0)),
            pl.BlockSpec((1, D_MODEL), c2),
            pl.BlockSpec((D_MODEL, 2 * GW), c2),
            pl.BlockSpec((1, GW), c2),
            pl.BlockSpec((GW, GW), c2),
            pl.BlockSpec((GW, GW), c2),
            pl.BlockSpec((1, GW), c2),
            pl.BlockSpec((1, GW), c2),
        ],
        out_specs=[
            pl.BlockSpec((1, s, GW), lambda i: (i, 0, 0)),
            pl.BlockSpec((1, s, GW), lambda i: (i, 0, 0)),
        ],
        out_shape=[
            jax.ShapeDtypeStruct((b, s, GW), BF16),
            jax.ShapeDtypeStruct((b, s, GW), BF16),
        ],
        scratch_shapes=[
            pltpu.VMEM((s + 16, GW), F32),
            pltpu.VMEM((3, s + 16, GW), F32),
            pltpu.VMEM((nm, GW), BF16),
            pltpu.VMEM((NH * nm, GW), BF16),
        ],
        compiler_params=pltpu.CompilerParams(
            dimension_semantics=("arbitrary",), vmem_limit_bytes=VMEM_LIMIT),
    )(f3, p3, p3, p3, mem, mng, wkv, gmk, bd, wp, ps, win)


def _out_kernel(a_ref, b_ref, c_ref, d_ref, e_ref, w_ref, x_ref, o_ref):
    mixed = jnp.concatenate([a_ref[...], b_ref[...], c_ref[...], d_ref[...], e_ref[...]], axis=1)
    o_ref[...] = x_ref[...] + _dot(mixed, w_ref[...])


def _out(parts, w, x2, tm):
    m = x2.shape[0]
    gspec = pl.BlockSpec((tm, GW), lambda i: (i, 0))
    return pl.pallas_call(
        _out_kernel,
        grid=(m // tm,),
        in_specs=[gspec] * 5 + [
            pl.BlockSpec((5 * GW, D_MODEL), lambda i: (0, 0)),
            pl.BlockSpec((tm, D_MODEL), lambda i: (i, 0)),
        ],
        out_specs=pl.BlockSpec((tm, D_MODEL), lambda i: (i, 0)),
        out_shape=jax.ShapeDtypeStruct((m, D_MODEL), F32),
        compiler_params=pltpu.CompilerParams(
            dimension_semantics=("arbitrary",), vmem_limit_bytes=VMEM_LIMIT),
    )(*parts, w, x2)


def _out_proj_kernel(a_ref, b_ref, c_ref, d_ref, e_ref, w_ref, x_ref, g_ref, wb_ref, wf_ref, hgain_ref, bd_ref,
                     o_ref, pb_ref, pf_ref):
    _out_kernel(a_ref, b_ref, c_ref, d_ref, e_ref, w_ref, x_ref, o_ref)
    _proj_kernel(o_ref, g_ref, wb_ref, wf_ref, hgain_ref, bd_ref, pb_ref, pf_ref)


def _out_proj(parts, w, x2, g, wb, wf, hgain, bd, tm):
    m = x2.shape[0]
    gspec = pl.BlockSpec((tm, GW), lambda i: (i, 0))
    c2 = lambda i: (0, 0)
    return pl.pallas_call(
        _out_proj_kernel,
        grid=(m // tm,),
        in_specs=[gspec] * 5 + [
            pl.BlockSpec((5 * GW, D_MODEL), c2),
            pl.BlockSpec((tm, D_MODEL), lambda i: (i, 0)),
            pl.BlockSpec((1, D_MODEL), c2),
            pl.BlockSpec((D_MODEL, NP_COLS), c2),
            pl.BlockSpec((D_MODEL, NF_COLS), c2),
            pl.BlockSpec((8, GW), c2),
            pl.BlockSpec((GW, GW), c2),
        ],
        out_specs=[
            pl.BlockSpec((tm, D_MODEL), lambda i: (i, 0)),
            pl.BlockSpec((tm, NP_COLS), lambda i: (i, 0)),
            pl.BlockSpec((tm, NF_COLS), lambda i: (i, 0)),
        ],
        out_shape=[
            jax.ShapeDtypeStruct((m, D_MODEL), F32),
            jax.ShapeDtypeStruct((m, NP_COLS), BF16),
            jax.ShapeDtypeStruct((m, NF_COLS), F32),
        ],
        compiler_params=pltpu.CompilerParams(
            dimension_semantics=("arbitrary",), vmem_limit_bytes=VMEM_LIMIT),
    )(*parts, w, x2, g, wb, wf, hgain, bd)


def _tile_heads(g):
    return jnp.tile(g.astype(F32), NH).reshape(1, GW)


def kernel(x, mem, norm_g, w_in, fox_f_bias, fox_q_norm, fox_k_norm, hgrn_lb_logits, hgrn_out_norm,
           pool_w, pool_scale, mem_norm_g, mem_w_kv, mem_q_norm, mem_k_norm, w_out):
    bsz, seq, _ = x.shape
    depth = w_in.shape[0]
    m = bsz * seq
    tq = 256
    tm = 512
    tm_out = 1024

    pr = jax.nn.softmax(hgrn_lb_logits.astype(F32), axis=0)
    lower_bounds = jnp.clip(jnp.cumsum(pr, axis=0) - pr[0:1], 0.0, 1.0 - 1e-6)

    bd_np = _bd_ones()
    bd = jnp.asarray(bd_np, BF16)
    tri = jnp.asarray(np.triu(np.ones((tq, tq), np.float32)), BF16)

    def suffix_and_ones(w):
        jj = np.arange(w)
        suffix = (jj[:, None] > jj[None, :]).astype(np.float32)
        return jnp.asarray(np.concatenate([suffix, np.ones((w, SB_CW), np.float32)], axis=1), BF16)

    to = suffix_and_ones(SB_CW)
    tof = suffix_and_ones(SB_TQ)
    a_np, masks_np, hm_np, maskc_np, hmc_np = _hgrn_constants()
    a_all = jnp.asarray(a_np, BF16)
    masks = jnp.asarray(masks_np, F32)
    hm = jnp.asarray(hm_np, BF16)
    maskc = jnp.asarray(maskc_np, F32)
    hmc = jnp.asarray(hmc_np, BF16)
    win = jnp.asarray(np.repeat(np.array(POOL_WINDOWS, np.float32), HD).reshape(1, GW))

    g = GW
    wb_all, wf_all = _regroup_w_in(w_in)
    x2 = x.reshape(m, D_MODEL)
    norm_gs = [norm_g[l].reshape(1, D_MODEL).astype(F32) for l in range(depth)]
    head_gains = [jnp.concatenate([_tile_heads(fox_q_norm[l]), _tile_heads(fox_k_norm[l]),
                                   _tile_heads(mem_q_norm[l]), jnp.zeros((5, GW), F32)], axis=0)
                  for l in range(depth)]
    pb, pf = _proj(x2, norm_gs[0], wb_all[0], wf_all[0], head_gains[0], bd, tm)
    for l in range(depth):
        p3 = pb.reshape(bsz, seq, NP_COLS)
        f3 = pf.reshape(bsz, seq, NF_COLS)

        bias = jnp.broadcast_to(jnp.pad(fox_f_bias[l].astype(F32), (0, 8 - NH))[:, None], (8, tq))
        out_a = _fox(p3, f3, bias, _tile_heads(fox_q_norm[l]), _tile_heads(fox_k_norm[l]), bd, tri)
        out_b = _sb(p3, to, tof, bd)

        lb = lower_bounds[l].reshape(1, g)
        out_c = _hgrn(p3, f3, jnp.maximum(lb, LB_FLOOR), 1.0 - lb,
                      hgrn_out_norm[l].reshape(1, g).astype(F32), bd, a_all, masks, hm, maskc, hmc)

        wp = jax.scipy.linalg.block_diag(*[pool_w[l, i] for i in range(len(POOL_WINDOWS))]).astype(BF16)
        out_d, out_e = _pm(p3, f3, mem, mem_norm_g[l].reshape(1, D_MODEL).astype(F32),
                           mem_w_kv[l].astype(BF16), _tile_heads(mem_k_norm[l]),
                           bd, wp, pool_scale[l].reshape(1, g).astype(F32), win, tq)

        parts = [o.reshape(m, g) for o in (out_a, out_b, out_c, out_d, out_e)]
        if l + 1 < depth:
            x2, pb, pf = _out_proj(parts, w_out[l].astype(BF16), x2, norm_gs[l + 1],
                                   wb_all[l + 1], wf_all[l + 1], head_gains[l + 1], bd, tm)
        else:
            x2 = _out(parts, w_out[l].astype(BF16), x2, tm_out)
    return x2.reshape(bsz, seq, D_MODEL)
```

```python
import functools

import numpy as np
import jax
import jax.numpy as jnp
from jax import lax
from jax.experimental import pallas as pl
from jax.experimental.pallas import tpu as pltpu

F32 = jnp.float32
BF16 = jnp.bfloat16

D_MODEL = 1024
GW = 256
NH = 4
HD = 64
CHUNK = 64
POOL_WINDOWS = (2, 4, 8, 16)
EPS = 1e-6
NEG_BIG = -1e30
LB_FLOOR = 1e-30
SCALE = HD ** -0.5

NP_COLS = 14 * GW
P_FQ, P_FK, P_FV, P_FG, P_SQ, P_SK, P_SV, P_SG, P_HQ, P_HI, P_HG, P_PG, P_MQ, P_MG = range(14)
NF_COLS = 2 * GW + 128
F_HF, F_PV, F_FF = range(3)
VMEM_LIMIT = 56 * 1024 * 1024

TQ = 128
WIDE = 3 * TQ
EXP_ZERO = -104.0
NORM_SLACK = 1.01
LOGIT_SLACK = 0.05
FOX_UNROLL = 2
NCOARSE = 3
COARSE_STARTS = ((0,), (0, 32), (0, 16, 32, 48))
COARSE_RUN = (32, 16, 8)
HSB = 8
SB_TQ = 128
SB_WIDE = 384
SB_CW = 128
SB_UNROLL = 2
SOFTPLUS_LINEAR = 80.0


def _dot(a, b):
    return jnp.dot(a, b, preferred_element_type=F32)


def _dot_nt(a, b):
    return lax.dot_general(a, b, (((1,), (1,)), ((), ())), preferred_element_type=F32)


def _dot_tn(a, b):
    return lax.dot_general(a, b, (((0,), (0,)), ((), ())), preferred_element_type=F32)


def _aligned(x, m):
    return x if isinstance(x, int) else pl.multiple_of(x, m)


def _split_bf16(x, n):
    parts = []
    r = x
    for i in range(n):
        p = r.astype(BF16)
        parts.append(p)
        if i + 1 < n:
            r = r - p.astype(F32)
    return parts


def _dot_f32_rhs01(x, m01, n=3):
    acc = None
    for p in _split_bf16(x, n):
        t = _dot(p, m01)
        acc = t if acc is None else acc + t
    return acc


def _silu(x):
    return x / (1.0 + jnp.exp(-x))


def _log_sigmoid(x):
    return jnp.minimum(x, 0.0) - jnp.log(1.0 + jnp.exp(-jnp.abs(x)))


def _head_rms(x, gain, bd):
    ss = _dot((x * x).astype(BF16), bd)
    return x * lax.rsqrt(ss * (1.0 / HD) + EPS) * gain


def _bd_ones():
    h = np.arange(GW) // HD
    return (h[:, None] == h[None, :]).astype(np.float32)


def _hgrn_constants():
    c = CHUNK
    t = np.arange(c)
    j = np.arange(c)[None, :]
    mats = []
    masks = []
    for n in (64, 32, 16, 8, 4):
        blk, pos = t // n, t % n
        ref = blk * n + n // 2 - 1
        aq = (pos[:, None] >= n // 2) & (j > ref[:, None]) & (j <= t[:, None])
        ak = (pos[:, None] < n // 2) & (j > t[:, None]) & (j <= ref[:, None])
        mats.append((aq | ak).astype(np.float32))
        m = (blk[:, None] == blk[None, :]) & (pos[:, None] >= n // 2) & (pos[None, :] < n // 2)
        masks.append(m.astype(np.float32))
    m = ((t[:, None] // 2) == (t[None, :] // 2)) & (t[None, :] <= t[:, None])
    masks.append(m.astype(np.float32))
    mats.append((j <= t[:, None]).astype(np.float32))
    mats.append((j > t[:, None]).astype(np.float32))
    a_all = np.concatenate(mats, axis=0)
    a_all = np.concatenate([a_all, a_all], axis=1)
    masks = np.stack([np.tile(m, (1, NH)) for m in masks])
    total = masks[:, :, :c].sum(0)
    assert np.array_equal(total, np.tril(np.ones((c, c), np.float32)))
    hm = (np.arange(NH * c)[:, None] // c == np.arange(GW)[None, :] // HD).astype(np.float32)
    maskc = []
    for l, n in enumerate((64, 32, 16, 8, 4)[:NCOARSE]):
        s = np.concatenate([np.arange(lo, lo + COARSE_RUN[l]) for lo in COARSE_STARTS[l]])
        m = (t[:, None] // n == s[None, :] // n) & (t[:, None] % n >= n // 2)
        maskc.append(np.tile(m.astype(np.float32), (1, NH)))
    hmc = (np.arange(NH * (c // 2))[:, None] // (c // 2) == np.arange(GW)[None, :] // HD).astype(np.float32)
    return a_all, masks, hm, np.stack(maskc), hmc


def _regroup_kernel(w_ref, wb_ref, wf_ref):
    g = GW
    o_ff, o_sb, o_hg, o_pl, o_mm = 4 * g, 4 * g + NH, 8 * g + NH, 12 * g + NH, 14 * g + NH
    w = w_ref[0]
    bf_src = {P_FQ: 0, P_FK: g, P_FV: 2 * g, P_FG: 3 * g,
              P_SQ: o_sb, P_SK: o_sb + g, P_SV: o_sb + 2 * g, P_SG: o_sb + 3 * g,
              P_HQ: o_hg, P_HI: o_hg + 2 * g, P_HG: o_hg + 3 * g,
              P_PG: o_pl + g, P_MQ: o_mm, P_MG: o_mm + g}
    f_src = {F_HF: o_hg + g, F_PV: o_pl}
    for dst, c0 in bf_src.items():
        wb_ref[0, :, dst * g:(dst + 1) * g] = w[:, c0:c0 + g].astype(BF16)
    for dst, c0 in f_src.items():
        wf_ref[0, :, dst * g:(dst + 1) * g] = w[:, c0:c0 + g].astype(BF16)
    ff = jnp.concatenate([w[:, o_ff:o_ff + NH], jnp.zeros((w.shape[0], 128 - NH), F32)], axis=1)
    wf_ref[0, :, F_FF * g:F_FF * g + 128] = ff.astype(BF16)


def _regroup_w_in(w_in):
    depth, d, n = w_in.shape
    tr = 128
    return pl.pallas_call(
        _regroup_kernel,
        grid=(depth, d // tr),
        in_specs=[pl.BlockSpec((1, tr, n), lambda l, i: (l, i, 0))],
        out_specs=[
            pl.BlockSpec((1, tr, NP_COLS), lambda l, i: (l, i, 0)),
            pl.BlockSpec((1, tr, NF_COLS), lambda l, i: (l, i, 0)),
        ],
        out_shape=[
            jax.ShapeDtypeStruct((depth, d, NP_COLS), BF16),
            jax.ShapeDtypeStruct((depth, d, NF_COLS), BF16),
        ],
        compiler_params=pltpu.CompilerParams(
            dimension_semantics=("arbitrary", "arbitrary"), vmem_limit_bytes=VMEM_LIMIT),
    )(w_in)


def _proj_kernel(x_ref, g_ref, wb_ref, wf_ref, hgain_ref, bd_ref, pb_ref, pf_ref):
    x = x_ref[...]
    ms = jnp.mean(x * x, axis=-1, keepdims=True)
    h = (x * lax.rsqrt(ms + EPS) * g_ref[...]).astype(BF16)
    gains = hgain_ref[...]
    bd = bd_ref[...]
    normed = {P_FQ: (0, SCALE), P_FK: (1, 1.0), P_MQ: (2, SCALE)}
    post = {P_SQ: lambda r: r * SCALE}
    post.update({gi: _silu for gi in (P_FG, P_SG, P_HQ, P_HG, P_PG, P_MG)})

    def cols(gi):
        return slice(gi * GW, (gi + 1) * GW)

    raw = {gi: _dot(h, wb_ref[:, cols(gi)]) for gi in normed}
    for gi in range(NP_COLS // GW):
        if gi not in normed:
            r = _dot(h, wb_ref[:, cols(gi)])
            pb_ref[:, cols(gi)] = post.get(gi, lambda r: r)(r).astype(BF16)
    pf_ref[...] = _dot(h, wf_ref[...])
    for gi, (row, scale) in normed.items():
        pb_ref[:, cols(gi)] = (_head_rms(raw[gi], gains[row:row + 1], bd) * scale).astype(BF16)


def _proj(x2, g, wb, wf, hgain, bd, tm):
    m = x2.shape[0]
    return pl.pallas_call(
        _proj_kernel,
        grid=(m // tm,),
        in_specs=[
            pl.BlockSpec((tm, D_MODEL), lambda i: (i, 0)),
            pl.BlockSpec((1, D_MODEL), lambda i: (0, 0)),
            pl.BlockSpec((D_MODEL, NP_COLS), lambda i: (0, 0)),
            pl.BlockSpec((D_MODEL, NF_COLS), lambda i: (0, 0)),
            pl.BlockSpec((8, GW), lambda i: (0, 0)),
            pl.BlockSpec((GW, GW), lambda i: (0, 0)),
        ],
        out_specs=[
            pl.BlockSpec((tm, NP_COLS), lambda i: (i, 0)),
            pl.BlockSpec((tm, NF_COLS), lambda i: (i, 0)),
        ],
        out_shape=[
            jax.ShapeDtypeStruct((m, NP_COLS), BF16),
            jax.ShapeDtypeStruct((m, NF_COLS), F32),
        ],
        compiler_params=pltpu.CompilerParams(
            dimension_semantics=("arbitrary",), vmem_limit_bytes=VMEM_LIMIT),
    )(x2, g, wb, wf, hgain, bd)


def _fox_kernel(p_ref, ff_ref, bias_ref, gq_ref, gk_ref, bd_ref, tri_ref, o_ref,
                kt, vx, crow, qk_buf, worst_ref, *, seq):
    bd = bd_ref[...]
    tri = tri_ref[...]
    pb = tri.shape[0]
    hmask = [bd[h * HD:h * HD + 1, :] for h in range(NH)]
    hmask_f = [m.astype(F32) for m in hmask]
    qk_max = (HD * SCALE * NORM_SLACK) * (jnp.max(jnp.abs(gq_ref[...]), axis=-1, keepdims=True)
                                          * jnp.max(jnp.abs(gk_ref[...]), axis=-1, keepdims=True))
    carry = jnp.zeros((8, 1), F32)
    for b in range(seq // pb):
        r = slice(b * pb, (b + 1) * pb)
        lf = _log_sigmoid(ff_ref[0, r, :].T[:8, :] + bias_ref[...])
        cb = _dot_f32_rhs01(lf, tri, 3) + carry
        carry = cb[:, pb - 1:pb]
        crow[:, r] = cb
        kt[:, r] = p_ref[0, r, GW:2 * GW].T
        v = p_ref[0, r, 2 * GW:3 * GW]
        for h in range(NH):
            vx[h, r, :] = v * hmask[h]

    rel_w = (lax.broadcasted_iota(jnp.int32, (TQ, WIDE), 1) - lax.broadcasted_iota(jnp.int32, (TQ, WIDE), 0))
    nfull = (WIDE - TQ) // TQ

    def qk_all_heads(r0, start, width):
        qb = p_ref[0, pl.ds(r0, TQ), 0:GW]
        qx = jnp.concatenate([qb * m for m in hmask], axis=0)
        return _dot(qx, kt[:, pl.ds(start, width)])

    def wide_start(r0):
        if isinstance(r0, int):
            return max(r0 - (WIDE - TQ), 0)
        return pl.multiple_of(jnp.maximum(r0 - (WIDE - TQ), 0), TQ)

    def wide_qk(i):
        r0 = _aligned(i * TQ, TQ)
        return qk_all_heads(r0, wide_start(r0), WIDE)

    def q_block(i, full_tile, qk_wide):
        r0 = _aligned(i * TQ, TQ)
        s0 = wide_start(r0)
        rows = pl.ds(r0, TQ)

        def mask_wide(s):
            if full_tile:
                diag = jnp.where(rel_w[:, WIDE - TQ:] <= WIDE - TQ, s[:, WIDE - TQ:], NEG_BIG)
                return jnp.concatenate([s[:, :WIDE - TQ], diag], axis=1)
            return jnp.where(rel_w <= r0 - s0, s, NEG_BIG)

        def tiles(qk, start, width):
            return [qk[h * TQ:(h + 1) * TQ] - crow[h:h + 1, pl.ds(start, width)] for h in range(NH)]

        def weighted_values(ps, start, width):
            p_all = jnp.concatenate([p.astype(BF16) for p in ps], axis=1)
            v_all = jnp.concatenate([vx[h, pl.ds(start, width), :] for h in range(NH)], axis=0)
            return _dot(p_all, v_all)

        def per_head_lanes(cols):
            out = cols[0] * hmask_f[0]
            for h in range(1, NH):
                out = out + cols[h] * hmask_f[h]
            return out

        def weight_bound(s_end, ms):
            last = pl.multiple_of(jnp.maximum(s_end - TQ, 0), TQ)
            bound = None
            for h in range(NH):
                c_last = crow[h:h + 1, pl.ds(last, TQ)][:, TQ - 1:TQ]
                b_h = (qk_max + LOGIT_SLACK) - c_last - jnp.min(ms[h], axis=0, keepdims=True)
                bound = b_h if bound is None else jnp.maximum(bound, b_h)
            return bound

        def keep_going(s_end, ms):
            return (jnp.max(weight_bound(s_end, ms)) > EXP_ZERO).astype(jnp.int32)

        ss = [mask_wide(s) for s in tiles(qk_wide, s0, WIDE)]
        ms = [jnp.max(s, axis=-1, keepdims=True) for s in ss]
        ps = [jnp.exp(s - m) for s, m in zip(ss, ms)]
        ls = [jnp.sum(p, axis=-1, keepdims=True) for p in ps]
        acc = weighted_values(ps, s0, WIDE)

        def emit(acc, ls):
            g = p_ref[0, rows, 3 * GW:4 * GW].astype(F32)
            o_ref[0, rows, :] = (acc * per_head_lanes([1.0 / l for l in ls]) * g).astype(BF16)

        emit(acc, ls)
        if full_tile:
            worst_ref[...] = jnp.maximum(worst_ref[...], jnp.where(s0 > 0, weight_bound(s0, ms), NEG_BIG))

        def cond(st):
            return jnp.logical_and(st[0] > 0, st[1] > 0)

        def body(st):
            s_end, _, ms, ls, acc = st
            sb = pl.multiple_of(s_end - TQ, TQ)
            ss = tiles(qk_all_heads(r0, sb, TQ), sb, TQ)
            ms2 = [jnp.maximum(m, jnp.max(s, axis=-1, keepdims=True)) for s, m in zip(ss, ms)]
            ps = [jnp.exp(s - m) for s, m in zip(ss, ms2)]
            alphas = [jnp.exp(m - m2) for m, m2 in zip(ms, ms2)]
            ls2 = [a * l + jnp.sum(p, axis=-1, keepdims=True) for a, l, p in zip(alphas, ls, ps)]
            acc2 = acc * per_head_lanes(alphas) + weighted_values(ps, sb, TQ)
            return sb, keep_going(sb, ms2), tuple(ms2), tuple(ls2), acc2

        def rare_tail():
            @pl.when(jnp.logical_and(s0 > 0, keep_going(s0, ms) > 0))
            def _():
                _, _, _, ls_f, acc_f = lax.while_loop(cond, body, (s0, jnp.int32(1), tuple(ms), tuple(ls), acc))
                emit(acc_f, ls_f)

        return rare_tail

    nblk = seq // TQ
    grp = FOX_UNROLL
    lead = nfull + (nblk - nfull) % (2 * grp)
    worst_ref[...] = jnp.full(worst_ref.shape, NEG_BIG, F32)
    for i in range(lead):
        q_block(i, i >= nfull, wide_qk(i))
    for k in range(grp):
        qk_buf[k] = wide_qk(lead + k)

    def group(first, cur, nxt):
        for k in range(grp):
            qk_buf[nxt * grp + k] = wide_qk(jnp.minimum(first + grp + k, nblk - 1))
        for k in range(grp):
            q_block(first + k, True, qk_buf[cur * grp + k])

    def two_groups(j, _):
        i = lead + 2 * grp * j
        group(i, 0, 1)
        group(i + grp, 1, 0)
        return 0

    lax.fori_loop(0, (nblk - lead) // (2 * grp), two_groups, 0)

    @pl.when(jnp.max(worst_ref[...]) > EXP_ZERO)
    def _():
        def redo(i, _):
            q_block(i, True, wide_qk(i))()
            return 0

        lax.fori_loop(nfull, nblk, redo, 0)


def _fox(p3, f3, bias, gq, gk, bd, tri):
    b, s, _ = p3.shape
    assert s % tri.shape[0] == 0 and s >= WIDE + 2 * FOX_UNROLL * TQ
    kern = functools.partial(_fox_kernel, seq=s)
    c2 = lambda i: (0, 0)
    return pl.pallas_call(
        kern,
        grid=(b,),
        in_specs=[
            pl.BlockSpec((1, s, 4 * GW), lambda i: (i, 0, P_FQ // 4)),
            pl.BlockSpec((1, s, 128), lambda i: (i, 0, F_FF * GW // 128)),
            pl.BlockSpec((8, tri.shape[0]), c2),
            pl.BlockSpec((1, GW), c2),
            pl.BlockSpec((1, GW), c2),
            pl.BlockSpec((GW, GW), c2),
            pl.BlockSpec(tri.shape, c2),
        ],
        out_specs=pl.BlockSpec((1, s, GW), lambda i: (i, 0, 0)),
        out_shape=jax.ShapeDtypeStruct((b, s, GW), BF16),
        scratch_shapes=[
            pltpu.VMEM((GW, s), BF16),
            pltpu.VMEM((NH, s, GW), BF16),
            pltpu.VMEM((8, s), F32),
            pltpu.VMEM((2 * FOX_UNROLL, NH * TQ, WIDE), F32),
            pltpu.VMEM((8, 128), F32),
        ],
        compiler_params=pltpu.CompilerParams(
            dimension_semantics=("arbitrary",), vmem_limit_bytes=VMEM_LIMIT),
    )(p3, f3, bias, gq, gk, bd, tri)


def _sb_kernel(p_ref, to_ref, tof_ref, bd_ref, o_ref, kt, vx, qk_buf, worst_ref, *, seq):
    to = to_ref[...]
    tq, wide, cw = SB_TQ, SB_WIDE, SB_CW
    hmask = [bd_ref[h * HD:h * HD + 1, :] for h in range(NH)]
    pb = 256
    for b in range(seq // pb):
        r = slice(b * pb, (b + 1) * pb)
        kt[:, r] = p_ref[0, r, GW:2 * GW].T
        v = p_ref[0, r, 2 * GW:3 * GW]
        for h in range(NH):
            vx[h, r, :] = v * hmask[h]

    rel_w = (lax.broadcasted_iota(jnp.int32, (tq, wide), 1) - lax.broadcasted_iota(jnp.int32, (tq, wide), 0))
    nsub = wide // cw
    nfull = -(-(wide - tq) // tq)

    def log_one_minus_sigmoid(z):
        return -jnp.maximum(z, jnp.log(1.0 + jnp.exp(jnp.minimum(z, SOFTPLUS_LINEAR))))

    def suffix_sums(lom, mat):
        w = lom.shape[1]
        cs = _dot(lom.astype(BF16), mat)
        return cs[:, :w], cs[:, w:]

    def qk_all_heads(r0, start, width):
        qb = p_ref[0, pl.ds(r0, tq), 0:GW]
        qx = jnp.concatenate([qb * m for m in hmask], axis=0)
        return _dot(qx, kt[:, pl.ds(start, width)])

    def wide_start(r0):
        if isinstance(r0, int):
            return max(r0 - (wide - tq), 0)
        return pl.multiple_of(jnp.maximum(r0 - (wide - tq), 0), tq)

    def wide_qk(i):
        r0 = _aligned(i * tq, tq)
        return qk_all_heads(r0, wide_start(r0), wide)

    def q_block(i, full_tile, qk_wide):
        r0 = _aligned(i * tq, tq)
        s0 = wide_start(r0)
        rows = pl.ds(r0, tq)

        def mask_wide(x):
            if full_tile:
                diag = jnp.where(rel_w[:, wide - cw:] < wide - tq, x[:, wide - cw:], 0.0)
                return jnp.concatenate([x[:, :wide - cw], diag], axis=1)
            return jnp.where(rel_w < r0 - s0, x, 0.0)

        def per_head(qk):
            return [qk[h * tq:(h + 1) * tq] for h in range(NH)]

        def weighted_values(ws, start, width):
            w_all = jnp.concatenate([w.astype(BF16) for w in ws], axis=1)
            v_all = jnp.concatenate([vx[h, pl.ds(start, width), :] for h in range(NH)], axis=0)
            return _dot(w_all, v_all)

        def emit(acc):
            g = p_ref[0, rows, 3 * GW:4 * GW].astype(F32)
            o_ref[0, rows, :] = (acc * g).astype(BF16)

        zs = per_head(qk_wide)
        loms = [log_one_minus_sigmoid(z) for z in zs]
        log_betas = [z + lom for z, lom in zip(zs, loms)]
        sums = [[suffix_sums(mask_wide(lom)[:, c * cw:(c + 1) * cw], to) for c in range(nsub)] for lom in loms]
        carries, ws = [], []
        for h in range(NH):
            between = [None] * nsub
            carry = jnp.zeros((tq, cw), F32)
            for c in reversed(range(nsub)):
                rc, tot = sums[h][c]
                between[c] = rc + carry
                carry = carry + tot
            ws.append(mask_wide(jnp.exp(log_betas[h] + jnp.concatenate(between, axis=1))))
            carries.append(carry)
        acc = weighted_values(ws, s0, wide)
        emit(acc)

        def worst_carry(carries):
            cm = jnp.maximum(jnp.maximum(carries[0], carries[1]), jnp.maximum(carries[2], carries[3]))
            return jnp.max(cm.reshape(tq // 8, 8, cw), axis=0)

        def keep_going(carries):
            return (jnp.max(worst_carry(carries)) > EXP_ZERO).astype(jnp.int32)

        if full_tile:
            worst_ref[0] = jnp.maximum(worst_ref[0], jnp.where(s0 > 0, worst_carry(carries), NEG_BIG))

        def cond(st):
            return jnp.logical_and(st[0] > 0, st[1] > 0)

        def body(st):
            s_end, _, carries, acc = st
            sb = pl.multiple_of(s_end - tq, tq)
            zs = per_head(qk_all_heads(r0, sb, tq))
            loms = [log_one_minus_sigmoid(z) for z in zs]
            sums = [suffix_sums(lom, tof_ref[...]) for lom in loms]
            ws = [jnp.exp(z + lom + rc + cr[:, :tq]) for z, lom, (rc, _), cr in zip(zs, loms, sums, carries)]
            c2 = [cr + tot for cr, (_, tot) in zip(carries, sums)]
            return sb, keep_going(c2), tuple(c2), acc + weighted_values(ws, sb, tq)

        def rare_tail():
            @pl.when(jnp.logical_and(s0 > 0, keep_going(carries) > 0))
            def _():
                st = lax.while_loop(cond, body, (s0, jnp.int32(1), tuple(carries), acc))
                emit(st[3])

        return rare_tail

    nblk = seq // tq
    grp = SB_UNROLL
    lead = nfull + (nblk - nfull) % (2 * grp)
    worst_ref[0] = jnp.full((8, cw), NEG_BIG, F32)
    for i in range(lead):
        q_block(i, i >= nfull, wide_qk(i))
    for k in range(grp):
        qk_buf[k] = wide_qk(lead + k)

    def group(first, cur, nxt):
        for k in range(grp):
            qk_buf[nxt * grp + k] = wide_qk(jnp.minimum(first + grp + k, nblk - 1))
        for k in range(grp):
            q_block(first + k, True, qk_buf[cur * grp + k])

    def two_groups(j, _):
        i = lead + 2 * grp * j
        group(i, 0, 1)
        group(i + grp, 1, 0)
        return 0

    lax.fori_loop(0, (nblk - lead) // (2 * grp), two_groups, 0)

    @pl.when(jnp.max(worst_ref[0]) > EXP_ZERO)
    def _():
        def redo(i, _):
            q_block(i, True, wide_qk(i))()
            return 0

        lax.fori_loop(nfull, nblk, redo, 0)


def _sb(p3, to, tof, bd):
    b, s, _ = p3.shape
    assert s % 256 == 0 and s >= SB_WIDE + 2 * SB_UNROLL * SB_TQ
    kern = functools.partial(_sb_kernel, seq=s)
    return pl.pallas_call(
        kern,
        grid=(b,),
        in_specs=[
            pl.BlockSpec((1, s, 4 * GW), lambda i: (i, 0, P_SQ // 4)),
            pl.BlockSpec(to.shape, lambda i: (0, 0)),
            pl.BlockSpec(tof.shape, lambda i: (0, 0)),
            pl.BlockSpec((GW, GW), lambda i: (0, 0)),
        ],
        out_specs=pl.BlockSpec((1, s, GW), lambda i: (i, 0, 0)),
        out_shape=jax.ShapeDtypeStruct((b, s, GW), BF16),
        scratch_shapes=[
            pltpu.VMEM((GW, s), BF16),
            pltpu.VMEM((NH, s, GW), BF16),
            pltpu.VMEM((2 * SB_UNROLL, NH * SB_TQ, SB_WIDE), F32),
            pltpu.VMEM((1, 8, SB_CW), F32),
        ],
        compiler_params=pltpu.CompilerParams(
            dimension_semantics=("arbitrary",), vmem_limit_bytes=VMEM_LIMIT),
    )(p3, to, tof, bd)


def _hgrn_kernel(hq_ref, hi_ref, hg_ref, hf_ref, lbm_ref, oml_ref, gout_ref, bd_ref, a_ref,
                 mask_ref, hm_ref, maskc_ref, hmc_ref, o_ref, st_ref, *, seq):
    bd = bd_ref[...]
    a_all = a_ref[...]
    hm = hm_ref[...]
    hm_t = hm.T
    lbm = lbm_ref[...]
    oml = oml_ref[...]
    c = CHUNK
    st_ref[...] = jnp.zeros((GW, GW), F32)

    n = HSB * c
    nlev = mask_ref.shape[0] - 1
    odd =(lax.broadcasted_iota(jnp.int32, (n, GW), 0) & 1) == 1

    def superblock(bi, _):
        r0 = pl.multiple_of(bi * n, n)
        rows = pl.ds(r0, n)
        hf = hf_ref[0, rows, :]
        sg = 1.0 / (1.0 + jnp.exp(-hf))
        f = lbm + oml * sg
        g = jnp.log(f)
        kk = oml * (1.0 - sg)
        q = hq_ref[0, rows, :].astype(F32)
        v = hi_ref[0, rows, :]
        gh = g.astype(BF16)
        gl = (g - gh.astype(F32)).astype(BF16)
        q2 = (q * jnp.where(odd, f, 1.0)).astype(BF16)
        k2 = (kk * jnp.where(odd, 1.0 / f, 1.0)).astype(BF16)

        def scores(qf, kf, l):
            kxt = jnp.concatenate([kf] * NH, axis=0).T * hm_t
            return _dot(qf, kxt) * mask_ref[l]

        sls = [slice(ci * c, (ci + 1) * c) for ci in range(HSB)]
        exs = [_dot(a_all, jnp.concatenate([gh[sl], gl[sl]], axis=0)) for sl in sls]

        st = st_ref[...]
        o_inter = []
        for ci, sl in enumerate(sls):
            eb = exs[ci][nlev * c:(nlev + 1) * c]
            er = exs[ci][(nlev + 1) * c:(nlev + 2) * c]
            qd = (q[sl] * jnp.exp(eb)).astype(BF16)
            o_inter.append(_dot(qd, st.T.astype(BF16) * bd))
            upd = _dot_tn(v[sl], (kk[sl] * jnp.exp(er)).astype(BF16))
            st = st * jnp.exp(eb[c - 1:c, :]) + upd
        st_ref[...] = st

        ps = [scores(q2[sl], k2[sl], nlev) for sl in sls]
        qb = q.astype(BF16)
        kb = kk.astype(BF16)
        hm_c = hmc_ref[...]
        hm_ct = hm_c.T

        def key_rows(a, l):
            return jnp.concatenate([a[lo:lo + COARSE_RUN[l]] for lo in COARSE_STARTS[l]], axis=0)

        pcs = [[] for _ in sls]
        for l in range(NCOARSE):
            for ci, sl in enumerate(sls):
                x = jnp.exp(exs[ci][l * c:(l + 1) * c]).astype(BF16)
                kxt = jnp.concatenate([key_rows(kb[sl] * x, l)] * NH, axis=0).T * hm_ct
                pcs[ci].append(_dot(qb[sl] * x, kxt) * maskc_ref[l])
        for l in range(NCOARSE, nlev):
            for ci, sl in enumerate(sls):
                x = jnp.exp(exs[ci][l * c:(l + 1) * c]).astype(BF16)
                ps[ci] = ps[ci] + scores(qb[sl] * x, kb[sl] * x, l)
        outs = []
        for ci, sl in enumerate(sls):
            vc = v[sl]
            vxs = [jnp.concatenate([key_rows(vc, l)] * NH, axis=0) * hm_c for l in range(NCOARSE)]
            vx = jnp.concatenate(vxs + [jnp.concatenate([vc] * NH, axis=0) * hm], axis=0)
            p_all = jnp.concatenate(pcs[ci] + [ps[ci]], axis=1).astype(BF16)
            outs.append(o_inter[ci] + _dot(p_all, vx))

        o = _head_rms(jnp.concatenate(outs, axis=0), gout_ref[...], bd)
        o_ref[0, rows, :] = (o * hg_ref[0, rows, :].astype(F32)).astype(BF16)
        return 0

    lax.fori_loop(0, seq // n, superblock, 0)


def _hgrn(p3, f3, lbm, oml, gout, bd, a_all, masks, hm, maskc, hmc):
    b, s, _ = p3.shape
    assert s % (HSB * CHUNK) == 0
    kern = functools.partial(_hgrn_kernel, seq=s)
    c2 = lambda i: (0, 0)
    return pl.pallas_call(
        kern,
        grid=(b,),
        in_specs=[
            pl.BlockSpec((1, s, GW), lambda i: (i, 0, P_HQ)),
            pl.BlockSpec((1, s, GW), lambda i: (i, 0, P_HI)),
            pl.BlockSpec((1, s, GW), lambda i: (i, 0, P_HG)),
            pl.BlockSpec((1, s, GW), lambda i: (i, 0, F_HF)),
            pl.BlockSpec((1, GW), c2),
            pl.BlockSpec((1, GW), c2),
            pl.BlockSpec((1, GW), c2),
            pl.BlockSpec((GW, GW), c2),
            pl.BlockSpec(a_all.shape, c2),
            pl.BlockSpec(masks.shape, lambda i: (0, 0, 0)),
            pl.BlockSpec(hm.shape, c2),
            pl.BlockSpec(maskc.shape, lambda i: (0, 0, 0)),
            pl.BlockSpec(hmc.shape, c2),
        ],
        out_specs=pl.BlockSpec((1, s, GW), lambda i: (i, 0, 0)),
        out_shape=jax.ShapeDtypeStruct((b, s, GW), BF16),
        scratch_shapes=[pltpu.VMEM((GW, GW), F32)],
        compiler_params=pltpu.CompilerParams(
            dimension_semantics=("arbitrary",), vmem_limit_bytes=VMEM_LIMIT),
    )(p3, p3, p3, f3, lbm, oml, gout, bd, a_all, masks, hm, maskc, hmc)


def _pm_kernel(pv_ref, pg_ref, mq_ref, mg_ref, mem_ref, mng_ref, wkv_ref, gmk_ref, bd_ref,
               wp_ref, ps_ref, win_ref, d_ref, e_ref, ubuf, wbuf, kmem, vxm, *, seq, tq):
    bd = bd_ref[...]
    hmask = [bd[h * HD:h * HD + 1, :] for h in range(NH)]
    hmask_f = [m.astype(F32) for m in hmask]
    halo = 16
    mem = mem_ref[0]
    ms = jnp.mean(mem * mem, axis=-1, keepdims=True)
    mn = (mem * lax.rsqrt(ms + EPS) * mng_ref[...]).astype(BF16)
    kv = _dot(mn, wkv_ref[...])
    kn = _head_rms(kv[:, :GW], gmk_ref[...], bd).astype(BF16)
    vv = kv[:, GW:].astype(BF16)
    nm = kn.shape[0]
    kmem[...] = kn
    for h in range(NH):
        vxm[h * nm:(h + 1) * nm, :] = vv * hmask[h]

    ubuf[0:halo, :] = jnp.zeros((halo, GW), F32)
    ubuf[halo:halo + seq, :] = pv_ref[0]
    win = win_ref[...]
    inv_win = 1.0 / win
    for k in range(3):
        sh = 1 << k
        wbuf[k, 0:halo, :] = jnp.zeros((halo, GW), F32)
        for b in range(seq // tq):
            base = halo + b * tq
            if k == 0:
                wbuf[k, base:base + tq, :] = ubuf[base:base + tq, :] + ubuf[base - sh:base - sh + tq, :]
            else:
                wbuf[k, base:base + tq, :] = (wbuf[k - 1, base:base + tq, :]
                                              + wbuf[k - 1, base - sh:base - sh + tq, :])

    for b in range(seq // tq):
        r = slice(b * tq, (b + 1) * tq)
        base = halo + b * tq
        u = ubuf[base:base + tq, :]
        s2 = wbuf[0, base:base + tq, :]
        s4 = wbuf[1, base:base + tq, :]
        s8 = wbuf[2, base:base + tq, :]
        s16 = s8 + wbuf[2, base - 8:base - 8 + tq, :]
        sw = jnp.where(win == 2.0, s2, jnp.where(win == 4.0, s4, jnp.where(win == 8.0, s8, s16)))
        if b == 0:
            pos = (lax.broadcasted_iota(jnp.int32, (tq, GW), 0) + 1).astype(F32)
            pooled = sw / jnp.minimum(pos, win)
        else:
            pooled = sw * inv_win
        y = _dot((pooled - u).astype(BF16), wp_ref[...]) * ps_ref[...]
        d_ref[0, r, :] = (y * pg_ref[0, r, :].astype(F32)).astype(BF16)

        qn = mq_ref[0, r, :]
        s_all = _dot_nt(jnp.concatenate([qn * m for m in hmask], axis=0), kmem[...])
        ss = [s_all[h * tq:(h + 1) * tq] for h in range(NH)]
        ps = [jnp.exp(s - jnp.max(s, axis=-1, keepdims=True)) for s in ss]
        ls = [jnp.sum(p, axis=-1, keepdims=True) for p in ps]
        inv_l = (1.0 / ls[0]) * hmask_f[0]
        for h in range(1, NH):
            inv_l = inv_l + (1.0 / ls[h]) * hmask_f[h]
        oe = _dot(jnp.concatenate([p.astype(BF16) for p in ps], axis=1), vxm[...]) * inv_l
        e_ref[0, r, :] = (oe * mg_ref[0, r, :].astype(F32)).astype(BF16)


def _pm(p3, f3, mem, mng, wkv, gmk, bd, wp, ps, win, tq):
    b, s, _ = p3.shape
    nm = mem.shape[1]
    kern = functools.partial(_pm_kernel, seq=s, tq=tq)
    c2 = lambda i: (0, 0)
    return pl.pallas_call(
        kern,
        grid=(b,),
        in_specs=[
            pl.BlockSpec((1, s, GW), lambda i: (i, 0, F_PV)),
            pl.BlockSpec((1, s, GW), lambda i: (i, 0, P_PG)),
            pl.BlockSpec((1, s, GW), lambda i: (i, 0, P_MQ)),
            pl.BlockSpec((1, s, GW), lambda i: (i, 0, P_MG)),
            pl.BlockSpec((1, nm, D_MODEL), lambda i: (i, 0, 0)),
            pl.BlockSpec((1, D_MODEL), c2),
            pl.BlockSpec((D_MODEL, 2 * GW), c2),
            pl.BlockSpec((1, GW), c2),
            pl.BlockSpec((GW, GW), c2),
            pl.BlockSpec((GW, GW), c2),
            pl.BlockSpec((1, GW), c2),
            pl.BlockSpec((1, GW), c2),
        ],
        out_specs=[
            pl.BlockSpec((1, s, GW), lambda i: (i, 0, 0)),
            pl.BlockSpec((1, s, GW), lambda i: (i, 0, 0)),
        ],
        out_shape=[
            jax.ShapeDtypeStruct((b, s, GW), BF16),
            jax.ShapeDtypeStruct((b, s, GW), BF16),
        ],
        scratch_shapes=[
            pltpu.VMEM((s + 16, GW), F32),
            pltpu.VMEM((3, s + 16, GW), F32),
            pltpu.VMEM((nm, GW), BF16),
            pltpu.VMEM((NH * nm, GW), BF16),
        ],
        compiler_params=pltpu.CompilerParams(
            dimension_semantics=("arbitrary",), vmem_limit_bytes=VMEM_LIMIT),
    )(f3, p3, p3, p3, mem, mng, wkv, gmk, bd, wp, ps, win)


def _out_kernel(a_ref, b_ref, c_ref, d_ref, e_ref, w_ref, x_ref, o_ref):
    mixed = jnp.concatenate([a_ref[...], b_ref[...], c_ref[...], d_ref[...], e_ref[...]], axis=1)
    o_ref[...] = x_ref[...] + _dot(mixed, w_ref[...])


def _out(parts, w, x2, tm):
    m = x2.shape[0]
    gspec = pl.BlockSpec((tm, GW), lambda i: (i, 0))
    return pl.pallas_call(
        _out_kernel,
        grid=(m // tm,),
        in_specs=[gspec] * 5 + [
            pl.BlockSpec((5 * GW, D_MODEL), lambda i: (0, 0)),
            pl.BlockSpec((tm, D_MODEL), lambda i: (i, 0)),
        ],
        out_specs=pl.BlockSpec((tm, D_MODEL), lambda i: (i, 0)),
        out_shape=jax.ShapeDtypeStruct((m, D_MODEL), F32),
        compiler_params=pltpu.CompilerParams(
            dimension_semantics=("arbitrary",), vmem_limit_bytes=VMEM_LIMIT),
    )(*parts, w, x2)


def _out_proj_kernel(a_ref, b_ref, c_ref, d_ref, e_ref, w_ref, x_ref, g_ref, wb_ref, wf_ref, hgain_ref, bd_ref,
                     o_ref, pb_ref, pf_ref):
    _out_kernel(a_ref, b_ref, c_ref, d_ref, e_ref, w_ref, x_ref, o_ref)
    _proj_kernel(o_ref, g_ref, wb_ref, wf_ref, hgain_ref, bd_ref, pb_ref, pf_ref)


def _out_proj(parts, w, x2, g, wb, wf, hgain, bd, tm):
    m = x2.shape[0]
    gspec = pl.BlockSpec((tm, GW), lambda i: (i, 0))
    c2 = lambda i: (0, 0)
    return pl.pallas_call(
        _out_proj_kernel,
        grid=(m // tm,),
        in_specs=[gspec] * 5 + [
            pl.BlockSpec((5 * GW, D_MODEL), c2),
            pl.BlockSpec((tm, D_MODEL), lambda i: (i, 0)),
            pl.BlockSpec((1, D_MODEL), c2),
            pl.BlockSpec((D_MODEL, NP_COLS), c2),
            pl.BlockSpec((D_MODEL, NF_COLS), c2),
            pl.BlockSpec((8, GW), c2),
            pl.BlockSpec((GW, GW), c2),
        ],
        out_specs=[
            pl.BlockSpec((tm, D_MODEL), lambda i: (i, 0)),
            pl.BlockSpec((tm, NP_COLS), lambda i: (i, 0)),
            pl.BlockSpec((tm, NF_COLS), lambda i: (i, 0)),
        ],
        out_shape=[
            jax.ShapeDtypeStruct((m, D_MODEL), F32),
            jax.ShapeDtypeStruct((m, NP_COLS), BF16),
            jax.ShapeDtypeStruct((m, NF_COLS), F32),
        ],
        compiler_params=pltpu.CompilerParams(
            dimension_semantics=("arbitrary",), vmem_limit_bytes=VMEM_LIMIT),
    )(*parts, w, x2, g, wb, wf, hgain, bd)


def _tile_heads(g):
    return jnp.tile(g.astype(F32), NH).reshape(1, GW)


def kernel(x, mem, norm_g, w_in, fox_f_bias, fox_q_norm, fox_k_norm, hgrn_lb_logits, hgrn_out_norm,
           pool_w, pool_scale, mem_norm_g, mem_w_kv, mem_q_norm, mem_k_norm, w_out):
    bsz, seq, _ = x.shape
    depth = w_in.shape[0]
    m = bsz * seq
    tq = 256
    tm = 512
    tm_out = 1024

    pr = jax.nn.softmax(hgrn_lb_logits.astype(F32), axis=0)
    lower_bounds = jnp.clip(jnp.cumsum(pr, axis=0) - pr[0:1], 0.0, 1.0 - 1e-6)

    bd_np = _bd_ones()
    bd = jnp.asarray(bd_np, BF16)
    tri = jnp.asarray(np.triu(np.ones((tq, tq), np.float32)), BF16)

    def suffix_and_ones(w):
        jj = np.arange(w)
        suffix = (jj[:, None] > jj[None, :]).astype(np.float32)
        return jnp.asarray(np.concatenate([suffix, np.ones((w, SB_CW), np.float32)], axis=1), BF16)

    to = suffix_and_ones(SB_CW)
    tof = suffix_and_ones(SB_TQ)
    a_np, masks_np, hm_np, maskc_np, hmc_np = _hgrn_constants()
    a_all = jnp.asarray(a_np, BF16)
    masks = jnp.asarray(masks_np, F32)
    hm = jnp.asarray(hm_np, BF16)
    maskc = jnp.asarray(maskc_np, F32)
    hmc = jnp.asarray(hmc_np, BF16)
    win = jnp.asarray(np.repeat(np.array(POOL_WINDOWS, np.float32), HD).reshape(1, GW))

    g = GW
    wb_all, wf_all = _regroup_w_in(w_in)
    x2 = x.reshape(m, D_MODEL)
    norm_gs = [norm_g[l].reshape(1, D_MODEL).astype(F32) for l in range(depth)]
    head_gains = [jnp.concatenate([_tile_heads(fox_q_norm[l]), _tile_heads(fox_k_norm[l]),
                                   _tile_heads(mem_q_norm[l]), jnp.zeros((5, GW), F32)], axis=0)
                  for l in range(depth)]
    pb, pf = _proj(x2, norm_gs[0], wb_all[0], wf_all[0], head_gains[0], bd, tm)
    for l in range(depth):
        p3 = pb.reshape(bsz, seq, NP_COLS)
        f3 = pf.reshape(bsz, seq, NF_COLS)

        bias = jnp.broadcast_to(jnp.pad(fox_f_bias[l].astype(F32), (0, 8 - NH))[:, None], (8, tq))
        out_a = _fox(p3, f3, bias, _tile_heads(fox_q_norm[l]), _tile_heads(fox_k_norm[l]), bd, tri)
        out_b = _sb(p3, to, tof, bd)

        lb = lower_bounds[l].reshape(1, g)
        out_c = _hgrn(p3, f3, jnp.maximum(lb, LB_FLOOR), 1.0 - lb,
                      hgrn_out_norm[l].reshape(1, g).astype(F32), bd, a_all, masks, hm, maskc, hmc)

        wp = jax.scipy.linalg.block_diag(*[pool_w[l, i] for i in range(len(POOL_WINDOWS))]).astype(BF16)
        out_d, out_e = _pm(p3, f3, mem, mem_norm_g[l].reshape(1, D_MODEL).astype(F32),
                           mem_w_kv[l].astype(BF16), _tile_heads(mem_k_norm[l]),
                           bd, wp, pool_scale[l].reshape(1, g).astype(F32), win, tq)

        parts = [o.reshape(m, g) for o in (out_a, out_b, out_c, out_d, out_e)]
        if l + 1 < depth:
            x2, pb, pf = _out_proj(parts, w_out[l].astype(BF16), x2, norm_gs[l + 1],
                                   wb_all[l + 1], wf_all[l + 1], head_gains[l + 1], bd, tm)
        else:
            x2 = _out(parts, w_out[l].astype(BF16), x2, tm_out)
    return x2.reshape(bsz, seq, D_MODEL)
```

```python
import functools

import numpy as np
import jax
import jax.numpy as jnp
from jax import lax
from jax.experimental import pallas as pl
from jax.experimental.pallas import tpu as pltpu

F32 = jnp.float32
BF16 = jnp.bfloat16

D_MODEL = 1024
GW = 256
NH = 4
HD = 64
CHUNK = 64
POOL_WINDOWS = (2, 4, 8, 16)
EPS = 1e-6
NEG_BIG = -1e30
LB_FLOOR = 1e-30
SCALE = HD ** -0.5

NP_COLS = 14 * GW
P_FQ, P_FK, P_FV, P_FG, P_SQ, P_SK, P_SV, P_SG, P_HQ, P_HI, P_HG, P_PG, P_MQ, P_MG = range(14)
NF_COLS = 2 * GW + 128
F_HF, F_PV, F_FF = range(3)
VMEM_LIMIT = 56 * 1024 * 1024

TQ = 128
WIDE = 3 * TQ
EXP_ZERO = -104.0
NORM_SLACK = 1.01
LOGIT_SLACK = 0.05
FOX_UNROLL = 2
NCOARSE = 3
COARSE_STARTS = ((0,), (0, 32), (0, 16, 32, 48))
COARSE_RUN = (32, 16, 8)
HSB = 8
HGRP = 4
SB_TQ = 128
SB_WIDE = 384
SB_CW = 128
SB_UNROLL = 2
SOFTPLUS_LINEAR = 80.0


def _dot(a, b):
    return jnp.dot(a, b, preferred_element_type=F32)


def _dot_nt(a, b):
    return lax.dot_general(a, b, (((1,), (1,)), ((), ())), preferred_element_type=F32)


def _dot_tn(a, b):
    return lax.dot_general(a, b, (((0,), (0,)), ((), ())), preferred_element_type=F32)


def _aligned(x, m):
    return x if isinstance(x, int) else pl.multiple_of(x, m)


def _split_bf16(x, n):
    parts = []
    r = x
    for i in range(n):
        p = r.astype(BF16)
        parts.append(p)
        if i + 1 < n:
            r = r - p.astype(F32)
    return parts


def _dot_f32_rhs01(x, m01, n=3):
    acc = None
    for p in _split_bf16(x, n):
        t = _dot(p, m01)
        acc = t if acc is None else acc + t
    return acc


def _silu(x):
    return x / (1.0 + jnp.exp(-x))


def _log_sigmoid(x):
    return jnp.minimum(x, 0.0) - jnp.log(1.0 + jnp.exp(-jnp.abs(x)))


def _head_rms(x, gain, bd):
    ss = _dot((x * x).astype(BF16), bd)
    return x * lax.rsqrt(ss * (1.0 / HD) + EPS) * gain


def _bd_ones():
    h = np.arange(GW) // HD
    return (h[:, None] == h[None, :]).astype(np.float32)


def _hgrn_constants():
    c = CHUNK
    t = np.arange(c)
    j = np.arange(c)[None, :]
    mats = []
    masks = []
    for n in (64, 32, 16, 8, 4):
        blk, pos = t // n, t % n
        ref = blk * n + n // 2 - 1
        aq = (pos[:, None] >= n // 2) & (j > ref[:, None]) & (j <= t[:, None])
        ak = (pos[:, None] < n // 2) & (j > t[:, None]) & (j <= ref[:, None])
        mats.append((aq | ak).astype(np.float32))
        m = (blk[:, None] == blk[None, :]) & (pos[:, None] >= n // 2) & (pos[None, :] < n // 2)
        masks.append(m.astype(np.float32))
    m = ((t[:, None] // 2) == (t[None, :] // 2)) & (t[None, :] <= t[:, None])
    masks.append(m.astype(np.float32))
    mats.append((j <= t[:, None]).astype(np.float32))
    mats.append((j > t[:, None]).astype(np.float32))
    a_all = np.concatenate(mats, axis=0)
    a_all = np.concatenate([a_all, a_all], axis=1)
    masks = np.stack([np.tile(m, (1, NH)) for m in masks])
    total = masks[:, :, :c].sum(0)
    assert np.array_equal(total, np.tril(np.ones((c, c), np.float32)))
    hm = (np.arange(NH * c)[:, None] // c == np.arange(GW)[None, :] // HD).astype(np.float32)
    maskc = []
    for l, n in enumerate((64, 32, 16, 8, 4)[:NCOARSE]):
        s = np.concatenate([np.arange(lo, lo + COARSE_RUN[l]) for lo in COARSE_STARTS[l]])
        m = (t[:, None] // n == s[None, :] // n) & (t[:, None] % n >= n // 2)
        maskc.append(np.tile(m.astype(np.float32), (1, NH)))
    hmc = (np.arange(NH * (c // 2))[:, None] // (c // 2) == np.arange(GW)[None, :] // HD).astype(np.float32)
    return a_all, masks, hm, np.stack(maskc), hmc


def _regroup_kernel(w_ref, wb_ref, wf_ref):
    g = GW
    o_ff, o_sb, o_hg, o_pl, o_mm = 4 * g, 4 * g + NH, 8 * g + NH, 12 * g + NH, 14 * g + NH
    w = w_ref[0]
    bf_src = {P_FQ: 0, P_FK: g, P_FV: 2 * g, P_FG: 3 * g,
              P_SQ: o_sb, P_SK: o_sb + g, P_SV: o_sb + 2 * g, P_SG: o_sb + 3 * g,
              P_HQ: o_hg, P_HI: o_hg + 2 * g, P_HG: o_hg + 3 * g,
              P_PG: o_pl + g, P_MQ: o_mm, P_MG: o_mm + g}
    f_src = {F_HF: o_hg + g, F_PV: o_pl}
    for dst, c0 in bf_src.items():
        wb_ref[0, :, dst * g:(dst + 1) * g] = w[:, c0:c0 + g].astype(BF16)
    for dst, c0 in f_src.items():
        wf_ref[0, :, dst * g:(dst + 1) * g] = w[:, c0:c0 + g].astype(BF16)
    ff = jnp.concatenate([w[:, o_ff:o_ff + NH], jnp.zeros((w.shape[0], 128 - NH), F32)], axis=1)
    wf_ref[0, :, F_FF * g:F_FF * g + 128] = ff.astype(BF16)


def _regroup_w_in(w_in):
    depth, d, n = w_in.shape
    tr = 128
    return pl.pallas_call(
        _regroup_kernel,
        grid=(depth, d // tr),
        in_specs=[pl.BlockSpec((1, tr, n), lambda l, i: (l, i, 0))],
        out_specs=[
            pl.BlockSpec((1, tr, NP_COLS), lambda l, i: (l, i, 0)),
            pl.BlockSpec((1, tr, NF_COLS), lambda l, i: (l, i, 0)),
        ],
        out_shape=[
            jax.ShapeDtypeStruct((depth, d, NP_COLS), BF16),
            jax.ShapeDtypeStruct((depth, d, NF_COLS), BF16),
        ],
        compiler_params=pltpu.CompilerParams(
            dimension_semantics=("arbitrary", "arbitrary"), vmem_limit_bytes=VMEM_LIMIT),
    )(w_in)


def _proj_kernel(x_ref, g_ref, wb_ref, wf_ref, hgain_ref, bd_ref, pb_ref, pf_ref):
    x = x_ref[...]
    ms = jnp.mean(x * x, axis=-1, keepdims=True)
    h = (x * lax.rsqrt(ms + EPS) * g_ref[...]).astype(BF16)
    gains = hgain_ref[...]
    bd = bd_ref[...]
    normed = {P_FQ: (0, SCALE), P_FK: (1, 1.0), P_MQ: (2, SCALE)}
    post = {P_SQ: lambda r: r * SCALE}
    post.update({gi: _silu for gi in (P_FG, P_SG, P_HQ, P_HG, P_PG, P_MG)})

    def cols(gi):
        return slice(gi * GW, (gi + 1) * GW)

    raw = {gi: _dot(h, wb_ref[:, cols(gi)]) for gi in normed}
    for gi in range(NP_COLS // GW):
        if gi not in normed:
            r = _dot(h, wb_ref[:, cols(gi)])
            pb_ref[:, cols(gi)] = post.get(gi, lambda r: r)(r).astype(BF16)
    pf_ref[...] = _dot(h, wf_ref[...])
    for gi, (row, scale) in normed.items():
        pb_ref[:, cols(gi)] = (_head_rms(raw[gi], gains[row:row + 1], bd) * scale).astype(BF16)


def _proj(x2, g, wb, wf, hgain, bd, tm):
    m = x2.shape[0]
    return pl.pallas_call(
        _proj_kernel,
        grid=(m // tm,),
        in_specs=[
            pl.BlockSpec((tm, D_MODEL), lambda i: (i, 0)),
            pl.BlockSpec((1, D_MODEL), lambda i: (0, 0)),
            pl.BlockSpec((D_MODEL, NP_COLS), lambda i: (0, 0)),
            pl.BlockSpec((D_MODEL, NF_COLS), lambda i: (0, 0)),
            pl.BlockSpec((8, GW), lambda i: (0, 0)),
            pl.BlockSpec((GW, GW), lambda i: (0, 0)),
        ],
        out_specs=[
            pl.BlockSpec((tm, NP_COLS), lambda i: (i, 0)),
            pl.BlockSpec((tm, NF_COLS), lambda i: (i, 0)),
        ],
        out_shape=[
            jax.ShapeDtypeStruct((m, NP_COLS), BF16),
            jax.ShapeDtypeStruct((m, NF_COLS), F32),
        ],
        compiler_params=pltpu.CompilerParams(
            dimension_semantics=("arbitrary",), vmem_limit_bytes=VMEM_LIMIT),
    )(x2, g, wb, wf, hgain, bd)


def _fox_kernel(p_ref, ff_ref, bias_ref, gq_ref, gk_ref, bd_ref, tri_ref, o_ref,
                vx, crow, qk_buf, worst_ref, *, seq):
    bd = bd_ref[...]
    tri = tri_ref[...]
    pb = tri.shape[0]
    hmask = [bd[h * HD:h * HD + 1, :] for h in range(NH)]
    hmask_f = [m.astype(F32) for m in hmask]
    qk_max = (HD * SCALE * NORM_SLACK) * (jnp.max(jnp.abs(gq_ref[...]), axis=-1, keepdims=True)
                                          * jnp.max(jnp.abs(gk_ref[...]), axis=-1, keepdims=True))
    carry = jnp.zeros((8, 1), F32)
    for b in range(seq // pb):
        r = slice(b * pb, (b + 1) * pb)
        lf = _log_sigmoid(ff_ref[0, r, :].T[:8, :] + bias_ref[...])
        cb = _dot_f32_rhs01(lf, tri, 3) + carry
        carry = cb[:, pb - 1:pb]
        crow[:, r] = cb
        v = p_ref[0, r, 2 * GW:3 * GW]
        for h in range(NH):
            vx[h, r, :] = v * hmask[h]

    rel_w = (lax.broadcasted_iota(jnp.int32, (TQ, WIDE), 1) - lax.broadcasted_iota(jnp.int32, (TQ, WIDE), 0))
    nfull = (WIDE - TQ) // TQ

    def qk_all_heads(r0, start, width):
        qb = p_ref[0, pl.ds(r0, TQ), 0:GW]
        qx = jnp.concatenate([qb * m for m in hmask], axis=0)
        return _dot_nt(qx, p_ref[0, pl.ds(start, width), GW:2 * GW])

    def wide_start(r0):
        if isinstance(r0, int):
            return max(r0 - (WIDE - TQ), 0)
        return pl.multiple_of(jnp.maximum(r0 - (WIDE - TQ), 0), TQ)

    def wide_qk(i):
        r0 = _aligned(i * TQ, TQ)
        return qk_all_heads(r0, wide_start(r0), WIDE)

    def q_block(i, full_tile, qk_wide):
        r0 = _aligned(i * TQ, TQ)
        s0 = wide_start(r0)
        rows = pl.ds(r0, TQ)

        def mask_wide(s):
            if full_tile:
                diag = jnp.where(rel_w[:, WIDE - TQ:] <= WIDE - TQ, s[:, WIDE - TQ:], NEG_BIG)
                return jnp.concatenate([s[:, :WIDE - TQ], diag], axis=1)
            return jnp.where(rel_w <= r0 - s0, s, NEG_BIG)

        def tiles(qk, start, width):
            return [qk[h * TQ:(h + 1) * TQ] - crow[h:h + 1, pl.ds(start, width)] for h in range(NH)]

        def weighted_values(ps, start, width):
            p_all = jnp.concatenate([p.astype(BF16) for p in ps], axis=1)
            v_all = jnp.concatenate([vx[h, pl.ds(start, width), :] for h in range(NH)], axis=0)
            return _dot(p_all, v_all)

        def per_head_lanes(cols):
            out = cols[0] * hmask_f[0]
            for h in range(1, NH):
                out = out + cols[h] * hmask_f[h]
            return out

        def weight_bound(s_end, ms):
            last = pl.multiple_of(jnp.maximum(s_end - TQ, 0), TQ)
            bound = None
            for h in range(NH):
                c_last = crow[h:h + 1, pl.ds(last, TQ)][:, TQ - 1:TQ]
                b_h = (qk_max + LOGIT_SLACK) - c_last - jnp.min(ms[h], axis=0, keepdims=True)
                bound = b_h if bound is None else jnp.maximum(bound, b_h)
            return bound

        def keep_going(s_end, ms):
            return (jnp.max(weight_bound(s_end, ms)) > EXP_ZERO).astype(jnp.int32)

        ss = [mask_wide(s) for s in tiles(qk_wide, s0, WIDE)]
        ms = [jnp.max(s, axis=-1, keepdims=True) for s in ss]
        ps = [jnp.exp(s - m) for s, m in zip(ss, ms)]
        ls = [jnp.sum(p, axis=-1, keepdims=True) for p in ps]
        acc = weighted_values(ps, s0, WIDE)

        def emit(acc, ls):
            g = p_ref[0, rows, 3 * GW:4 * GW].astype(F32)
            o_ref[0, rows, :] = (acc * per_head_lanes([1.0 / l for l in ls]) * g).astype(BF16)

        emit(acc, ls)
        if full_tile:
            worst_ref[...] = jnp.maximum(worst_ref[...], jnp.where(s0 > 0, weight_bound(s0, ms), NEG_BIG))

        def cond(st):
            return jnp.logical_and(st[0] > 0, st[1] > 0)

        def body(st):
            s_end, _, ms, ls, acc = st
            sb = pl.multiple_of(s_end - TQ, TQ)
            ss = tiles(qk_all_heads(r0, sb, TQ), sb, TQ)
            ms2 = [jnp.maximum(m, jnp.max(s, axis=-1, keepdims=True)) for s, m in zip(ss, ms)]
            ps = [jnp.exp(s - m) for s, m in zip(ss, ms2)]
            alphas = [jnp.exp(m - m2) for m, m2 in zip(ms, ms2)]
            ls2 = [a * l + jnp.sum(p, axis=-1, keepdims=True) for a, l, p in zip(alphas, ls, ps)]
            acc2 = acc * per_head_lanes(alphas) + weighted_values(ps, sb, TQ)
            return sb, keep_going(sb, ms2), tuple(ms2), tuple(ls2), acc2

        def rare_tail():
            @pl.when(jnp.logical_and(s0 > 0, keep_going(s0, ms) > 0))
            def _():
                _, _, _, ls_f, acc_f = lax.while_loop(cond, body, (s0, jnp.int32(1), tuple(ms), tuple(ls), acc))
                emit(acc_f, ls_f)

        return rare_tail

    nblk = seq // TQ
    grp = FOX_UNROLL
    lead = nfull + (nblk - nfull) % (2 * grp)
    worst_ref[...] = jnp.full(worst_ref.shape, NEG_BIG, F32)
    for i in range(lead):
        q_block(i, i >= nfull, wide_qk(i))
    for k in range(grp):
        qk_buf[k] = wide_qk(lead + k)

    def group(first, cur, nxt):
        for k in range(grp):
            qk_buf[nxt * grp + k] = wide_qk(jnp.minimum(first + grp + k, nblk - 1))
        for k in range(grp):
            q_block(first + k, True, qk_buf[cur * grp + k])

    def two_groups(j, _):
        i = lead + 2 * grp * j
        group(i, 0, 1)
        group(i + grp, 1, 0)
        return 0

    lax.fori_loop(0, (nblk - lead) // (2 * grp), two_groups, 0)

    @pl.when(jnp.max(worst_ref[...]) > EXP_ZERO)
    def _():
        def redo(i, _):
            q_block(i, True, wide_qk(i))()
            return 0

        lax.fori_loop(nfull, nblk, redo, 0)


def _fox(p3, f3, bias, gq, gk, bd, tri):
    b, s, _ = p3.shape
    assert s % tri.shape[0] == 0 and s >= WIDE + 2 * FOX_UNROLL * TQ
    kern = functools.partial(_fox_kernel, seq=s)
    c2 = lambda i: (0, 0)
    return pl.pallas_call(
        kern,
        grid=(b,),
        in_specs=[
            pl.BlockSpec((1, s, 4 * GW), lambda i: (i, 0, P_FQ // 4)),
            pl.BlockSpec((1, s, 128), lambda i: (i, 0, F_FF * GW // 128)),
            pl.BlockSpec((8, tri.shape[0]), c2),
            pl.BlockSpec((1, GW), c2),
            pl.BlockSpec((1, GW), c2),
            pl.BlockSpec((GW, GW), c2),
            pl.BlockSpec(tri.shape, c2),
        ],
        out_specs=pl.BlockSpec((1, s, GW), lambda i: (i, 0, 0)),
        out_shape=jax.ShapeDtypeStruct((b, s, GW), BF16),
        scratch_shapes=[
            pltpu.VMEM((NH, s, GW), BF16),
            pltpu.VMEM((8, s), F32),
            pltpu.VMEM((2 * FOX_UNROLL, NH * TQ, WIDE), F32),
            pltpu.VMEM((8, 128), F32),
        ],
        compiler_params=pltpu.CompilerParams(
            dimension_semantics=("arbitrary",), vmem_limit_bytes=VMEM_LIMIT),
    )(p3, f3, bias, gq, gk, bd, tri)


def _sb_kernel(p_ref, to_ref, tof_ref, bd_ref, o_ref, kt, vx, qk_buf, worst_ref, *, seq):
    to = to_ref[...]
    tq, wide, cw = SB_TQ, SB_WIDE, SB_CW
    hmask = [bd_ref[h * HD:h * HD + 1, :] for h in range(NH)]
    pb = 256
    for b in range(seq // pb):
        r = slice(b * pb, (b + 1) * pb)
        kt[:, r] = p_ref[0, r, GW:2 * GW].T
        v = p_ref[0, r, 2 * GW:3 * GW]
        for h in range(NH):
            vx[h, r, :] = v * hmask[h]

    rel_w = (lax.broadcasted_iota(jnp.int32, (tq, wide), 1) - lax.broadcasted_iota(jnp.int32, (tq, wide), 0))
    nsub = wide // cw
    nfull = -(-(wide - tq) // tq)

    def log_one_minus_sigmoid(z):
        return -jnp.maximum(z, jnp.log(1.0 + jnp.exp(jnp.minimum(z, SOFTPLUS_LINEAR))))

    def suffix_sums(lom, mat):
        w = lom.shape[1]
        cs = _dot(lom.astype(BF16), mat)
        return cs[:, :w], cs[:, w:]

    def qk_all_heads(r0, start, width):
        qb = p_ref[0, pl.ds(r0, tq), 0:GW]
        qx = jnp.concatenate([qb * m for m in hmask], axis=0)
        return _dot(qx, kt[:, pl.ds(start, width)])

    def wide_start(r0):
        if isinstance(r0, int):
            return max(r0 - (wide - tq), 0)
        return pl.multiple_of(jnp.maximum(r0 - (wide - tq), 0), tq)

    def wide_qk(i):
        r0 = _aligned(i * tq, tq)
        return qk_all_heads(r0, wide_start(r0), wide)

    def q_block(i, full_tile, qk_wide):
        r0 = _aligned(i * tq, tq)
        s0 = wide_start(r0)
        rows = pl.ds(r0, tq)

        def mask_wide(x):
            if full_tile:
                diag = jnp.where(rel_w[:, wide - cw:] < wide - tq, x[:, wide - cw:], 0.0)
                return jnp.concatenate([x[:, :wide - cw], diag], axis=1)
            return jnp.where(rel_w < r0 - s0, x, 0.0)

        def per_head(qk):
            return [qk[h * tq:(h + 1) * tq] for h in range(NH)]

        def weighted_values(ws, start, width):
            w_all = jnp.concatenate([w.astype(BF16) for w in ws], axis=1)
            v_all = jnp.concatenate([vx[h, pl.ds(start, width), :] for h in range(NH)], axis=0)
            return _dot(w_all, v_all)

        def emit(acc):
            g = p_ref[0, rows, 3 * GW:4 * GW].astype(F32)
            o_ref[0, rows, :] = (acc * g).astype(BF16)

        zs = per_head(qk_wide)
        loms = [log_one_minus_sigmoid(z) for z in zs]
        log_betas = [z + lom for z, lom in zip(zs, loms)]
        sums = [[suffix_sums(mask_wide(lom)[:, c * cw:(c + 1) * cw], to) for c in range(nsub)] for lom in loms]
        carries, ws = [], []
        for h in range(NH):
            between = [None] * nsub
            carry = jnp.zeros((tq, cw), F32)
            for c in reversed(range(nsub)):
                rc, tot = sums[h][c]
                between[c] = rc + carry
                carry = carry + tot
            ws.append(mask_wide(jnp.exp(log_betas[h] + jnp.concatenate(between, axis=1))))
            carries.append(carry)
        acc = weighted_values(ws, s0, wide)
        emit(acc)

        def worst_carry(carries):
            cm = jnp.maximum(jnp.maximum(carries[0], carries[1]), jnp.maximum(carries[2], carries[3]))
            return jnp.max(cm.reshape(tq // 8, 8, cw), axis=0)

        def keep_going(carries):
            return (jnp.max(worst_carry(carries)) > EXP_ZERO).astype(jnp.int32)

        if full_tile:
            worst_ref[0] = jnp.maximum(worst_ref[0], jnp.where(s0 > 0, worst_carry(carries), NEG_BIG))

        def cond(st):
            return jnp.logical_and(st[0] > 0, st[1] > 0)

        def body(st):
            s_end, _, carries, acc = st
            sb = pl.multiple_of(s_end - tq, tq)
            zs = per_head(qk_all_heads(r0, sb, tq))
            loms = [log_one_minus_sigmoid(z) for z in zs]
            sums = [suffix_sums(lom, tof_ref[...]) for lom in loms]
            ws = [jnp.exp(z + lom + rc + cr[:, :tq]) for z, lom, (rc, _), cr in zip(zs, loms, sums, carries)]
            c2 = [cr + tot for cr, (_, tot) in zip(carries, sums)]
            return sb, keep_going(c2), tuple(c2), acc + weighted_values(ws, sb, tq)

        def rare_tail():
            @pl.when(jnp.logical_and(s0 > 0, keep_going(carries) > 0))
            def _():
                st = lax.while_loop(cond, body, (s0, jnp.int32(1), tuple(carries), acc))
                emit(st[3])

        return rare_tail

    nblk = seq // tq
    grp = SB_UNROLL
    lead = nfull + (nblk - nfull) % (2 * grp)
    worst_ref[0] = jnp.full((8, cw), NEG_BIG, F32)
    for i in range(lead):
        q_block(i, i >= nfull, wide_qk(i))
    for k in range(grp):
        qk_buf[k] = wide_qk(lead + k)

    def group(first, cur, nxt):
        for k in range(grp):
            qk_buf[nxt * grp + k] = wide_qk(jnp.minimum(first + grp + k, nblk - 1))
        for k in range(grp):
            q_block(first + k, True, qk_buf[cur * grp + k])

    def two_groups(j, _):
        i = lead + 2 * grp * j
        group(i, 0, 1)
        group(i + grp, 1, 0)
        return 0

    lax.fori_loop(0, (nblk - lead) // (2 * grp), two_groups, 0)

    @pl.when(jnp.max(worst_ref[0]) > EXP_ZERO)
    def _():
        def redo(i, _):
            q_block(i, True, wide_qk(i))()
            return 0

        lax.fori_loop(nfull, nblk, redo, 0)


def _sb(p3, to, tof, bd):
    b, s, _ = p3.shape
    assert s % 256 == 0 and s >= SB_WIDE + 2 * SB_UNROLL * SB_TQ
    kern = functools.partial(_sb_kernel, seq=s)
    return pl.pallas_call(
        kern,
        grid=(b,),
        in_specs=[
            pl.BlockSpec((1, s, 4 * GW), lambda i: (i, 0, P_SQ // 4)),
            pl.BlockSpec(to.shape, lambda i: (0, 0)),
            pl.BlockSpec(tof.shape, lambda i: (0, 0)),
            pl.BlockSpec((GW, GW), lambda i: (0, 0)),
        ],
        out_specs=pl.BlockSpec((1, s, GW), lambda i: (i, 0, 0)),
        out_shape=jax.ShapeDtypeStruct((b, s, GW), BF16),
        scratch_shapes=[
            pltpu.VMEM((GW, s), BF16),
            pltpu.VMEM((NH, s, GW), BF16),
            pltpu.VMEM((2 * SB_UNROLL, NH * SB_TQ, SB_WIDE), F32),
            pltpu.VMEM((1, 8, SB_CW), F32),
        ],
        compiler_params=pltpu.CompilerParams(
            dimension_semantics=("arbitrary",), vmem_limit_bytes=VMEM_LIMIT),
    )(p3, to, tof, bd)


def _hgrn_kernel(hq_ref, hi_ref, hg_ref, hf_ref, lbm_ref, oml_ref, gout_ref, bd_ref, a_ref,
                 mask_ref, hm_ref, maskc_ref, hmc_ref, o_ref, st_ref, *, seq):
    bd = bd_ref[...]
    a_all = a_ref[...]
    hm = hm_ref[...]
    hm_t = hm.T
    lbm = lbm_ref[...]
    oml = oml_ref[...]
    c = CHUNK
    st_ref[...] = jnp.zeros((GW, GW), F32)

    n = HSB * c
    nlev = mask_ref.shape[0] - 1
    odd =(lax.broadcasted_iota(jnp.int32, (n, GW), 0) & 1) == 1

    def superblock(bi, _):
        r0 = pl.multiple_of(bi * n, n)
        rows = pl.ds(r0, n)
        hf = hf_ref[0, rows, :]
        sg = 1.0 / (1.0 + jnp.exp(-hf))
        f = lbm + oml * sg
        g = jnp.log(f)
        kk = oml * (1.0 - sg)
        q = hq_ref[0, rows, :].astype(F32)
        v = hi_ref[0, rows, :]
        gh = g.astype(BF16)
        gl = (g - gh.astype(F32)).astype(BF16)
        q2 = (q * jnp.where(odd, f, 1.0)).astype(BF16)
        k2 = (kk * jnp.where(odd, 1.0 / f, 1.0)).astype(BF16)

        def scores(qf, kf, l):
            kxt = jnp.concatenate([kf] * NH, axis=0).T * hm_t
            return _dot(qf, kxt) * mask_ref[l]

        qb = q.astype(BF16)
        kb = kk.astype(BF16)
        hm_c = hmc_ref[...]
        hm_ct = hm_c.T

        def key_rows(a, l):
            return jnp.concatenate([a[lo:lo + COARSE_RUN[l]] for lo in COARSE_STARTS[l]], axis=0)

        st = st_ref[...]
        outs = []
        for g0 in range(0, HSB, HGRP):
            sls = [slice(ci * c, (ci + 1) * c) for ci in range(g0, g0 + HGRP)]
            exs = [_dot(a_all, jnp.concatenate([gh[sl], gl[sl]], axis=0)) for sl in sls]

            o_inter = []
            for ci, sl in enumerate(sls):
                eb = exs[ci][nlev * c:(nlev + 1) * c]
                er = exs[ci][(nlev + 1) * c:(nlev + 2) * c]
                qd = (q[sl] * jnp.exp(eb)).astype(BF16)
                o_inter.append(_dot(qd, st.T.astype(BF16) * bd))
                upd = _dot_tn(v[sl], (kk[sl] * jnp.exp(er)).astype(BF16))
                st = st * jnp.exp(eb[c - 1:c, :]) + upd

            ps = [scores(q2[sl], k2[sl], nlev) for sl in sls]
            pcs = [[] for _ in sls]
            for l in range(NCOARSE):
                for ci, sl in enumerate(sls):
                    x = jnp.exp(exs[ci][l * c:(l + 1) * c]).astype(BF16)
                    kxt = jnp.concatenate([key_rows(kb[sl] * x, l)] * NH, axis=0).T * hm_ct
                    pcs[ci].append(_dot(qb[sl] * x, kxt) * maskc_ref[l])
            for l in range(NCOARSE, nlev):
                for ci, sl in enumerate(sls):
                    x = jnp.exp(exs[ci][l * c:(l + 1) * c]).astype(BF16)
                    ps[ci] = ps[ci] + scores(qb[sl] * x, kb[sl] * x, l)
            for ci, sl in enumerate(sls):
                vc = v[sl]
                vxs = [jnp.concatenate([key_rows(vc, l)] * NH, axis=0) * hm_c for l in range(NCOARSE)]
                vx = jnp.concatenate(vxs + [jnp.concatenate([vc] * NH, axis=0) * hm], axis=0)
                p_all = jnp.concatenate(pcs[ci] + [ps[ci]], axis=1).astype(BF16)
                outs.append(o_inter[ci] + _dot(p_all, vx))
        st_ref[...] = st

        o = _head_rms(jnp.concatenate(outs, axis=0), gout_ref[...], bd)
        o_ref[0, rows, :] = (o * hg_ref[0, rows, :].astype(F32)).astype(BF16)
        return 0

    lax.fori_loop(0, seq // n, superblock, 0)


def _hgrn(p3, f3, lbm, oml, gout, bd, a_all, masks, hm, maskc, hmc):
    b, s, _ = p3.shape
    assert s % (HSB * CHUNK) == 0
    kern = functools.partial(_hgrn_kernel, seq=s)
    c2 = lambda i: (0, 0)
    return pl.pallas_call(
        kern,
        grid=(b,),
        in_specs=[
            pl.BlockSpec((1, s, GW), lambda i: (i, 0, P_HQ)),
            pl.BlockSpec((1, s, GW), lambda i: (i, 0, P_HI)),
            pl.BlockSpec((1, s, GW), lambda i: (i, 0, P_HG)),
            pl.BlockSpec((1, s, GW), lambda i: (i, 0, F_HF)),
            pl.BlockSpec((1, GW), c2),
            pl.BlockSpec((1, GW), c2),
            pl.BlockSpec((1, GW), c2),
            pl.BlockSpec((GW, GW), c2),
            pl.BlockSpec(a_all.shape, c2),
            pl.BlockSpec(masks.shape, lambda i: (0, 0, 0)),
            pl.BlockSpec(hm.shape, c2),
            pl.BlockSpec(maskc.shape, lambda i: (0, 0, 0)),
            pl.BlockSpec(hmc.shape, c2),
        ],
        out_specs=pl.BlockSpec((1, s, GW), lambda i: (i, 0, 0)),
        out_shape=jax.ShapeDtypeStruct((b, s, GW), BF16),
        scratch_shapes=[pltpu.VMEM((GW, GW), F32)],
        compiler_params=pltpu.CompilerParams(
            dimension_semantics=("arbitrary",), vmem_limit_bytes=VMEM_LIMIT),
    )(p3, p3, p3, f3, lbm, oml, gout, bd, a_all, masks, hm, maskc, hmc)


def _pm_kernel(pv_ref, pg_ref, mq_ref, mg_ref, mem_ref, mng_ref, wkv_ref, gmk_ref, bd_ref,
               wp_ref, ps_ref, win_ref, d_ref, e_ref, ubuf, wbuf, kmem, vxm, *, seq, tq):
    bd = bd_ref[...]
    hmask = [bd[h * HD:h * HD + 1, :] for h in range(NH)]
    hmask_f = [m.astype(F32) for m in hmask]
    halo = 16
    mem = mem_ref[0]
    ms = jnp.mean(mem * mem, axis=-1, keepdims=True)
    mn = (mem * lax.rsqrt(ms + EPS) * mng_ref[...]).astype(BF16)
    kv = _dot(mn, wkv_ref[...])
    kn = _head_rms(kv[:, :GW], gmk_ref[...], bd).astype(BF16)
    vv = kv[:, GW:].astype(BF16)
    nm = kn.shape[0]
    kmem[...] = kn
    for h in range(NH):
        vxm[h * nm:(h + 1) * nm, :] = vv * hmask[h]

    ubuf[0:halo, :] = jnp.zeros((halo, GW), F32)
    ubuf[halo:halo + seq, :] = pv_ref[0]
    win = win_ref[...]
    inv_win = 1.0 / win
    for k in range(3):
        sh = 1 << k
        wbuf[k, 0:halo, :] = jnp.zeros((halo, GW), F32)
        for b in range(seq // tq):
            base = halo + b * tq
            if k == 0:
                wbuf[k, base:base + tq, :] = ubuf[base:base + tq, :] + ubuf[base - sh:base - sh + tq, :]
            else:
                wbuf[k, base:base + tq, :] = (wbuf[k - 1, base:base + tq, :]
                                              + wbuf[k - 1, base - sh:base - sh + tq, :])

    for b in range(seq // tq):
        r = slice(b * tq, (b + 1) * tq)
        base = halo + b * tq
        u = ubuf[base:base + tq, :]
        s2 = wbuf[0, base:base + tq, :]
        s4 = wbuf[1, base:base + tq, :]
        s8 = wbuf[2, base:base + tq, :]
        s16 = s8 + wbuf[2, base - 8:base - 8 + tq, :]
        sw = jnp.where(win == 2.0, s2, jnp.where(win == 4.0, s4, jnp.where(win == 8.0, s8, s16)))
        if b == 0:
            pos = (lax.broadcasted_iota(jnp.int32, (tq, GW), 0) + 1).astype(F32)
            pooled = sw / jnp.minimum(pos, win)
        else:
            pooled = sw * inv_win
        y = _dot((pooled - u).astype(BF16), wp_ref[...]) * ps_ref[...]
        d_ref[0, r, :] = (y * pg_ref[0, r, :].astype(F32)).astype(BF16)

        qn = mq_ref[0, r, :]
        s_all = _dot_nt(jnp.concatenate([qn * m for m in hmask], axis=0), kmem[...])
        ss = [s_all[h * tq:(h + 1) * tq] for h in range(NH)]
        ps = [jnp.exp(s - jnp.max(s, axis=-1, keepdims=True)) for s in ss]
        ls = [jnp.sum(p, axis=-1, keepdims=True) for p in ps]
        inv_l = (1.0 / ls[0]) * hmask_f[0]
        for h in range(1, NH):
            inv_l = inv_l + (1.0 / ls[h]) * hmask_f[h]
        oe = _dot(jnp.concatenate([p.astype(BF16) for p in ps], axis=1), vxm[...]) * inv_l
        e_ref[0, r, :] = (oe * mg_ref[0, r, :].astype(F32)).astype(BF16)


def _pm(p3, f3, mem, mng, wkv, gmk, bd, wp, ps, win, tq):
    b, s, _ = p3.shape
    nm = mem.shape[1]
    kern = functools.partial(_pm_kernel, seq=s, tq=tq)
    c2 = lambda i: (0, 0)
    return pl.pallas_call(
        kern,
        grid=(b,),
        in_specs=[
            pl.BlockSpec((1, s, GW), lambda i: (i, 0, F_PV)),
            pl.BlockSpec((1, s, GW), lambda i: (i, 0, P_PG)),
            pl.BlockSpec((1, s, GW), lambda i: (i, 0, P_MQ)),
            pl.BlockSpec((1, s, GW), lambda i: (i, 0, P_MG)),
            pl.BlockSpec((1, nm, D_MODEL), lambda i: (i, 0, 0)),
            pl.BlockSpec((1, D_MODEL), c2),
            pl.BlockSpec((D_MODEL, 2 * GW), c2),
            pl.BlockSpec((1, GW), c2),
            pl.BlockSpec((GW, GW), c2),
            pl.BlockSpec((GW, GW), c2),
            pl.BlockSpec((1, GW), c2),
            pl.BlockSpec((1, GW), c2),
        ],
        out_specs=[
            pl.BlockSpec((1, s, GW), lambda i: (i, 0, 0)),
            pl.BlockSpec((1, s, GW), lambda i: (i, 0, 0)),
        ],
        out_shape=[
            jax.ShapeDtypeStruct((b, s, GW), BF16),
            jax.ShapeDtypeStruct((b, s, GW), BF16),
        ],
        scratch_shapes=[
            pltpu.VMEM((s + 16, GW), F32),
            pltpu.VMEM((3, s + 16, GW), F32),
            pltpu.VMEM((nm, GW), BF16),
            pltpu.VMEM((NH * nm, GW), BF16),
        ],
        compiler_params=pltpu.CompilerParams(
            dimension_semantics=("arbitrary",), vmem_limit_bytes=VMEM_LIMIT),
    )(f3, p3, p3, p3, mem, mng, wkv, gmk, bd, wp, ps, win)


def _out_kernel(a_ref, b_ref, c_ref, d_ref, e_ref, w_ref, x_ref, o_ref):
    mixed = jnp.concatenate([a_ref[...], b_ref[...], c_ref[...], d_ref[...], e_ref[...]], axis=1)
    o_ref[...] = x_ref[...] + _dot(mixed, w_ref[...])


def _out(parts, w, x2, tm):
    m = x2.shape[0]
    gspec = pl.BlockSpec((tm, GW), lambda i: (i, 0))
    return pl.pallas_call(
        _out_kernel,
        grid=(m // tm,),
        in_specs=[gspec] * 5 + [
            pl.BlockSpec((5 * GW, D_MODEL), lambda i: (0, 0)),
            pl.BlockSpec((tm, D_MODEL), lambda i: (i, 0)),
        ],
        out_specs=pl.BlockSpec((tm, D_MODEL), lambda i: (i, 0)),
        out_shape=jax.ShapeDtypeStruct((m, D_MODEL), F32),
        compiler_params=pltpu.CompilerParams(
            dimension_semantics=("arbitrary",), vmem_limit_bytes=VMEM_LIMIT),
    )(*parts, w, x2)


def _out_proj_kernel(a_ref, b_ref, c_ref, d_ref, e_ref, w_ref, x_ref, g_ref, wb_ref, wf_ref, hgain_ref, bd_ref,
                     o_ref, pb_ref, pf_ref):
    _out_kernel(a_ref, b_ref, c_ref, d_ref, e_ref, w_ref, x_ref, o_ref)
    _proj_kernel(o_ref, g_ref, wb_ref, wf_ref, hgain_ref, bd_ref, pb_ref, pf_ref)


def _out_proj(parts, w, x2, g, wb, wf, hgain, bd, tm):
    m = x2.shape[0]
    gspec = pl.BlockSpec((tm, GW), lambda i: (i, 0))
    c2 = lambda i: (0, 0)
    return pl.pallas_call(
        _out_proj_kernel,
        grid=(m // tm,),
        in_specs=[gspec] * 5 + [
            pl.BlockSpec((5 * GW, D_MODEL), c2),
            pl.BlockSpec((tm, D_MODEL), lambda i: (i, 0)),
            pl.BlockSpec((1, D_MODEL), c2),
            pl.BlockSpec((D_MODEL, NP_COLS), c2),
            pl.BlockSpec((D_MODEL, NF_COLS), c2),
            pl.BlockSpec((8, GW), c2),
            pl.BlockSpec((GW, GW), c2),
        ],
        out_specs=[
            pl.BlockSpec((tm, D_MODEL), lambda i: (i, 0)),
            pl.BlockSpec((tm, NP_COLS), lambda i: (i, 0)),
            pl.BlockSpec((tm, NF_COLS), lambda i: (i, 0)),
        ],
        out_shape=[
            jax.ShapeDtypeStruct((m, D_MODEL), F32),
            jax.ShapeDtypeStruct((m, NP_COLS), BF16),
            jax.ShapeDtypeStruct((m, NF_COLS), F32),
        ],
        compiler_params=pltpu.CompilerParams(
            dimension_semantics=("arbitrary",), vmem_limit_bytes=VMEM_LIMIT),
    )(*parts, w, x2, g, wb, wf, hgain, bd)


def _tile_heads(g):
    return jnp.tile(g.astype(F32), NH).reshape(1, GW)


def kernel(x, mem, norm_g, w_in, fox_f_bias, fox_q_norm, fox_k_norm, hgrn_lb_logits, hgrn_out_norm,
           pool_w, pool_scale, mem_norm_g, mem_w_kv, mem_q_norm, mem_k_norm, w_out):
    bsz, seq, _ = x.shape
    depth = w_in.shape[0]
    m = bsz * seq
    tq = 256
    tm = 512
    tm_out = 1024

    pr = jax.nn.softmax(hgrn_lb_logits.astype(F32), axis=0)
    lower_bounds = jnp.clip(jnp.cumsum(pr, axis=0) - pr[0:1], 0.0, 1.0 - 1e-6)

    bd_np = _bd_ones()
    bd = jnp.asarray(bd_np, BF16)
    tri = jnp.asarray(np.triu(np.ones((tq, tq), np.float32)), BF16)

    def suffix_and_ones(w):
        jj = np.arange(w)
        suffix = (jj[:, None] > jj[None, :]).astype(np.float32)
        return jnp.asarray(np.concatenate([suffix, np.ones((w, SB_CW), np.float32)], axis=1), BF16)

    to = suffix_and_ones(SB_CW)
    tof = suffix_and_ones(SB_TQ)
    a_np, masks_np, hm_np, maskc_np, hmc_np = _hgrn_constants()
    a_all = jnp.asarray(a_np, BF16)
    masks = jnp.asarray(masks_np, F32)
    hm = jnp.asarray(hm_np, BF16)
    maskc = jnp.asarray(maskc_np, F32)
    hmc = jnp.asarray(hmc_np, BF16)
    win = jnp.asarray(np.repeat(np.array(POOL_WINDOWS, np.float32), HD).reshape(1, GW))

    g = GW
    wb_all, wf_all = _regroup_w_in(w_in)
    x2 = x.reshape(m, D_MODEL)
    norm_gs = [norm_g[l].reshape(1, D_MODEL).astype(F32) for l in range(depth)]
    head_gains = [jnp.concatenate([_tile_heads(fox_q_norm[l]), _tile_heads(fox_k_norm[l]),
                                   _tile_heads(mem_q_norm[l]), jnp.zeros((5, GW), F32)], axis=0)
                  for l in range(depth)]
    pb, pf = _proj(x2, norm_gs[0], wb_all[0], wf_all[0], head_gains[0], bd, tm)
    for l in range(depth):
        p3 = pb.reshape(bsz, seq, NP_COLS)
        f3 = pf.reshape(bsz, seq, NF_COLS)

        bias = jnp.broadcast_to(jnp.pad(fox_f_bias[l].astype(F32), (0, 8 - NH))[:, None], (8, tq))
        out_a = _fox(p3, f3, bias, _tile_heads(fox_q_norm[l]), _tile_heads(fox_k_norm[l]), bd, tri)
        out_b = _sb(p3, to, tof, bd)

        lb = lower_bounds[l].reshape(1, g)
        out_c = _hgrn(p3, f3, jnp.maximum(lb, LB_FLOOR), 1.0 - lb,
                      hgrn_out_norm[l].reshape(1, g).astype(F32), bd, a_all, masks, hm, maskc, hmc)

        wp = jax.scipy.linalg.block_diag(*[pool_w[l, i] for i in range(len(POOL_WINDOWS))]).astype(BF16)
        out_d, out_e = _pm(p3, f3, mem, mem_norm_g[l].reshape(1, D_MODEL).astype(F32),
                           mem_w_kv[l].astype(BF16), _tile_heads(mem_k_norm[l]),
                           bd, wp, pool_scale[l].reshape(1, g).astype(F32), win, tq)

        parts = [o.reshape(m, g) for o in (out_a, out_b, out_c, out_d, out_e)]
        if l + 1 < depth:
            x2, pb, pf = _out_proj(parts, w_out[l].astype(BF16), x2, norm_gs[l + 1],
                                   wb_all[l + 1], wf_all[l + 1], head_gains[l + 1], bd, tm)
        else:
            x2 = _out(parts, w_out[l].astype(BF16), x2, tm_out)
    return x2.reshape(bsz, seq, D_MODEL)
```

```python
import functools

import numpy as np
import jax
import jax.numpy as jnp
from jax import lax
from jax.experimental import pallas as pl
from jax.experimental.pallas import tpu as pltpu

F32 = jnp.float32
BF16 = jnp.bfloat16

D_MODEL = 1024
GW = 256
NH = 4
HD = 64
CHUNK = 64
POOL_WINDOWS = (2, 4, 8, 16)
EPS = 1e-6
NEG_BIG = -1e30
LB_FLOOR = 1e-30
SCALE = HD ** -0.5

NP_COLS = 14 * GW
P_FQ, P_FK, P_FV, P_FG, P_SQ, P_SK, P_SV, P_SG, P_HQ, P_HI, P_HG, P_PG, P_MQ, P_MG = range(14)
NF_COLS = 2 * GW + 128
F_HF, F_PV, F_FF = range(3)
VMEM_LIMIT = 56 * 1024 * 1024

TQ = 128
WIDE = 3 * TQ
EXP_ZERO = -104.0
NORM_SLACK = 1.01
LOGIT_SLACK = 0.05
FOX_UNROLL = 2
NCOARSE = 3
COARSE_STARTS = ((0,), (0, 32), (0, 16, 32, 48))
COARSE_RUN = (32, 16, 8)
HSB = 8
HGRP = 4
SB_TQ = 128
SB_WIDE = 384
SB_CW = 128
SB_UNROLL = 2
SOFTPLUS_LINEAR = 80.0


def _dot(a, b):
    return jnp.dot(a, b, preferred_element_type=F32)


def _dot_nt(a, b):
    return lax.dot_general(a, b, (((1,), (1,)), ((), ())), preferred_element_type=F32)


def _dot_tn(a, b):
    return lax.dot_general(a, b, (((0,), (0,)), ((), ())), preferred_element_type=F32)


def _aligned(x, m):
    return x if isinstance(x, int) else pl.multiple_of(x, m)


def _split_bf16(x, n):
    parts = []
    r = x
    for i in range(n):
        p = r.astype(BF16)
        parts.append(p)
        if i + 1 < n:
            r = r - p.astype(F32)
    return parts


def _dot_f32_rhs01(x, m01, n=3):
    acc = None
    for p in _split_bf16(x, n):
        t = _dot(p, m01)
        acc = t if acc is None else acc + t
    return acc


def _silu(x):
    return x / (1.0 + jnp.exp(-x))


def _log_sigmoid(x):
    return jnp.minimum(x, 0.0) - jnp.log(1.0 + jnp.exp(-jnp.abs(x)))


def _head_rms(x, gain, bd):
    ss = _dot((x * x).astype(BF16), bd)
    return x * lax.rsqrt(ss * (1.0 / HD) + EPS) * gain


def _bd_ones():
    h = np.arange(GW) // HD
    return (h[:, None] == h[None, :]).astype(np.float32)


def _hgrn_constants():
    c = CHUNK
    t = np.arange(c)
    j = np.arange(c)[None, :]
    mats = []
    masks = []
    for n in (64, 32, 16, 8, 4):
        blk, pos = t // n, t % n
        ref = blk * n + n // 2 - 1
        aq = (pos[:, None] >= n // 2) & (j > ref[:, None]) & (j <= t[:, None])
        ak = (pos[:, None] < n // 2) & (j > t[:, None]) & (j <= ref[:, None])
        mats.append((aq | ak).astype(np.float32))
        m = (blk[:, None] == blk[None, :]) & (pos[:, None] >= n // 2) & (pos[None, :] < n // 2)
        masks.append(m.astype(np.float32))
    m = ((t[:, None] // 2) == (t[None, :] // 2)) & (t[None, :] <= t[:, None])
    masks.append(m.astype(np.float32))
    mats.append((j <= t[:, None]).astype(np.float32))
    mats.append((j > t[:, None]).astype(np.float32))
    a_all = np.concatenate(mats, axis=0)
    a_all = np.concatenate([a_all, a_all], axis=1)
    masks = np.stack([np.tile(m, (1, NH)) for m in masks])
    total = masks[:, :, :c].sum(0)
    assert np.array_equal(total, np.tril(np.ones((c, c), np.float32)))
    hm = (np.arange(NH * c)[:, None] // c == np.arange(GW)[None, :] // HD).astype(np.float32)
    maskc = []
    for l, n in enumerate((64, 32, 16, 8, 4)[:NCOARSE]):
        s = np.concatenate([np.arange(lo, lo + COARSE_RUN[l]) for lo in COARSE_STARTS[l]])
        m = (t[:, None] // n == s[None, :] // n) & (t[:, None] % n >= n // 2)
        maskc.append(np.tile(m.astype(np.float32), (1, NH)))
    hmc = (np.arange(NH * (c // 2))[:, None] // (c // 2) == np.arange(GW)[None, :] // HD).astype(np.float32)
    return a_all, masks, hm, np.stack(maskc), hmc


def _regroup_kernel(w_ref, wb_ref, wf_ref):
    g = GW
    o_ff, o_sb, o_hg, o_pl, o_mm = 4 * g, 4 * g + NH, 8 * g + NH, 12 * g + NH, 14 * g + NH
    w = w_ref[0]
    bf_src = {P_FQ: 0, P_FK: g, P_FV: 2 * g, P_FG: 3 * g,
              P_SQ: o_sb, P_SK: o_sb + g, P_SV: o_sb + 2 * g, P_SG: o_sb + 3 * g,
              P_HQ: o_hg, P_HI: o_hg + 2 * g, P_HG: o_hg + 3 * g,
              P_PG: o_pl + g, P_MQ: o_mm, P_MG: o_mm + g}
    f_src = {F_HF: o_hg + g, F_PV: o_pl}
    for dst, c0 in bf_src.items():
        wb_ref[0, :, dst * g:(dst + 1) * g] = w[:, c0:c0 + g].astype(BF16)
    for dst, c0 in f_src.items():
        wf_ref[0, :, dst * g:(dst + 1) * g] = w[:, c0:c0 + g].astype(BF16)
    ff = jnp.concatenate([w[:, o_ff:o_ff + NH], jnp.zeros((w.shape[0], 128 - NH), F32)], axis=1)
    wf_ref[0, :, F_FF * g:F_FF * g + 128] = ff.astype(BF16)


def _regroup_w_in(w_in):
    depth, d, n = w_in.shape
    tr = 128
    return pl.pallas_call(
        _regroup_kernel,
        grid=(depth, d // tr),
        in_specs=[pl.BlockSpec((1, tr, n), lambda l, i: (l, i, 0))],
        out_specs=[
            pl.BlockSpec((1, tr, NP_COLS), lambda l, i: (l, i, 0)),
            pl.BlockSpec((1, tr, NF_COLS), lambda l, i: (l, i, 0)),
        ],
        out_shape=[
            jax.ShapeDtypeStruct((depth, d, NP_COLS), BF16),
            jax.ShapeDtypeStruct((depth, d, NF_COLS), BF16),
        ],
        compiler_params=pltpu.CompilerParams(
            dimension_semantics=("arbitrary", "arbitrary"), vmem_limit_bytes=VMEM_LIMIT),
    )(w_in)


def _proj_kernel(x_ref, g_ref, wb_ref, wf_ref, hgain_ref, bd_ref, pb_ref, pf_ref):
    x = x_ref[...]
    ms = jnp.mean(x * x, axis=-1, keepdims=True)
    h = (x * lax.rsqrt(ms + EPS) * g_ref[...]).astype(BF16)
    gains = hgain_ref[...]
    bd = bd_ref[...]
    normed = {P_FQ: (0, SCALE), P_FK: (1, 1.0), P_MQ: (2, SCALE)}
    post = {P_SQ: lambda r: r * SCALE}
    post.update({gi: _silu for gi in (P_FG, P_SG, P_HQ, P_HG, P_PG, P_MG)})

    def cols(gi):
        return slice(gi * GW, (gi + 1) * GW)

    raw = {gi: _dot(h, wb_ref[:, cols(gi)]) for gi in normed}
    for gi in range(NP_COLS // GW):
        if gi not in normed:
            r = _dot(h, wb_ref[:, cols(gi)])
            pb_ref[:, cols(gi)] = post.get(gi, lambda r: r)(r).astype(BF16)
    pf_ref[...] = _dot(h, wf_ref[...])
    for gi, (row, scale) in normed.items():
        pb_ref[:, cols(gi)] = (_head_rms(raw[gi], gains[row:row + 1], bd) * scale).astype(BF16)


def _proj(x2, g, wb, wf, hgain, bd, tm):
    m = x2.shape[0]
    return pl.pallas_call(
        _proj_kernel,
        grid=(m // tm,),
        in_specs=[
            pl.BlockSpec((tm, D_MODEL), lambda i: (i, 0)),
            pl.BlockSpec((1, D_MODEL), lambda i: (0, 0)),
            pl.BlockSpec((D_MODEL, NP_COLS), lambda i: (0, 0)),
            pl.BlockSpec((D_MODEL, NF_COLS), lambda i: (0, 0)),
            pl.BlockSpec((8, GW), lambda i: (0, 0)),
            pl.BlockSpec((GW, GW), lambda i: (0, 0)),
        ],
        out_specs=[
            pl.BlockSpec((tm, NP_COLS), lambda i: (i, 0)),
            pl.BlockSpec((tm, NF_COLS), lambda i: (i, 0)),
        ],
        out_shape=[
            jax.ShapeDtypeStruct((m, NP_COLS), BF16),
            jax.ShapeDtypeStruct((m, NF_COLS), F32),
        ],
        compiler_params=pltpu.CompilerParams(
            dimension_semantics=("arbitrary",), vmem_limit_bytes=VMEM_LIMIT),
    )(x2, g, wb, wf, hgain, bd)


def _fox_kernel(p_ref, ff_ref, bias_ref, gq_ref, gk_ref, bd_ref, tri_ref, o_ref,
                vx, crow, qk_buf, worst_ref, *, seq):
    bd = bd_ref[...]
    tri = tri_ref[...]
    pb = tri.shape[0]
    hmask = [bd[h * HD:h * HD + 1, :] for h in range(NH)]
    hmask_f = [m.astype(F32) for m in hmask]
    qk_max = (HD * SCALE * NORM_SLACK) * (jnp.max(jnp.abs(gq_ref[...]), axis=-1, keepdims=True)
                                          * jnp.max(jnp.abs(gk_ref[...]), axis=-1, keepdims=True))
    carry = jnp.zeros((8, 1), F32)
    for b in range(seq // pb):
        r = slice(b * pb, (b + 1) * pb)
        lf = _log_sigmoid(ff_ref[0, r, :].T[:8, :] + bias_ref[...])
        cb = _dot_f32_rhs01(lf, tri, 3) + carry
        carry = cb[:, pb - 1:pb]
        crow[:, r] = cb
        v = p_ref[0, r, 2 * GW:3 * GW]
        for h in range(NH):
            vx[h, r, :] = v * hmask[h]

    rel_w = (lax.broadcasted_iota(jnp.int32, (TQ, WIDE), 1) - lax.broadcasted_iota(jnp.int32, (TQ, WIDE), 0))
    nfull = (WIDE - TQ) // TQ

    def qk_all_heads(r0, start, width):
        qb = p_ref[0, pl.ds(r0, TQ), 0:GW]
        qx = jnp.concatenate([qb * m for m in hmask], axis=0)
        return _dot_nt(qx, p_ref[0, pl.ds(start, width), GW:2 * GW])

    def wide_start(r0):
        if isinstance(r0, int):
            return max(r0 - (WIDE - TQ), 0)
        return pl.multiple_of(jnp.maximum(r0 - (WIDE - TQ), 0), TQ)

    def wide_qk(i):
        r0 = _aligned(i * TQ, TQ)
        return qk_all_heads(r0, wide_start(r0), WIDE)

    def q_block(i, full_tile, qk_wide):
        r0 = _aligned(i * TQ, TQ)
        s0 = wide_start(r0)
        rows = pl.ds(r0, TQ)

        def mask_wide(s):
            if full_tile:
                diag = jnp.where(rel_w[:, WIDE - TQ:] <= WIDE - TQ, s[:, WIDE - TQ:], NEG_BIG)
                return jnp.concatenate([s[:, :WIDE - TQ], diag], axis=1)
            return jnp.where(rel_w <= r0 - s0, s, NEG_BIG)

        def tiles(qk, start, width):
            return [qk[h * TQ:(h + 1) * TQ] - crow[h:h + 1, pl.ds(start, width)] for h in range(NH)]

        def weighted_values(ps, start, width):
            p_all = jnp.concatenate([p.astype(BF16) for p in ps], axis=1)
            v_all = jnp.concatenate([vx[h, pl.ds(start, width), :] for h in range(NH)], axis=0)
            return _dot(p_all, v_all)

        def per_head_lanes(cols):
            out = cols[0] * hmask_f[0]
            for h in range(1, NH):
                out = out + cols[h] * hmask_f[h]
            return out

        def weight_bound(s_end, ms):
            last = pl.multiple_of(jnp.maximum(s_end - TQ, 0), TQ)
            bound = None
            for h in range(NH):
                c_last = crow[h:h + 1, pl.ds(last, TQ)][:, TQ - 1:TQ]
                b_h = (qk_max + LOGIT_SLACK) - c_last - jnp.min(ms[h], axis=0, keepdims=True)
                bound = b_h if bound is None else jnp.maximum(bound, b_h)
            return bound

        def keep_going(s_end, ms):
            return (jnp.max(weight_bound(s_end, ms)) > EXP_ZERO).astype(jnp.int32)

        ss = [mask_wide(s) for s in tiles(qk_wide, s0, WIDE)]
        ms = [jnp.max(s, axis=-1, keepdims=True) for s in ss]
        ps = [jnp.exp(s - m) for s, m in zip(ss, ms)]
        ls = [jnp.sum(p, axis=-1, keepdims=True) for p in ps]
        acc = weighted_values(ps, s0, WIDE)

        def emit(acc, ls):
            g = p_ref[0, rows, 3 * GW:4 * GW].astype(F32)
            o_ref[0, rows, :] = (acc * per_head_lanes([1.0 / l for l in ls]) * g).astype(BF16)

        emit(acc, ls)
        if full_tile:
            worst_ref[...] = jnp.maximum(worst_ref[...], jnp.where(s0 > 0, weight_bound(s0, ms), NEG_BIG))

        def cond(st):
            return jnp.logical_and(st[0] > 0, st[1] > 0)

        def body(st):
            s_end, _, ms, ls, acc = st
            sb = pl.multiple_of(s_end - TQ, TQ)
            ss = tiles(qk_all_heads(r0, sb, TQ), sb, TQ)
            ms2 = [jnp.maximum(m, jnp.max(s, axis=-1, keepdims=True)) for s, m in zip(ss, ms)]
            ps = [jnp.exp(s - m) for s, m in zip(ss, ms2)]
            alphas = [jnp.exp(m - m2) for m, m2 in zip(ms, ms2)]
            ls2 = [a * l + jnp.sum(p, axis=-1, keepdims=True) for a, l, p in zip(alphas, ls, ps)]
            acc2 = acc * per_head_lanes(alphas) + weighted_values(ps, sb, TQ)
            return sb, keep_going(sb, ms2), tuple(ms2), tuple(ls2), acc2

        def rare_tail():
            @pl.when(jnp.logical_and(s0 > 0, keep_going(s0, ms) > 0))
            def _():
                _, _, _, ls_f, acc_f = lax.while_loop(cond, body, (s0, jnp.int32(1), tuple(ms), tuple(ls), acc))
                emit(acc_f, ls_f)

        return rare_tail

    nblk = seq // TQ
    grp = FOX_UNROLL
    lead = nfull + (nblk - nfull) % (2 * grp)
    worst_ref[...] = jnp.full(worst_ref.shape, NEG_BIG, F32)
    for i in range(lead):
        q_block(i, i >= nfull, wide_qk(i))
    for k in range(grp):
        qk_buf[k] = wide_qk(lead + k)

    def group(first, cur, nxt):
        for k in range(grp):
            qk_buf[nxt * grp + k] = wide_qk(jnp.minimum(first + grp + k, nblk - 1))
        for k in range(grp):
            q_block(first + k, True, qk_buf[cur * grp + k])

    def two_groups(j, _):
        i = lead + 2 * grp * j
        group(i, 0, 1)
        group(i + grp, 1, 0)
        return 0

    lax.fori_loop(0, (nblk - lead) // (2 * grp), two_groups, 0)

    @pl.when(jnp.max(worst_ref[...]) > EXP_ZERO)
    def _():
        def redo(i, _):
            q_block(i, True, wide_qk(i))()
            return 0

        lax.fori_loop(nfull, nblk, redo, 0)


def _fox(p3, f3, bias, gq, gk, bd, tri):
    b, s, _ = p3.shape
    assert s % tri.shape[0] == 0 and s >= WIDE + 2 * FOX_UNROLL * TQ
    kern = functools.partial(_fox_kernel, seq=s)
    c2 = lambda i: (0, 0)
    return pl.pallas_call(
        kern,
        grid=(b,),
        in_specs=[
            pl.BlockSpec((1, s, 4 * GW), lambda i: (i, 0, P_FQ // 4)),
            pl.BlockSpec((1, s, 128), lambda i: (i, 0, F_FF * GW // 128)),
            pl.BlockSpec((8, tri.shape[0]), c2),
            pl.BlockSpec((1, GW), c2),
            pl.BlockSpec((1, GW), c2),
            pl.BlockSpec((GW, GW), c2),
            pl.BlockSpec(tri.shape, c2),
        ],
        out_specs=pl.BlockSpec((1, s, GW), lambda i: (i, 0, 0)),
        out_shape=jax.ShapeDtypeStruct((b, s, GW), BF16),
        scratch_shapes=[
            pltpu.VMEM((NH, s, GW), BF16),
            pltpu.VMEM((8, s), F32),
            pltpu.VMEM((2 * FOX_UNROLL, NH * TQ, WIDE), F32),
            pltpu.VMEM((8, 128), F32),
        ],
        compiler_params=pltpu.CompilerParams(
            dimension_semantics=("arbitrary",), vmem_limit_bytes=VMEM_LIMIT),
    )(p3, f3, bias, gq, gk, bd, tri)


def _sb_kernel(p_ref, to_ref, tof_ref, bd_ref, o_ref, kt, vx, qk_buf, worst_ref, *, seq):
    to = to_ref[...]
    tq, wide, cw = SB_TQ, SB_WIDE, SB_CW
    hmask = [bd_ref[h * HD:h * HD + 1, :] for h in range(NH)]
    pb = 256
    for b in range(seq // pb):
        r = slice(b * pb, (b + 1) * pb)
        kt[:, r] = p_ref[0, r, GW:2 * GW].T
        v = p_ref[0, r, 2 * GW:3 * GW]
        for h in range(NH):
            vx[h, r, :] = v * hmask[h]

    rel_w = (lax.broadcasted_iota(jnp.int32, (tq, wide), 1) - lax.broadcasted_iota(jnp.int32, (tq, wide), 0))
    nsub = wide // cw
    nfull = -(-(wide - tq) // tq)

    def log_one_minus_sigmoid(z):
        return -jnp.maximum(z, jnp.log(1.0 + jnp.exp(jnp.minimum(z, SOFTPLUS_LINEAR))))

    def suffix_sums(lom, mat):
        w = lom.shape[1]
        cs = _dot(lom.astype(BF16), mat)
        return cs[:, :w], cs[:, w:]

    def qk_all_heads(r0, start, width):
        qb = p_ref[0, pl.ds(r0, tq), 0:GW]
        qx = jnp.concatenate([qb * m for m in hmask], axis=0)
        return _dot(qx, kt[:, pl.ds(start, width)])

    def wide_start(r0):
        if isinstance(r0, int):
            return max(r0 - (wide - tq), 0)
        return pl.multiple_of(jnp.maximum(r0 - (wide - tq), 0), tq)

    def wide_qk(i):
        r0 = _aligned(i * tq, tq)
        return qk_all_heads(r0, wide_start(r0), wide)

    def q_block(i, full_tile, qk_wide):
        r0 = _aligned(i * tq, tq)
        s0 = wide_start(r0)
        rows = pl.ds(r0, tq)

        def mask_wide(x):
            if full_tile:
                diag = jnp.where(rel_w[:, wide - cw:] < wide - tq, x[:, wide - cw:], 0.0)
                return jnp.concatenate([x[:, :wide - cw], diag], axis=1)
            return jnp.where(rel_w < r0 - s0, x, 0.0)

        def per_head(qk):
            return [qk[h * tq:(h + 1) * tq] for h in range(NH)]

        def weighted_values(ws, start, width):
            w_all = jnp.concatenate([w.astype(BF16) for w in ws], axis=1)
            v_all = jnp.concatenate([vx[h, pl.ds(start, width), :] for h in range(NH)], axis=0)
            return _dot(w_all, v_all)

        def emit(acc):
            g = p_ref[0, rows, 3 * GW:4 * GW].astype(F32)
            o_ref[0, rows, :] = (acc * g).astype(BF16)

        zs = per_head(qk_wide)
        loms = [log_one_minus_sigmoid(z) for z in zs]
        log_betas = [z + lom for z, lom in zip(zs, loms)]
        sums = [[suffix_sums(mask_wide(lom)[:, c * cw:(c + 1) * cw], to) for c in range(nsub)] for lom in loms]
        carries, ws = [], []
        for h in range(NH):
            between = [None] * nsub
            carry = jnp.zeros((tq, cw), F32)
            for c in reversed(range(nsub)):
                rc, tot = sums[h][c]
                between[c] = rc + carry
                carry = carry + tot
            ws.append(mask_wide(jnp.exp(log_betas[h] + jnp.concatenate(between, axis=1))))
            carries.append(carry)
        acc = weighted_values(ws, s0, wide)
        emit(acc)

        def worst_carry(carries):
            cm = jnp.maximum(jnp.maximum(carries[0], carries[1]), jnp.maximum(carries[2], carries[3]))
            return jnp.max(cm.reshape(tq // 8, 8, cw), axis=0)

        def keep_going(carries):
            return (jnp.max(worst_carry(carries)) > EXP_ZERO).astype(jnp.int32)

        if full_tile:
            worst_ref[0] = jnp.maximum(worst_ref[0], jnp.where(s0 > 0, worst_carry(carries), NEG_BIG))

        def cond(st):
            return jnp.logical_and(st[0] > 0, st[1] > 0)

        def body(st):
            s_end, _, carries, acc = st
            sb = pl.multiple_of(s_end - tq, tq)
            zs = per_head(qk_all_heads(r0, sb, tq))
            loms = [log_one_minus_sigmoid(z) for z in zs]
            sums = [suffix_sums(lom, tof_ref[...]) for lom in loms]
            ws = [jnp.exp(z + lom + rc + cr[:, :tq]) for z, lom, (rc, _), cr in zip(zs, loms, sums, carries)]
            c2 = [cr + tot for cr, (_, tot) in zip(carries, sums)]
            return sb, keep_going(c2), tuple(c2), acc + weighted_values(ws, sb, tq)

        def rare_tail():
            @pl.when(jnp.logical_and(s0 > 0, keep_going(carries) > 0))
            def _():
                st = lax.while_loop(cond, body, (s0, jnp.int32(1), tuple(carries), acc))
                emit(st[3])

        return rare_tail

    nblk = seq // tq
    grp = SB_UNROLL
    lead = nfull + (nblk - nfull) % (2 * grp)
    worst_ref[0] = jnp.full((8, cw), NEG_BIG, F32)
    for i in range(lead):
        q_block(i, i >= nfull, wide_qk(i))
    for k in range(grp):
        qk_buf[k] = wide_qk(lead + k)

    def group(first, cur, nxt):
        for k in range(grp):
            qk_buf[nxt * grp + k] = wide_qk(jnp.minimum(first + grp + k, nblk - 1))
        for k in range(grp):
            q_block(first + k, True, qk_buf[cur * grp + k])

    def two_groups(j, _):
        i = lead + 2 * grp * j
        group(i, 0, 1)
        group(i + grp, 1, 0)
        return 0

    lax.fori_loop(0, (nblk - lead) // (2 * grp), two_groups, 0)

    @pl.when(jnp.max(worst_ref[0]) > EXP_ZERO)
    def _():
        def redo(i, _):
            q_block(i, True, wide_qk(i))()
            return 0

        lax.fori_loop(nfull, nblk, redo, 0)


def _sb(p3, to, tof, bd):
    b, s, _ = p3.shape
    assert s % 256 == 0 and s >= SB_WIDE + 2 * SB_UNROLL * SB_TQ
    kern = functools.partial(_sb_kernel, seq=s)
    return pl.pallas_call(
        kern,
        grid=(b,),
        in_specs=[
            pl.BlockSpec((1, s, 4 * GW), lambda i: (i, 0, P_SQ // 4)),
            pl.BlockSpec(to.shape, lambda i: (0, 0)),
            pl.BlockSpec(tof.shape, lambda i: (0, 0)),
            pl.BlockSpec((GW, GW), lambda i: (0, 0)),
        ],
        out_specs=pl.BlockSpec((1, s, GW), lambda i: (i, 0, 0)),
        out_shape=jax.ShapeDtypeStruct((b, s, GW), BF16),
        scratch_shapes=[
            pltpu.VMEM((GW, s), BF16),
            pltpu.VMEM((NH, s, GW), BF16),
            pltpu.VMEM((2 * SB_UNROLL, NH * SB_TQ, SB_WIDE), F32),
            pltpu.VMEM((1, 8, SB_CW), F32),
        ],
        compiler_params=pltpu.CompilerParams(
            dimension_semantics=("arbitrary",), vmem_limit_bytes=VMEM_LIMIT),
    )(p3, to, tof, bd)


def _hgrn_kernel(hq_ref, hi_ref, hg_ref, hf_ref, lbm_ref, oml_ref, gout_ref, bd_ref, a_ref,
                 mask_ref, hm_ref, maskc_ref, hmc_ref, o_ref, st_ref, *, seq):
    bd = bd_ref[...]
    a_all = a_ref[...]
    hm = hm_ref[...]
    hm_t = hm.T
    lbm = lbm_ref[...]
    oml = oml_ref[...]
    c = CHUNK
    st_ref[...] = jnp.zeros((GW, GW), F32)

    n = HSB * c
    nlev = mask_ref.shape[0] - 1
    odd =(lax.broadcasted_iota(jnp.int32, (n, GW), 0) & 1) == 1

    def superblock(bi, _):
        r0 = pl.multiple_of(bi * n, n)
        rows = pl.ds(r0, n)
        hf = hf_ref[0, rows, :]
        sg = 1.0 / (1.0 + jnp.exp(-hf))
        f = lbm + oml * sg
        g = jnp.log(f)
        kk = oml * (1.0 - sg)
        q = hq_ref[0, rows, :].astype(F32)
        v = hi_ref[0, rows, :]
        gh = g.astype(BF16)
        gl = (g - gh.astype(F32)).astype(BF16)
        q2 = (q * jnp.where(odd, f, 1.0)).astype(BF16)
        k2 = (kk * jnp.where(odd, 1.0 / f, 1.0)).astype(BF16)

        def scores(qf, kf, l):
            kxt = jnp.concatenate([kf] * NH, axis=0).T * hm_t
            return _dot(qf, kxt) * mask_ref[l]

        qb = q.astype(BF16)
        kb = kk.astype(BF16)
        hm_c = hmc_ref[...]
        hm_ct = hm_c.T

        def key_rows(a, l):
            return jnp.concatenate([a[lo:lo + COARSE_RUN[l]] for lo in COARSE_STARTS[l]], axis=0)

        st = st_ref[...]
        outs = []
        for g0 in range(0, HSB, HGRP):
            sls = [slice(ci * c, (ci + 1) * c) for ci in range(g0, g0 + HGRP)]
            exs = [_dot(a_all, jnp.concatenate([gh[sl], gl[sl]], axis=0)) for sl in sls]

            o_inter = []
            for ci, sl in enumerate(sls):
                eb = exs[ci][nlev * c:(nlev + 1) * c]
                er = exs[ci][(nlev + 1) * c:(nlev + 2) * c]
                qd = (q[sl] * jnp.exp(eb)).astype(BF16)
                o_inter.append(_dot(qd, st.T.astype(BF16) * bd))
                upd = _dot_tn(v[sl], (kk[sl] * jnp.exp(er)).astype(BF16))
                st = st * jnp.exp(eb[c - 1:c, :]) + upd

            ps = [scores(q2[sl], k2[sl], nlev) for sl in sls]
            pcs = [[] for _ in sls]
            for l in range(NCOARSE):
                for ci, sl in enumerate(sls):
                    x = jnp.exp(exs[ci][l * c:(l + 1) * c]).astype(BF16)
                    kxt = jnp.concatenate([key_rows(kb[sl] * x, l)] * NH, axis=0).T * hm_ct
                    pcs[ci].append(_dot(qb[sl] * x, kxt) * maskc_ref[l])
            for l in range(NCOARSE, nlev):
                for ci, sl in enumerate(sls):
                    x = jnp.exp(exs[ci][l * c:(l + 1) * c]).astype(BF16)
                    ps[ci] = ps[ci] + scores(qb[sl] * x, kb[sl] * x, l)
            for ci, sl in enumerate(sls):
                vc = v[sl]
                vxs = [jnp.concatenate([key_rows(vc, l)] * NH, axis=0) * hm_c for l in range(NCOARSE)]
                vx = jnp.concatenate(vxs + [jnp.concatenate([vc] * NH, axis=0) * hm], axis=0)
                p_all = jnp.concatenate(pcs[ci] + [ps[ci]], axis=1).astype(BF16)
                outs.append(o_inter[ci] + _dot(p_all, vx))
        st_ref[...] = st

        o = _head_rms(jnp.concatenate(outs, axis=0), gout_ref[...], bd)
        o_ref[0, rows, :] = (o * hg_ref[0, rows, :].astype(F32)).astype(BF16)
        return 0

    lax.fori_loop(0, seq // n, superblock, 0)


def _hgrn(p3, f3, lbm, oml, gout, bd, a_all, masks, hm, maskc, hmc):
    b, s, _ = p3.shape
    assert s % (HSB * CHUNK) == 0
    kern = functools.partial(_hgrn_kernel, seq=s)
    c2 = lambda i: (0, 0)
    return pl.pallas_call(
        kern,
        grid=(b,),
        in_specs=[
            pl.BlockSpec((1, s, GW), lambda i: (i, 0, P_HQ)),
            pl.BlockSpec((1, s, GW), lambda i: (i, 0, P_HI)),
            pl.BlockSpec((1, s, GW), lambda i: (i, 0, P_HG)),
            pl.BlockSpec((1, s, GW), lambda i: (i, 0, F_HF)),
            pl.BlockSpec((1, GW), c2),
            pl.BlockSpec((1, GW), c2),
            pl.BlockSpec((1, GW), c2),
            pl.BlockSpec((GW, GW), c2),
            pl.BlockSpec(a_all.shape, c2),
            pl.BlockSpec(masks.shape, lambda i: (0, 0, 0)),
            pl.BlockSpec(hm.shape, c2),
            pl.BlockSpec(maskc.shape, lambda i: (0, 0, 0)),
            pl.BlockSpec(hmc.shape, c2),
        ],
        out_specs=pl.BlockSpec((1, s, GW), lambda i: (i, 0, 0)),
        out_shape=jax.ShapeDtypeStruct((b, s, GW), BF16),
        scratch_shapes=[pltpu.VMEM((GW, GW), F32)],
        compiler_params=pltpu.CompilerParams(
            dimension_semantics=("arbitrary",), vmem_limit_bytes=VMEM_LIMIT),
    )(p3, p3, p3, f3, lbm, oml, gout, bd, a_all, masks, hm, maskc, hmc)


def _pm_kernel(pv_ref, pg_ref, mq_ref, mg_ref, mem_ref, mng_ref, wkv_ref, gmk_ref, bd_ref,
               wp_ref, ps_ref, win_ref, d_ref, e_ref, ubuf, wbuf, kmem, vxm, *, seq, tq):
    bd = bd_ref[...]
    hmask = [bd[h * HD:h * HD + 1, :] for h in range(NH)]
    hmask_f = [m.astype(F32) for m in hmask]
    halo = 16
    mem = mem_ref[0]
    ms = jnp.mean(mem * mem, axis=-1, keepdims=True)
    mn = (mem * lax.rsqrt(ms + EPS) * mng_ref[...]).astype(BF16)
    kv = _dot(mn, wkv_ref[...])
    kn = _head_rms(kv[:, :GW], gmk_ref[...], bd).astype(BF16)
    vv = kv[:, GW:].astype(BF16)
    nm = kn.shape[0]
    kmem[...] = kn
    for h in range(NH):
        vxm[h * nm:(h + 1) * nm, :] = vv * hmask[h]

    ubuf[0:halo, :] = jnp.zeros((halo, GW), F32)
    ubuf[halo:halo + seq, :] = pv_ref[0]
    win = win_ref[...]
    inv_win = 1.0 / win
    for k in range(3):
        sh = 1 << k
        wbuf[k, 0:halo, :] = jnp.zeros((halo, GW), F32)
        for b in range(seq // tq):
            base = halo + b * tq
            if k == 0:
                wbuf[k, base:base + tq, :] = ubuf[base:base + tq, :] + ubuf[base - sh:base - sh + tq, :]
            else:
                wbuf[k, base:base + tq, :] = (wbuf[k - 1, base:base + tq, :]
                                              + wbuf[k - 1, base - sh:base - sh + tq, :])

    for b in range(seq // tq):
        r = slice(b * tq, (b + 1) * tq)
        base = halo + b * tq
        u = ubuf[base:base + tq, :]
        s2 = wbuf[0, base:base + tq, :]
        s4 = wbuf[1, base:base + tq, :]
        s8 = wbuf[2, base:base + tq, :]
        s16 = s8 + wbuf[2, base - 8:base - 8 + tq, :]
        sw = jnp.where(win == 2.0, s2, jnp.where(win == 4.0, s4, jnp.where(win == 8.0, s8, s16)))
        if b == 0:
            pos = (lax.broadcasted_iota(jnp.int32, (tq, GW), 0) + 1).astype(F32)
            pooled = sw / jnp.minimum(pos, win)
        else:
            pooled = sw * inv_win
        y = _dot((pooled - u).astype(BF16), wp_ref[...]) * ps_ref[...]
        d_ref[0, r, :] = (y * pg_ref[0, r, :].astype(F32)).astype(BF16)

        qn = mq_ref[0, r, :]
        s_all = _dot_nt(jnp.concatenate([qn * m for m in hmask], axis=0), kmem[...])
        ss = [s_all[h * tq:(h + 1) * tq] for h in range(NH)]
        ps = [jnp.exp(s - jnp.max(s, axis=-1, keepdims=True)) for s in ss]
        ls = [jnp.sum(p, axis=-1, keepdims=True) for p in ps]
        inv_l = (1.0 / ls[0]) * hmask_f[0]
        for h in range(1, NH):
            inv_l = inv_l + (1.0 / ls[h]) * hmask_f[h]
        oe = _dot(jnp.concatenate([p.astype(BF16) for p in ps], axis=1), vxm[...]) * inv_l
        e_ref[0, r, :] = (oe * mg_ref[0, r, :].astype(F32)).astype(BF16)


def _pm(p3, f3, mem, mng, wkv, gmk, bd, wp, ps, win, tq):
    b, s, _ = p3.shape
    nm = mem.shape[1]
    kern = functools.partial(_pm_kernel, seq=s, tq=tq)
    c2 = lambda i: (0, 0)
    return pl.pallas_call(
        kern,
        grid=(b,),
        in_specs=[
            pl.BlockSpec((1, s, GW), lambda i: (i, 0, F_PV)),
            pl.BlockSpec((1, s, GW), lambda i: (i, 0, P_PG)),
            pl.BlockSpec((1, s, GW), lambda i: (i, 0, P_MQ)),
            pl.BlockSpec((1, s, GW), lambda i: (i, 0, P_MG)),
            pl.BlockSpec((1, nm, D_MODEL), lambda i: (i, 0, 0)),
            pl.BlockSpec((1, D_MODEL), c2),
            pl.BlockSpec((D_MODEL, 2 * GW), c2),
            pl.BlockSpec((1, GW), c2),
            pl.BlockSpec((GW, GW), c2),
            pl.BlockSpec((GW, GW), c2),
            pl.BlockSpec((1, GW), c2),
            pl.BlockSpec((1, GW), c2),
        ],
        out_specs=[
            pl.BlockSpec((1, s, GW), lambda i: (i, 0, 0)),
            pl.BlockSpec((1, s, GW), lambda i: (i, 0, 0)),
        ],
        out_shape=[
            jax.ShapeDtypeStruct((b, s, GW), BF16),
            jax.ShapeDtypeStruct((b, s, GW), BF16),
        ],
        scratch_shapes=[
            pltpu.VMEM((s + 16, GW), F32),
            pltpu.VMEM((3, s + 16, GW), F32),
            pltpu.VMEM((nm, GW), BF16),
            pltpu.VMEM((NH * nm, GW), BF16),
        ],
        compiler_params=pltpu.CompilerParams(
            dimension_semantics=("arbitrary",), vmem_limit_bytes=VMEM_LIMIT),
    )(f3, p3, p3, p3, mem, mng, wkv, gmk, bd, wp, ps, win)


def _out_kernel(a_ref, b_ref, c_ref, d_ref, e_ref, w_ref, x_ref, o_ref):
    mixed = jnp.concatenate([a_ref[...], b_ref[...], c_ref[...], d_ref[...], e_ref[...]], axis=1)
    o_ref[...] = x_ref[...] + _dot(mixed, w_ref[...])


def _out(parts, w, x2, tm):
    m = x2.shape[0]
    gspec = pl.BlockSpec((tm, GW), lambda i: (i, 0))
    return pl.pallas_call(
        _out_kernel,
        grid=(m // tm,),
        in_specs=[gspec] * 5 + [
            pl.BlockSpec((5 * GW, D_MODEL), lambda i: (0, 0)),
            pl.BlockSpec((tm, D_MODEL), lambda i: (i, 0)),
        ],
        out_specs=pl.BlockSpec((tm, D_MODEL), lambda i: (i, 0)),
        out_shape=jax.ShapeDtypeStruct((m, D_MODEL), F32),
        compiler_params=pltpu.CompilerParams(
            dimension_semantics=("arbitrary",), vmem_limit_bytes=VMEM_LIMIT),
    )(*parts, w, x2)


def _out_proj_kernel(a_ref, b_ref, c_ref, d_ref, e_ref, w_ref, x_ref, g_ref, wb_ref, wf_ref, hgain_ref, bd_ref,
                     o_ref, pb_ref, pf_ref):
    _out_kernel(a_ref, b_ref, c_ref, d_ref, e_ref, w_ref, x_ref, o_ref)
    _proj_kernel(o_ref, g_ref, wb_ref, wf_ref, hgain_ref, bd_ref, pb_ref, pf_ref)


def _out_proj(parts, w, x2, g, wb, wf, hgain, bd, tm):
    m = x2.shape[0]
    gspec = pl.BlockSpec((tm, GW), lambda i: (i, 0))
    c2 = lambda i: (0, 0)
    return pl.pallas_call(
        _out_proj_kernel,
        grid=(m // tm,),
        in_specs=[gspec] * 5 + [
            pl.BlockSpec((5 * GW, D_MODEL), c2),
            pl.BlockSpec((tm, D_MODEL), lambda i: (i, 0)),
            pl.BlockSpec((1, D_MODEL), c2),
            pl.BlockSpec((D_MODEL, NP_COLS), c2),
            pl.BlockSpec((D_MODEL, NF_COLS), c2),
            pl.BlockSpec((8, GW), c2),
            pl.BlockSpec((GW, GW), c2),
        ],
        out_specs=[
            pl.BlockSpec((tm, D_MODEL), lambda i: (i, 0)),
            pl.BlockSpec((tm, NP_COLS), lambda i: (i, 0)),
            pl.BlockSpec((tm, NF_COLS), lambda i: (i, 0)),
        ],
        out_shape=[
            jax.ShapeDtypeStruct((m, D_MODEL), F32),
            jax.ShapeDtypeStruct((m, NP_COLS), BF16),
            jax.ShapeDtypeStruct((m, NF_COLS), F32),
        ],
        compiler_params=pltpu.CompilerParams(
            dimension_semantics=("arbitrary",), vmem_limit_bytes=VMEM_LIMIT),
    )(*parts, w, x2, g, wb, wf, hgain, bd)


def _tile_heads(g):
    return jnp.tile(g.astype(F32), NH).reshape(1, GW)


def kernel(x, mem, norm_g, w_in, fox_f_bias, fox_q_norm, fox_k_norm, hgrn_lb_logits, hgrn_out_norm,
           pool_w, pool_scale, mem_norm_g, mem_w_kv, mem_q_norm, mem_k_norm, w_out):
    bsz, seq, _ = x.shape
    depth = w_in.shape[0]
    m = bsz * seq
    tq = 256
    tm = 512
    tm_out = 1024
    tq_pm = min(seq, 4 * tq)

    pr = jax.nn.softmax(hgrn_lb_logits.astype(F32), axis=0)
    lower_bounds = jnp.clip(jnp.cumsum(pr, axis=0) - pr[0:1], 0.0, 1.0 - 1e-6)

    bd_np = _bd_ones()
    bd = jnp.asarray(bd_np, BF16)
    tri = jnp.asarray(np.triu(np.ones((tq, tq), np.float32)), BF16)

    def suffix_and_ones(w):
        jj = np.arange(w)
        suffix = (jj[:, None] > jj[None, :]).astype(np.float32)
        return jnp.asarray(np.concatenate([suffix, np.ones((w, SB_CW), np.float32)], axis=1), BF16)

    to = suffix_and_ones(SB_CW)
    tof = suffix_and_ones(SB_TQ)
    a_np, masks_np, hm_np, maskc_np, hmc_np = _hgrn_constants()
    a_all = jnp.asarray(a_np, BF16)
    masks = jnp.asarray(masks_np, F32)
    hm = jnp.asarray(hm_np, BF16)
    maskc = jnp.asarray(maskc_np, F32)
    hmc = jnp.asarray(hmc_np, BF16)
    win = jnp.asarray(np.repeat(np.array(POOL_WINDOWS, np.float32), HD).reshape(1, GW))

    g = GW
    wb_all, wf_all = _regroup_w_in(w_in)
    x2 = x.reshape(m, D_MODEL)
    norm_gs = [norm_g[l].reshape(1, D_MODEL).astype(F32) for l in range(depth)]
    head_gains = [jnp.concatenate([_tile_heads(fox_q_norm[l]), _tile_heads(fox_k_norm[l]),
                                   _tile_heads(mem_q_norm[l]), jnp.zeros((5, GW), F32)], axis=0)
                  for l in range(depth)]
    pb, pf = _proj(x2, norm_gs[0], wb_all[0], wf_all[0], head_gains[0], bd, tm)
    for l in range(depth):
        p3 = pb.reshape(bsz, seq, NP_COLS)
        f3 = pf.reshape(bsz, seq, NF_COLS)

        bias = jnp.broadcast_to(jnp.pad(fox_f_bias[l].astype(F32), (0, 8 - NH))[:, None], (8, tq))
        out_a = _fox(p3, f3, bias, _tile_heads(fox_q_norm[l]), _tile_heads(fox_k_norm[l]), bd, tri)
        out_b = _sb(p3, to, tof, bd)

        lb = lower_bounds[l].reshape(1, g)
        out_c = _hgrn(p3, f3, jnp.maximum(lb, LB_FLOOR), 1.0 - lb,
                      hgrn_out_norm[l].reshape(1, g).astype(F32), bd, a_all, masks, hm, maskc, hmc)

        wp = jax.scipy.linalg.block_diag(*[pool_w[l, i] for i in range(len(POOL_WINDOWS))]).astype(BF16)
        out_d, out_e = _pm(p3, f3, mem, mem_norm_g[l].reshape(1, D_MODEL).astype(F32),
                           mem_w_kv[l].astype(BF16), _tile_heads(mem_k_norm[l]),
                           bd, wp, pool_scale[l].reshape(1, g).astype(F32), win, tq_pm)

        parts = [o.reshape(m, g) for o in (out_a, out_b, out_c, out_d, out_e)]
        if l + 1 < depth:
            x2, pb, pf = _out_proj(parts, w_out[l].astype(BF16), x2, norm_gs[l + 1],
                                   wb_all[l + 1], wf_all[l + 1], head_gains[l + 1], bd, tm)
        else:
            x2 = _out(parts, w_out[l].astype(BF16), x2, tm_out)
    return x2.reshape(bsz, seq, D_MODEL)
```

```python
import functools

import numpy as np
import jax
import jax.numpy as jnp
from jax import lax
from jax.experimental import pallas as pl
from jax.experimental.pallas import tpu as pltpu

F32 = jnp.float32
BF16 = jnp.bfloat16

D_MODEL = 1024
GW = 256
NH = 4
HD = 64
CHUNK = 64
POOL_WINDOWS = (2, 4, 8, 16)
EPS = 1e-6
NEG_BIG = -1e30
LB_FLOOR = 1e-30
SCALE = HD ** -0.5

NP_COLS = 14 * GW
P_FQ, P_FK, P_FV, P_FG, P_SQ, P_SK, P_SV, P_SG, P_HQ, P_HI, P_HG, P_PG, P_MQ, P_MG = range(14)
NF_COLS = 2 * GW + 128
F_HF, F_PV, F_FF = range(3)
VMEM_LIMIT = 56 * 1024 * 1024

TQ = 128
WIDE = 3 * TQ
EXP_ZERO = -104.0
NORM_SLACK = 1.01
LOGIT_SLACK = 0.05
FOX_UNROLL = 2
NCOARSE = 3
COARSE_STARTS = ((0,), (0, 32), (0, 16, 32, 48))
COARSE_RUN = (32, 16, 8)
HSB = 16
HGRP = 4
SB_TQ = 128
SB_WIDE = 384
SB_CW = 128
SB_UNROLL = 2
SOFTPLUS_LINEAR = 80.0


def _dot(a, b):
    return jnp.dot(a, b, preferred_element_type=F32)


def _dot_nt(a, b):
    return lax.dot_general(a, b, (((1,), (1,)), ((), ())), preferred_element_type=F32)


def _dot_tn(a, b):
    return lax.dot_general(a, b, (((0,), (0,)), ((), ())), preferred_element_type=F32)


def _aligned(x, m):
    return x if isinstance(x, int) else pl.multiple_of(x, m)


def _split_bf16(x, n):
    parts = []
    r = x
    for i in range(n):
        p = r.astype(BF16)
        parts.append(p)
        if i + 1 < n:
            r = r - p.astype(F32)
    return parts


def _dot_f32_rhs01(x, m01, n=3):
    acc = None
    for p in _split_bf16(x, n):
        t = _dot(p, m01)
        acc = t if acc is None else acc + t
    return acc


def _silu(x):
    return x / (1.0 + jnp.exp(-x))


def _log_sigmoid(x):
    return jnp.minimum(x, 0.0) - jnp.log(1.0 + jnp.exp(-jnp.abs(x)))


def _head_rms(x, gain, bd):
    ss = _dot((x * x).astype(BF16), bd)
    return x * lax.rsqrt(ss * (1.0 / HD) + EPS) * gain


def _bd_ones():
    h = np.arange(GW) // HD
    return (h[:, None] == h[None, :]).astype(np.float32)


def _hgrn_constants():
    c = CHUNK
    t = np.arange(c)
    j = np.arange(c)[None, :]
    mats = []
    masks = []
    for n in (64, 32, 16, 8, 4):
        blk, pos = t // n, t % n
        ref = blk * n + n // 2 - 1
        aq = (pos[:, None] >= n // 2) & (j > ref[:, None]) & (j <= t[:, None])
        ak = (pos[:, None] < n // 2) & (j > t[:, None]) & (j <= ref[:, None])
        mats.append((aq | ak).astype(np.float32))
        m = (blk[:, None] == blk[None, :]) & (pos[:, None] >= n // 2) & (pos[None, :] < n // 2)
        masks.append(m.astype(np.float32))
    m = ((t[:, None] // 2) == (t[None, :] // 2)) & (t[None, :] <= t[:, None])
    masks.append(m.astype(np.float32))
    mats.append((j <= t[:, None]).astype(np.float32))
    mats.append((j > t[:, None]).astype(np.float32))
    a_all = np.concatenate(mats, axis=0)
    a_all = np.concatenate([a_all, a_all], axis=1)
    masks = np.stack([np.tile(m, (1, NH)) for m in masks])
    total = masks[:, :, :c].sum(0)
    assert np.array_equal(total, np.tril(np.ones((c, c), np.float32)))
    hm = (np.arange(NH * c)[:, None] // c == np.arange(GW)[None, :] // HD).astype(np.float32)
    maskc = []
    for l, n in enumerate((64, 32, 16, 8, 4)[:NCOARSE]):
        s = np.concatenate([np.arange(lo, lo + COARSE_RUN[l]) for lo in COARSE_STARTS[l]])
        m = (t[:, None] // n == s[None, :] // n) & (t[:, None] % n >= n // 2)
        maskc.append(np.tile(m.astype(np.float32), (1, NH)))
    hmc = (np.arange(NH * (c // 2))[:, None] // (c // 2) == np.arange(GW)[None, :] // HD).astype(np.float32)
    return a_all, masks, hm, np.stack(maskc), hmc


def _regroup_kernel(w_ref, wb_ref, wf_ref):
    g = GW
    o_ff, o_sb, o_hg, o_pl, o_mm = 4 * g, 4 * g + NH, 8 * g + NH, 12 * g + NH, 14 * g + NH
    w = w_ref[0]
    bf_src = {P_FQ: 0, P_FK: g, P_FV: 2 * g, P_FG: 3 * g,
              P_SQ: o_sb, P_SK: o_sb + g, P_SV: o_sb + 2 * g, P_SG: o_sb + 3 * g,
              P_HQ: o_hg, P_HI: o_hg + 2 * g, P_HG: o_hg + 3 * g,
              P_PG: o_pl + g, P_MQ: o_mm, P_MG: o_mm + g}
    f_src = {F_HF: o_hg + g, F_PV: o_pl}
    for dst, c0 in bf_src.items():
        wb_ref[0, :, dst * g:(dst + 1) * g] = w[:, c0:c0 + g].astype(BF16)
    for dst, c0 in f_src.items():
        wf_ref[0, :, dst * g:(dst + 1) * g] = w[:, c0:c0 + g].astype(BF16)
    ff = jnp.concatenate([w[:, o_ff:o_ff + NH], jnp.zeros((w.shape[0], 128 - NH), F32)], axis=1)
    wf_ref[0, :, F_FF * g:F_FF * g + 128] = ff.astype(BF16)


def _regroup_w_in(w_in):
    depth, d, n = w_in.shape
    tr = 128
    return pl.pallas_call(
        _regroup_kernel,
        grid=(depth, d // tr),
        in_specs=[pl.BlockSpec((1, tr, n), lambda l, i: (l, i, 0))],
        out_specs=[
            pl.BlockSpec((1, tr, NP_COLS), lambda l, i: (l, i, 0)),
            pl.BlockSpec((1, tr, NF_COLS), lambda l, i: (l, i, 0)),
        ],
        out_shape=[
            jax.ShapeDtypeStruct((depth, d, NP_COLS), BF16),
            jax.ShapeDtypeStruct((depth, d, NF_COLS), BF16),
        ],
        compiler_params=pltpu.CompilerParams(
            dimension_semantics=("arbitrary", "arbitrary"), vmem_limit_bytes=VMEM_LIMIT),
    )(w_in)


def _proj_kernel(x_ref, g_ref, wb_ref, wf_ref, hgain_ref, bd_ref, pb_ref, pf_ref):
    x = x_ref[...]
    ms = jnp.mean(x * x, axis=-1, keepdims=True)
    h = (x * lax.rsqrt(ms + EPS) * g_ref[...]).astype(BF16)
    gains = hgain_ref[...]
    bd = bd_ref[...]
    normed = {P_FQ: (0, SCALE), P_FK: (1, 1.0), P_MQ: (2, SCALE)}
    post = {P_SQ: lambda r: r * SCALE}
    post.update({gi: _silu for gi in (P_FG, P_SG, P_HQ, P_HG, P_PG, P_MG)})

    def cols(gi):
        return slice(gi * GW, (gi + 1) * GW)

    raw = {gi: _dot(h, wb_ref[:, cols(gi)]) for gi in normed}
    for gi in range(NP_COLS // GW):
        if gi not in normed:
            r = _dot(h, wb_ref[:, cols(gi)])
            pb_ref[:, cols(gi)] = post.get(gi, lambda r: r)(r).astype(BF16)
    pf_ref[...] = _dot(h, wf_ref[...])
    for gi, (row, scale) in normed.items():
        pb_ref[:, cols(gi)] = (_head_rms(raw[gi], gains[row:row + 1], bd) * scale).astype(BF16)


def _proj(x2, g, wb, wf, hgain, bd, tm):
    m = x2.shape[0]
    return pl.pallas_call(
        _proj_kernel,
        grid=(m // tm,),
        in_specs=[
            pl.BlockSpec((tm, D_MODEL), lambda i: (i, 0)),
            pl.BlockSpec((1, D_MODEL), lambda i: (0, 0)),
            pl.BlockSpec((D_MODEL, NP_COLS), lambda i: (0, 0)),
            pl.BlockSpec((D_MODEL, NF_COLS), lambda i: (0, 0)),
            pl.BlockSpec((8, GW), lambda i: (0, 0)),
            pl.BlockSpec((GW, GW), lambda i: (0, 0)),
        ],
        out_specs=[
            pl.BlockSpec((tm, NP_COLS), lambda i: (i, 0)),
            pl.BlockSpec((tm, NF_COLS), lambda i: (i, 0)),
        ],
        out_shape=[
            jax.ShapeDtypeStruct((m, NP_COLS), BF16),
            jax.ShapeDtypeStruct((m, NF_COLS), F32),
        ],
        compiler_params=pltpu.CompilerParams(
            dimension_semantics=("arbitrary",), vmem_limit_bytes=VMEM_LIMIT),
    )(x2, g, wb, wf, hgain, bd)


def _fox_kernel(p_ref, ff_ref, bias_ref, gq_ref, gk_ref, bd_ref, tri_ref, o_ref,
                vx, crow, qk_buf, worst_ref, *, seq):
    bd = bd_ref[...]
    tri = tri_ref[...]
    pb = tri.shape[0]
    hmask = [bd[h * HD:h * HD + 1, :] for h in range(NH)]
    hmask_f = [m.astype(F32) for m in hmask]
    qk_max = (HD * SCALE * NORM_SLACK) * (jnp.max(jnp.abs(gq_ref[...]), axis=-1, keepdims=True)
                                          * jnp.max(jnp.abs(gk_ref[...]), axis=-1, keepdims=True))
    carry = jnp.zeros((8, 1), F32)
    for b in range(seq // pb):
        r = slice(b * pb, (b + 1) * pb)
        lf = _log_sigmoid(ff_ref[0, r, :].T[:8, :] + bias_ref[...])
        cb = _dot_f32_rhs01(lf, tri, 3) + carry
        carry = cb[:, pb - 1:pb]
        crow[:, r] = cb
        v = p_ref[0, r, 2 * GW:3 * GW]
        for h in range(NH):
            vx[h, r, :] = v * hmask[h]

    rel_w = (lax.broadcasted_iota(jnp.int32, (TQ, WIDE), 1) - lax.broadcasted_iota(jnp.int32, (TQ, WIDE), 0))
    nfull = (WIDE - TQ) // TQ

    def qk_all_heads(r0, start, width):
        qb = p_ref[0, pl.ds(r0, TQ), 0:GW]
        qx = jnp.concatenate([qb * m for m in hmask], axis=0)
        return _dot_nt(qx, p_ref[0, pl.ds(start, width), GW:2 * GW])

    def wide_start(r0):
        if isinstance(r0, int):
            return max(r0 - (WIDE - TQ), 0)
        return pl.multiple_of(jnp.maximum(r0 - (WIDE - TQ), 0), TQ)

    def wide_qk(i):
        r0 = _aligned(i * TQ, TQ)
        return qk_all_heads(r0, wide_start(r0), WIDE)

    def q_block(i, full_tile, qk_wide):
        r0 = _aligned(i * TQ, TQ)
        s0 = wide_start(r0)
        rows = pl.ds(r0, TQ)

        def mask_wide(s):
            if full_tile:
                diag = jnp.where(rel_w[:, WIDE - TQ:] <= WIDE - TQ, s[:, WIDE - TQ:], NEG_BIG)
                return jnp.concatenate([s[:, :WIDE - TQ], diag], axis=1)
            return jnp.where(rel_w <= r0 - s0, s, NEG_BIG)

        def tiles(qk, start, width):
            return [qk[h * TQ:(h + 1) * TQ] - crow[h:h + 1, pl.ds(start, width)] for h in range(NH)]

        def weighted_values(ps, start, width):
            p_all = jnp.concatenate([p.astype(BF16) for p in ps], axis=1)
            v_all = jnp.concatenate([vx[h, pl.ds(start, width), :] for h in range(NH)], axis=0)
            return _dot(p_all, v_all)

        def per_head_lanes(cols):
            out = cols[0] * hmask_f[0]
            for h in range(1, NH):
                out = out + cols[h] * hmask_f[h]
            return out

        def weight_bound(s_end, ms):
            last = pl.multiple_of(jnp.maximum(s_end - TQ, 0), TQ)
            bound = None
            for h in range(NH):
                c_last = crow[h:h + 1, pl.ds(last, TQ)][:, TQ - 1:TQ]
                b_h = (qk_max + LOGIT_SLACK) - c_last - jnp.min(ms[h], axis=0, keepdims=True)
                bound = b_h if bound is None else jnp.maximum(bound, b_h)
            return bound

        def keep_going(s_end, ms):
            return (jnp.max(weight_bound(s_end, ms)) > EXP_ZERO).astype(jnp.int32)

        ss = [mask_wide(s) for s in tiles(qk_wide, s0, WIDE)]
        ms = [jnp.max(s, axis=-1, keepdims=True) for s in ss]
        ps = [jnp.exp(s - m) for s, m in zip(ss, ms)]
        ls = [jnp.sum(p, axis=-1, keepdims=True) for p in ps]
        acc = weighted_values(ps, s0, WIDE)

        def emit(acc, ls):
            g = p_ref[0, rows, 3 * GW:4 * GW].astype(F32)
            o_ref[0, rows, :] = (acc * per_head_lanes([1.0 / l for l in ls]) * g).astype(BF16)

        emit(acc, ls)
        if full_tile:
            worst_ref[...] = jnp.maximum(worst_ref[...], jnp.where(s0 > 0, weight_bound(s0, ms), NEG_BIG))

        def cond(st):
            return jnp.logical_and(st[0] > 0, st[1] > 0)

        def body(st):
            s_end, _, ms, ls, acc = st
            sb = pl.multiple_of(s_end - TQ, TQ)
            ss = tiles(qk_all_heads(r0, sb, TQ), sb, TQ)
            ms2 = [jnp.maximum(m, jnp.max(s, axis=-1, keepdims=True)) for s, m in zip(ss, ms)]
            ps = [jnp.exp(s - m) for s, m in zip(ss, ms2)]
            alphas = [jnp.exp(m - m2) for m, m2 in zip(ms, ms2)]
            ls2 = [a * l + jnp.sum(p, axis=-1, keepdims=True) for a, l, p in zip(alphas, ls, ps)]
            acc2 = acc * per_head_lanes(alphas) + weighted_values(ps, sb, TQ)
            return sb, keep_going(sb, ms2), tuple(ms2), tuple(ls2), acc2

        def rare_tail():
            @pl.when(jnp.logical_and(s0 > 0, keep_going(s0, ms) > 0))
            def _():
                _, _, _, ls_f, acc_f = lax.while_loop(cond, body, (s0, jnp.int32(1), tuple(ms), tuple(ls), acc))
                emit(acc_f, ls_f)

        return rare_tail

    nblk = seq // TQ
    grp = FOX_UNROLL
    lead = nfull + (nblk - nfull) % (2 * grp)
    worst_ref[...] = jnp.full(worst_ref.shape, NEG_BIG, F32)
    for i in range(lead):
        q_block(i, i >= nfull, wide_qk(i))
    for k in range(grp):
        qk_buf[k] = wide_qk(lead + k)

    def group(first, cur, nxt):
        for k in range(grp):
            qk_buf[nxt * grp + k] = wide_qk(jnp.minimum(first + grp + k, nblk - 1))
        for k in range(grp):
            q_block(first + k, True, qk_buf[cur * grp + k])

    def two_groups(j, _):
        i = lead + 2 * grp * j
        group(i, 0, 1)
        group(i + grp, 1, 0)
        return 0

    lax.fori_loop(0, (nblk - lead) // (2 * grp), two_groups, 0)

    @pl.when(jnp.max(worst_ref[...]) > EXP_ZERO)
    def _():
        def redo(i, _):
            q_block(i, True, wide_qk(i))()
            return 0

        lax.fori_loop(nfull, nblk, redo, 0)


def _fox(p3, f3, bias, gq, gk, bd, tri):
    b, s, _ = p3.shape
    assert s % tri.shape[0] == 0 and s >= WIDE + 2 * FOX_UNROLL * TQ
    kern = functools.partial(_fox_kernel, seq=s)
    c2 = lambda i: (0, 0)
    return pl.pallas_call(
        kern,
        grid=(b,),
        in_specs=[
            pl.BlockSpec((1, s, 4 * GW), lambda i: (i, 0, P_FQ // 4)),
            pl.BlockSpec((1, s, 128), lambda i: (i, 0, F_FF * GW // 128)),
            pl.BlockSpec((8, tri.shape[0]), c2),
            pl.BlockSpec((1, GW), c2),
            pl.BlockSpec((1, GW), c2),
            pl.BlockSpec((GW, GW), c2),
            pl.BlockSpec(tri.shape, c2),
        ],
        out_specs=pl.BlockSpec((1, s, GW), lambda i: (i, 0, 0)),
        out_shape=jax.ShapeDtypeStruct((b, s, GW), BF16),
        scratch_shapes=[
            pltpu.VMEM((NH, s, GW), BF16),
            pltpu.VMEM((8, s), F32),
            pltpu.VMEM((2 * FOX_UNROLL, NH * TQ, WIDE), F32),
            pltpu.VMEM((8, 128), F32),
        ],
        compiler_params=pltpu.CompilerParams(
            dimension_semantics=("arbitrary",), vmem_limit_bytes=VMEM_LIMIT),
    )(p3, f3, bias, gq, gk, bd, tri)


def _sb_kernel(p_ref, to_ref, tof_ref, bd_ref, o_ref, kt, vx, qk_buf, worst_ref, *, seq):
    to = to_ref[...]
    tq, wide, cw = SB_TQ, SB_WIDE, SB_CW
    hmask = [bd_ref[h * HD:h * HD + 1, :] for h in range(NH)]
    pb = 256
    for b in range(seq // pb):
        r = slice(b * pb, (b + 1) * pb)
        kt[:, r] = p_ref[0, r, GW:2 * GW].T
        v = p_ref[0, r, 2 * GW:3 * GW]
        for h in range(NH):
            vx[h, r, :] = v * hmask[h]

    rel_w = (lax.broadcasted_iota(jnp.int32, (tq, wide), 1) - lax.broadcasted_iota(jnp.int32, (tq, wide), 0))
    nsub = wide // cw
    nfull = -(-(wide - tq) // tq)

    def log_one_minus_sigmoid(z):
        return -jnp.maximum(z, jnp.log(1.0 + jnp.exp(jnp.minimum(z, SOFTPLUS_LINEAR))))

    def suffix_sums(lom, mat):
        w = lom.shape[1]
        cs = _dot(lom.astype(BF16), mat)
        return cs[:, :w], cs[:, w:]

    def qk_all_heads(r0, start, width):
        qb = p_ref[0, pl.ds(r0, tq), 0:GW]
        qx = jnp.concatenate([qb * m for m in hmask], axis=0)
        return _dot(qx, kt[:, pl.ds(start, width)])

    def wide_start(r0):
        if isinstance(r0, int):
            return max(r0 - (wide - tq), 0)
        return pl.multiple_of(jnp.maximum(r0 - (wide - tq), 0), tq)

    def wide_qk(i):
        r0 = _aligned(i * tq, tq)
        return qk_all_heads(r0, wide_start(r0), wide)

    def q_block(i, full_tile, qk_wide):
        r0 = _aligned(i * tq, tq)
        s0 = wide_start(r0)
        rows = pl.ds(r0, tq)

        def mask_wide(x):
            if full_tile:
                diag = jnp.where(rel_w[:, wide - cw:] < wide - tq, x[:, wide - cw:], 0.0)
                return jnp.concatenate([x[:, :wide - cw], diag], axis=1)
            return jnp.where(rel_w < r0 - s0, x, 0.0)

        def per_head(qk):
            return [qk[h * tq:(h + 1) * tq] for h in range(NH)]

        def weighted_values(ws, start, width):
            w_all = jnp.concatenate([w.astype(BF16) for w in ws], axis=1)
            v_all = jnp.concatenate([vx[h, pl.ds(start, width), :] for h in range(NH)], axis=0)
            return _dot(w_all, v_all)

        def emit(acc):
            g = p_ref[0, rows, 3 * GW:4 * GW].astype(F32)
            o_ref[0, rows, :] = (acc * g).astype(BF16)

        zs = per_head(qk_wide)
        loms = [log_one_minus_sigmoid(z) for z in zs]
        log_betas = [z + lom for z, lom in zip(zs, loms)]
        sums = [[suffix_sums(mask_wide(lom)[:, c * cw:(c + 1) * cw], to) for c in range(nsub)] for lom in loms]
        carries, ws = [], []
        for h in range(NH):
            between = [None] * nsub
            carry = jnp.zeros((tq, cw), F32)
            for c in reversed(range(nsub)):
                rc, tot = sums[h][c]
                between[c] = rc + carry
                carry = carry + tot
            ws.append(mask_wide(jnp.exp(log_betas[h] + jnp.concatenate(between, axis=1))))
            carries.append(carry)
        acc = weighted_values(ws, s0, wide)
        emit(acc)

        def worst_carry(carries):
            cm = jnp.maximum(jnp.maximum(carries[0], carries[1]), jnp.maximum(carries[2], carries[3]))
            return jnp.max(cm.reshape(tq // 8, 8, cw), axis=0)

        def keep_going(carries):
            return (jnp.max(worst_carry(carries)) > EXP_ZERO).astype(jnp.int32)

        if full_tile:
            worst_ref[0] = jnp.maximum(worst_ref[0], jnp.where(s0 > 0, worst_carry(carries), NEG_BIG))

        def cond(st):
            return jnp.logical_and(st[0] > 0, st[1] > 0)

        def body(st):
            s_end, _, carries, acc = st
            sb = pl.multiple_of(s_end - tq, tq)
            zs = per_head(qk_all_heads(r0, sb, tq))
            loms = [log_one_minus_sigmoid(z) for z in zs]
            sums = [suffix_sums(lom, tof_ref[...]) for lom in loms]
            ws = [jnp.exp(z + lom + rc + cr[:, :tq]) for z, lom, (rc, _), cr in zip(zs, loms, sums, carries)]
            c2 = [cr + tot for cr, (_, tot) in zip(carries, sums)]
            return sb, keep_going(c2), tuple(c2), acc + weighted_values(ws, sb, tq)

        def rare_tail():
            @pl.when(jnp.logical_and(s0 > 0, keep_going(carries) > 0))
            def _():
                st = lax.while_loop(cond, body, (s0, jnp.int32(1), tuple(carries), acc))
                emit(st[3])

        return rare_tail

    nblk = seq // tq
    grp = SB_UNROLL
    lead = nfull + (nblk - nfull) % (2 * grp)
    worst_ref[0] = jnp.full((8, cw), NEG_BIG, F32)
    for i in range(lead):
        q_block(i, i >= nfull, wide_qk(i))
    for k in range(grp):
        qk_buf[k] = wide_qk(lead + k)

    def group(first, cur, nxt):
        for k in range(grp):
            qk_buf[nxt * grp + k] = wide_qk(jnp.minimum(first + grp + k, nblk - 1))
        for k in range(grp):
            q_block(first + k, True, qk_buf[cur * grp + k])

    def two_groups(j, _):
        i = lead + 2 * grp * j
        group(i, 0, 1)
        group(i + grp, 1, 0)
        return 0

    lax.fori_loop(0, (nblk - lead) // (2 * grp), two_groups, 0)

    @pl.when(jnp.max(worst_ref[0]) > EXP_ZERO)
    def _():
        def redo(i, _):
            q_block(i, True, wide_qk(i))()
            return 0

        lax.fori_loop(nfull, nblk, redo, 0)


def _sb(p3, to, tof, bd):
    b, s, _ = p3.shape
    assert s % 256 == 0 and s >= SB_WIDE + 2 * SB_UNROLL * SB_TQ
    kern = functools.partial(_sb_kernel, seq=s)
    return pl.pallas_call(
        kern,
        grid=(b,),
        in_specs=[
            pl.BlockSpec((1, s, 4 * GW), lambda i: (i, 0, P_SQ // 4)),
            pl.BlockSpec(to.shape, lambda i: (0, 0)),
            pl.BlockSpec(tof.shape, lambda i: (0, 0)),
            pl.BlockSpec((GW, GW), lambda i: (0, 0)),
        ],
        out_specs=pl.BlockSpec((1, s, GW), lambda i: (i, 0, 0)),
        out_shape=jax.ShapeDtypeStruct((b, s, GW), BF16),
        scratch_shapes=[
            pltpu.VMEM((GW, s), BF16),
            pltpu.VMEM((NH, s, GW), BF16),
            pltpu.VMEM((2 * SB_UNROLL, NH * SB_TQ, SB_WIDE), F32),
            pltpu.VMEM((1, 8, SB_CW), F32),
        ],
        compiler_params=pltpu.CompilerParams(
            dimension_semantics=("arbitrary",), vmem_limit_bytes=VMEM_LIMIT),
    )(p3, to, tof, bd)


def _hgrn_kernel(hq_ref, hi_ref, hg_ref, hf_ref, lbm_ref, oml_ref, gout_ref, bd_ref, a_ref,
                 mask_ref, hm_ref, maskc_ref, hmc_ref, o_ref, st_ref, *, seq):
    bd = bd_ref[...]
    a_all = a_ref[...]
    hm = hm_ref[...]
    hm_t = hm.T
    lbm = lbm_ref[...]
    oml = oml_ref[...]
    c = CHUNK
    st_ref[...] = jnp.zeros((GW, GW), F32)

    n = HSB * c
    nlev = mask_ref.shape[0] - 1
    odd =(lax.broadcasted_iota(jnp.int32, (n, GW), 0) & 1) == 1

    def superblock(bi, _):
        r0 = pl.multiple_of(bi * n, n)
        rows = pl.ds(r0, n)
        hf = hf_ref[0, rows, :]
        sg = 1.0 / (1.0 + jnp.exp(-hf))
        f = lbm + oml * sg
        g = jnp.log(f)
        kk = oml * (1.0 - sg)
        q = hq_ref[0, rows, :].astype(F32)
        v = hi_ref[0, rows, :]
        gh = g.astype(BF16)
        gl = (g - gh.astype(F32)).astype(BF16)
        q2 = (q * jnp.where(odd, f, 1.0)).astype(BF16)
        k2 = (kk * jnp.where(odd, 1.0 / f, 1.0)).astype(BF16)

        def scores(qf, kf, l):
            kxt = jnp.concatenate([kf] * NH, axis=0).T * hm_t
            return _dot(qf, kxt) * mask_ref[l]

        qb = q.astype(BF16)
        kb = kk.astype(BF16)
        hm_c = hmc_ref[...]
        hm_ct = hm_c.T

        def key_rows(a, l):
            return jnp.concatenate([a[lo:lo + COARSE_RUN[l]] for lo in COARSE_STARTS[l]], axis=0)

        st = st_ref[...]
        outs = []
        for g0 in range(0, HSB, HGRP):
            sls = [slice(ci * c, (ci + 1) * c) for ci in range(g0, g0 + HGRP)]
            exs = [_dot(a_all, jnp.concatenate([gh[sl], gl[sl]], axis=0)) for sl in sls]

            o_inter = []
            for ci, sl in enumerate(sls):
                eb = exs[ci][nlev * c:(nlev + 1) * c]
                er = exs[ci][(nlev + 1) * c:(nlev + 2) * c]
                qd = (q[sl] * jnp.exp(eb)).astype(BF16)
                o_inter.append(_dot(qd, st.T.astype(BF16) * bd))
                upd = _dot_tn(v[sl], (kk[sl] * jnp.exp(er)).astype(BF16))
                st = st * jnp.exp(eb[c - 1:c, :]) + upd

            ps = [scores(q2[sl], k2[sl], nlev) for sl in sls]
            pcs = [[] for _ in sls]
            for l in range(NCOARSE):
                for ci, sl in enumerate(sls):
                    x = jnp.exp(exs[ci][l * c:(l + 1) * c]).astype(BF16)
                    kxt = jnp.concatenate([key_rows(kb[sl] * x, l)] * NH, axis=0).T * hm_ct
                    pcs[ci].append(_dot(qb[sl] * x, kxt) * maskc_ref[l])
            for l in range(NCOARSE, nlev):
                for ci, sl in enumerate(sls):
                    x = jnp.exp(exs[ci][l * c:(l + 1) * c]).astype(BF16)
                    ps[ci] = ps[ci] + scores(qb[sl] * x, kb[sl] * x, l)
            for ci, sl in enumerate(sls):
                vc = v[sl]
                vxs = [jnp.concatenate([key_rows(vc, l)] * NH, axis=0) * hm_c for l in range(NCOARSE)]
                vx = jnp.concatenate(vxs + [jnp.concatenate([vc] * NH, axis=0) * hm], axis=0)
                p_all = jnp.concatenate(pcs[ci] + [ps[ci]], axis=1).astype(BF16)
                outs.append(o_inter[ci] + _dot(p_all, vx))
        st_ref[...] = st

        o = _head_rms(jnp.concatenate(outs, axis=0), gout_ref[...], bd)
        o_ref[0, rows, :] = (o * hg_ref[0, rows, :].astype(F32)).astype(BF16)
        return 0

    lax.fori_loop(0, seq // n, superblock, 0)


def _hgrn(p3, f3, lbm, oml, gout, bd, a_all, masks, hm, maskc, hmc):
    b, s, _ = p3.shape
    assert s % (HSB * CHUNK) == 0
    kern = functools.partial(_hgrn_kernel, seq=s)
    c2 = lambda i: (0, 0)
    return pl.pallas_call(
        kern,
        grid=(b,),
        in_specs=[
            pl.BlockSpec((1, s, GW), lambda i: (i, 0, P_HQ)),
            pl.BlockSpec((1, s, GW), lambda i: (i, 0, P_HI)),
            pl.BlockSpec((1, s, GW), lambda i: (i, 0, P_HG)),
            pl.BlockSpec((1, s, GW), lambda i: (i, 0, F_HF)),
            pl.BlockSpec((1, GW), c2),
            pl.BlockSpec((1, GW), c2),
            pl.BlockSpec((1, GW), c2),
            pl.BlockSpec((GW, GW), c2),
            pl.BlockSpec(a_all.shape, c2),
            pl.BlockSpec(masks.shape, lambda i: (0, 0, 0)),
            pl.BlockSpec(hm.shape, c2),
            pl.BlockSpec(maskc.shape, lambda i: (0, 0, 0)),
            pl.BlockSpec(hmc.shape, c2),
        ],
        out_specs=pl.BlockSpec((1, s, GW), lambda i: (i, 0, 0)),
        out_shape=jax.ShapeDtypeStruct((b, s, GW), BF16),
        scratch_shapes=[pltpu.VMEM((GW, GW), F32)],
        compiler_params=pltpu.CompilerParams(
            dimension_semantics=("arbitrary",), vmem_limit_bytes=VMEM_LIMIT),
    )(p3, p3, p3, f3, lbm, oml, gout, bd, a_all, masks, hm, maskc, hmc)


def _pm_kernel(pv_ref, pg_ref, mq_ref, mg_ref, mem_ref, mng_ref, wkv_ref, gmk_ref, bd_ref,
               wp_ref, ps_ref, win_ref, d_ref, e_ref, ubuf, wbuf, kmem, vxm, *, seq, tq):
    bd = bd_ref[...]
    hmask = [bd[h * HD:h * HD + 1, :] for h in range(NH)]
    hmask_f = [m.astype(F32) for m in hmask]
    halo = 16
    mem = mem_ref[0]
    ms = jnp.mean(mem * mem, axis=-1, keepdims=True)
    mn = (mem * lax.rsqrt(ms + EPS) * mng_ref[...]).astype(BF16)
    kv = _dot(mn, wkv_ref[...])
    kn = _head_rms(kv[:, :GW], gmk_ref[...], bd).astype(BF16)
    vv = kv[:, GW:].astype(BF16)
    nm = kn.shape[0]
    kmem[...] = kn
    for h in range(NH):
        vxm[h * nm:(h + 1) * nm, :] = vv * hmask[h]

    ubuf[0:halo, :] = jnp.zeros((halo, GW), F32)
    ubuf[halo:halo + seq, :] = pv_ref[0]
    win = win_ref[...]
    inv_win = 1.0 / win
    for k in range(3):
        sh = 1 << k
        wbuf[k, 0:halo, :] = jnp.zeros((halo, GW), F32)
        for b in range(seq // tq):
            base = halo + b * tq
            if k == 0:
                wbuf[k, base:base + tq, :] = ubuf[base:base + tq, :] + ubuf[base - sh:base - sh + tq, :]
            else:
                wbuf[k, base:base + tq, :] = (wbuf[k - 1, base:base + tq, :]
                                              + wbuf[k - 1, base - sh:base - sh + tq, :])

    for b in range(seq // tq):
        r = slice(b * tq, (b + 1) * tq)
        base = halo + b * tq
        u = ubuf[base:base + tq, :]
        s2 = wbuf[0, base:base + tq, :]
        s4 = wbuf[1, base:base + tq, :]
        s8 = wbuf[2, base:base + tq, :]
        s16 = s8 + wbuf[2, base - 8:base - 8 + tq, :]
        sw = jnp.where(win == 2.0, s2, jnp.where(win == 4.0, s4, jnp.where(win == 8.0, s8, s16)))
        if b == 0:
            pos = (lax.broadcasted_iota(jnp.int32, (tq, GW), 0) + 1).astype(F32)
            pooled = sw / jnp.minimum(pos, win)
        else:
            pooled = sw * inv_win
        y = _dot((pooled - u).astype(BF16), wp_ref[...]) * ps_ref[...]
        d_ref[0, r, :] = (y * pg_ref[0, r, :].astype(F32)).astype(BF16)

        qn = mq_ref[0, r, :]
        s_all = _dot_nt(jnp.concatenate([qn * m for m in hmask], axis=0), kmem[...])
        ss = [s_all[h * tq:(h + 1) * tq] for h in range(NH)]
        ps = [jnp.exp(s - jnp.max(s, axis=-1, keepdims=True)) for s in ss]
        ls = [jnp.sum(p, axis=-1, keepdims=True) for p in ps]
        inv_l = (1.0 / ls[0]) * hmask_f[0]
        for h in range(1, NH):
            inv_l = inv_l + (1.0 / ls[h]) * hmask_f[h]
        oe = _dot(jnp.concatenate([p.astype(BF16) for p in ps], axis=1), vxm[...]) * inv_l
        e_ref[0, r, :] = (oe * mg_ref[0, r, :].astype(F32)).astype(BF16)


def _pm(p3, f3, mem, mng, wkv, gmk, bd, wp, ps, win, tq):
    b, s, _ = p3.shape
    nm = mem.shape[1]
    kern = functools.partial(_pm_kernel, seq=s, tq=tq)
    c2 = lambda i: (0, 0)
    return pl.pallas_call(
        kern,
        grid=(b,),
        in_specs=[
            pl.BlockSpec((1, s, GW), lambda i: (i, 0, F_PV)),
            pl.BlockSpec((1, s, GW), lambda i: (i, 0, P_PG)),
            pl.BlockSpec((1, s, GW), lambda i: (i, 0, P_MQ)),
            pl.BlockSpec((1, s, GW), lambda i: (i, 0, P_MG)),
            pl.BlockSpec((1, nm, D_MODEL), lambda i: (i, 0, 0)),
            pl.BlockSpec((1, D_MODEL), c2),
            pl.BlockSpec((D_MODEL, 2 * GW), c2),
            pl.BlockSpec((1, GW), c2),
            pl.BlockSpec((GW, GW), c2),
            pl.BlockSpec((GW, GW), c2),
            pl.BlockSpec((1, GW), c2),
            pl.BlockSpec((1, GW), c2),
        ],
        out_specs=[
            pl.BlockSpec((1, s, GW), lambda i: (i, 0, 0)),
            pl.BlockSpec((1, s, GW), lambda i: (i, 0, 0)),
        ],
        out_shape=[
            jax.ShapeDtypeStruct((b, s, GW), BF16),
            jax.ShapeDtypeStruct((b, s, GW), BF16),
        ],
        scratch_shapes=[
            pltpu.VMEM((s + 16, GW), F32),
            pltpu.VMEM((3, s + 16, GW), F32),
            pltpu.VMEM((nm, GW), BF16),
            pltpu.VMEM((NH * nm, GW), BF16),
        ],
        compiler_params=pltpu.CompilerParams(
            dimension_semantics=("arbitrary",), vmem_limit_bytes=VMEM_LIMIT),
    )(f3, p3, p3, p3, mem, mng, wkv, gmk, bd, wp, ps, win)


def _out_kernel(a_ref, b_ref, c_ref, d_ref, e_ref, w_ref, x_ref, o_ref):
    mixed = jnp.concatenate([a_ref[...], b_ref[...], c_ref[...], d_ref[...], e_ref[...]], axis=1)
    o_ref[...] = x_ref[...] + _dot(mixed, w_ref[...])


def _out(parts, w, x2, tm):
    m = x2.shape[0]
    gspec = pl.BlockSpec((tm, GW), lambda i: (i, 0))
    return pl.pallas_call(
        _out_kernel,
        grid=(m // tm,),
        in_specs=[gspec] * 5 + [
            pl.BlockSpec((5 * GW, D_MODEL), lambda i: (0, 0)),
            pl.BlockSpec((tm, D_MODEL), lambda i: (i, 0)),
        ],
        out_specs=pl.BlockSpec((tm, D_MODEL), lambda i: (i, 0)),
        out_shape=jax.ShapeDtypeStruct((m, D_MODEL), F32),
        compiler_params=pltpu.CompilerParams(
            dimension_semantics=("arbitrary",), vmem_limit_bytes=VMEM_LIMIT),
    )(*parts, w, x2)


def _out_proj_kernel(a_ref, b_ref, c_ref, d_ref, e_ref, w_ref, x_ref, g_ref, wb_ref, wf_ref, hgain_ref, bd_ref,
                     o_ref, pb_ref, pf_ref):
    _out_kernel(a_ref, b_ref, c_ref, d_ref, e_ref, w_ref, x_ref, o_ref)
    _proj_kernel(o_ref, g_ref, wb_ref, wf_ref, hgain_ref, bd_ref, pb_ref, pf_ref)


def _out_proj(parts, w, x2, g, wb, wf, hgain, bd, tm):
    m = x2.shape[0]
    gspec = pl.BlockSpec((tm, GW), lambda i: (i, 0))
    c2 = lambda i: (0, 0)
    return pl.pallas_call(
        _out_proj_kernel,
        grid=(m // tm,),
        in_specs=[gspec] * 5 + [
            pl.BlockSpec((5 * GW, D_MODEL), c2),
            pl.BlockSpec((tm, D_MODEL), lambda i: (i, 0)),
            pl.BlockSpec((1, D_MODEL), c2),
            pl.BlockSpec((D_MODEL, NP_COLS), c2),
            pl.BlockSpec((D_MODEL, NF_COLS), c2),
            pl.BlockSpec((8, GW), c2),
            pl.BlockSpec((GW, GW), c2),
        ],
        out_specs=[
            pl.BlockSpec((tm, D_MODEL), lambda i: (i, 0)),
            pl.BlockSpec((tm, NP_COLS), lambda i: (i, 0)),
            pl.BlockSpec((tm, NF_COLS), lambda i: (i, 0)),
        ],
        out_shape=[
            jax.ShapeDtypeStruct((m, D_MODEL), F32),
            jax.ShapeDtypeStruct((m, NP_COLS), BF16),
            jax.ShapeDtypeStruct((m, NF_COLS), F32),
        ],
        compiler_params=pltpu.CompilerParams(
            dimension_semantics=("arbitrary",), vmem_limit_bytes=VMEM_LIMIT),
    )(*parts, w, x2, g, wb, wf, hgain, bd)


def _tile_heads(g):
    return jnp.tile(g.astype(F32), NH).reshape(1, GW)


def kernel(x, mem, norm_g, w_in, fox_f_bias, fox_q_norm, fox_k_norm, hgrn_lb_logits, hgrn_out_norm,
           pool_w, pool_scale, mem_norm_g, mem_w_kv, mem_q_norm, mem_k_norm, w_out):
    bsz, seq, _ = x.shape
    depth = w_in.shape[0]
    m = bsz * seq
    tq = 256
    tm = 512
    tm_out = 1024
    tq_pm = min(seq, 4 * tq)

    pr = jax.nn.softmax(hgrn_lb_logits.astype(F32), axis=0)
    lower_bounds = jnp.clip(jnp.cumsum(pr, axis=0) - pr[0:1], 0.0, 1.0 - 1e-6)

    bd_np = _bd_ones()
    bd = jnp.asarray(bd_np, BF16)
    tri = jnp.asarray(np.triu(np.ones((tq, tq), np.float32)), BF16)

    def suffix_and_ones(w):
        jj = np.arange(w)
        suffix = (jj[:, None] > jj[None, :]).astype(np.float32)
        return jnp.asarray(np.concatenate([suffix, np.ones((w, SB_CW), np.float32)], axis=1), BF16)

    to = suffix_and_ones(SB_CW)
    tof = suffix_and_ones(SB_TQ)
    a_np, masks_np, hm_np, maskc_np, hmc_np = _hgrn_constants()
    a_all = jnp.asarray(a_np, BF16)
    masks = jnp.asarray(masks_np, F32)
    hm = jnp.asarray(hm_np, BF16)
    maskc = jnp.asarray(maskc_np, F32)
    hmc = jnp.asarray(hmc_np, BF16)
    win = jnp.asarray(np.repeat(np.array(POOL_WINDOWS, np.float32), HD).reshape(1, GW))

    g = GW
    wb_all, wf_all = _regroup_w_in(w_in)
    x2 = x.reshape(m, D_MODEL)
    norm_gs = [norm_g[l].reshape(1, D_MODEL).astype(F32) for l in range(depth)]
    head_gains = [jnp.concatenate([_tile_heads(fox_q_norm[l]), _tile_heads(fox_k_norm[l]),
                                   _tile_heads(mem_q_norm[l]), jnp.zeros((5, GW), F32)], axis=0)
                  for l in range(depth)]
    pb, pf = _proj(x2, norm_gs[0], wb_all[0], wf_all[0], head_gains[0], bd, tm)
    for l in range(depth):
        p3 = pb.reshape(bsz, seq, NP_COLS)
        f3 = pf.reshape(bsz, seq, NF_COLS)

        bias = jnp.broadcast_to(jnp.pad(fox_f_bias[l].astype(F32), (0, 8 - NH))[:, None], (8, tq))
        out_a = _fox(p3, f3, bias, _tile_heads(fox_q_norm[l]), _tile_heads(fox_k_norm[l]), bd, tri)
        out_b = _sb(p3, to, tof, bd)

        lb = lower_bounds[l].reshape(1, g)
        out_c = _hgrn(p3, f3, jnp.maximum(lb, LB_FLOOR), 1.0 - lb,
                      hgrn_out_norm[l].reshape(1, g).astype(F32), bd, a_all, masks, hm, maskc, hmc)

        wp = jax.scipy.linalg.block_diag(*[pool_w[l, i] for i in range(len(POOL_WINDOWS))]).astype(BF16)
        out_d, out_e = _pm(p3, f3, mem, mem_norm_g[l].reshape(1, D_MODEL).astype(F32),
                           mem_w_kv[l].astype(BF16), _tile_heads(mem_k_norm[l]),
                           bd, wp, pool_scale[l].reshape(1, g).astype(F32), win, tq_pm)

        parts = [o.reshape(m, g) for o in (out_a, out_b, out_c, out_d, out_e)]
        if l + 1 < depth:
            x2, pb, pf = _out_proj(parts, w_out[l].astype(BF16), x2, norm_gs[l + 1],
                                   wb_all[l + 1], wf_all[l + 1], head_gains[l + 1], bd, tm)
        else:
            x2 = _out(parts, w_out[l].astype(BF16), x2, tm_out)
    return x2.reshape(bsz, seq, D_MODEL)
```
